```python
import math
import jax, jax.numpy as jnp
from jax import lax
import numpy as np

D_MODEL = 2048
BATCH = 8
SEQ = 8192
DEPTH = 4

HEAD_DIM = 64
SGU_WIDTH = D_MODEL // 2
N_SGU_HEADS = SGU_WIDTH // HEAD_DIM
SGU_HEAD_DIM = SGU_WIDTH // N_SGU_HEADS
CHUNK = 128
ATTN_WIDTH = D_MODEL - SGU_WIDTH
N_Q_HEADS = ATTN_WIDTH // HEAD_DIM
N_KV_HEADS = 4
GROUP = N_Q_HEADS // N_KV_HEADS
WINDOW = 128
BLOCK = 128
NUM_BUCKETS = 32
MAX_DISTANCE = 128
IN_WIDTH = 2 * SGU_WIDTH + N_Q_HEADS * HEAD_DIM + 2 * N_KV_HEADS * HEAD_DIM
D_FF = -(-8 * D_MODEL // (3 * 256)) * 256
EPS = 1e-6
NEG_INF = -1e30

kernel_name = "hybrid_sgu_swa_sink_trunk"


def rms_norm(x, g):
    xf = x.astype(jnp.float32)
    y = xf * lax.rsqrt(jnp.mean(xf * xf, axis=-1, keepdims=True) + EPS)
    return (y * g.astype(jnp.float32)).astype(x.dtype)


def t5_causal_bucket(dist):
    n = jnp.maximum(dist, 0)
    max_exact = NUM_BUCKETS // 2
    nf = jnp.maximum(n, 1).astype(jnp.float32)
    large = max_exact + (jnp.log(nf / max_exact) / math.log(MAX_DISTANCE / max_exact)
                         * (NUM_BUCKETS - max_exact)).astype(jnp.int32)
    large = jnp.minimum(large, NUM_BUCKETS - 1)
    return jnp.where(n < max_exact, n, large)


def chunked_sgu(z, v_norm_g, w_s, b_s):
    b, s, _ = z.shape
    u, v = jnp.split(z, 2, axis=-1)
    v = v.reshape(b, s // CHUNK, CHUNK, N_SGU_HEADS, SGU_HEAD_DIM)
    v = rms_norm(v, v_norm_g)
    w = w_s * jnp.tril(jnp.ones((CHUNK, CHUNK), w_s.dtype))
    gate = jnp.einsum('hts,bcshd->bcthd', w, v) + b_s.T[None, None, :, :, None]
    return u * gate.reshape(b, s, SGU_WIDTH)


def swa_sink_attention(q, k, v, q_norm_g, k_norm_g, sinks, rel_bias):
    b, s, _ = q.shape
    nb = s // BLOCK
    q = rms_norm(q.reshape(b, nb, BLOCK, N_KV_HEADS, GROUP, HEAD_DIM), q_norm_g)
    k = rms_norm(k.reshape(b, nb, BLOCK, N_KV_HEADS, HEAD_DIM), k_norm_g)
    v = v.reshape(b, nb, BLOCK, N_KV_HEADS, HEAD_DIM)

    def band(t):
        prev = jnp.concatenate([jnp.zeros_like(t[:, :1]), t[:, :-1]], axis=1)
        return jnp.concatenate([prev, t], axis=2)

    kb, vb = band(k), band(v)
    scale = 1.0 / math.sqrt(HEAD_DIM)
    scores = jnp.einsum('bnqkgd,bnskd->bnkgqs', q, kb).astype(jnp.float32) * scale

    qi = jnp.arange(BLOCK)[:, None]
    kj = jnp.arange(2 * BLOCK)[None, :]
    dist = qi + BLOCK - kj
    bias = rel_bias[t5_causal_bucket(dist)].astype(jnp.float32)
    bias = jnp.transpose(bias, (2, 0, 1)).reshape(N_KV_HEADS, GROUP, BLOCK, 2 * BLOCK)
    in_window = (dist >= 0) & (dist < WINDOW)
    key_pos = jnp.arange(nb)[:, None] * BLOCK - BLOCK + kj
    valid = in_window[None] & (key_pos >= 0)[:, None, :]
    scores = jnp.where(valid[None, :, None, None], scores + bias, NEG_INF)

    sink = jnp.broadcast_to(
        sinks.astype(jnp.float32).reshape(N_KV_HEADS, GROUP)[None, None, :, :, None, None],
        scores.shape[:-1] + (1,))
    probs = jax.nn.softmax(jnp.concatenate([scores, sink], axis=-1), axis=-1)[..., :-1]
    out = jnp.einsum('bnkgqs,bnskd->bnqkgd', probs.astype(vb.dtype), vb)
    return out.reshape(b, s, ATTN_WIDTH)


def _fwd_setup_inputs(seed: int = 0) -> dict:
    key = jax.random.key(seed)
    ks = jax.random.split(key, 20)
    f32 = jnp.float32
    nrm = lambda k, shape, sc: jax.random.normal(k, shape, f32) * sc
    gain = lambda k, shape: 1.0 + 0.02 * jax.random.normal(k, shape, f32)
    return {
        "x": jax.random.normal(ks[0], (BATCH, SEQ, D_MODEL), f32),
        "rel_bias": nrm(ks[1], (NUM_BUCKETS, N_Q_HEADS), 0.1),
        "norm1_g": gain(ks[2], (DEPTH, D_MODEL)),
        "w_in": nrm(ks[3], (DEPTH, D_MODEL, IN_WIDTH), D_MODEL ** -0.5),
        "sgu_norm_g": gain(ks[4], (DEPTH, N_SGU_HEADS, SGU_HEAD_DIM)),
        "sgu_w": nrm(ks[5], (DEPTH, N_SGU_HEADS, CHUNK, CHUNK), CHUNK ** -0.5),
        "sgu_b": gain(ks[6], (DEPTH, N_SGU_HEADS, CHUNK)),
        "q_norm_g": gain(ks[7], (DEPTH, HEAD_DIM)),
        "k_norm_g": gain(ks[8], (DEPTH, HEAD_DIM)),
        "sinks": nrm(ks[9], (DEPTH, N_Q_HEADS), 0.5),
        "out_norm_a": gain(ks[10], (DEPTH, SGU_WIDTH)),
        "out_norm_b": gain(ks[11], (DEPTH, ATTN_WIDTH)),
        "w_out": nrm(ks[12], (DEPTH, D_MODEL, D_MODEL), D_MODEL ** -0.5),
        "norm2_g": gain(ks[13], (DEPTH, D_MODEL)),
        "w_gate": nrm(ks[14], (DEPTH, D_MODEL, D_FF), D_MODEL ** -0.5),
        "w_up": nrm(ks[15], (DEPTH, D_MODEL, D_FF), D_MODEL ** -0.5),
        "w_down": nrm(ks[16], (DEPTH, D_FF, D_MODEL), D_FF ** -0.5),
    }


def _fwd_reference(x, rel_bias, norm1_g, w_in, sgu_norm_g, sgu_w, sgu_b, q_norm_g, k_norm_g,
              sinks, out_norm_a, out_norm_b, w_out, norm2_g, w_gate, w_up, w_down):
    q_end = 2 * SGU_WIDTH + N_Q_HEADS * HEAD_DIM
    k_end = q_end + N_KV_HEADS * HEAD_DIM
    for l in range(DEPTH):
        h = rms_norm(x, norm1_g[l])
        z = h @ w_in[l]
        z_sgu = jax.nn.gelu(z[..., :2 * SGU_WIDTH], approximate=False)
        out_a = chunked_sgu(z_sgu, sgu_norm_g[l], sgu_w[l], sgu_b[l])
        out_b = swa_sink_attention(z[..., 2 * SGU_WIDTH:q_end], z[..., q_end:k_end],
                                   z[..., k_end:], q_norm_g[l], k_norm_g[l], sinks[l],
                                   rel_bias)
        mixed = jnp.concatenate([rms_norm(out_a, out_norm_a[l]),
                                 rms_norm(out_b, out_norm_b[l])], axis=-1)
        x = x + mixed @ w_out[l]
        h2 = rms_norm(x, norm2_g[l])
        x = x + (jax.nn.silu(h2 @ w_gate[l]) * (h2 @ w_up[l])) @ w_down[l]
    return x


import jax as _jax
import jax.numpy as _jnp

TWIN_FORMAT = 'train_step'
FWD_PARAMS = ['x', 'rel_bias', 'norm1_g', 'w_in', 'sgu_norm_g', 'sgu_w', 'sgu_b', 'q_norm_g', 'k_norm_g', 'sinks', 'out_norm_a', 'out_norm_b', 'w_out', 'norm2_g', 'w_gate', 'w_up', 'w_down']
TWIN_WEIGHTS = ['rel_bias', 'norm1_g', 'w_in', 'sgu_norm_g', 'sgu_w', 'sgu_b', 'q_norm_g', 'k_norm_g', 'sinks', 'out_norm_a', 'out_norm_b', 'w_out', 'norm2_g', 'w_gate', 'w_up', 'w_down']
TWIN_DIFF_INPUT = 'x'
TWIN_INPUTS = ['x', 'rel_bias', 'norm1_g', 'w_in', 'sgu_norm_g', 'sgu_w', 'sgu_b', 'q_norm_g', 'k_norm_g', 'sinks', 'out_norm_a', 'out_norm_b', 'w_out', 'norm2_g', 'w_gate', 'w_up', 'w_down', 'loss_target', 'm_rel_bias', 'm_norm1_g', 'm_w_in', 'm_sgu_norm_g', 'm_sgu_w', 'm_sgu_b', 'm_q_norm_g', 'm_k_norm_g', 'm_sinks', 'm_out_norm_a', 'm_out_norm_b', 'm_w_out', 'm_norm2_g', 'm_w_gate', 'm_w_up', 'm_w_down', 'v_rel_bias', 'v_norm1_g', 'v_w_in', 'v_sgu_norm_g', 'v_sgu_w', 'v_sgu_b', 'v_q_norm_g', 'v_k_norm_g', 'v_sinks', 'v_out_norm_a', 'v_out_norm_b', 'v_w_out', 'v_norm2_g', 'v_w_gate', 'v_w_up', 'v_w_down']
TWIN_OUTPUTS = ['loss', 'grad_x', 'grad_rel_bias', 'grad_norm1_g', 'grad_w_in', 'grad_sgu_norm_g', 'grad_sgu_w', 'grad_sgu_b', 'grad_q_norm_g', 'grad_k_norm_g', 'grad_sinks', 'grad_out_norm_a', 'grad_out_norm_b', 'grad_w_out', 'grad_norm2_g', 'grad_w_gate', 'grad_w_up', 'grad_w_down', 'delta_rel_bias', 'delta_norm1_g', 'delta_w_in', 'delta_sgu_norm_g', 'delta_sgu_w', 'delta_sgu_b', 'delta_q_norm_g', 'delta_k_norm_g', 'delta_sinks', 'delta_out_norm_a', 'delta_out_norm_b', 'delta_w_out', 'delta_norm2_g', 'delta_w_gate', 'delta_w_up', 'delta_w_down', 'new_m_rel_bias', 'new_m_norm1_g', 'new_m_w_in', 'new_m_sgu_norm_g', 'new_m_sgu_w', 'new_m_sgu_b', 'new_m_q_norm_g', 'new_m_k_norm_g', 'new_m_sinks', 'new_m_out_norm_a', 'new_m_out_norm_b', 'new_m_w_out', 'new_m_norm2_g', 'new_m_w_gate', 'new_m_w_up', 'new_m_w_down', 'new_v_rel_bias', 'new_v_norm1_g', 'new_v_w_in', 'new_v_sgu_norm_g', 'new_v_sgu_w', 'new_v_sgu_b', 'new_v_q_norm_g', 'new_v_k_norm_g', 'new_v_sinks', 'new_v_out_norm_a', 'new_v_out_norm_b', 'new_v_w_out', 'new_v_norm2_g', 'new_v_w_gate', 'new_v_w_up', 'new_v_w_down']
TWIN_LEAF_KINDS = {'loss': 'loss', 'grad_x': 'grad_x', 'grad_rel_bias': 'grad_w', 'grad_norm1_g': 'grad_w', 'grad_w_in': 'grad_w', 'grad_sgu_norm_g': 'grad_w', 'grad_sgu_w': 'grad_w', 'grad_sgu_b': 'grad_w', 'grad_q_norm_g': 'grad_w', 'grad_k_norm_g': 'grad_w', 'grad_sinks': 'grad_w', 'grad_out_norm_a': 'grad_w', 'grad_out_norm_b': 'grad_w', 'grad_w_out': 'grad_w', 'grad_norm2_g': 'grad_w', 'grad_w_gate': 'grad_w', 'grad_w_up': 'grad_w', 'grad_w_down': 'grad_w', 'delta_rel_bias': 'delta_w', 'delta_norm1_g': 'delta_w', 'delta_w_in': 'delta_w', 'delta_sgu_norm_g': 'delta_w', 'delta_sgu_w': 'delta_w', 'delta_sgu_b': 'delta_w', 'delta_q_norm_g': 'delta_w', 'delta_k_norm_g': 'delta_w', 'delta_sinks': 'delta_w', 'delta_out_norm_a': 'delta_w', 'delta_out_norm_b': 'delta_w', 'delta_w_out': 'delta_w', 'delta_norm2_g': 'delta_w', 'delta_w_gate': 'delta_w', 'delta_w_up': 'delta_w', 'delta_w_down': 'delta_w', 'new_m_rel_bias': 'new_m', 'new_m_norm1_g': 'new_m', 'new_m_w_in': 'new_m', 'new_m_sgu_norm_g': 'new_m', 'new_m_sgu_w': 'new_m', 'new_m_sgu_b': 'new_m', 'new_m_q_norm_g': 'new_m', 'new_m_k_norm_g': 'new_m', 'new_m_sinks': 'new_m', 'new_m_out_norm_a': 'new_m', 'new_m_out_norm_b': 'new_m', 'new_m_w_out': 'new_m', 'new_m_norm2_g': 'new_m', 'new_m_w_gate': 'new_m', 'new_m_w_up': 'new_m', 'new_m_w_down': 'new_m', 'new_v_rel_bias': 'new_v', 'new_v_norm1_g': 'new_v', 'new_v_w_in': 'new_v', 'new_v_sgu_norm_g': 'new_v', 'new_v_sgu_w': 'new_v', 'new_v_sgu_b': 'new_v', 'new_v_q_norm_g': 'new_v', 'new_v_k_norm_g': 'new_v', 'new_v_sinks': 'new_v', 'new_v_out_norm_a': 'new_v', 'new_v_out_norm_b': 'new_v', 'new_v_w_out': 'new_v', 'new_v_norm2_g': 'new_v', 'new_v_w_gate': 'new_v', 'new_v_w_up': 'new_v', 'new_v_w_down': 'new_v'}


def _forward(args):
    return _fwd_reference(*[args[k] for k in FWD_PARAMS])


def _output_shape():
    def fwd():
        inp = _fwd_setup_inputs(0)
        return _fwd_reference(*[inp[k] for k in FWD_PARAMS])
    out = _jax.eval_shape(fwd)
    return out.shape, out.dtype

N_MICROBATCH = 1
ADAM_LR = 0.001
ADAM_B1 = 0.9
ADAM_B2 = 0.999
ADAM_EPS = 1e-08
ADAM_WD = 0.01
ADAM_STEP = 10
PER_EXAMPLE_BATCH_AXIS = {'x': 0, 'loss_target': 0}
SHARED_INPUTS = []
_WEIGHT_DTYPES = {'rel_bias': _jnp.float32, 'norm1_g': _jnp.float32, 'w_in': _jnp.float32, 'sgu_norm_g': _jnp.float32, 'sgu_w': _jnp.float32, 'sgu_b': _jnp.float32, 'q_norm_g': _jnp.float32, 'k_norm_g': _jnp.float32, 'sinks': _jnp.float32, 'out_norm_a': _jnp.float32, 'out_norm_b': _jnp.float32, 'w_out': _jnp.float32, 'norm2_g': _jnp.float32, 'w_gate': _jnp.float32, 'w_up': _jnp.float32, 'w_down': _jnp.float32}
MOMENT_SCALE = {'rel_bias': 1.445701e+00, 'norm1_g': 1.354805e+01, 'w_in': 9.839485e+00, 'sgu_norm_g': 6.593761e-01, 'sgu_w': 3.568590e-01, 'sgu_b': 9.240627e-01, 'q_norm_g': 4.052479e+00, 'k_norm_g': 4.063761e+00, 'sinks': 1.113879e+00, 'out_norm_a': 4.003183e+01, 'out_norm_b': 4.030603e+01, 'w_out': 1.962244e+01, 'norm2_g': 2.522859e+01, 'w_gate': 1.438652e+00, 'w_up': 1.898383e+00, 'w_down': 3.057243e+00}


def _to_microbatches(a, axis):
    t = _jnp.moveaxis(a, axis, 0)
    t = t.reshape((N_MICROBATCH, t.shape[0] // N_MICROBATCH) + t.shape[1:])
    return _jnp.moveaxis(t, 1, axis + 1)


def setup_inputs(seed: int = 0) -> dict:
    inp = _fwd_setup_inputs(seed)
    key = _jax.random.fold_in(_jax.random.key(seed), 7919)
    shape, _ = _output_shape()
    out = dict(inp)
    out["loss_target"] = _jax.random.normal(_jax.random.fold_in(key, 0), shape, _jnp.float32)
    for i, name in enumerate(TWIN_WEIGHTS):
        w = inp[name].astype(_jnp.float32)
        if MOMENT_SCALE is None:
            s = _jnp.sqrt(_jnp.mean(_jnp.square(w)) + 1e-30)
        else:
            s = MOMENT_SCALE[name]
        km, kv = _jax.random.split(_jax.random.fold_in(key, i + 1))
        out[name] = w
        out["m_" + name] = s * _jax.random.normal(km, w.shape, _jnp.float32)
        out["v_" + name] = (s * s) * _jax.random.uniform(kv, w.shape, _jnp.float32, 0.5, 1.5)
    if N_MICROBATCH > 1:
        for name, axis in PER_EXAMPLE_BATCH_AXIS.items():
            out[name] = _to_microbatches(out[name], axis)
    return {'x': out['x'], 'rel_bias': out['rel_bias'], 'norm1_g': out['norm1_g'], 'w_in': out['w_in'], 'sgu_norm_g': out['sgu_norm_g'], 'sgu_w': out['sgu_w'], 'sgu_b': out['sgu_b'], 'q_norm_g': out['q_norm_g'], 'k_norm_g': out['k_norm_g'], 'sinks': out['sinks'], 'out_norm_a': out['out_norm_a'], 'out_norm_b': out['out_norm_b'], 'w_out': out['w_out'], 'norm2_g': out['norm2_g'], 'w_gate': out['w_gate'], 'w_up': out['w_up'], 'w_down': out['w_down'], 'loss_target': out['loss_target'], 'm_rel_bias': out['m_rel_bias'], 'm_norm1_g': out['m_norm1_g'], 'm_w_in': out['m_w_in'], 'm_sgu_norm_g': out['m_sgu_norm_g'], 'm_sgu_w': out['m_sgu_w'], 'm_sgu_b': out['m_sgu_b'], 'm_q_norm_g': out['m_q_norm_g'], 'm_k_norm_g': out['m_k_norm_g'], 'm_sinks': out['m_sinks'], 'm_out_norm_a': out['m_out_norm_a'], 'm_out_norm_b': out['m_out_norm_b'], 'm_w_out': out['m_w_out'], 'm_norm2_g': out['m_norm2_g'], 'm_w_gate': out['m_w_gate'], 'm_w_up': out['m_w_up'], 'm_w_down': out['m_w_down'], 'v_rel_bias': out['v_rel_bias'], 'v_norm1_g': out['v_norm1_g'], 'v_w_in': out['v_w_in'], 'v_sgu_norm_g': out['v_sgu_norm_g'], 'v_sgu_w': out['v_sgu_w'], 'v_sgu_b': out['v_sgu_b'], 'v_q_norm_g': out['v_q_norm_g'], 'v_k_norm_g': out['v_k_norm_g'], 'v_sinks': out['v_sinks'], 'v_out_norm_a': out['v_out_norm_a'], 'v_out_norm_b': out['v_out_norm_b'], 'v_w_out': out['v_w_out'], 'v_norm2_g': out['v_norm2_g'], 'v_w_gate': out['v_w_gate'], 'v_w_up': out['v_w_up'], 'v_w_down': out['v_w_down']}


def _loss(weights, diff, rest, loss_target):
    with _jax.named_scope("forward"):
        args = {**rest, TWIN_DIFF_INPUT: diff, **{k: w.astype(_WEIGHT_DTYPES[k]) for k, w in weights.items()}}
        y = _forward(args)
    with _jax.named_scope("loss_head"):
        err = _jnp.square(y.astype(_jnp.float32) - loss_target)
        return 0.5 * _jnp.sum(_jnp.mean(err, axis=-1)) if err.ndim else 0.5 * err


def _adamw(w, g, m, v):
    m = ADAM_B1 * m + (1.0 - ADAM_B1) * g
    v = ADAM_B2 * v + (1.0 - ADAM_B2) * _jnp.square(g)
    m_hat = m / (1.0 - ADAM_B1 ** ADAM_STEP)
    v_hat = v / (1.0 - ADAM_B2 ** ADAM_STEP)
    delta = -ADAM_LR * (m_hat / (_jnp.sqrt(v_hat) + ADAM_EPS) + ADAM_WD * w)
    return delta, m, v


def reference(x, rel_bias, norm1_g, w_in, sgu_norm_g, sgu_w, sgu_b, q_norm_g, k_norm_g, sinks, out_norm_a, out_norm_b, w_out, norm2_g, w_gate, w_up, w_down, loss_target, m_rel_bias, m_norm1_g, m_w_in, m_sgu_norm_g, m_sgu_w, m_sgu_b, m_q_norm_g, m_k_norm_g, m_sinks, m_out_norm_a, m_out_norm_b, m_w_out, m_norm2_g, m_w_gate, m_w_up, m_w_down, v_rel_bias, v_norm1_g, v_w_in, v_sgu_norm_g, v_sgu_w, v_sgu_b, v_q_norm_g, v_k_norm_g, v_sinks, v_out_norm_a, v_out_norm_b, v_w_out, v_norm2_g, v_w_gate, v_w_up, v_w_down):
    given = dict(x=x, rel_bias=rel_bias, norm1_g=norm1_g, w_in=w_in, sgu_norm_g=sgu_norm_g, sgu_w=sgu_w, sgu_b=sgu_b, q_norm_g=q_norm_g, k_norm_g=k_norm_g, sinks=sinks, out_norm_a=out_norm_a, out_norm_b=out_norm_b, w_out=w_out, norm2_g=norm2_g, w_gate=w_gate, w_up=w_up, w_down=w_down, loss_target=loss_target, m_rel_bias=m_rel_bias, m_norm1_g=m_norm1_g, m_w_in=m_w_in, m_sgu_norm_g=m_sgu_norm_g, m_sgu_w=m_sgu_w, m_sgu_b=m_sgu_b, m_q_norm_g=m_q_norm_g, m_k_norm_g=m_k_norm_g, m_sinks=m_sinks, m_out_norm_a=m_out_norm_a, m_out_norm_b=m_out_norm_b, m_w_out=m_w_out, m_norm2_g=m_norm2_g, m_w_gate=m_w_gate, m_w_up=m_w_up, m_w_down=m_w_down, v_rel_bias=v_rel_bias, v_norm1_g=v_norm1_g, v_w_in=v_w_in, v_sgu_norm_g=v_sgu_norm_g, v_sgu_w=v_sgu_w, v_sgu_b=v_sgu_b, v_q_norm_g=v_q_norm_g, v_k_norm_g=v_k_norm_g, v_sinks=v_sinks, v_out_norm_a=v_out_norm_a, v_out_norm_b=v_out_norm_b, v_w_out=v_w_out, v_norm2_g=v_norm2_g, v_w_gate=v_w_gate, v_w_up=v_w_up, v_w_down=v_w_down)
    weights = {n: given[n] for n in TWIN_WEIGHTS}
    shared = {n: given[n] for n in SHARED_INPUTS}
    per_example = {n: given[n] for n in ['x']}
    grad_fn = _jax.value_and_grad(_loss, argnums=(0, 1))

    def one_microbatch(ex, loss_target):
        ex = dict(ex)
        diff = ex.pop(TWIN_DIFF_INPUT)
        return grad_fn(weights, diff, {**shared, **ex}, loss_target)

    if N_MICROBATCH == 1:
        loss, (grad_w, grad_x) = one_microbatch(per_example, given["loss_target"])
    else:
        def body(carry, xs):
            loss_sum, grad_sum = carry
            l_k, (gw_k, gx_k) = one_microbatch(xs[0], xs[1])
            with _jax.named_scope("update"):
                return (loss_sum + l_k, _jax.tree.map(_jnp.add, grad_sum, gw_k)), gx_k

        init = (_jnp.zeros((), _jnp.float32), _jax.tree.map(_jnp.zeros_like, weights))
        (loss, grad_w), grad_x = _jax.lax.scan(body, init, (per_example, given["loss_target"]))
    with _jax.named_scope("update"):
        delta_w, new_m, new_v = {}, {}, {}
        for n in TWIN_WEIGHTS:
            delta_w[n], new_m[n], new_v[n] = _adamw(weights[n], grad_w[n], given["m_" + n], given["v_" + n])
    return (loss, grad_x, *[grad_w[n] for n in TWIN_WEIGHTS], *[delta_w[n] for n in TWIN_WEIGHTS],
            *[new_m[n] for n in TWIN_WEIGHTS], *[new_v[n] for n in TWIN_WEIGHTS])
```

```python
import functools
import math

import numpy as np

import jax
import jax.numpy as jnp
from jax import lax
from jax.experimental import pallas as pl
from jax.experimental.pallas import tpu as pltpu

F32 = jnp.float32
BF16 = jnp.bfloat16
MESH = pl.DeviceIdType.MESH
ANY = pl.BlockSpec(memory_space=pl.ANY)

HEAD_DIM = 64
BLOCK = 128
NUM_BUCKETS = 32
MAX_DISTANCE = 128
EPS = 1e-6
NEG_INF = -1e30
ADAM_LR, ADAM_B1, ADAM_B2, ADAM_EPS, ADAM_WD, ADAM_STEP = 0.001, 0.9, 0.999, 1e-08, 0.01, 10

LANES = 128
VMEM_LIMIT = 56 * 1024 * 1024
N_CHIPS = 4
N_DEV = 8


def _cparams(sem=None):
    return pltpu.CompilerParams(dimension_semantics=sem, vmem_limit_bytes=VMEM_LIMIT)


def _tile(dim, target, align=LANES):
    best = None
    for t in range(align, min(dim, target) + 1, align):
        if dim % t == 0:
            best = t
    return best if best is not None else dim


def _split3(x):
    hi = x.astype(BF16)
    r1 = x - hi.astype(F32)
    mid = r1.astype(BF16)
    lo = (r1 - mid.astype(F32)).astype(BF16)
    return hi, mid, lo


def _dot3(x, g):
    hi, mid, lo = _split3(x)
    d = lambda a: jnp.dot(a, g, preferred_element_type=F32)
    return d(hi) + d(mid) + d(lo)


def _matmul(a, b, *, mode, out_dtype, name, layer=None, res=None, exact=False, tm=1024, tn=1024, tk=2048):
    M, K = a.shape
    N = b.shape[-1] if mode == "nn" else b.shape[-2]
    tm, tn, tk = _tile(M, tm, 8 if M < LANES else LANES), _tile(N, tn), _tile(K, tk)
    nk = K // tk
    dn = (((1,), (0,)), ((), ())) if mode == "nn" else (((1,), (1,)), ((), ()))

    def body(*refs):
        a_ref, b_ref = refs[0], refs[1]
        r_ref = refs[2] if res is not None else None
        o_ref = refs[3] if res is not None else refs[2]
        if exact:
            part = _dot3(a_ref[...], b_ref[...])
        else:
            part = lax.dot_general(a_ref[...].astype(BF16), b_ref[...].astype(BF16), dn, preferred_element_type=F32)

        def finish(total):
            if r_ref is not None:
                total = r_ref[...] + total
            o_ref[...] = total.astype(o_ref.dtype)

        if nk == 1:
            finish(part)
        else:
            acc = refs[-1]
            k = pl.program_id(2)

            @pl.when(k == 0)
            def _():
                acc[...] = part

            @pl.when(k > 0)
            def _():
                acc[...] += part

            @pl.when(k == nk - 1)
            def _():
                finish(acc[...])

    lead = () if layer is None else (layer,)
    lead_blk = () if layer is None else (None,)
    if mode == "nn":
        b_spec = pl.BlockSpec(lead_blk + (tk, tn), lambda j, i, k: lead + (k, j))
    else:
        b_spec = pl.BlockSpec(lead_blk + (tn, tk), lambda j, i, k: lead + (j, k))
    in_specs = [pl.BlockSpec((tm, tk), lambda j, i, k: (i, k)), b_spec]
    args = [a, b]
    if res is not None:
        in_specs.append(pl.BlockSpec((tm, tn), lambda j, i, k: (i, j)))
        args.append(res)
    return pl.pallas_call(
        body,
        out_shape=jax.ShapeDtypeStruct((M, N), out_dtype),
        grid=(N // tn, M // tm, nk),
        in_specs=in_specs,
        out_specs=pl.BlockSpec((tm, tn), lambda j, i, k: (i, j)),
        scratch_shapes=[pltpu.VMEM((tm, tn), F32)] if nk > 1 else [],
        compiler_params=_cparams(("parallel", "parallel", "arbitrary")),
        name=name,
    )(*args)


def _matmul_tn(a, b, *, out_dtype, name, tm=2048, tn=1024, tt=1024):
    T, Mo = a.shape
    N = b.shape[1]
    tm, tn, tt = _tile(Mo, tm), _tile(N, tn), _tile(T, tt)
    nt = T // tt

    def body(a_ref, b_ref, o_ref, acc):
        t = pl.program_id(2)
        part = lax.dot_general(a_ref[...].astype(BF16), b_ref[...].astype(BF16), (((0,), (0,)), ((), ())),
                               preferred_element_type=F32)

        @pl.when(t == 0)
        def _():
            acc[...] = part

        @pl.when(t > 0)
        def _():
            acc[...] += part

        @pl.when(t == nt - 1)
        def _():
            o_ref[...] = acc[...].astype(o_ref.dtype)

    return pl.pallas_call(
        body,
        out_shape=jax.ShapeDtypeStruct((Mo, N), out_dtype),
        grid=(Mo // tm, N // tn, nt),
        in_specs=[pl.BlockSpec((tt, tm), lambda i, j, t: (t, i)), pl.BlockSpec((tt, tn), lambda i, j, t: (t, j))],
        out_specs=pl.BlockSpec((tm, tn), lambda i, j, t: (i, j)),
        scratch_shapes=[pltpu.VMEM((tm, tn), F32)],
        compiler_params=_cparams(("parallel", "parallel", "arbitrary")),
        name=name,
    )(a, b)


def _rms_fwd(x, g, *, name, tr=256):
    R, D = x.shape
    tr = _tile(R, tr, 8)

    def body(x_ref, g_ref, o_ref):
        xv = x_ref[...]
        r = lax.rsqrt(jnp.mean(xv * xv, axis=-1, keepdims=True) + EPS)
        o_ref[...] = (xv * r * g_ref[...]).astype(o_ref.dtype)

    return pl.pallas_call(
        body, out_shape=jax.ShapeDtypeStruct((R, D), BF16), grid=(R // tr,),
        in_specs=[pl.BlockSpec((tr, D), lambda i: (i, 0)), pl.BlockSpec((1, D), lambda i: (0, 0))],
        out_specs=pl.BlockSpec((tr, D), lambda i: (i, 0)),
        compiler_params=_cparams(("parallel",)), name=name,
    )(x, g)


def _rms_bwd(x, g, dh, dres, *, name, tr=256):
    R, D = x.shape
    tr = _tile(R, tr, 8)

    def body(x_ref, g_ref, dh_ref, dres_ref, dx_ref, dg_ref):
        i = pl.program_id(0)
        xv = x_ref[...]
        r = lax.rsqrt(jnp.mean(xv * xv, axis=-1, keepdims=True) + EPS)
        y = xv * r
        dhv = dh_ref[...]
        dy = dhv * g_ref[...]
        dx_ref[...] = dres_ref[...] + r * (dy - y * jnp.mean(dy * y, axis=-1, keepdims=True))
        dg = jnp.sum(dhv * y, axis=0, keepdims=True)

        @pl.when(i == 0)
        def _():
            dg_ref[...] = dg

        @pl.when(i > 0)
        def _():
            dg_ref[...] += dg

    row = pl.BlockSpec((tr, D), lambda i: (i, 0))
    vec = pl.BlockSpec((1, D), lambda i: (0, 0))
    return pl.pallas_call(
        body, out_shape=(jax.ShapeDtypeStruct((R, D), F32), jax.ShapeDtypeStruct((1, D), F32)), grid=(R // tr,),
        in_specs=[row, vec, row, row], out_specs=(row, vec),
        compiler_params=_cparams(("arbitrary",)), name=name,
    )(x, g, dh, dres)


def _swiglu_fwd(ab, *, name, tr=128):
    R, F2 = ab.shape
    FF = F2 // 2
    tr = _tile(R, tr, 8)

    def body(a_ref, b_ref, o_ref):
        a = a_ref[...]
        o_ref[...] = (a * jax.nn.sigmoid(a) * b_ref[...]).astype(o_ref.dtype)

    return pl.pallas_call(
        body, out_shape=jax.ShapeDtypeStruct((R, FF), BF16), grid=(R // tr,),
        in_specs=[pl.BlockSpec((tr, FF), lambda i: (i, 0)), pl.BlockSpec((tr, FF), lambda i: (i, 1))],
        out_specs=pl.BlockSpec((tr, FF), lambda i: (i, 0)),
        compiler_params=_cparams(("parallel",)), name=name,
    )(ab, ab)


def _swiglu_bwd(ab, df, *, name, tr=128):
    R, F2 = ab.shape
    FF = F2 // 2
    tr = _tile(R, tr, 8)

    def body(a_ref, b_ref, df_ref, o_ref):
        a = a_ref[...]
        dfv = df_ref[...]
        s = jax.nn.sigmoid(a)
        o_ref[:, :FF] = (dfv * b_ref[...] * (s * (1.0 + a * (1.0 - s)))).astype(o_ref.dtype)
        o_ref[:, FF:] = (dfv * (a * s)).astype(o_ref.dtype)

    return pl.pallas_call(
        body, out_shape=jax.ShapeDtypeStruct((R, F2), BF16), grid=(R // tr,),
        in_specs=[pl.BlockSpec((tr, FF), lambda i: (i, 0)), pl.BlockSpec((tr, FF), lambda i: (i, 1)),
                  pl.BlockSpec((tr, FF), lambda i: (i, 0))],
        out_specs=pl.BlockSpec((tr, F2), lambda i: (i, 0)),
        compiler_params=_cparams(("parallel",)), name=name,
    )(ab, ab, df)


def _loss_head(y, target, *, name, tr=256):
    R, D = y.shape
    tr = _tile(R, tr, 8)

    def body(y_ref, t_ref, dy_ref, l_ref):
        i = pl.program_id(0)
        e = y_ref[...] - t_ref[...]
        dy_ref[...] = e * (1.0 / D)
        part = 0.5 * jnp.sum(jnp.mean(e * e, axis=-1, keepdims=True), axis=0, keepdims=True)
        part = jnp.broadcast_to(part, (8, LANES))

        @pl.when(i == 0)
        def _():
            l_ref[...] = part

        @pl.when(i > 0)
        def _():
            l_ref[...] += part

    row = pl.BlockSpec((tr, D), lambda i: (i, 0))
    return pl.pallas_call(
        body, out_shape=(jax.ShapeDtypeStruct((R, D), F32), jax.ShapeDtypeStruct((8, LANES), F32)), grid=(R // tr,),
        in_specs=[row, row], out_specs=(row, pl.BlockSpec((8, LANES), lambda i: (0, 0))),
        compiler_params=_cparams(("arbitrary",)), name=name,
    )(y, target)


def _gelu(x):
    return 0.5 * x * (1.0 + lax.erf(x * math.sqrt(0.5)))


def _gelu_grad(x):
    return 0.5 * (1.0 + lax.erf(x * math.sqrt(0.5))) + x * jnp.exp(-0.5 * x * x) * (1.0 / math.sqrt(2.0 * math.pi))


def _group_consts(width):
    lane = np.arange(width)
    col = np.arange(LANES)
    grp = (lane[:, None] // HEAD_DIM == col[None, :]).astype(np.float32)
    mod = ((lane[:, None] % HEAD_DIM == col[None, :]) & (col[None, :] < HEAD_DIM)).astype(np.float32)
    return jnp.asarray(grp, BF16), jnp.asarray(grp.T, BF16), jnp.asarray(mod, BF16)


def _bucket_onehot():
    qi = np.arange(BLOCK)[:, None]
    kj = np.arange(2 * BLOCK)[None, :]
    n = np.maximum(qi + BLOCK - kj, 0)
    max_exact = NUM_BUCKETS // 2
    nf = np.maximum(n, 1).astype(np.float32)
    large = max_exact + (np.log(nf / np.float32(max_exact)) / np.float32(math.log(MAX_DISTANCE / max_exact))
                         * np.float32(NUM_BUCKETS - max_exact)).astype(np.int32)
    large = np.minimum(large, NUM_BUCKETS - 1)
    bucket = jnp.asarray(np.where(n < max_exact, n, large).reshape(-1).astype(np.int32))
    return (bucket[:, None] == jnp.arange(LANES, dtype=jnp.int32)[None, :]).astype(BF16)


class _MixerDims:
    def __init__(self, S, IN, SW, AW, KVW, H, NQ):
        self.S, self.IN, self.SW, self.AW, self.KVW, self.H, self.NQ = S, IN, SW, AW, KVW, H, NQ
        self.NKV = KVW // HEAD_DIM
        self.GROUP = NQ // self.NKV
        self.nb = S // BLOCK
        self.koff = 2 * SW + AW
        self.voff = self.koff + KVW
        assert SW % LANES == 0 and AW % LANES == 0 and KVW % LANES == 0 and self.GROUP % 2 == 0
        assert self.koff % (2 * KVW) == 0 and IN == self.voff + KVW and S % BLOCK == 0


def _mixer_block(dm, n, z, kvp, prm):
    SW, AW, KVW, H, NQ = dm.SW, dm.AW, dm.KVW, dm.H, dm.NQ
    lane = lax.broadcasted_iota(jnp.int32, (BLOCK, LANES), 1)
    lo = lane < HEAD_DIM
    row = lax.broadcasted_iota(jnp.int32, (BLOCK, BLOCK), 0)
    col = lax.broadcasted_iota(jnp.int32, (BLOCK, BLOCK), 1)
    tril = row >= col
    gs, gst = prm["gs"][...], prm["gst"][...]
    inv = 1.0 / HEAD_DIM

    def group_rsqrt(x, g, gt):
        ms = _dot3(x * x, g) * inv
        return _dot3(lax.rsqrt(ms + EPS), gt)

    zu, zv = z[:, :SW], z[:, SW:2 * SW]
    u, v = _gelu(zu), _gelu(zv)
    rv = group_rsqrt(v, gs, gst)
    vn = v * rv * prm["sgu_g"][...]
    vnb = vn.astype(BF16)
    tmats, gate_blocks = [], []
    for p in range(H // 2):
        blk = slice(LANES * p, LANES * (p + 1))
        t0 = jnp.where(tril, prm["sgu_w"][2 * p], 0.0).astype(BF16)
        t1 = jnp.where(tril, prm["sgu_w"][2 * p + 1], 0.0).astype(BF16)
        tmats += [t0, t1]
        g0 = jnp.dot(t0, vnb[:, blk], preferred_element_type=F32)
        g1 = jnp.dot(t1, vnb[:, blk], preferred_element_type=F32)
        gate_blocks.append(jnp.where(lo, g0, g1) + prm["sgu_bias"][:, blk])
    gate = jnp.concatenate(gate_blocks, axis=1)
    outa = u * gate
    ra = lax.rsqrt(jnp.mean(outa * outa, axis=-1, keepdims=True) + EPS)

    q = z[:, 2 * SW:2 * SW + AW]
    kcat = jnp.concatenate([kvp[:, :KVW], z[:, dm.koff:dm.koff + KVW]], axis=0)
    vcat = jnp.concatenate([kvp[:, KVW:], z[:, dm.voff:dm.voff + KVW]], axis=0)
    rq = group_rsqrt(q, gs, gst) if AW == SW else None
    assert rq is not None
    rk = group_rsqrt(kcat, prm["gk"][...], prm["gkt"][...])
    qn = q * rq * prm["q_g"][...]
    kn = kcat * rk * prm["k_g"][...]
    knb, vcb = kn.astype(BF16), vcat.astype(BF16)
    qi = lax.broadcasted_iota(jnp.int32, (BLOCK, 2 * BLOCK), 0)
    kj = lax.broadcasted_iota(jnp.int32, (BLOCK, 2 * BLOCK), 1)
    valid = (kj > qi) & (kj <= qi + BLOCK) & ((n > 0) | (kj >= BLOCK))
    scale = 1.0 / math.sqrt(HEAD_DIM)
    heads = []
    out_blocks = []
    for hq in range(NQ):
        mb, e = hq // 2, hq % 2
        kv = hq // dm.GROUP
        kb, ek = kv // 2, kv % 2
        qblk = qn[:, LANES * mb:LANES * (mb + 1)]
        if e != ek:
            qblk = pltpu.roll(qblk, HEAD_DIM, 1)
        half = lo if ek == 0 else jnp.logical_not(lo)
        qm = jnp.where(half, qblk, 0.0).astype(BF16)
        kblk = knb[:, LANES * kb:LANES * (kb + 1)]
        vblk = vcb[:, LANES * kb:LANES * (kb + 1)]
        s = lax.dot_general(qm, kblk, (((1,), (1,)), ((), ())), preferred_element_type=F32) * scale + prm["bias"][hq]
        s = jnp.where(valid, s, NEG_INF)
        sink = prm["sinks"][hq]
        mx = jnp.maximum(jnp.max(s, axis=-1, keepdims=True), sink)
        ex = jnp.exp(s - mx)
        den = jnp.sum(ex, axis=-1, keepdims=True) + jnp.exp(sink - mx)
        pr = ex / den
        psink = jnp.exp(sink - mx) / den
        prb = pr.astype(BF16)
        r_h = jnp.dot(prb, vblk, preferred_element_type=F32)
        if e != ek:
            r_h = pltpu.roll(r_h, HEAD_DIM, 1)
        heads.append(dict(qm=qm, kblk=kblk, vblk=vblk, pr=pr, prb=prb, psink=psink, half=half, mb=mb, e=e, ek=ek, kb=kb))
        if e == 1:
            out_blocks.append(jnp.where(lo, prev_r, r_h))
        prev_r = r_h
    outb = jnp.concatenate(out_blocks, axis=1)
    rb = lax.rsqrt(jnp.mean(outb * outb, axis=-1, keepdims=True) + EPS)
    return dict(lo=lo, tril=tril, gs=gs, gst=gst, zu=zu, zv=zv, u=u, v=v, rv=rv, vnb=vnb, tmats=tmats, gate=gate,
                outa=outa, ra=ra, q=q, kcat=kcat, rq=rq, rk=rk, heads=heads, outb=outb, rb=rb, scale=scale)


_MIXER_PARAMS = ("sgu_g", "sgu_w", "sgu_bias", "norm_a", "q_g", "k_g", "norm_b", "sinks", "bias", "gs", "gst", "gk", "gkt")


def _mixer_param_specs(dm, idx):
    SW, AW, KVW = dm.SW, dm.AW, dm.KVW
    full = lambda shape: pl.BlockSpec(shape, lambda n: (0,) * len(shape))
    return [full((1, SW)), full((dm.H, BLOCK, BLOCK)), full((BLOCK, SW)), full((1, SW)), full((1, AW)), full((1, KVW)),
            full((1, AW)), pl.BlockSpec(memory_space=pltpu.SMEM), full((dm.NQ, BLOCK, 2 * BLOCK)),
            full((SW, LANES)), full((LANES, SW)), full((KVW, LANES)), full((LANES, KVW))]


def _mixer_fwd(dm, z, params, *, name):
    nb = dm.nb

    def body(z_ref, kvp_ref, *rest):
        prm = dict(zip(_MIXER_PARAMS, rest[:len(_MIXER_PARAMS)]))
        o_ref = rest[len(_MIXER_PARAMS)]
        n = pl.program_id(0)
        c = _mixer_block(dm, n, z_ref[...], kvp_ref[...], prm)
        o_ref[:, :dm.SW] = (c["outa"] * c["ra"] * prm["norm_a"][...]).astype(o_ref.dtype)
        o_ref[:, dm.SW:] = (c["outb"] * c["rb"] * prm["norm_b"][...]).astype(o_ref.dtype)

    kvblk = dm.koff // (2 * dm.KVW)
    in_specs = [pl.BlockSpec((BLOCK, dm.IN), lambda n: (n, 0)),
                pl.BlockSpec((BLOCK, 2 * dm.KVW), lambda n: (jnp.maximum(n - 1, 0), kvblk))] + _mixer_param_specs(dm, None)
    return pl.pallas_call(
        body, out_shape=jax.ShapeDtypeStruct((dm.S, dm.SW + dm.AW), BF16), grid=(nb,),
        in_specs=in_specs, out_specs=pl.BlockSpec((BLOCK, dm.SW + dm.AW), lambda n: (n, 0)),
        compiler_params=_cparams(("arbitrary",)), name=name,
    )(z, z, *params)


def _mixer_bwd(dm, z, dmixed, dbias_in, params, gmod_q, gmod_k, *, name):
    SW, AW, KVW, H, NQ, nb, IN = dm.SW, dm.AW, dm.KVW, dm.H, dm.NQ, dm.nb, dm.IN
    QW = 2 * SW + AW
    NP = len(_MIXER_PARAMS)

    def body(z_ref, kvp_ref, dm_ref, dbin_ref, *rest):
        prm = dict(zip(_MIXER_PARAMS, rest[:NP]))
        gmq_ref, gmk_ref = rest[NP], rest[NP + 1]
        (dz_ref, dsg_ref, dt_ref, dsb_ref, dna_ref, dqg_ref, dkg_ref, dnb_ref, dsk_ref, dbias_ref) = rest[NP + 2:NP + 12]
        hold, tmpkv, newkv, carry, accb, accq, acck = rest[NP + 12:]
        n = pl.program_id(0)

        @pl.when(n == 0)
        def _():
            for r in (dsg_ref, dt_ref, dna_ref, dnb_ref, dsk_ref, accb, accq, acck):
                r[...] = jnp.zeros(r.shape, r.dtype)
            dbias_ref[...] = dbin_ref[...]

        @pl.when(n < nb)
        def _():
            c = _mixer_block(dm, n, z_ref[...], kvp_ref[...], prm)
            lo, gs, gst = c["lo"], c["gs"], c["gst"]
            dmx = dm_ref[...]
            inv = 1.0 / HEAD_DIM

            def rms_bwd_full(dy_scaled, y, r):
                return r * (dy_scaled - y * jnp.mean(dy_scaled * y, axis=-1, keepdims=True))

            def group_mean_b(x, g, gt):
                return _dot3(_dot3(x, g) * inv, gt)

            dma = dmx[:, :SW]
            ya = c["outa"] * c["ra"]
            dna_ref[...] += jnp.sum(dma * ya, axis=0, keepdims=True)
            douta = rms_bwd_full(dma * prm["norm_a"][...], ya, c["ra"])
            du = douta * c["gate"]
            dgate = douta * c["u"]
            accb[...] += dgate
            dgb16 = dgate.astype(BF16)
            dvn_blocks = []
            for p in range(H // 2):
                blk = slice(LANES * p, LANES * (p + 1))
                dg = dgate[:, blk]
                d0 = jnp.where(lo, dg, 0.0).astype(BF16)
                d1 = jnp.where(lo, 0.0, dg).astype(BF16)
                vb = c["vnb"][:, blk]
                nt = lambda a, b: lax.dot_general(a, b, (((1,), (1,)), ((), ())), preferred_element_type=F32)
                tn = lambda a, b: lax.dot_general(a, b, (((0,), (0,)), ((), ())), preferred_element_type=F32)
                dt_ref[2 * p] += nt(d0, vb)
                dt_ref[2 * p + 1] += nt(d1, vb)
                dvn_blocks.append(jnp.where(lo, tn(c["tmats"][2 * p], dgb16[:, blk]), tn(c["tmats"][2 * p + 1], dgb16[:, blk])))
            dvn = jnp.concatenate(dvn_blocks, axis=1)
            yv = c["v"] * c["rv"]
            dsg_ref[...] += jnp.sum(dvn * yv, axis=0, keepdims=True)
            dyv = dvn * prm["sgu_g"][...]
            dv = c["rv"] * (dyv - yv * group_mean_b(dyv * yv, gs, gst))
            dzu = du * _gelu_grad(c["zu"])
            dzv = dv * _gelu_grad(c["zv"])

            dmb = dmx[:, SW:]
            yb = c["outb"] * c["rb"]
            dnb_ref[...] += jnp.sum(dmb * yb, axis=0, keepdims=True)
            doutb = rms_bwd_full(dmb * prm["norm_b"][...], yb, c["rb"])
            lane1 = lax.broadcasted_iota(jnp.int32, (1, LANES), 1)
            dqn_blocks = [None] * (AW // LANES)
            dkn_blocks = [None] * (KVW // LANES)
            dvc_blocks = [None] * (KVW // LANES)
            dsink_vec = jnp.zeros((1, LANES), F32)
            add = lambda old, new: new if old is None else old + new
            for hq, hd in enumerate(c["heads"]):
                mb, e, ek, kb, half = hd["mb"], hd["e"], hd["ek"], hd["kb"], hd["half"]
                dr = doutb[:, LANES * mb:LANES * (mb + 1)]
                if e != ek:
                    dr = pltpu.roll(dr, HEAD_DIM, 1)
                drm = jnp.where(half, dr, 0.0).astype(BF16)
                dp = lax.dot_general(drm, hd["vblk"], (((1,), (1,)), ((), ())), preferred_element_type=F32)
                dvc_blocks[kb] = add(dvc_blocks[kb], lax.dot_general(hd["prb"], drm, (((0,), (0,)), ((), ())),
                                                                     preferred_element_type=F32))
                rowdot = jnp.sum(hd["pr"] * dp, axis=-1, keepdims=True)
                ds = hd["pr"] * (dp - rowdot)
                dsink = jnp.sum(-hd["psink"] * rowdot, axis=0, keepdims=True)
                dsink_vec = dsink_vec + jnp.where(lane1 == hq, dsink, 0.0)
                dbias_ref[hq] += ds
                dsb = (ds * c["scale"]).astype(BF16)
                dqm = jnp.dot(dsb, hd["kblk"], preferred_element_type=F32)
                dqm = jnp.where(half, dqm, 0.0)
                if e != ek:
                    dqm = pltpu.roll(dqm, HEAD_DIM, 1)
                dqn_blocks[mb] = add(dqn_blocks[mb], dqm)
                dkn_blocks[kb] = add(dkn_blocks[kb], lax.dot_general(dsb, hd["qm"], (((0,), (0,)), ((), ())),
                                                                     preferred_element_type=F32))
            dsk_ref[...] += dsink_vec
            dqn = jnp.concatenate(dqn_blocks, axis=1)
            dkn = jnp.concatenate(dkn_blocks, axis=1)
            dvc = jnp.concatenate(dvc_blocks, axis=1)
            yq = c["q"] * c["rq"]
            accq[...] += jnp.sum(dqn * yq, axis=0, keepdims=True)
            dyq = dqn * prm["q_g"][...]
            dq = c["rq"] * (dyq - yq * group_mean_b(dyq * yq, gs, gst))
            yk = c["kcat"] * c["rk"]
            acck[...] += jnp.sum(dkn * yk, axis=0, keepdims=True)
            dyk = dkn * prm["k_g"][...]
            dk = c["rk"] * (dyk - yk * group_mean_b(dyk * yk, prm["gk"][...], prm["gkt"][...]))

            slot = n % 2
            hold[slot, :, :SW] = dzu
            hold[slot, :, SW:2 * SW] = dzv
            hold[slot, :, 2 * SW:] = dq
            tmpkv[:, :KVW] = dk[:BLOCK]
            tmpkv[:, KVW:] = dvc[:BLOCK]
            newkv[:, :KVW] = dk[BLOCK:]
            newkv[:, KVW:] = dvc[BLOCK:]

        @pl.when(n >= 1)
        def _():
            dz_ref[:, :QW] = hold[(n - 1) % 2].astype(dz_ref.dtype)

        @pl.when((n >= 1) & (n < nb))
        def _():
            dz_ref[:, QW:] = (carry[...] + tmpkv[...]).astype(dz_ref.dtype)

        @pl.when(n == nb)
        def _():
            dz_ref[:, QW:] = carry[...].astype(dz_ref.dtype)
            row = lax.broadcasted_iota(jnp.int32, (BLOCK, BLOCK), 0)
            col = lax.broadcasted_iota(jnp.int32, (BLOCK, BLOCK), 1)
            for h in range(H):
                dt_ref[h] = jnp.where(row >= col, dt_ref[h], 0.0)
            dsb_ref[...] = _dot3(accb[...], prm["gs"][...])
            dqg_ref[...] = _dot3(accq[...], gmq_ref[...])
            dkg_ref[...] = _dot3(acck[...], gmk_ref[...])

        @pl.when(n < nb)
        def _():
            carry[...] = newkv[...]

    kvblk = dm.koff // (2 * KVW)
    clamp = lambda n: jnp.minimum(n, nb - 1)
    full = lambda shape: pl.BlockSpec(shape, lambda n: (0,) * len(shape))
    in_specs = [pl.BlockSpec((BLOCK, IN), lambda n: (clamp(n), 0)),
                pl.BlockSpec((BLOCK, 2 * KVW), lambda n: (jnp.maximum(clamp(n) - 1, 0), kvblk)),
                pl.BlockSpec((BLOCK, SW + AW), lambda n: (clamp(n), 0)),
                full((NQ, BLOCK, 2 * BLOCK))] + _mixer_param_specs(dm, None) + [full((AW, LANES)), full((KVW, LANES))]
    out_shape = (jax.ShapeDtypeStruct((dm.S, IN), BF16),
                 jax.ShapeDtypeStruct((1, SW), F32), jax.ShapeDtypeStruct((H, BLOCK, BLOCK), F32),
                 jax.ShapeDtypeStruct((BLOCK, LANES), F32), jax.ShapeDtypeStruct((1, SW), F32),
                 jax.ShapeDtypeStruct((1, LANES), F32), jax.ShapeDtypeStruct((1, LANES), F32),
                 jax.ShapeDtypeStruct((1, AW), F32), jax.ShapeDtypeStruct((1, LANES), F32),
                 jax.ShapeDtypeStruct((NQ, BLOCK, 2 * BLOCK), F32))
    out_specs = (pl.BlockSpec((BLOCK, IN), lambda n: (jnp.maximum(n - 1, 0), 0)),
                 full((1, SW)), full((H, BLOCK, BLOCK)), full((BLOCK, LANES)), full((1, SW)), full((1, LANES)),
                 full((1, LANES)), full((1, AW)), full((1, LANES)), full((NQ, BLOCK, 2 * BLOCK)))
    scratch = [pltpu.VMEM((2, BLOCK, QW), F32), pltpu.VMEM((BLOCK, 2 * KVW), F32), pltpu.VMEM((BLOCK, 2 * KVW), F32),
               pltpu.VMEM((BLOCK, 2 * KVW), F32), pltpu.VMEM((BLOCK, SW), F32), pltpu.VMEM((1, AW), F32),
               pltpu.VMEM((1, KVW), F32)]
    return pl.pallas_call(
        body, out_shape=out_shape, grid=(nb + 1,), in_specs=in_specs, out_specs=out_specs, scratch_shapes=scratch,
        compiler_params=_cparams(("arbitrary",)), name=name,
    )(z, z, dmixed, dbias_in, *params, gmod_q, gmod_k)


def _adamw(w, gparts, m, v, *, name, tr=256):
    R, C = w.shape
    tr = _tile(R, max(8, min(tr, (1 << 18) // C)), 8)
    ng = len(gparts)
    bc1 = 1.0 - ADAM_B1 ** ADAM_STEP
    bc2 = 1.0 - ADAM_B2 ** ADAM_STEP

    def body(w_ref, *rest):
        g_refs, (m_ref, v_ref, go_ref, d_ref, mo_ref, vo_ref) = rest[:ng], rest[ng:]
        g = g_refs[0][...].astype(F32)
        for r in g_refs[1:]:
            g = g + r[...].astype(F32)
        mn = ADAM_B1 * m_ref[...] + (1.0 - ADAM_B1) * g
        vn = ADAM_B2 * v_ref[...] + (1.0 - ADAM_B2) * jnp.square(g)
        m_hat = mn / bc1
        v_hat = vn / bc2
        go_ref[...] = g
        d_ref[...] = -ADAM_LR * (m_hat / (jnp.sqrt(v_hat) + ADAM_EPS) + ADAM_WD * w_ref[...])
        mo_ref[...] = mn
        vo_ref[...] = vn

    blk = pl.BlockSpec((tr, C), lambda i: (i, 0))
    out = jax.ShapeDtypeStruct((R, C), F32)
    return pl.pallas_call(
        body, out_shape=(out, out, out, out), grid=(R // tr,),
        in_specs=[blk] * (3 + ng), out_specs=(blk, blk, blk, blk),
        compiler_params=_cparams(("parallel",)), name=name,
    )(w, *gparts, m, v)


def _sum_parts(parts, *, name, tr=256):
    R, C = parts[0].shape
    tr = _tile(R, max(8, min(tr, (1 << 18) // C)), 8)

    def body(*refs):
        acc = refs[0][...].astype(F32)
        for r in refs[1:-1]:
            acc = acc + r[...].astype(F32)
        refs[-1][...] = acc

    blk = pl.BlockSpec((tr, C), lambda i: (i, 0))
    return pl.pallas_call(
        body, out_shape=jax.ShapeDtypeStruct((R, C), F32), grid=(R // tr,),
        in_specs=[blk] * len(parts), out_specs=blk, compiler_params=_cparams(("parallel",)), name=name,
    )(*parts)


def _mesh_pos():
    return lax.axis_index("x"), lax.axis_index("y"), lax.axis_index("c")


def _other_chips(x, y):
    return [(1 - x, y), (x, 1 - y), (1 - x, 1 - y)]


def _shard_slab(ref, axis, base, width, j, half=None):
    start = pl.multiple_of(base + j * width, LANES)
    if axis == 2:
        rows = ref.shape[1]
        r = pl.ds(0, rows) if half is None else pl.ds(pl.multiple_of(half * (rows // 2), 16), rows // 2)
        return ref.at[:, r, pl.ds(start, width)]
    cols = ref.shape[2]
    if half is None:
        return ref.at[:, pl.ds(start, width), :]
    return ref.at[:, pl.ds(pl.multiple_of(start + half * (width // 2), 16), width // 2), :]


def _gather_weights(shards, streams, out_shapes, *, name):
    ns = len(shards)
    no = len(out_shapes)

    def body(*refs):
        srcs, outs = refs[:ns], refs[ns:ns + no]
        send, recv, fsend, frecv, local = refs[ns + no:]
        x, y, c = _mesh_pos()
        me_j = 2 * x + y
        sib = (x, y, 1 - c)
        local_copies, sends, arrivals, forwards, fwd_arrivals = [], [], [], [], []
        for s, (src, (oi, axis, base)) in enumerate(zip(srcs, streams)):
            out = outs[oi]
            width = src.shape[axis]
            whole = pltpu.make_async_copy(src, _shard_slab(out, axis, base, width, me_j), local.at[s])
            whole.start()
            local_copies.append(whole)
            half_rows = src.shape[1] // 2
            mine = src.at[:, pl.ds(pl.multiple_of(c * half_rows, 16), half_rows), :]
            for k, (px, py) in enumerate(_other_chips(x, y)):
                cp = pltpu.make_async_remote_copy(
                    src_ref=mine, dst_ref=_shard_slab(out, axis, base, width, me_j, half=c),
                    send_sem=send.at[s, k], recv_sem=recv.at[s, k], device_id=(px, py, c), device_id_type=MESH)
                cp.start()
                sends.append(cp)
                pj = 2 * px + py
                got = _shard_slab(out, axis, base, width, pj, half=c)
                arrivals.append(pltpu.make_async_remote_copy(
                    src_ref=got, dst_ref=got, send_sem=send.at[s, k], recv_sem=recv.at[s, k],
                    device_id=(px, py, c), device_id_type=MESH))
                forwards.append(pltpu.make_async_remote_copy(
                    src_ref=got, dst_ref=got, send_sem=fsend.at[s, k], recv_sem=frecv.at[s, k],
                    device_id=sib, device_id_type=MESH))
                theirs = _shard_slab(out, axis, base, width, pj, half=1 - c)
                fwd_arrivals.append(pltpu.make_async_remote_copy(
                    src_ref=theirs, dst_ref=theirs, send_sem=fsend.at[s, k], recv_sem=frecv.at[s, k],
                    device_id=sib, device_id_type=MESH))
        for a, f in zip(arrivals, forwards):
            a.wait_recv()
            f.start()
        for a in fwd_arrivals:
            a.wait_recv()
        for cp in sends + forwards:
            cp.wait_send()
        for cp in local_copies:
            cp.wait()

    return pl.pallas_call(
        body, out_shape=tuple(jax.ShapeDtypeStruct(s, BF16) for s in out_shapes),
        in_specs=[ANY] * ns, out_specs=tuple([ANY] * no),
        scratch_shapes=[pltpu.SemaphoreType.DMA((ns, 3)), pltpu.SemaphoreType.DMA((ns, 3)),
                        pltpu.SemaphoreType.DMA((ns, 3)), pltpu.SemaphoreType.DMA((ns, 3)),
                        pltpu.SemaphoreType.DMA((ns,))],
        name=name,
    )(*shards)


def _scatter_grads(grads, streams, *, name):
    ng = len(grads)
    nst = len(streams)

    def shard_shape(g, axis, width):
        return (g.shape[0], width, g.shape[2]) if axis == 1 else (g.shape[0], g.shape[1], width)

    out_shapes = [(3,) + shard_shape(grads[gi], axis, width) for gi, axis, base, width in streams]

    def body(*refs):
        srcs, outs = refs[:ng], refs[ng:ng + nst]
        send, recv = refs[ng + nst:]
        x, y, c = _mesh_pos()
        copies = []
        for s, (gi, axis, base, width) in enumerate(streams):
            for k, (px, py) in enumerate(_other_chips(x, y)):
                cp = pltpu.make_async_remote_copy(
                    src_ref=_shard_slab(srcs[gi], axis, base, width, 2 * px + py), dst_ref=outs[s].at[k],
                    send_sem=send.at[s, k], recv_sem=recv.at[s, k], device_id=(px, py, c), device_id_type=MESH)
                cp.start()
                copies.append(cp)
        for cp in copies:
            cp.wait()

    return pl.pallas_call(
        body, out_shape=tuple(jax.ShapeDtypeStruct(s, BF16) for s in out_shapes),
        in_specs=[ANY] * ng, out_specs=tuple([ANY] * nst),
        scratch_shapes=[pltpu.SemaphoreType.DMA((nst, 3)), pltpu.SemaphoreType.DMA((nst, 3))],
        name=name,
    )(*grads)


def _sibling_swap(arrays, *, name):
    na = len(arrays)

    def body(*refs):
        srcs, outs = refs[:na], refs[na:2 * na]
        send, recv = refs[2 * na:]
        x, y, c = _mesh_pos()
        copies = []
        for s in range(na):
            cp = pltpu.make_async_remote_copy(src_ref=srcs[s], dst_ref=outs[s], send_sem=send.at[s], recv_sem=recv.at[s],
                                              device_id=(x, y, 1 - c), device_id_type=MESH)
            cp.start()
            copies.append(cp)
        for cp in copies:
            cp.wait()

    return pl.pallas_call(
        body, out_shape=tuple(jax.ShapeDtypeStruct(a.shape, a.dtype) for a in arrays),
        in_specs=[ANY] * na, out_specs=tuple([ANY] * na),
        scratch_shapes=[pltpu.SemaphoreType.DMA((na,)), pltpu.SemaphoreType.DMA((na,))],
        name=name,
    )(*arrays)


def _all_gather_rows(part, *, name):
    R, C = part.shape

    def body(x_ref, out_ref, send, recv, local):
        x, y, c = _mesh_pos()
        sib = (x, y, 1 - c)
        chips = _other_chips(x, y)
        slot = lambda px, py, pc: out_ref.at[4 * px + 2 * py + pc]

        def copy(k, block, to, src=None):
            return pltpu.make_async_remote_copy(
                src_ref=slot(*block) if src is None else src, dst_ref=slot(*block),
                send_sem=send.at[k], recv_sem=recv.at[k], device_id=to, device_id_type=MESH)

        mine = pltpu.make_async_copy(x_ref, slot(x, y, c), local)
        mine.start()
        first = [copy(0, (x, y, c), sib, src=x_ref)]
        first += [copy(1 + j, (x, y, c), (*chip, c), src=x_ref) for j, chip in enumerate(chips)]
        for cp in first:
            cp.start()
        passed = [copy(4 + j, (*chip, c), sib) for j, chip in enumerate(chips)]
        for j, chip in enumerate(chips):
            copy(1 + j, (*chip, c), (x, y, c)).wait_recv()
            passed[j].start()
        copy(0, (x, y, 1 - c), (x, y, c)).wait_recv()
        for j, chip in enumerate(chips):
            copy(4 + j, (*chip, 1 - c), (x, y, c)).wait_recv()
        for cp in first + passed:
            cp.wait_send()
        mine.wait()

    return pl.pallas_call(
        body, out_shape=jax.ShapeDtypeStruct((N_DEV, R, C), part.dtype), in_specs=[ANY], out_specs=ANY,
        scratch_shapes=[pltpu.SemaphoreType.DMA((7,)), pltpu.SemaphoreType.DMA((7,)), pltpu.SemaphoreType.DMA],
        name=name,
    )(part)


def _pack(arrays):
    parts = []
    for a in arrays:
        flat = a.reshape(-1).astype(F32)
        pad = (-flat.shape[0]) % (8 * LANES)
        parts.append(jnp.pad(flat, (0, pad)))
    return jnp.concatenate(parts).reshape(-1, LANES)


def _unpack(packed, like):
    flat = packed.reshape(-1)
    out, off = [], 0
    for a in like:
        n = int(np.prod(a.shape))
        out.append(flat[off:off + n].reshape(a.shape))
        off += n + ((-n) % (8 * LANES))
    return out


def kernel(x, rel_bias, norm1_g, w_in, sgu_norm_g, sgu_w, sgu_b, q_norm_g, k_norm_g, sinks, out_norm_a, out_norm_b, w_out, norm2_g, w_gate, w_up, w_down, loss_target, m_rel_bias, m_norm1_g, m_w_in, m_sgu_norm_g, m_sgu_w, m_sgu_b, m_q_norm_g, m_k_norm_g, m_sinks, m_out_norm_a, m_out_norm_b, m_w_out, m_norm2_g, m_w_gate, m_w_up, m_w_down, v_rel_bias, v_norm1_g, v_w_in, v_sgu_norm_g, v_sgu_w, v_sgu_b, v_q_norm_g, v_k_norm_g, v_sinks, v_out_norm_a, v_out_norm_b, v_w_out, v_norm2_g, v_w_gate, v_w_up, v_w_down):
    L, D, n_in = w_in.shape
    S = x.shape[1]
    IN = N_CHIPS * n_in
    n_ff = w_gate.shape[2]
    FF = N_CHIPS * n_ff
    H = sgu_w.shape[1]
    NQ = sinks.shape[1]
    SW, AW = H * HEAD_DIM, NQ * HEAD_DIM
    KVW = (IN - 2 * SW - AW) // 2
    dm = _MixerDims(S, IN, SW, AW, KVW, H, NQ)
    assert sgu_w.shape[2] == BLOCK and q_norm_g.shape[1] == HEAD_DIM and SW + AW == D

    to_bf = lambda w: w.astype(BF16)
    W_in, W_out, W_gu, W_d = _gather_weights(
        [to_bf(w_in), to_bf(w_out), to_bf(w_gate), to_bf(w_up), to_bf(w_down)],
        [(0, 2, 0), (1, 1, 0), (2, 2, 0), (2, 2, FF), (3, 1, 0)],
        [(L, D, IN), (L, D, D), (L, D, 2 * FF), (L, FF, D)], name="gather_weights")

    gs, gst, gmod_q = _group_consts(SW)
    gk, gkt, gmod_k = _group_consts(KVW)
    onehot = _bucket_onehot()
    rbt = jnp.pad(rel_bias.T, ((0, 0), (0, LANES - NUM_BUCKETS)))
    bias = _matmul(rbt, onehot.T, mode="nn", out_dtype=F32, exact=True, tn=4096, name="bias_table")
    bias = bias.reshape(NQ, BLOCK, 2 * BLOCK)

    def mixer_params(l):
        return [sgu_norm_g[l].reshape(1, SW), sgu_w[l], jnp.repeat(sgu_b[l].T, HEAD_DIM, axis=1),
                out_norm_a[l].reshape(1, SW), jnp.tile(q_norm_g[l], NQ).reshape(1, AW),
                jnp.tile(k_norm_g[l], dm.NKV).reshape(1, KVW), out_norm_b[l].reshape(1, AW), sinks[l], bias,
                gs, gst, gk, gkt]

    xs = x.reshape(S, D)
    saved = []
    for l in range(L):
        h = _rms_fwd(xs, norm1_g[l].reshape(1, D), name="norm1_fwd")
        z = _matmul(h, W_in, layer=l, mode="nn", out_dtype=F32, tn=1792, name="in_proj")
        mixed = _mixer_fwd(dm, z, mixer_params(l), name="mixer_fwd")
        x1 = _matmul(mixed, W_out, layer=l, mode="nn", out_dtype=F32, res=xs, name="out_proj")
        h2 = _rms_fwd(x1, norm2_g[l].reshape(1, D), name="norm2_fwd")
        ab = _matmul(h2, W_gu, layer=l, mode="nn", out_dtype=F32, name="gate_up_proj")
        f = _swiglu_fwd(ab, name="swiglu_fwd")
        x2 = _matmul(f, W_d, layer=l, mode="nn", out_dtype=F32, res=x1, tk=2816, name="down_proj")
        saved.append((xs, h, z, mixed, x1, h2, ab, f))
        xs = x2

    dx, loss_part = _loss_head(xs, loss_target.reshape(S, D), name="loss_head")
    loss = lax.psum(loss_part[0, 0], ("x", "y", "c"))

    dbias = jnp.zeros((NQ, BLOCK, 2 * BLOCK), F32)
    g_in, g_out, g_gu, g_d = [None] * L, [None] * L, [None] * L, [None] * L
    small = {k: [None] * L for k in ("norm1_g", "sgu_norm_g", "sgu_w", "sgu_b", "q_norm_g", "k_norm_g", "sinks",
                                     "out_norm_a", "out_norm_b", "norm2_g")}
    for l in reversed(range(L)):
        xl, h, z, mixed, x1, h2, ab, f = saved[l]
        dxb = dx.astype(BF16)
        df = _matmul(dxb, W_d, layer=l, mode="nt", out_dtype=F32, tn=512, name="down_proj_dx")
        g_d[l] = _matmul_tn(f, dxb, out_dtype=BF16, tm=1408, tn=2048, name="down_proj_dw")
        dab = _swiglu_bwd(ab, df, name="swiglu_bwd")
        g_gu[l] = _matmul_tn(h2, dab, out_dtype=BF16, name="gate_up_proj_dw")
        dh2 = _matmul(dab, W_gu, layer=l, mode="nt", out_dtype=F32, tn=2048, tk=1024, name="gate_up_proj_dx")
        dx1, dg2 = _rms_bwd(x1, norm2_g[l].reshape(1, D), dh2, dx, name="norm2_bwd")
        dx1b = dx1.astype(BF16)
        dmixed = _matmul(dx1b, W_out, layer=l, mode="nt", out_dtype=F32, name="out_proj_dx")
        g_out[l] = _matmul_tn(mixed, dx1b, out_dtype=BF16, name="out_proj_dw")
        (dz, d_sg, d_t, d_sb, d_na, d_qg, d_kg, d_nb, d_sk, dbias) = _mixer_bwd(
            dm, z, dmixed, dbias, mixer_params(l), gmod_q, gmod_k, name="mixer_bwd")
        g_in[l] = _matmul_tn(h, dz, out_dtype=BF16, tn=896, name="in_proj_dw")
        dh = _matmul(dz, W_in, layer=l, mode="nt", out_dtype=F32, tk=1792, name="in_proj_dx")
        dx, dg1 = _rms_bwd(xl, norm1_g[l].reshape(1, D), dh, dx1, name="norm1_bwd")
        small["norm1_g"][l] = dg1.reshape(D)
        small["norm2_g"][l] = dg2.reshape(D)
        small["sgu_norm_g"][l] = d_sg.reshape(H, HEAD_DIM)
        small["sgu_w"][l] = d_t
        small["sgu_b"][l] = d_sb[:, :H].T
        small["q_norm_g"][l] = d_qg[0, :HEAD_DIM]
        small["k_norm_g"][l] = d_kg[0, :HEAD_DIM]
        small["sinks"][l] = d_sk[0, :NQ]
        small["out_norm_a"][l] = d_na.reshape(SW)
        small["out_norm_b"][l] = d_nb.reshape(AW)
    grad_x = dx.reshape(1, S, D)
    d_rb = _matmul(dbias.reshape(NQ, BLOCK * 2 * BLOCK), onehot, mode="nn", out_dtype=F32, exact=True, tk=4096,
                   name="rel_bias_grad")
    d_rel_bias = d_rb[:, :NUM_BUCKETS].T

    stack = lambda parts: jnp.stack(parts)
    G_in, G_out, G_gu, G_d = stack(g_in), stack(g_out), stack(g_gu), stack(g_d)
    streams = [(0, 2, 0, n_in), (1, 1, 0, D // N_CHIPS), (2, 2, 0, n_ff), (2, 2, FF, n_ff), (3, 1, 0, n_ff)]
    recvd = _scatter_grads([G_in, G_out, G_gu, G_d], streams, name="scatter_grads")
    jx, jy = lax.axis_index("x"), lax.axis_index("y")
    me_j = 2 * jx + jy
    full = [G_in, G_out, G_gu, G_d]
    sums = []
    for (gi, axis, base, width), got in zip(streams, recvd):
        own = lax.dynamic_slice_in_dim(full[gi], base + me_j * width, width, axis=axis)
        two = lambda a: a.reshape(-1, a.shape[-1])
        sums.append(_sum_parts([two(own), two(got[0]), two(got[1]), two(got[2])], name="sum_chip_partials"))
    theirs = _sibling_swap(sums, name="sibling_swap")
    big = {}
    for nm, w, m, v, mine, sib in zip(("w_in", "w_out", "w_gate", "w_up", "w_down"), (w_in, w_out, w_gate, w_up, w_down),
                                      (m_w_in, m_w_out, m_w_gate, m_w_up, m_w_down),
                                      (v_w_in, v_w_out, v_w_gate, v_w_up, v_w_down), sums, theirs):
        two = lambda a: a.reshape(-1, a.shape[-1])
        res = _adamw(two(w), [mine, sib], two(m), two(v), name="adamw_" + nm)
        big[nm] = [r.reshape(w.shape) for r in res]

    names_small = ("rel_bias", "norm1_g", "sgu_norm_g", "sgu_w", "sgu_b", "q_norm_g", "k_norm_g", "sinks", "out_norm_a",
                   "out_norm_b", "norm2_g")
    w_small = dict(rel_bias=rel_bias, norm1_g=norm1_g, sgu_norm_g=sgu_norm_g, sgu_w=sgu_w, sgu_b=sgu_b, q_norm_g=q_norm_g,
                   k_norm_g=k_norm_g, sinks=sinks, out_norm_a=out_norm_a, out_norm_b=out_norm_b, norm2_g=norm2_g)
    m_small = dict(rel_bias=m_rel_bias, norm1_g=m_norm1_g, sgu_norm_g=m_sgu_norm_g, sgu_w=m_sgu_w, sgu_b=m_sgu_b,
                   q_norm_g=m_q_norm_g, k_norm_g=m_k_norm_g, sinks=m_sinks, out_norm_a=m_out_norm_a,
                   out_norm_b=m_out_norm_b, norm2_g=m_norm2_g)
    v_small = dict(rel_bias=v_rel_bias, norm1_g=v_norm1_g, sgu_norm_g=v_sgu_norm_g, sgu_w=v_sgu_w, sgu_b=v_sgu_b,
                   q_norm_g=v_q_norm_g, k_norm_g=v_k_norm_g, sinks=v_sinks, out_norm_a=v_out_norm_a,
                   out_norm_b=v_out_norm_b, norm2_g=v_norm2_g)
    g_small = {k: jnp.stack(vs) for k, vs in small.items()}
    g_small["rel_bias"] = d_rel_bias
    like = [w_small[k] for k in names_small]
    gathered = _all_gather_rows(_pack([g_small[k] for k in names_small]), name="gather_small_grads")
    res = _adamw(_pack(like), [gathered[i] for i in range(N_DEV)], _pack([m_small[k] for k in names_small]),
                 _pack([v_small[k] for k in names_small]), name="adamw_small")
    sm = {k: vals for k, vals in zip(names_small, zip(*[_unpack(r, like) for r in res]))}

    order = ("rel_bias", "norm1_g", "w_in", "sgu_norm_g", "sgu_w", "sgu_b", "q_norm_g", "k_norm_g", "sinks", "out_norm_a",
             "out_norm_b", "w_out", "norm2_g", "w_gate", "w_up", "w_down")
    pick = lambda k, i: big[k][i] if k in big else sm[k][i]
    outs = [loss, grad_x]
    for i in range(4):
        outs += [pick(k, i) for k in order]
    return tuple(outs)
```

```python
import functools
import math

import numpy as np

import jax
import jax.numpy as jnp
from jax import lax
from jax.experimental import pallas as pl
from jax.experimental.pallas import tpu as pltpu

F32 = jnp.float32
BF16 = jnp.bfloat16
MESH = pl.DeviceIdType.MESH
ANY = pl.BlockSpec(memory_space=pl.ANY)

HEAD_DIM = 64
BLOCK = 128
NUM_BUCKETS = 32
MAX_DISTANCE = 128
EPS = 1e-6
NEG_INF = -1e30
ADAM_LR, ADAM_B1, ADAM_B2, ADAM_EPS, ADAM_WD, ADAM_STEP = 0.001, 0.9, 0.999, 1e-08, 0.01, 10

LANES = 128
VMEM_LIMIT = 56 * 1024 * 1024
N_CHIPS = 4
N_DEV = 8


def _cparams(sem=None):
    return pltpu.CompilerParams(dimension_semantics=sem, vmem_limit_bytes=VMEM_LIMIT)


def _tile(dim, target, align=LANES):
    best = None
    for t in range(align, min(dim, target) + 1, align):
        if dim % t == 0:
            best = t
    return best if best is not None else dim


def _mesh_pos():
    return lax.axis_index("x"), lax.axis_index("y"), lax.axis_index("c")


def _other_chips(x, y):
    return [(1 - x, y), (x, 1 - y), (1 - x, 1 - y)]


def _slab(ref, axis, start, size, half=None):
    if axis == 1:
        rows = ref.shape[0]
        r = pl.ds(0, rows) if half is None else pl.ds(pl.multiple_of(half * (rows // 2), 16), rows // 2)
        return ref.at[r, pl.ds(pl.multiple_of(start, LANES), size)]
    if half is None:
        return ref.at[pl.ds(pl.multiple_of(start, 16), size), :]
    return ref.at[pl.ds(pl.multiple_of(start + half * (size // 2), 16), size // 2), :]


def _remote(src, dst, send_sem, recv_sem, to):
    return pltpu.make_async_remote_copy(src_ref=src, dst_ref=dst, send_sem=send_sem, recv_sem=recv_sem,
                                        device_id=to, device_id_type=MESH)


class _Gather:
    def __init__(self, shards, streams, out_shapes):
        self.ins, self.streams = list(shards), streams
        self.outs = [jax.ShapeDtypeStruct(s, BF16) for s in out_shapes]
        ns = len(streams)
        self.sems = [pltpu.SemaphoreType.DMA((ns, 3))] * 4 + [pltpu.SemaphoreType.DMA((ns,))]

    def _copies(self, srcs, outs, sems):
        send, recv, fsend, frecv, local = sems
        x, y, c = _mesh_pos()
        me_j = 2 * x + y
        sib = (x, y, 1 - c)
        own, sends, arrivals, forwards, fwd_arrivals = [], [], [], [], []
        for s, (si, oi, axis, base) in enumerate(self.streams):
            src, out = srcs[si], outs[oi]
            width = src.shape[axis]
            own.append(pltpu.make_async_copy(src, _slab(out, axis, base + me_j * width, width), local.at[s]))
            half_rows = src.shape[0] // 2
            mine = src.at[pl.ds(pl.multiple_of(c * half_rows, 16), half_rows), :]
            for k, (px, py) in enumerate(_other_chips(x, y)):
                sends.append(_remote(mine, _slab(out, axis, base + me_j * width, width, half=c),
                                     send.at[s, k], recv.at[s, k], (px, py, c)))
                start = base + (2 * px + py) * width
                got = _slab(out, axis, start, width, half=c)
                arrivals.append(_remote(got, got, send.at[s, k], recv.at[s, k], (px, py, c)))
                forwards.append(_remote(got, got, fsend.at[s, k], frecv.at[s, k], sib))
                theirs = _slab(out, axis, start, width, half=1 - c)
                fwd_arrivals.append(_remote(theirs, theirs, fsend.at[s, k], frecv.at[s, k], sib))
        return own, sends, arrivals, forwards, fwd_arrivals

    def start(self, srcs, outs, sems):
        own, sends, _, _, _ = self._copies(srcs, outs, sems)
        for cp in own + sends:
            cp.start()

    def finish(self, srcs, outs, sems):
        own, sends, arrivals, forwards, fwd_arrivals = self._copies(srcs, outs, sems)
        for a, f in zip(arrivals, forwards):
            a.wait_recv()
            f.start()
        for a in fwd_arrivals:
            a.wait_recv()
        for cp in sends + forwards:
            cp.wait_send()
        for cp in own:
            cp.wait()


class _Scatter:
    def __init__(self, grads, streams):
        self.ins, self.streams = list(grads), streams
        shard = lambda g, axis, width: (width, g.shape[1]) if axis == 0 else (g.shape[0], width)
        self.outs = [jax.ShapeDtypeStruct((3,) + shard(grads[gi], axis, width), BF16) for gi, axis, base, width in streams]
        self.sems = [pltpu.SemaphoreType.DMA((len(streams), 3))] * 2

    def _copies(self, srcs, outs, sems):
        send, recv = sems
        x, y, c = _mesh_pos()
        copies = []
        for s, (gi, axis, base, width) in enumerate(self.streams):
            for k, (px, py) in enumerate(_other_chips(x, y)):
                copies.append(_remote(_slab(srcs[gi], axis, base + (2 * px + py) * width, width), outs[s].at[k],
                                      send.at[s, k], recv.at[s, k], (px, py, c)))
        return copies

    def start(self, srcs, outs, sems):
        for cp in self._copies(srcs, outs, sems):
            cp.start()

    def finish(self, srcs, outs, sems):
        for cp in self._copies(srcs, outs, sems):
            cp.wait()


def _call(body, *, grid, in_specs, out_specs, out_shape, args, name, scratch=(), sem=None, comm=()):
    out_shape, out_specs = tuple(out_shape), tuple(out_specs)
    n_in, n_out, n_scr = len(in_specs), len(out_shape), len(scratch)
    c_ins = [a for u in comm for a in u.ins]
    c_outs = [o for u in comm for o in u.outs]
    c_sems = [s for u in comm for s in u.sems]

    def wrapped(*refs):
        ins, rest = refs[:n_in], refs[n_in:]
        cin, rest = rest[:len(c_ins)], rest[len(c_ins):]
        outs, rest = rest[:n_out], rest[n_out:]
        cout, rest = rest[:len(c_outs)], rest[len(c_outs):]
        scr, csem = rest[:n_scr], rest[n_scr:]

        def each(fn_name):
            i = o = s = 0
            for u in comm:
                getattr(u, fn_name)(cin[i:i + len(u.ins)], cout[o:o + len(u.outs)], csem[s:s + len(u.sems)])
                i, o, s = i + len(u.ins), o + len(u.outs), s + len(u.sems)

        if comm:
            pids = [pl.program_id(d) for d in range(len(grid))]
            first = functools.reduce(jnp.logical_and, [p == 0 for p in pids])
            last = functools.reduce(jnp.logical_and, [p == g - 1 for p, g in zip(pids, grid)])
            pl.when(first)(lambda: each("start"))
        body(*ins, *outs, *scr)
        if comm:
            pl.when(last)(lambda: each("finish"))

    if comm:
        sem = ("arbitrary",) * len(grid)
    return pl.pallas_call(
        wrapped, out_shape=out_shape + tuple(c_outs), grid=grid,
        in_specs=list(in_specs) + [ANY] * len(c_ins), out_specs=out_specs + tuple([ANY] * len(c_outs)),
        scratch_shapes=list(scratch) + c_sems, compiler_params=_cparams(sem), name=name,
    )(*args, *c_ins)


def _comm_only(comm, *, name):
    def body(tick_ref):
        tick_ref[...] = jnp.zeros(tick_ref.shape, tick_ref.dtype)

    outs = _call(body, grid=(1,), in_specs=[], out_specs=[pl.BlockSpec((8, LANES), lambda i: (0, 0))],
                 out_shape=[jax.ShapeDtypeStruct((8, LANES), F32)], args=[], name=name, comm=comm)
    return outs[1:]


def _split3(x):
    hi = x.astype(BF16)
    r1 = x - hi.astype(F32)
    mid = r1.astype(BF16)
    lo = (r1 - mid.astype(F32)).astype(BF16)
    return hi, mid, lo


def _dot3(x, g):
    hi, mid, lo = _split3(x)
    d = lambda a: jnp.dot(a, g, preferred_element_type=F32)
    return d(hi) + d(mid) + d(lo)


def _matmul(a, b, *, mode, out_dtype, name, res=None, exact=False, tm=1024, tn=1024, tk=2048, comm=()):
    M, K = a.shape
    N = b.shape[1] if mode == "nn" else b.shape[0]
    tm, tn, tk = _tile(M, tm, 8 if M < LANES else LANES), _tile(N, tn), _tile(K, tk)
    nk = K // tk
    dn = (((1,), (0,)), ((), ())) if mode == "nn" else (((1,), (1,)), ((), ()))

    def body(*refs):
        a_ref, b_ref = refs[0], refs[1]
        r_ref = refs[2] if res is not None else None
        o_ref = refs[3] if res is not None else refs[2]
        if exact:
            part = _dot3(a_ref[...], b_ref[...])
        else:
            part = lax.dot_general(a_ref[...].astype(BF16), b_ref[...].astype(BF16), dn, preferred_element_type=F32)

        def finish(total):
            if r_ref is not None:
                total = r_ref[...] + total
            o_ref[...] = total.astype(o_ref.dtype)

        if nk == 1:
            finish(part)
        else:
            acc = refs[-1]
            k = pl.program_id(2)

            @pl.when(k == 0)
            def _():
                acc[...] = part

            @pl.when(k > 0)
            def _():
                acc[...] += part

            @pl.when(k == nk - 1)
            def _():
                finish(acc[...])

    if mode == "nn":
        b_spec = pl.BlockSpec((tk, tn), lambda j, i, k: (k, j))
    else:
        b_spec = pl.BlockSpec((tn, tk), lambda j, i, k: (j, k))
    in_specs = [pl.BlockSpec((tm, tk), lambda j, i, k: (i, k)), b_spec]
    args = [a, b]
    if res is not None:
        in_specs.append(pl.BlockSpec((tm, tn), lambda j, i, k: (i, j)))
        args.append(res)
    return _call(
        body, grid=(N // tn, M // tm, nk), in_specs=in_specs,
        out_specs=[pl.BlockSpec((tm, tn), lambda j, i, k: (i, j))], out_shape=[jax.ShapeDtypeStruct((M, N), out_dtype)],
        scratch=[pltpu.VMEM((tm, tn), F32)] if nk > 1 else [], sem=("parallel", "parallel", "arbitrary"),
        args=args, name=name, comm=comm)


def _matmul_tn(a, b, *, out_dtype, name, tm=2048, tn=1024, tt=1024, comm=()):
    T, Mo = a.shape
    N = b.shape[1]
    tm, tn, tt = _tile(Mo, tm), _tile(N, tn), _tile(T, tt)
    nt = T // tt

    def body(a_ref, b_ref, o_ref, acc):
        t = pl.program_id(2)
        part = lax.dot_general(a_ref[...].astype(BF16), b_ref[...].astype(BF16), (((0,), (0,)), ((), ())),
                               preferred_element_type=F32)

        @pl.when(t == 0)
        def _():
            acc[...] = part

        @pl.when(t > 0)
        def _():
            acc[...] += part

        @pl.when(t == nt - 1)
        def _():
            o_ref[...] = acc[...].astype(o_ref.dtype)

    return _call(
        body, grid=(Mo // tm, N // tn, nt),
        in_specs=[pl.BlockSpec((tt, tm), lambda i, j, t: (t, i)), pl.BlockSpec((tt, tn), lambda i, j, t: (t, j))],
        out_specs=[pl.BlockSpec((tm, tn), lambda i, j, t: (i, j))], out_shape=[jax.ShapeDtypeStruct((Mo, N), out_dtype)],
        scratch=[pltpu.VMEM((tm, tn), F32)], sem=("parallel", "parallel", "arbitrary"), args=[a, b], name=name, comm=comm)


def _rms_fwd(x, g, *, name, tr=256):
    R, D = x.shape
    tr = _tile(R, tr, 8)

    def body(x_ref, g_ref, o_ref):
        xv = x_ref[...]
        r = lax.rsqrt(jnp.mean(xv * xv, axis=-1, keepdims=True) + EPS)
        o_ref[...] = (xv * r * g_ref[...]).astype(o_ref.dtype)

    return pl.pallas_call(
        body, out_shape=jax.ShapeDtypeStruct((R, D), BF16), grid=(R // tr,),
        in_specs=[pl.BlockSpec((tr, D), lambda i: (i, 0)), pl.BlockSpec((1, D), lambda i: (0, 0))],
        out_specs=pl.BlockSpec((tr, D), lambda i: (i, 0)),
        compiler_params=_cparams(("parallel",)), name=name,
    )(x, g)


def _rms_bwd(x, g, dh, dres, *, name, tr=256):
    R, D = x.shape
    tr = _tile(R, tr, 8)

    def body(x_ref, g_ref, dh_ref, dres_ref, dx_ref, dg_ref):
        i = pl.program_id(0)
        xv = x_ref[...]
        r = lax.rsqrt(jnp.mean(xv * xv, axis=-1, keepdims=True) + EPS)
        y = xv * r
        dhv = dh_ref[...]
        dy = dhv * g_ref[...]
        dx_ref[...] = dres_ref[...] + r * (dy - y * jnp.mean(dy * y, axis=-1, keepdims=True))
        dg = jnp.sum(dhv * y, axis=0, keepdims=True)

        @pl.when(i == 0)
        def _():
            dg_ref[...] = dg

        @pl.when(i > 0)
        def _():
            dg_ref[...] += dg

    row = pl.BlockSpec((tr, D), lambda i: (i, 0))
    vec = pl.BlockSpec((1, D), lambda i: (0, 0))
    return pl.pallas_call(
        body, out_shape=(jax.ShapeDtypeStruct((R, D), F32), jax.ShapeDtypeStruct((1, D), F32)), grid=(R // tr,),
        in_specs=[row, vec, row, row], out_specs=(row, vec),
        compiler_params=_cparams(("arbitrary",)), name=name,
    )(x, g, dh, dres)


def _swiglu_fwd(ab, *, name, tr=128):
    R, F2 = ab.shape
    FF = F2 // 2
    tr = _tile(R, tr, 8)

    def body(a_ref, b_ref, o_ref):
        a = a_ref[...]
        o_ref[...] = (a * jax.nn.sigmoid(a) * b_ref[...]).astype(o_ref.dtype)

    return pl.pallas_call(
        body, out_shape=jax.ShapeDtypeStruct((R, FF), BF16), grid=(R // tr,),
        in_specs=[pl.BlockSpec((tr, FF), lambda i: (i, 0)), pl.BlockSpec((tr, FF), lambda i: (i, 1))],
        out_specs=pl.BlockSpec((tr, FF), lambda i: (i, 0)),
        compiler_params=_cparams(("parallel",)), name=name,
    )(ab, ab)


def _swiglu_bwd(ab, df, *, name, tr=128):
    R, F2 = ab.shape
    FF = F2 // 2
    tr = _tile(R, tr, 8)

    def body(a_ref, b_ref, df_ref, o_ref):
        a = a_ref[...]
        dfv = df_ref[...]
        s = jax.nn.sigmoid(a)
        o_ref[:, :FF] = (dfv * b_ref[...] * (s * (1.0 + a * (1.0 - s)))).astype(o_ref.dtype)
        o_ref[:, FF:] = (dfv * (a * s)).astype(o_ref.dtype)

    return pl.pallas_call(
        body, out_shape=jax.ShapeDtypeStruct((R, F2), BF16), grid=(R // tr,),
        in_specs=[pl.BlockSpec((tr, FF), lambda i: (i, 0)), pl.BlockSpec((tr, FF), lambda i: (i, 1)),
                  pl.BlockSpec((tr, FF), lambda i: (i, 0))],
        out_specs=pl.BlockSpec((tr, F2), lambda i: (i, 0)),
        compiler_params=_cparams(("parallel",)), name=name,
    )(ab, ab, df)


def _loss_head(y, target, *, name, tr=256):
    R, D = y.shape
    tr = _tile(R, tr, 8)

    def body(y_ref, t_ref, dy_ref, l_ref):
        i = pl.program_id(0)
        e = y_ref[...] - t_ref[...]
        dy_ref[...] = e * (1.0 / D)
        part = 0.5 * jnp.sum(jnp.mean(e * e, axis=-1, keepdims=True), axis=0, keepdims=True)
        part = jnp.broadcast_to(part, (8, LANES))

        @pl.when(i == 0)
        def _():
            l_ref[...] = part

        @pl.when(i > 0)
        def _():
            l_ref[...] += part

    row = pl.BlockSpec((tr, D), lambda i: (i, 0))
    return pl.pallas_call(
        body, out_shape=(jax.ShapeDtypeStruct((R, D), F32), jax.ShapeDtypeStruct((8, LANES), F32)), grid=(R // tr,),
        in_specs=[row, row], out_specs=(row, pl.BlockSpec((8, LANES), lambda i: (0, 0))),
        compiler_params=_cparams(("arbitrary",)), name=name,
    )(y, target)


def _gelu(x):
    return 0.5 * x * (1.0 + lax.erf(x * math.sqrt(0.5)))


def _gelu_grad(x):
    return 0.5 * (1.0 + lax.erf(x * math.sqrt(0.5))) + x * jnp.exp(-0.5 * x * x) * (1.0 / math.sqrt(2.0 * math.pi))


def _group_consts(width):
    lane = np.arange(width)
    col = np.arange(LANES)
    grp = (lane[:, None] // HEAD_DIM == col[None, :]).astype(np.float32)
    mod = ((lane[:, None] % HEAD_DIM == col[None, :]) & (col[None, :] < HEAD_DIM)).astype(np.float32)
    return jnp.asarray(grp, BF16), jnp.asarray(grp.T, BF16), jnp.asarray(mod, BF16)


def _bucket_onehot():
    qi = np.arange(BLOCK)[:, None]
    kj = np.arange(2 * BLOCK)[None, :]
    n = np.maximum(qi + BLOCK - kj, 0)
    max_exact = NUM_BUCKETS // 2
    nf = np.maximum(n, 1).astype(np.float32)
    large = max_exact + (np.log(nf / np.float32(max_exact)) / np.float32(math.log(MAX_DISTANCE / max_exact))
                         * np.float32(NUM_BUCKETS - max_exact)).astype(np.int32)
    large = np.minimum(large, NUM_BUCKETS - 1)
    bucket = jnp.asarray(np.where(n < max_exact, n, large).reshape(-1).astype(np.int32))
    return (bucket[:, None] == jnp.arange(LANES, dtype=jnp.int32)[None, :]).astype(BF16)


class _MixerDims:
    def __init__(self, S, IN, SW, AW, KVW, H, NQ):
        self.S, self.IN, self.SW, self.AW, self.KVW, self.H, self.NQ = S, IN, SW, AW, KVW, H, NQ
        self.NKV = KVW // HEAD_DIM
        self.GROUP = NQ // self.NKV
        self.nb = S // BLOCK
        self.koff = 2 * SW + AW
        self.voff = self.koff + KVW
        assert SW % LANES == 0 and AW % LANES == 0 and KVW % LANES == 0 and self.GROUP % 2 == 0
        assert self.koff % (2 * KVW) == 0 and IN == self.voff + KVW and S % BLOCK == 0


def _mixer_block(dm, n, z, kvp, prm):
    SW, AW, KVW, H, NQ = dm.SW, dm.AW, dm.KVW, dm.H, dm.NQ
    lane = lax.broadcasted_iota(jnp.int32, (BLOCK, LANES), 1)
    lo = lane < HEAD_DIM
    row = lax.broadcasted_iota(jnp.int32, (BLOCK, BLOCK), 0)
    col = lax.broadcasted_iota(jnp.int32, (BLOCK, BLOCK), 1)
    tril = row >= col
    gs, gst = prm["gs"][...], prm["gst"][...]
    inv = 1.0 / HEAD_DIM

    def group_rsqrt(x, g, gt):
        ms = _dot3(x * x, g) * inv
        return _dot3(lax.rsqrt(ms + EPS), gt)

    zu, zv = z[:, :SW], z[:, SW:2 * SW]
    u, v = _gelu(zu), _gelu(zv)
    rv = group_rsqrt(v, gs, gst)
    vn = v * rv * prm["sgu_g"][...]
    vnb = vn.astype(BF16)
    tmats, gate_blocks = [], []
    for p in range(H // 2):
        blk = slice(LANES * p, LANES * (p + 1))
        t0 = jnp.where(tril, prm["sgu_w"][2 * p], 0.0).astype(BF16)
        t1 = jnp.where(tril, prm["sgu_w"][2 * p + 1], 0.0).astype(BF16)
        tmats += [t0, t1]
        g0 = jnp.dot(t0, vnb[:, blk], preferred_element_type=F32)
        g1 = jnp.dot(t1, vnb[:, blk], preferred_element_type=F32)
        gate_blocks.append(jnp.where(lo, g0, g1) + prm["sgu_bias"][:, blk])
    gate = jnp.concatenate(gate_blocks, axis=1)
    outa = u * gate
    ra = lax.rsqrt(jnp.mean(outa * outa, axis=-1, keepdims=True) + EPS)

    q = z[:, 2 * SW:2 * SW + AW]
    kcat = jnp.concatenate([kvp[:, :KVW], z[:, dm.koff:dm.koff + KVW]], axis=0)
    vcat = jnp.concatenate([kvp[:, KVW:], z[:, dm.voff:dm.voff + KVW]], axis=0)
    rq = group_rsqrt(q, gs, gst) if AW == SW else None
    assert rq is not None
    rk = group_rsqrt(kcat, prm["gk"][...], prm["gkt"][...])
    qn = q * rq * prm["q_g"][...]
    kn = kcat * rk * prm["k_g"][...]
    knb, vcb = kn.astype(BF16), vcat.astype(BF16)
    qi = lax.broadcasted_iota(jnp.int32, (BLOCK, 2 * BLOCK), 0)
    kj = lax.broadcasted_iota(jnp.int32, (BLOCK, 2 * BLOCK), 1)
    valid = (kj > qi) & (kj <= qi + BLOCK) & ((n > 0) | (kj >= BLOCK))
    scale = 1.0 / math.sqrt(HEAD_DIM)
    heads = []
    out_blocks = []
    for hq in range(NQ):
        mb, e = hq // 2, hq % 2
        kv = hq // dm.GROUP
        kb, ek = kv // 2, kv % 2
        qblk = qn[:, LANES * mb:LANES * (mb + 1)]
        if e != ek:
            qblk = pltpu.roll(qblk, HEAD_DIM, 1)
        half = lo if ek == 0 else jnp.logical_not(lo)
        qm = jnp.where(half, qblk, 0.0).astype(BF16)
        kblk = knb[:, LANES * kb:LANES * (kb + 1)]
        vblk = vcb[:, LANES * kb:LANES * (kb + 1)]
        s = lax.dot_general(qm, kblk, (((1,), (1,)), ((), ())), preferred_element_type=F32) * scale + prm["bias"][hq]
        s = jnp.where(valid, s, NEG_INF)
        sink = prm["sinks"][hq]
        mx = jnp.maximum(jnp.max(s, axis=-1, keepdims=True), sink)
        ex = jnp.exp(s - mx)
        den = jnp.sum(ex, axis=-1, keepdims=True) + jnp.exp(sink - mx)
        pr = ex / den
        psink = jnp.exp(sink - mx) / den
        prb = pr.astype(BF16)
        r_h = jnp.dot(prb, vblk, preferred_element_type=F32)
        if e != ek:
            r_h = pltpu.roll(r_h, HEAD_DIM, 1)
        heads.append(dict(qm=qm, kblk=kblk, vblk=vblk, pr=pr, prb=prb, psink=psink, half=half, mb=mb, e=e, ek=ek, kb=kb))
        if e == 1:
            out_blocks.append(jnp.where(lo, prev_r, r_h))
        prev_r = r_h
    outb = jnp.concatenate(out_blocks, axis=1)
    rb = lax.rsqrt(jnp.mean(outb * outb, axis=-1, keepdims=True) + EPS)
    return dict(lo=lo, tril=tril, gs=gs, gst=gst, zu=zu, zv=zv, u=u, v=v, rv=rv, vnb=vnb, tmats=tmats, gate=gate,
                outa=outa, ra=ra, q=q, kcat=kcat, rq=rq, rk=rk, heads=heads, outb=outb, rb=rb, scale=scale)


_MIXER_PARAMS = ("sgu_g", "sgu_w", "sgu_bias", "norm_a", "q_g", "k_g", "norm_b", "sinks", "bias", "gs", "gst", "gk", "gkt")


def _mixer_param_specs(dm, idx):
    SW, AW, KVW = dm.SW, dm.AW, dm.KVW
    full = lambda shape: pl.BlockSpec(shape, lambda n: (0,) * len(shape))
    return [full((1, SW)), full((dm.H, BLOCK, BLOCK)), full((BLOCK, SW)), full((1, SW)), full((1, AW)), full((1, KVW)),
            full((1, AW)), pl.BlockSpec(memory_space=pltpu.SMEM), full((dm.NQ, BLOCK, 2 * BLOCK)),
            full((SW, LANES)), full((LANES, SW)), full((KVW, LANES)), full((LANES, KVW))]


def _mixer_fwd(dm, z, params, *, name, comm=()):
    nb = dm.nb

    def body(z_ref, kvp_ref, *rest):
        prm = dict(zip(_MIXER_PARAMS, rest[:len(_MIXER_PARAMS)]))
        o_ref = rest[len(_MIXER_PARAMS)]
        n = pl.program_id(0)
        c = _mixer_block(dm, n, z_ref[...], kvp_ref[...], prm)
        o_ref[:, :dm.SW] = (c["outa"] * c["ra"] * prm["norm_a"][...]).astype(o_ref.dtype)
        o_ref[:, dm.SW:] = (c["outb"] * c["rb"] * prm["norm_b"][...]).astype(o_ref.dtype)

    kvblk = dm.koff // (2 * dm.KVW)
    in_specs = [pl.BlockSpec((BLOCK, dm.IN), lambda n: (n, 0)),
                pl.BlockSpec((BLOCK, 2 * dm.KVW), lambda n: (jnp.maximum(n - 1, 0), kvblk))] + _mixer_param_specs(dm, None)
    return _call(
        body, out_shape=[jax.ShapeDtypeStruct((dm.S, dm.SW + dm.AW), BF16)], grid=(nb,),
        in_specs=in_specs, out_specs=[pl.BlockSpec((BLOCK, dm.SW + dm.AW), lambda n: (n, 0))],
        sem=("arbitrary",), args=[z, z, *params], name=name, comm=comm)


def _mixer_bwd(dm, z, dmixed, dbias_in, params, gmod_q, gmod_k, *, name, comm=()):
    SW, AW, KVW, H, NQ, nb, IN = dm.SW, dm.AW, dm.KVW, dm.H, dm.NQ, dm.nb, dm.IN
    QW = 2 * SW + AW
    NP = len(_MIXER_PARAMS)

    def body(z_ref, kvp_ref, dm_ref, dbin_ref, *rest):
        prm = dict(zip(_MIXER_PARAMS, rest[:NP]))
        gmq_ref, gmk_ref = rest[NP], rest[NP + 1]
        (dz_ref, dsg_ref, dt_ref, dsb_ref, dna_ref, dqg_ref, dkg_ref, dnb_ref, dsk_ref, dbias_ref) = rest[NP + 2:NP + 12]
        hold, tmpkv, newkv, carry, accb, accq, acck = rest[NP + 12:]
        n = pl.program_id(0)

        @pl.when(n == 0)
        def _():
            for r in (dsg_ref, dt_ref, dna_ref, dnb_ref, dsk_ref, accb, accq, acck):
                r[...] = jnp.zeros(r.shape, r.dtype)
            dbias_ref[...] = dbin_ref[...]

        @pl.when(n < nb)
        def _():
            c = _mixer_block(dm, n, z_ref[...], kvp_ref[...], prm)
            lo, gs, gst = c["lo"], c["gs"], c["gst"]
            dmx = dm_ref[...]
            inv = 1.0 / HEAD_DIM

            def rms_bwd_full(dy_scaled, y, r):
                return r * (dy_scaled - y * jnp.mean(dy_scaled * y, axis=-1, keepdims=True))

            def group_mean_b(x, g, gt):
                return _dot3(_dot3(x, g) * inv, gt)

            dma = dmx[:, :SW]
            ya = c["outa"] * c["ra"]
            dna_ref[...] += jnp.sum(dma * ya, axis=0, keepdims=True)
            douta = rms_bwd_full(dma * prm["norm_a"][...], ya, c["ra"])
            du = douta * c["gate"]
            dgate = douta * c["u"]
            accb[...] += dgate
            dgb16 = dgate.astype(BF16)
            dvn_blocks = []
            for p in range(H // 2):
                blk = slice(LANES * p, LANES * (p + 1))
                dg = dgate[:, blk]
                d0 = jnp.where(lo, dg, 0.0).astype(BF16)
                d1 = jnp.where(lo, 0.0, dg).astype(BF16)
                vb = c["vnb"][:, blk]
                nt = lambda a, b: lax.dot_general(a, b, (((1,), (1,)), ((), ())), preferred_element_type=F32)
                tn = lambda a, b: lax.dot_general(a, b, (((0,), (0,)), ((), ())), preferred_element_type=F32)
                dt_ref[2 * p] += nt(d0, vb)
                dt_ref[2 * p + 1] += nt(d1, vb)
                dvn_blocks.append(jnp.where(lo, tn(c["tmats"][2 * p], dgb16[:, blk]), tn(c["tmats"][2 * p + 1], dgb16[:, blk])))
            dvn = jnp.concatenate(dvn_blocks, axis=1)
            yv = c["v"] * c["rv"]
            dsg_ref[...] += jnp.sum(dvn * yv, axis=0, keepdims=True)
            dyv = dvn * prm["sgu_g"][...]
            dv = c["rv"] * (dyv - yv * group_mean_b(dyv * yv, gs, gst))
            dzu = du * _gelu_grad(c["zu"])
            dzv = dv * _gelu_grad(c["zv"])

            dmb = dmx[:, SW:]
            yb = c["outb"] * c["rb"]
            dnb_ref[...] += jnp.sum(dmb * yb, axis=0, keepdims=True)
            doutb = rms_bwd_full(dmb * prm["norm_b"][...], yb, c["rb"])
            lane1 = lax.broadcasted_iota(jnp.int32, (1, LANES), 1)
            dqn_blocks = [None] * (AW // LANES)
            dkn_blocks = [None] * (KVW // LANES)
            dvc_blocks = [None] * (KVW // LANES)
            dsink_vec = jnp.zeros((1, LANES), F32)
            add = lambda old, new: new if old is None else old + new
            for hq, hd in enumerate(c["heads"]):
                mb, e, ek, kb, half = hd["mb"], hd["e"], hd["ek"], hd["kb"], hd["half"]
                dr = doutb[:, LANES * mb:LANES * (mb + 1)]
                if e != ek:
                    dr = pltpu.roll(dr, HEAD_DIM, 1)
                drm = jnp.where(half, dr, 0.0).astype(BF16)
                dp = lax.dot_general(drm, hd["vblk"], (((1,), (1,)), ((), ())), preferred_element_type=F32)
                dvc_blocks[kb] = add(dvc_blocks[kb], lax.dot_general(hd["prb"], drm, (((0,), (0,)), ((), ())),
                                                                     preferred_element_type=F32))
                rowdot = jnp.sum(hd["pr"] * dp, axis=-1, keepdims=True)
                ds = hd["pr"] * (dp - rowdot)
                dsink = jnp.sum(-hd["psink"] * rowdot, axis=0, keepdims=True)
                dsink_vec = dsink_vec + jnp.where(lane1 == hq, dsink, 0.0)
                dbias_ref[hq] += ds
                dsb = (ds * c["scale"]).astype(BF16)
                dqm = jnp.dot(dsb, hd["kblk"], preferred_element_type=F32)
                dqm = jnp.where(half, dqm, 0.0)
                if e != ek:
                    dqm = pltpu.roll(dqm, HEAD_DIM, 1)
                dqn_blocks[mb] = add(dqn_blocks[mb], dqm)
                dkn_blocks[kb] = add(dkn_blocks[kb], lax.dot_general(dsb, hd["qm"], (((0,), (0,)), ((), ())),
                                                                     preferred_element_type=F32))
            dsk_ref[...] += dsink_vec
            dqn = jnp.concatenate(dqn_blocks, axis=1)
            dkn = jnp.concatenate(dkn_blocks, axis=1)
            dvc = jnp.concatenate(dvc_blocks, axis=1)
            yq = c["q"] * c["rq"]
            accq[...] += jnp.sum(dqn * yq, axis=0, keepdims=True)
            dyq = dqn * prm["q_g"][...]
            dq = c["rq"] * (dyq - yq * group_mean_b(dyq * yq, gs, gst))
            yk = c["kcat"] * c["rk"]
            acck[...] += jnp.sum(dkn * yk, axis=0, keepdims=True)
            dyk = dkn * prm["k_g"][...]
            dk = c["rk"] * (dyk - yk * group_mean_b(dyk * yk, prm["gk"][...], prm["gkt"][...]))

            slot = n % 2
            hold[slot, :, :SW] = dzu
            hold[slot, :, SW:2 * SW] = dzv
            hold[slot, :, 2 * SW:] = dq
            tmpkv[:, :KVW] = dk[:BLOCK]
            tmpkv[:, KVW:] = dvc[:BLOCK]
            newkv[:, :KVW] = dk[BLOCK:]
            newkv[:, KVW:] = dvc[BLOCK:]

        @pl.when(n >= 1)
        def _():
            dz_ref[:, :QW] = hold[(n - 1) % 2].astype(dz_ref.dtype)

        @pl.when((n >= 1) & (n < nb))
        def _():
            dz_ref[:, QW:] = (carry[...] + tmpkv[...]).astype(dz_ref.dtype)

        @pl.when(n == nb)
        def _():
            dz_ref[:, QW:] = carry[...].astype(dz_ref.dtype)
            row = lax.broadcasted_iota(jnp.int32, (BLOCK, BLOCK), 0)
            col = lax.broadcasted_iota(jnp.int32, (BLOCK, BLOCK), 1)
            for h in range(H):
                dt_ref[h] = jnp.where(row >= col, dt_ref[h], 0.0)
            dsb_ref[...] = _dot3(accb[...], prm["gs"][...])
            dqg_ref[...] = _dot3(accq[...], gmq_ref[...])
            dkg_ref[...] = _dot3(acck[...], gmk_ref[...])

        @pl.when(n < nb)
        def _():
            carry[...] = newkv[...]

    kvblk = dm.koff // (2 * KVW)
    clamp = lambda n: jnp.minimum(n, nb - 1)
    full = lambda shape: pl.BlockSpec(shape, lambda n: (0,) * len(shape))
    in_specs = [pl.BlockSpec((BLOCK, IN), lambda n: (clamp(n), 0)),
                pl.BlockSpec((BLOCK, 2 * KVW), lambda n: (jnp.maximum(clamp(n) - 1, 0), kvblk)),
                pl.BlockSpec((BLOCK, SW + AW), lambda n: (clamp(n), 0)),
                full((NQ, BLOCK, 2 * BLOCK))] + _mixer_param_specs(dm, None) + [full((AW, LANES)), full((KVW, LANES))]
    out_shape = (jax.ShapeDtypeStruct((dm.S, IN), BF16),
                 jax.ShapeDtypeStruct((1, SW), F32), jax.ShapeDtypeStruct((H, BLOCK, BLOCK), F32),
                 jax.ShapeDtypeStruct((BLOCK, LANES), F32), jax.ShapeDtypeStruct((1, SW), F32),
                 jax.ShapeDtypeStruct((1, LANES), F32), jax.ShapeDtypeStruct((1, LANES), F32),
                 jax.ShapeDtypeStruct((1, AW), F32), jax.ShapeDtypeStruct((1, LANES), F32),
                 jax.ShapeDtypeStruct((NQ, BLOCK, 2 * BLOCK), F32))
    out_specs = (pl.BlockSpec((BLOCK, IN), lambda n: (jnp.maximum(n - 1, 0), 0)),
                 full((1, SW)), full((H, BLOCK, BLOCK)), full((BLOCK, LANES)), full((1, SW)), full((1, LANES)),
                 full((1, LANES)), full((1, AW)), full((1, LANES)), full((NQ, BLOCK, 2 * BLOCK)))
    scratch = [pltpu.VMEM((2, BLOCK, QW), F32), pltpu.VMEM((BLOCK, 2 * KVW), F32), pltpu.VMEM((BLOCK, 2 * KVW), F32),
               pltpu.VMEM((BLOCK, 2 * KVW), F32), pltpu.VMEM((BLOCK, SW), F32), pltpu.VMEM((1, AW), F32),
               pltpu.VMEM((1, KVW), F32)]
    return _call(
        body, out_shape=out_shape, grid=(nb + 1,), in_specs=in_specs, out_specs=out_specs, scratch=scratch,
        sem=("arbitrary",), args=[z, z, dmixed, dbias_in, *params, gmod_q, gmod_k], name=name, comm=comm)


def _adamw(w, gparts, m, v, *, name, tr=256):
    R, C = w.shape
    tr = _tile(R, max(8, min(tr, (1 << 18) // C)), 8)
    ng = len(gparts)
    bc1 = 1.0 - ADAM_B1 ** ADAM_STEP
    bc2 = 1.0 - ADAM_B2 ** ADAM_STEP

    def body(w_ref, *rest):
        g_refs, (m_ref, v_ref, go_ref, d_ref, mo_ref, vo_ref) = rest[:ng], rest[ng:]
        g = g_refs[0][...].astype(F32)
        for r in g_refs[1:]:
            g = g + r[...].astype(F32)
        mn = ADAM_B1 * m_ref[...] + (1.0 - ADAM_B1) * g
        vn = ADAM_B2 * v_ref[...] + (1.0 - ADAM_B2) * jnp.square(g)
        m_hat = mn / bc1
        v_hat = vn / bc2
        go_ref[...] = g
        d_ref[...] = -ADAM_LR * (m_hat / (jnp.sqrt(v_hat) + ADAM_EPS) + ADAM_WD * w_ref[...])
        mo_ref[...] = mn
        vo_ref[...] = vn

    blk = pl.BlockSpec((tr, C), lambda i: (i, 0))
    out = jax.ShapeDtypeStruct((R, C), F32)
    return pl.pallas_call(
        body, out_shape=(out, out, out, out), grid=(R // tr,),
        in_specs=[blk] * (3 + ng), out_specs=(blk, blk, blk, blk),
        compiler_params=_cparams(("parallel",)), name=name,
    )(w, *gparts, m, v)


def _sum_chip_partials(own, got, *, name, tr=256):
    R, C = own.shape
    tr = _tile(R, max(8, min(tr, (1 << 18) // C)), 8)

    def body(own_ref, g0_ref, g1_ref, g2_ref, o_ref):
        o_ref[...] = ((own_ref[...].astype(F32) + g0_ref[...].astype(F32)) + g1_ref[...].astype(F32)) + g2_ref[...].astype(F32)

    blk = pl.BlockSpec((tr, C), lambda i: (i, 0))
    part = lambda k: pl.BlockSpec((None, tr, C), lambda i: (k, i, 0))
    return pl.pallas_call(
        body, out_shape=jax.ShapeDtypeStruct((R, C), F32), grid=(R // tr,),
        in_specs=[blk, part(0), part(1), part(2)], out_specs=blk, compiler_params=_cparams(("parallel",)), name=name,
    )(own, got, got, got)


def _sibling_swap(groups, *, name):
    flat = [a for g in groups for a in g]
    na = len(flat)

    def body(*refs):
        srcs, outs = refs[:na], refs[na:na + len(groups)]
        send, recv = refs[na + len(groups):]
        x, y, c = _mesh_pos()
        copies, s = [], 0
        for gi, g in enumerate(groups):
            for l in range(len(g)):
                copies.append(_remote(srcs[s], outs[gi].at[l], send.at[s], recv.at[s], (x, y, 1 - c)))
                s += 1
        for cp in copies:
            cp.start()
        for cp in copies:
            cp.wait()

    return pl.pallas_call(
        body, out_shape=tuple(jax.ShapeDtypeStruct((len(g),) + g[0].shape, g[0].dtype) for g in groups),
        in_specs=[ANY] * na, out_specs=tuple([ANY] * len(groups)),
        scratch_shapes=[pltpu.SemaphoreType.DMA((na,)), pltpu.SemaphoreType.DMA((na,))],
        name=name,
    )(*flat)


def _all_gather_rows(part, *, name):
    R, C = part.shape

    def body(x_ref, out_ref, send, recv, local):
        x, y, c = _mesh_pos()
        sib = (x, y, 1 - c)
        chips = _other_chips(x, y)
        slot = lambda px, py, pc: out_ref.at[4 * px + 2 * py + pc]

        def copy(k, block, to, src=None):
            return pltpu.make_async_remote_copy(
                src_ref=slot(*block) if src is None else src, dst_ref=slot(*block),
                send_sem=send.at[k], recv_sem=recv.at[k], device_id=to, device_id_type=MESH)

        mine = pltpu.make_async_copy(x_ref, slot(x, y, c), local)
        mine.start()
        first = [copy(0, (x, y, c), sib, src=x_ref)]
        first += [copy(1 + j, (x, y, c), (*chip, c), src=x_ref) for j, chip in enumerate(chips)]
        for cp in first:
            cp.start()
        passed = [copy(4 + j, (*chip, c), sib) for j, chip in enumerate(chips)]
        for j, chip in enumerate(chips):
            copy(1 + j, (*chip, c), (x, y, c)).wait_recv()
            passed[j].start()
        copy(0, (x, y, 1 - c), (x, y, c)).wait_recv()
        for j, chip in enumerate(chips):
            copy(4 + j, (*chip, 1 - c), (x, y, c)).wait_recv()
        for cp in first + passed:
            cp.wait_send()
        mine.wait()

    return pl.pallas_call(
        body, out_shape=jax.ShapeDtypeStruct((N_DEV, R, C), part.dtype), in_specs=[ANY], out_specs=ANY,
        scratch_shapes=[pltpu.SemaphoreType.DMA((7,)), pltpu.SemaphoreType.DMA((7,)), pltpu.SemaphoreType.DMA],
        name=name,
    )(part)


def _pack(arrays):
    parts = []
    for a in arrays:
        flat = a.reshape(-1).astype(F32)
        pad = (-flat.shape[0]) % (8 * LANES)
        parts.append(jnp.pad(flat, (0, pad)))
    return jnp.concatenate(parts).reshape(-1, LANES)


def _unpack(packed, like):
    flat = packed.reshape(-1)
    out, off = [], 0
    for a in like:
        n = int(np.prod(a.shape))
        out.append(flat[off:off + n].reshape(a.shape))
        off += n + ((-n) % (8 * LANES))
    return out


def kernel(x, rel_bias, norm1_g, w_in, sgu_norm_g, sgu_w, sgu_b, q_norm_g, k_norm_g, sinks, out_norm_a, out_norm_b, w_out, norm2_g, w_gate, w_up, w_down, loss_target, m_rel_bias, m_norm1_g, m_w_in, m_sgu_norm_g, m_sgu_w, m_sgu_b, m_q_norm_g, m_k_norm_g, m_sinks, m_out_norm_a, m_out_norm_b, m_w_out, m_norm2_g, m_w_gate, m_w_up, m_w_down, v_rel_bias, v_norm1_g, v_w_in, v_sgu_norm_g, v_sgu_w, v_sgu_b, v_q_norm_g, v_k_norm_g, v_sinks, v_out_norm_a, v_out_norm_b, v_w_out, v_norm2_g, v_w_gate, v_w_up, v_w_down):
    L, D, n_in = w_in.shape
    S = x.shape[1]
    IN = N_CHIPS * n_in
    n_ff = w_gate.shape[2]
    FF = N_CHIPS * n_ff
    H = sgu_w.shape[1]
    NQ = sinks.shape[1]
    SW, AW = H * HEAD_DIM, NQ * HEAD_DIM
    KVW = (IN - 2 * SW - AW) // 2
    dm = _MixerDims(S, IN, SW, AW, KVW, H, NQ)
    assert sgu_w.shape[2] == BLOCK and q_norm_g.shape[1] == HEAD_DIM and SW + AW == D

    wb = {k: w.astype(BF16) for k, w in (("in", w_in), ("out", w_out), ("gate", w_gate), ("up", w_up), ("down", w_down))}

    def gather(l, which):
        if which == "in":
            return _Gather([wb["in"][l]], [(0, 0, 1, 0)], [(D, IN)])
        if which == "out":
            return _Gather([wb["out"][l]], [(0, 0, 0, 0)], [(D, D)])
        if which == "gu":
            return _Gather([wb["gate"][l], wb["up"][l]], [(0, 0, 1, 0), (1, 0, 1, FF)], [(D, 2 * FF)])
        return _Gather([wb["down"][l]], [(0, 0, 0, 0)], [(FF, D)])

    nxt = lambda l, *which: [gather(l + 1, w) for w in which] if l + 1 < L else []
    W = {}
    W[0, "in"], W[0, "out"] = _comm_only([gather(0, "in"), gather(0, "out")], name="gather_first_weights")

    gs, gst, gmod_q = _group_consts(SW)
    gk, gkt, gmod_k = _group_consts(KVW)
    onehot = _bucket_onehot()
    rbt = jnp.pad(rel_bias.T, ((0, 0), (0, LANES - NUM_BUCKETS)))
    (bias,) = _matmul(rbt, onehot.T, mode="nn", out_dtype=F32, exact=True, tn=4096, name="bias_table")
    bias = bias.reshape(NQ, BLOCK, 2 * BLOCK)

    def mixer_params(l):
        return [sgu_norm_g[l].reshape(1, SW), sgu_w[l], jnp.repeat(sgu_b[l].T, HEAD_DIM, axis=1),
                out_norm_a[l].reshape(1, SW), jnp.tile(q_norm_g[l], NQ).reshape(1, AW),
                jnp.tile(k_norm_g[l], dm.NKV).reshape(1, KVW), out_norm_b[l].reshape(1, AW), sinks[l], bias,
                gs, gst, gk, gkt]

    xs = x.reshape(S, D)
    saved = []
    for l in range(L):
        h = _rms_fwd(xs, norm1_g[l].reshape(1, D), name="norm1_fwd")
        z, *got = _matmul(h, W[l, "in"], mode="nn", out_dtype=F32, tn=1792, name="in_proj", comm=nxt(l, "in", "out"))
        if got:
            W[l + 1, "in"], W[l + 1, "out"] = got
        first = [gather(0, "gu"), gather(0, "down")] if l == 0 else []
        mixed, *got = _mixer_fwd(dm, z, mixer_params(l), name="mixer_fwd", comm=first)
        if got:
            W[0, "gu"], W[0, "down"] = got
        (x1,) = _matmul(mixed, W[l, "out"], mode="nn", out_dtype=F32, res=xs, name="out_proj")
        h2 = _rms_fwd(x1, norm2_g[l].reshape(1, D), name="norm2_fwd")
        ab, *got = _matmul(h2, W[l, "gu"], mode="nn", out_dtype=F32, name="gate_up_proj", comm=nxt(l, "gu"))
        if got:
            (W[l + 1, "gu"],) = got
        f = _swiglu_fwd(ab, name="swiglu_fwd")
        x2, *got = _matmul(f, W[l, "down"], mode="nn", out_dtype=F32, res=x1, tk=2816, name="down_proj", comm=nxt(l, "down"))
        if got:
            (W[l + 1, "down"],) = got
        saved.append((xs, h, z, mixed, x1, h2, ab, f))
        xs = x2

    dx, loss_part = _loss_head(xs, loss_target.reshape(S, D), name="loss_head")
    loss = lax.psum(loss_part[0, 0], ("x", "y", "c"))

    dbias = jnp.zeros((NQ, BLOCK, 2 * BLOCK), F32)
    n_out = D // N_CHIPS
    big_names = ("w_in", "w_out", "w_gate", "w_up", "w_down")
    own = {k: [None] * L for k in big_names}
    got = {k: [None] * L for k in big_names}
    small = {k: [None] * L for k in ("norm1_g", "sgu_norm_g", "sgu_w", "sgu_b", "q_norm_g", "k_norm_g", "sinks",
                                     "out_norm_a", "out_norm_b", "norm2_g")}
    for l in reversed(range(L)):
        xl, h, z, mixed, x1, h2, ab, f = saved[l]
        dxb = dx.astype(BF16)
        (df,) = _matmul(dxb, W[l, "down"], mode="nt", out_dtype=F32, tn=512, name="down_proj_dx")
        (g_d,) = _matmul_tn(f, dxb, out_dtype=BF16, tm=1408, tn=2048, name="down_proj_dw")
        dab = _swiglu_bwd(ab, df, name="swiglu_bwd")
        g_gu, got["w_down"][l] = _matmul_tn(h2, dab, out_dtype=BF16, name="gate_up_proj_dw",
                                            comm=[_Scatter([g_d], [(0, 0, 0, n_ff)])])
        dh2, got["w_gate"][l] = _matmul(dab, W[l, "gu"], mode="nt", out_dtype=F32, tn=2048, tk=1024, name="gate_up_proj_dx",
                                        comm=[_Scatter([g_gu], [(0, 1, 0, n_ff)])])
        dx1, dg2 = _rms_bwd(x1, norm2_g[l].reshape(1, D), dh2, dx, name="norm2_bwd")
        dx1b = dx1.astype(BF16)
        (dmixed,) = _matmul(dx1b, W[l, "out"], mode="nt", out_dtype=F32, name="out_proj_dx")
        (g_out,) = _matmul_tn(mixed, dx1b, out_dtype=BF16, name="out_proj_dw")
        (dz, d_sg, d_t, d_sb, d_na, d_qg, d_kg, d_nb, d_sk, dbias, got["w_up"][l], got["w_out"][l]) = _mixer_bwd(
            dm, z, dmixed, dbias, mixer_params(l), gmod_q, gmod_k, name="mixer_bwd",
            comm=[_Scatter([g_gu, g_out], [(0, 1, FF, n_ff), (1, 0, 0, n_out)])])
        (g_in,) = _matmul_tn(h, dz, out_dtype=BF16, tn=896, name="in_proj_dw")
        dh, got["w_in"][l] = _matmul(dz, W[l, "in"], mode="nt", out_dtype=F32, tk=1792, name="in_proj_dx",
                                     comm=[_Scatter([g_in], [(0, 1, 0, n_in)])])
        dx, dg1 = _rms_bwd(xl, norm1_g[l].reshape(1, D), dh, dx1, name="norm1_bwd")
        own["w_in"][l], own["w_out"][l] = (g_in, 1, 0, n_in), (g_out, 0, 0, n_out)
        own["w_gate"][l], own["w_up"][l], own["w_down"][l] = (g_gu, 1, 0, n_ff), (g_gu, 1, FF, n_ff), (g_d, 0, 0, n_ff)
        small["norm1_g"][l] = dg1.reshape(D)
        small["norm2_g"][l] = dg2.reshape(D)
        small["sgu_norm_g"][l] = d_sg.reshape(H, HEAD_DIM)
        small["sgu_w"][l] = d_t
        small["sgu_b"][l] = d_sb[:, :H].T
        small["q_norm_g"][l] = d_qg[0, :HEAD_DIM]
        small["k_norm_g"][l] = d_kg[0, :HEAD_DIM]
        small["sinks"][l] = d_sk[0, :NQ]
        small["out_norm_a"][l] = d_na.reshape(SW)
        small["out_norm_b"][l] = d_nb.reshape(AW)
    grad_x = dx.reshape(1, S, D)
    (d_rb,) = _matmul(dbias.reshape(NQ, BLOCK * 2 * BLOCK), onehot, mode="nn", out_dtype=F32, exact=True, tk=4096,
                      name="rel_bias_grad")
    d_rel_bias = d_rb[:, :NUM_BUCKETS].T

    me_j = 2 * lax.axis_index("x") + lax.axis_index("y")
    sums = {}
    for nm in big_names:
        sums[nm] = []
        for l in range(L):
            g, axis, base, width = own[nm][l]
            mine = lax.dynamic_slice_in_dim(g, base + me_j * width, width, axis=axis)
            sums[nm].append(_sum_chip_partials(mine, got[nm][l], name="sum_chip_partials"))
    theirs = _sibling_swap([sums[nm] for nm in big_names], name="sibling_swap")
    big = {}
    for nm, w, m, v, sib in zip(big_names, (w_in, w_out, w_gate, w_up, w_down),
                                (m_w_in, m_w_out, m_w_gate, m_w_up, m_w_down),
                                (v_w_in, v_w_out, v_w_gate, v_w_up, v_w_down), theirs):
        two = lambda a: a.reshape(-1, a.shape[-1])
        res = _adamw(two(w), [two(jnp.stack(sums[nm])), two(sib)], two(m), two(v), name="adamw_" + nm)
        big[nm] = [r.reshape(w.shape) for r in res]

    names_small = ("rel_bias", "norm1_g", "sgu_norm_g", "sgu_w", "sgu_b", "q_norm_g", "k_norm_g", "sinks", "out_norm_a",
                   "out_norm_b", "norm2_g")
    w_small = dict(rel_bias=rel_bias, norm1_g=norm1_g, sgu_norm_g=sgu_norm_g, sgu_w=sgu_w, sgu_b=sgu_b, q_norm_g=q_norm_g,
                   k_norm_g=k_norm_g, sinks=sinks, out_norm_a=out_norm_a, out_norm_b=out_norm_b, norm2_g=norm2_g)
    m_small = dict(rel_bias=m_rel_bias, norm1_g=m_norm1_g, sgu_norm_g=m_sgu_norm_g, sgu_w=m_sgu_w, sgu_b=m_sgu_b,
                   q_norm_g=m_q_norm_g, k_norm_g=m_k_norm_g, sinks=m_sinks, out_norm_a=m_out_norm_a,
                   out_norm_b=m_out_norm_b, norm2_g=m_norm2_g)
    v_small = dict(rel_bias=v_rel_bias, norm1_g=v_norm1_g, sgu_norm_g=v_sgu_norm_g, sgu_w=v_sgu_w, sgu_b=v_sgu_b,
                   q_norm_g=v_q_norm_g, k_norm_g=v_k_norm_g, sinks=v_sinks, out_norm_a=v_out_norm_a,
                   out_norm_b=v_out_norm_b, norm2_g=v_norm2_g)
    g_small = {k: jnp.stack(vs) for k, vs in small.items()}
    g_small["rel_bias"] = d_rel_bias
    like = [w_small[k] for k in names_small]
    gathered = _all_gather_rows(_pack([g_small[k] for k in names_small]), name="gather_small_grads")
    res = _adamw(_pack(like), [gathered[i] for i in range(N_DEV)], _pack([m_small[k] for k in names_small]),
                 _pack([v_small[k] for k in names_small]), name="adamw_small")
    sm = {k: vals for k, vals in zip(names_small, zip(*[_unpack(r, like) for r in res]))}

    order = ("rel_bias", "norm1_g", "w_in", "sgu_norm_g", "sgu_w", "sgu_b", "q_norm_g", "k_norm_g", "sinks", "out_norm_a",
             "out_norm_b", "w_out", "norm2_g", "w_gate", "w_up", "w_down")
    pick = lambda k, i: big[k][i] if k in big else sm[k][i]
    outs = [loss, grad_x]
    for i in range(4):
        outs += [pick(k, i) for k in order]
    return tuple(outs)
```

```python
import functools
import math

import numpy as np

import jax
import jax.numpy as jnp
from jax import lax
from jax.experimental import pallas as pl
from jax.experimental.pallas import tpu as pltpu

F32 = jnp.float32
BF16 = jnp.bfloat16
MESH = pl.DeviceIdType.MESH
ANY = pl.BlockSpec(memory_space=pl.ANY)

HEAD_DIM = 64
BLOCK = 128
NUM_BUCKETS = 32
MAX_DISTANCE = 128
EPS = 1e-6
NEG_INF = -1e30
ADAM_LR, ADAM_B1, ADAM_B2, ADAM_EPS, ADAM_WD, ADAM_STEP = 0.001, 0.9, 0.999, 1e-08, 0.01, 10

LANES = 128
VMEM_LIMIT = 56 * 1024 * 1024
N_CHIPS = 4
N_DEV = 8


def _cparams(sem=None):
    return pltpu.CompilerParams(dimension_semantics=sem, vmem_limit_bytes=VMEM_LIMIT)


def _tile(dim, target, align=LANES):
    best = None
    for t in range(align, min(dim, target) + 1, align):
        if dim % t == 0:
            best = t
    return best if best is not None else dim


def _mesh_pos():
    return lax.axis_index("x"), lax.axis_index("y"), lax.axis_index("c")


def _other_chips(x, y):
    return [(1 - x, y), (x, 1 - y), (1 - x, 1 - y)]


def _slab(ref, axis, start, size, half=None):
    if axis == 1:
        rows = ref.shape[0]
        r = pl.ds(0, rows) if half is None else pl.ds(pl.multiple_of(half * (rows // 2), 16), rows // 2)
        return ref.at[r, pl.ds(pl.multiple_of(start, LANES), size)]
    if half is None:
        return ref.at[pl.ds(pl.multiple_of(start, 16), size), :]
    return ref.at[pl.ds(pl.multiple_of(start + half * (size // 2), 16), size // 2), :]


def _remote(src, dst, send_sem, recv_sem, to):
    return pltpu.make_async_remote_copy(src_ref=src, dst_ref=dst, send_sem=send_sem, recv_sem=recv_sem,
                                        device_id=to, device_id_type=MESH)


class _Gather:
    def __init__(self, shards, streams, out_shapes, layer):
        self.ins, self.streams, self.layer = list(shards), streams, layer
        self.outs = [jax.ShapeDtypeStruct(s, BF16) for s in out_shapes]
        ns = len(streams)
        self.sems = [pltpu.SemaphoreType.DMA((ns, 3))] * 4 + [pltpu.SemaphoreType.DMA((ns,))]

    def _sent(self, srcs, outs, sems):
        send, recv, _, _, local = sems
        x, y, c = _mesh_pos()
        me_j = 2 * x + y
        own, sends = [], []
        for s, (si, oi, axis, base) in enumerate(self.streams):
            src, out = srcs[si].at[self.layer], outs[oi]
            width = src.shape[axis]
            own.append(pltpu.make_async_copy(src, _slab(out, axis, base + me_j * width, width), local.at[s]))
            half_rows = src.shape[0] // 2
            mine = src.at[pl.ds(pl.multiple_of(c * half_rows, 16), half_rows), :]
            for k, (px, py) in enumerate(_other_chips(x, y)):
                sends.append(_remote(mine, _slab(out, axis, base + me_j * width, width, half=c),
                                     send.at[s, k], recv.at[s, k], (px, py, c)))
        return own, sends

    def start(self, srcs, outs, sems):
        own, sends = self._sent(srcs, outs, sems)
        for cp in own + sends:
            cp.start()

    def finish(self, srcs, outs, sems):
        send, recv, fsend, frecv, _ = sems
        x, y, c = _mesh_pos()
        sib = (x, y, 1 - c)
        forwards, fwd_arrivals = [], []
        for s, (si, oi, axis, base) in enumerate(self.streams):
            out = outs[oi]
            width = srcs[si].shape[1 + axis]
            for k, (px, py) in enumerate(_other_chips(x, y)):
                start = base + (2 * px + py) * width
                got = _slab(out, axis, start, width, half=c)
                _remote(got, got, send.at[s, k], recv.at[s, k], (px, py, c)).wait_recv()
                fwd = _remote(got, got, fsend.at[s, k], frecv.at[s, k], sib)
                fwd.start()
                forwards.append(fwd)
                theirs = _slab(out, axis, start, width, half=1 - c)
                fwd_arrivals.append(_remote(theirs, theirs, fsend.at[s, k], frecv.at[s, k], sib))
        for a in fwd_arrivals:
            a.wait_recv()
        for cp in forwards:
            cp.wait_send()
        own, sends = self._sent(srcs, outs, sems)
        for cp in sends:
            cp.wait_send()
        for cp in own:
            cp.wait()


class _Scatter:
    def __init__(self, grads, streams):
        self.ins, self.streams = list(grads), streams
        shard = lambda g, axis, width: (width, g.shape[1]) if axis == 0 else (g.shape[0], width)
        self.outs = [jax.ShapeDtypeStruct((3,) + shard(grads[gi], axis, width), BF16) for gi, axis, base, width in streams]
        self.sems = [pltpu.SemaphoreType.DMA((len(streams), 3))] * 2

    def _copies(self, srcs, outs, sems):
        send, recv = sems
        x, y, c = _mesh_pos()
        copies = []
        for s, (gi, axis, base, width) in enumerate(self.streams):
            for k, (px, py) in enumerate(_other_chips(x, y)):
                copies.append(_remote(_slab(srcs[gi], axis, base + (2 * px + py) * width, width), outs[s].at[k],
                                      send.at[s, k], recv.at[s, k], (px, py, c)))
        return copies

    def start(self, srcs, outs, sems):
        for cp in self._copies(srcs, outs, sems):
            cp.start()

    def finish(self, srcs, outs, sems):
        for cp in self._copies(srcs, outs, sems):
            cp.wait()


def _call(body, *, grid, in_specs, out_specs, out_shape, args, name, scratch=(), sem=None, comm=()):
    out_shape, out_specs = tuple(out_shape), tuple(out_specs)
    n_in, n_out, n_scr = len(in_specs), len(out_shape), len(scratch)
    c_ins = [a for u in comm for a in u.ins]
    c_outs = [o for u in comm for o in u.outs]
    c_sems = [s for u in comm for s in u.sems]

    def wrapped(*refs):
        ins, rest = refs[:n_in], refs[n_in:]
        cin, rest = rest[:len(c_ins)], rest[len(c_ins):]
        outs, rest = rest[:n_out], rest[n_out:]
        cout, rest = rest[:len(c_outs)], rest[len(c_outs):]
        scr, csem = rest[:n_scr], rest[n_scr:]

        def each(fn_name):
            i = o = s = 0
            for u in comm:
                getattr(u, fn_name)(cin[i:i + len(u.ins)], cout[o:o + len(u.outs)], csem[s:s + len(u.sems)])
                i, o, s = i + len(u.ins), o + len(u.outs), s + len(u.sems)

        if comm:
            pids = [pl.program_id(d) for d in range(len(grid))]
            first = functools.reduce(jnp.logical_and, [p == 0 for p in pids])
            last = functools.reduce(jnp.logical_and, [p == g - 1 for p, g in zip(pids, grid)])
            pl.when(first)(lambda: each("start"))
        body(*ins, *outs, *scr)
        if comm:
            pl.when(last)(lambda: each("finish"))

    if comm:
        sem = ("arbitrary",) * len(grid)
    return pl.pallas_call(
        wrapped, out_shape=out_shape + tuple(c_outs), grid=grid,
        in_specs=list(in_specs) + [ANY] * len(c_ins), out_specs=out_specs + tuple([ANY] * len(c_outs)),
        scratch_shapes=list(scratch) + c_sems, compiler_params=_cparams(sem), name=name,
    )(*args, *c_ins)


def _comm_only(comm, *, name):
    def body(tick_ref):
        tick_ref[...] = jnp.zeros(tick_ref.shape, tick_ref.dtype)

    outs = _call(body, grid=(1,), in_specs=[], out_specs=[pl.BlockSpec((8, LANES), lambda i: (0, 0))],
                 out_shape=[jax.ShapeDtypeStruct((8, LANES), F32)], args=[], name=name, comm=comm)
    return outs[1:]


def _split3(x):
    hi = x.astype(BF16)
    r1 = x - hi.astype(F32)
    mid = r1.astype(BF16)
    lo = (r1 - mid.astype(F32)).astype(BF16)
    return hi, mid, lo


def _dot3(x, g):
    hi, mid, lo = _split3(x)
    d = lambda a: jnp.dot(a, g, preferred_element_type=F32)
    return d(hi) + d(mid) + d(lo)


GROUP_TILE = 256


def _group_sum(x, pblk):
    hi = x.astype(BF16)
    lo = (x - hi.astype(F32)).astype(BF16)
    cols = []
    for b in range(x.shape[1] // GROUP_TILE):
        sl = slice(GROUP_TILE * b, GROUP_TILE * (b + 1))
        cols.append(jnp.dot(hi[:, sl], pblk, preferred_element_type=F32) + jnp.dot(lo[:, sl], pblk, preferred_element_type=F32))
    return cols[0] if len(cols) == 1 else jnp.concatenate(cols, axis=1)


def _matmul(a, b, *, mode, out_dtype, name, res=None, exact=False, tm=1024, tn=1024, tk=2048, comm=()):
    M, K = a.shape
    N = b.shape[1] if mode == "nn" else b.shape[0]
    tm, tn, tk = _tile(M, tm, 8 if M < LANES else LANES), _tile(N, tn), _tile(K, tk)
    nk = K // tk
    dn = (((1,), (0,)), ((), ())) if mode == "nn" else (((1,), (1,)), ((), ()))

    def body(*refs):
        a_ref, b_ref = refs[0], refs[1]
        r_ref = refs[2] if res is not None else None
        o_ref = refs[3] if res is not None else refs[2]
        if exact:
            part = _dot3(a_ref[...], b_ref[...])
        else:
            part = lax.dot_general(a_ref[...].astype(BF16), b_ref[...].astype(BF16), dn, preferred_element_type=F32)

        def finish(total):
            if r_ref is not None:
                total = r_ref[...] + total
            o_ref[...] = total.astype(o_ref.dtype)

        if nk == 1:
            finish(part)
        else:
            acc = refs[-1]
            k = pl.program_id(2)

            @pl.when(k == 0)
            def _():
                acc[...] = part

            @pl.when(k > 0)
            def _():
                acc[...] += part

            @pl.when(k == nk - 1)
            def _():
                finish(acc[...])

    if mode == "nn":
        b_spec = pl.BlockSpec((tk, tn), lambda j, i, k: (k, j))
    else:
        b_spec = pl.BlockSpec((tn, tk), lambda j, i, k: (j, k))
    in_specs = [pl.BlockSpec((tm, tk), lambda j, i, k: (i, k)), b_spec]
    args = [a, b]
    if res is not None:
        in_specs.append(pl.BlockSpec((tm, tn), lambda j, i, k: (i, j)))
        args.append(res)
    return _call(
        body, grid=(N // tn, M // tm, nk), in_specs=in_specs,
        out_specs=[pl.BlockSpec((tm, tn), lambda j, i, k: (i, j))], out_shape=[jax.ShapeDtypeStruct((M, N), out_dtype)],
        scratch=[pltpu.VMEM((tm, tn), F32)] if nk > 1 else [], sem=("parallel", "parallel", "arbitrary"),
        args=args, name=name, comm=comm)


def _matmul_tn(a, b, *, out_dtype, name, tm=2048, tn=1024, tt=1024, comm=()):
    T, Mo = a.shape
    N = b.shape[1]
    tm, tn, tt = _tile(Mo, tm), _tile(N, tn), _tile(T, tt)
    nt = T // tt

    def body(a_ref, b_ref, o_ref, acc):
        t = pl.program_id(2)
        part = lax.dot_general(a_ref[...].astype(BF16), b_ref[...].astype(BF16), (((0,), (0,)), ((), ())),
                               preferred_element_type=F32)

        @pl.when(t == 0)
        def _():
            acc[...] = part

        @pl.when(t > 0)
        def _():
            acc[...] += part

        @pl.when(t == nt - 1)
        def _():
            o_ref[...] = acc[...].astype(o_ref.dtype)

    return _call(
        body, grid=(Mo // tm, N // tn, nt),
        in_specs=[pl.BlockSpec((tt, tm), lambda i, j, t: (t, i)), pl.BlockSpec((tt, tn), lambda i, j, t: (t, j))],
        out_specs=[pl.BlockSpec((tm, tn), lambda i, j, t: (i, j))], out_shape=[jax.ShapeDtypeStruct((Mo, N), out_dtype)],
        scratch=[pltpu.VMEM((tm, tn), F32)], sem=("parallel", "parallel", "arbitrary"), args=[a, b], name=name, comm=comm)


def _rms_fwd(x, g, *, name, tr=256):
    R, D = x.shape
    tr = _tile(R, tr, 8)

    def body(x_ref, g_ref, o_ref):
        xv = x_ref[...]
        r = lax.rsqrt(jnp.mean(xv * xv, axis=-1, keepdims=True) + EPS)
        o_ref[...] = (xv * r * g_ref[...]).astype(o_ref.dtype)

    return pl.pallas_call(
        body, out_shape=jax.ShapeDtypeStruct((R, D), BF16), grid=(R // tr,),
        in_specs=[pl.BlockSpec((tr, D), lambda i: (i, 0)), pl.BlockSpec((1, D), lambda i: (0, 0))],
        out_specs=pl.BlockSpec((tr, D), lambda i: (i, 0)),
        compiler_params=_cparams(("parallel",)), name=name,
    )(x, g)


def _rms_bwd(x, g, dh, dres, *, name, tr=256):
    R, D = x.shape
    tr = _tile(R, tr, 8)

    def body(x_ref, g_ref, dh_ref, dres_ref, dx_ref, dxb_ref, dg_ref):
        i = pl.program_id(0)
        xv = x_ref[...]
        r = lax.rsqrt(jnp.mean(xv * xv, axis=-1, keepdims=True) + EPS)
        y = xv * r
        dhv = dh_ref[...]
        dy = dhv * g_ref[...]
        dx = dres_ref[...] + r * (dy - y * jnp.mean(dy * y, axis=-1, keepdims=True))
        dx_ref[...] = dx
        dxb_ref[...] = dx.astype(BF16)
        dg = jnp.sum(dhv * y, axis=0, keepdims=True)

        @pl.when(i == 0)
        def _():
            dg_ref[...] = dg

        @pl.when(i > 0)
        def _():
            dg_ref[...] += dg

    row = pl.BlockSpec((tr, D), lambda i: (i, 0))
    vec = pl.BlockSpec((1, D), lambda i: (0, 0))
    return pl.pallas_call(
        body, out_shape=(jax.ShapeDtypeStruct((R, D), F32), jax.ShapeDtypeStruct((R, D), BF16),
                         jax.ShapeDtypeStruct((1, D), F32)), grid=(R // tr,),
        in_specs=[row, vec, row, row], out_specs=(row, row, vec),
        compiler_params=_cparams(("arbitrary",)), name=name,
    )(x, g, dh, dres)


def _swiglu_fwd(ab, *, name, tr=128):
    R, F2 = ab.shape
    FF = F2 // 2
    tr = _tile(R, tr, 8)

    def body(a_ref, b_ref, o_ref):
        a = a_ref[...]
        o_ref[...] = (a * jax.nn.sigmoid(a) * b_ref[...]).astype(o_ref.dtype)

    return pl.pallas_call(
        body, out_shape=jax.ShapeDtypeStruct((R, FF), BF16), grid=(R // tr,),
        in_specs=[pl.BlockSpec((tr, FF), lambda i: (i, 0)), pl.BlockSpec((tr, FF), lambda i: (i, 1))],
        out_specs=pl.BlockSpec((tr, FF), lambda i: (i, 0)),
        compiler_params=_cparams(("parallel",)), name=name,
    )(ab, ab)


def _swiglu_bwd(ab, df, *, name, tr=128):
    R, F2 = ab.shape
    FF = F2 // 2
    tr = _tile(R, tr, 8)

    def body(a_ref, b_ref, df_ref, o_ref):
        a = a_ref[...]
        dfv = df_ref[...]
        s = jax.nn.sigmoid(a)
        o_ref[:, :FF] = (dfv * b_ref[...] * (s * (1.0 + a * (1.0 - s)))).astype(o_ref.dtype)
        o_ref[:, FF:] = (dfv * (a * s)).astype(o_ref.dtype)

    return pl.pallas_call(
        body, out_shape=jax.ShapeDtypeStruct((R, F2), BF16), grid=(R // tr,),
        in_specs=[pl.BlockSpec((tr, FF), lambda i: (i, 0)), pl.BlockSpec((tr, FF), lambda i: (i, 1)),
                  pl.BlockSpec((tr, FF), lambda i: (i, 0))],
        out_specs=pl.BlockSpec((tr, F2), lambda i: (i, 0)),
        compiler_params=_cparams(("parallel",)), name=name,
    )(ab, ab, df)


def _loss_head(y, target, *, name, tr=256):
    R, D = y.shape
    tr = _tile(R, tr, 8)

    def body(y_ref, t_ref, dy_ref, dyb_ref, l_ref):
        i = pl.program_id(0)
        e = y_ref[...] - t_ref[...]
        dy = e * (1.0 / D)
        dy_ref[...] = dy
        dyb_ref[...] = dy.astype(BF16)
        part = 0.5 * jnp.sum(jnp.mean(e * e, axis=-1, keepdims=True), axis=0, keepdims=True)
        part = jnp.broadcast_to(part, (8, LANES))

        @pl.when(i == 0)
        def _():
            l_ref[...] = part

        @pl.when(i > 0)
        def _():
            l_ref[...] += part

    row = pl.BlockSpec((tr, D), lambda i: (i, 0))
    return pl.pallas_call(
        body, out_shape=(jax.ShapeDtypeStruct((R, D), F32), jax.ShapeDtypeStruct((R, D), BF16),
                         jax.ShapeDtypeStruct((8, LANES), F32)), grid=(R // tr,),
        in_specs=[row, row], out_specs=(row, row, pl.BlockSpec((8, LANES), lambda i: (0, 0))),
        compiler_params=_cparams(("arbitrary",)), name=name,
    )(y, target)


def _gelu(x):
    return 0.5 * x * (1.0 + lax.erf(x * math.sqrt(0.5)))


def _gelu_grad(x):
    return 0.5 * (1.0 + lax.erf(x * math.sqrt(0.5))) + x * jnp.exp(-0.5 * x * x) * (1.0 / math.sqrt(2.0 * math.pi))


def _group_consts(width):
    lane = np.arange(width)
    col = np.arange(LANES)
    grp = (lane[:, None] // HEAD_DIM == col[None, :]).astype(np.float32)
    mod = ((lane[:, None] % HEAD_DIM == col[None, :]) & (col[None, :] < HEAD_DIM)).astype(np.float32)
    return jnp.asarray(grp, BF16), jnp.asarray(mod, BF16)


def _same_group():
    lane = np.arange(GROUP_TILE) // HEAD_DIM
    return jnp.asarray((lane[:, None] == lane[None, :]).astype(np.float32), BF16)


def _bucket_onehot():
    qi = np.arange(BLOCK)[:, None]
    kj = np.arange(2 * BLOCK)[None, :]
    n = np.maximum(qi + BLOCK - kj, 0)
    max_exact = NUM_BUCKETS // 2
    nf = np.maximum(n, 1).astype(np.float32)
    large = max_exact + (np.log(nf / np.float32(max_exact)) / np.float32(math.log(MAX_DISTANCE / max_exact))
                         * np.float32(NUM_BUCKETS - max_exact)).astype(np.int32)
    large = np.minimum(large, NUM_BUCKETS - 1)
    bucket = jnp.asarray(np.where(n < max_exact, n, large).reshape(-1).astype(np.int32))
    return (bucket[:, None] == jnp.arange(LANES, dtype=jnp.int32)[None, :]).astype(BF16)


class _MixerDims:
    def __init__(self, S, IN, SW, AW, KVW, H, NQ):
        self.S, self.IN, self.SW, self.AW, self.KVW, self.H, self.NQ = S, IN, SW, AW, KVW, H, NQ
        self.NKV = KVW // HEAD_DIM
        self.GROUP = NQ // self.NKV
        self.nb = S // BLOCK
        self.koff = 2 * SW + AW
        self.voff = self.koff + KVW
        assert SW % LANES == 0 and AW % LANES == 0 and KVW % LANES == 0 and self.GROUP % 2 == 0
        assert self.koff % (2 * KVW) == 0 and IN == self.voff + KVW and S % BLOCK == 0


def _mixer_block(dm, n, z, kvp, prm):
    SW, AW, KVW, H, NQ = dm.SW, dm.AW, dm.KVW, dm.H, dm.NQ
    lane = lax.broadcasted_iota(jnp.int32, (BLOCK, LANES), 1)
    lo = lane < HEAD_DIM
    row = lax.broadcasted_iota(jnp.int32, (BLOCK, BLOCK), 0)
    col = lax.broadcasted_iota(jnp.int32, (BLOCK, BLOCK), 1)
    tril = row >= col
    pblk = prm["pblk"][...]
    inv = 1.0 / HEAD_DIM

    def group_rsqrt(x):
        return lax.rsqrt(_group_sum(x * x, pblk) * inv + EPS)

    zu, zv = z[:, :SW], z[:, SW:2 * SW]
    u, v = _gelu(zu), _gelu(zv)
    rv = group_rsqrt(v)
    vn = v * rv * prm["sgu_g"][...]
    vnb = vn.astype(BF16)
    tmats, gate_blocks = [], []
    for p in range(H // 2):
        blk = slice(LANES * p, LANES * (p + 1))
        t0 = jnp.where(tril, prm["sgu_w"][2 * p], 0.0).astype(BF16)
        t1 = jnp.where(tril, prm["sgu_w"][2 * p + 1], 0.0).astype(BF16)
        tmats += [t0, t1]
        g0 = jnp.dot(t0, vnb[:, blk], preferred_element_type=F32)
        g1 = jnp.dot(t1, vnb[:, blk], preferred_element_type=F32)
        gate_blocks.append(jnp.where(lo, g0, g1) + prm["sgu_bias"][:, blk])
    gate = jnp.concatenate(gate_blocks, axis=1)
    outa = u * gate
    ra = lax.rsqrt(jnp.mean(outa * outa, axis=-1, keepdims=True) + EPS)

    q = z[:, 2 * SW:2 * SW + AW]
    kcat = jnp.concatenate([kvp[:, :KVW], z[:, dm.koff:dm.koff + KVW]], axis=0)
    vcat = jnp.concatenate([kvp[:, KVW:], z[:, dm.voff:dm.voff + KVW]], axis=0)
    rq = group_rsqrt(q)
    rk = group_rsqrt(kcat)
    qn = q * rq * prm["q_g"][...]
    kn = kcat * rk * prm["k_g"][...]
    knb, vcb = kn.astype(BF16), vcat.astype(BF16)
    qi = lax.broadcasted_iota(jnp.int32, (BLOCK, 2 * BLOCK), 0)
    kj = lax.broadcasted_iota(jnp.int32, (BLOCK, 2 * BLOCK), 1)
    valid = (kj > qi) & (kj <= qi + BLOCK) & ((n > 0) | (kj >= BLOCK))
    scale = 1.0 / math.sqrt(HEAD_DIM)
    heads = []
    out_blocks = []
    for hq in range(NQ):
        mb, e = hq // 2, hq % 2
        kv = hq // dm.GROUP
        kb, ek = kv // 2, kv % 2
        qblk = qn[:, LANES * mb:LANES * (mb + 1)]
        if e != ek:
            qblk = pltpu.roll(qblk, HEAD_DIM, 1)
        half = lo if ek == 0 else jnp.logical_not(lo)
        qm = jnp.where(half, qblk, 0.0).astype(BF16)
        kblk = knb[:, LANES * kb:LANES * (kb + 1)]
        vblk = vcb[:, LANES * kb:LANES * (kb + 1)]
        s = lax.dot_general(qm, kblk, (((1,), (1,)), ((), ())), preferred_element_type=F32) * scale + prm["bias"][hq]
        s = jnp.where(valid, s, NEG_INF)
        sink = prm["sinks"][hq]
        mx = jnp.maximum(jnp.max(s, axis=-1, keepdims=True), sink)
        ex = jnp.exp(s - mx)
        den = jnp.sum(ex, axis=-1, keepdims=True) + jnp.exp(sink - mx)
        pr = ex / den
        psink = jnp.exp(sink - mx) / den
        prb = pr.astype(BF16)
        r_h = jnp.dot(prb, vblk, preferred_element_type=F32)
        if e != ek:
            r_h = pltpu.roll(r_h, HEAD_DIM, 1)
        heads.append(dict(qm=qm, kblk=kblk, vblk=vblk, pr=pr, prb=prb, psink=psink, half=half, mb=mb, e=e, ek=ek, kb=kb))
        if e == 1:
            out_blocks.append(jnp.where(lo, prev_r, r_h))
        prev_r = r_h
    outb = jnp.concatenate(out_blocks, axis=1)
    rb = lax.rsqrt(jnp.mean(outb * outb, axis=-1, keepdims=True) + EPS)
    return dict(lo=lo, tril=tril, pblk=pblk, zu=zu, zv=zv, u=u, v=v, rv=rv, vnb=vnb, tmats=tmats, gate=gate,
                outa=outa, ra=ra, q=q, kcat=kcat, rq=rq, rk=rk, heads=heads, outb=outb, rb=rb, scale=scale)


_MIXER_PARAMS = ("sgu_g", "sgu_w", "sgu_bias", "norm_a", "q_g", "k_g", "norm_b", "sinks", "bias", "pblk")


def _mixer_param_specs(dm, idx):
    SW, AW, KVW = dm.SW, dm.AW, dm.KVW
    full = lambda shape: pl.BlockSpec(shape, lambda n: (0,) * len(shape))
    return [full((1, SW)), full((dm.H, BLOCK, BLOCK)), full((BLOCK, SW)), full((1, SW)), full((1, AW)), full((1, KVW)),
            full((1, AW)), pl.BlockSpec(memory_space=pltpu.SMEM), full((dm.NQ, BLOCK, 2 * BLOCK)),
            full((GROUP_TILE, GROUP_TILE))]


def _mixer_fwd(dm, z, params, *, name, comm=()):
    nb = dm.nb

    def body(z_ref, kvp_ref, *rest):
        prm = dict(zip(_MIXER_PARAMS, rest[:len(_MIXER_PARAMS)]))
        o_ref = rest[len(_MIXER_PARAMS)]
        n = pl.program_id(0)
        c = _mixer_block(dm, n, z_ref[...], kvp_ref[...], prm)
        o_ref[:, :dm.SW] = (c["outa"] * c["ra"] * prm["norm_a"][...]).astype(o_ref.dtype)
        o_ref[:, dm.SW:] = (c["outb"] * c["rb"] * prm["norm_b"][...]).astype(o_ref.dtype)

    kvblk = dm.koff // (2 * dm.KVW)
    in_specs = [pl.BlockSpec((BLOCK, dm.IN), lambda n: (n, 0)),
                pl.BlockSpec((BLOCK, 2 * dm.KVW), lambda n: (jnp.maximum(n - 1, 0), kvblk))] + _mixer_param_specs(dm, None)
    return _call(
        body, out_shape=[jax.ShapeDtypeStruct((dm.S, dm.SW + dm.AW), BF16)], grid=(nb,),
        in_specs=in_specs, out_specs=[pl.BlockSpec((BLOCK, dm.SW + dm.AW), lambda n: (n, 0))],
        sem=("arbitrary",), args=[z, z, *params], name=name, comm=comm)


def _mixer_bwd(dm, z, dmixed, dbias_in, params, gmats, *, name, comm=()):
    SW, AW, KVW, H, NQ, nb, IN = dm.SW, dm.AW, dm.KVW, dm.H, dm.NQ, dm.nb, dm.IN
    QW = 2 * SW + AW
    NP = len(_MIXER_PARAMS)

    def body(z_ref, kvp_ref, dm_ref, dbin_ref, *rest):
        prm = dict(zip(_MIXER_PARAMS, rest[:NP]))
        gs_ref, gmq_ref, gmk_ref = rest[NP:NP + 3]
        (dz_ref, dsg_ref, dt_ref, dsb_ref, dna_ref, dqg_ref, dkg_ref, dnb_ref, dsk_ref, dbias_ref) = rest[NP + 3:NP + 13]
        hold, tmpkv, newkv, carry, accb, accq, acck = rest[NP + 13:]
        n = pl.program_id(0)

        @pl.when(n == 0)
        def _():
            for r in (dsg_ref, dt_ref, dna_ref, dnb_ref, dsk_ref, accb, accq, acck):
                r[...] = jnp.zeros(r.shape, r.dtype)
            dbias_ref[...] = dbin_ref[...]

        @pl.when(n < nb)
        def _():
            c = _mixer_block(dm, n, z_ref[...], kvp_ref[...], prm)
            lo = c["lo"]
            dmx = dm_ref[...]
            inv = 1.0 / HEAD_DIM

            def rms_bwd_full(dy_scaled, y, r):
                return r * (dy_scaled - y * jnp.mean(dy_scaled * y, axis=-1, keepdims=True))

            def group_mean_b(x):
                return _group_sum(x, c["pblk"]) * inv

            dma = dmx[:, :SW]
            ya = c["outa"] * c["ra"]
            dna_ref[...] += jnp.sum(dma * ya, axis=0, keepdims=True)
            douta = rms_bwd_full(dma * prm["norm_a"][...], ya, c["ra"])
            du = douta * c["gate"]
            dgate = douta * c["u"]
            accb[...] += dgate
            dgb16 = dgate.astype(BF16)
            dvn_blocks = []
            for p in range(H // 2):
                blk = slice(LANES * p, LANES * (p + 1))
                dg = dgate[:, blk]
                d0 = jnp.where(lo, dg, 0.0).astype(BF16)
                d1 = jnp.where(lo, 0.0, dg).astype(BF16)
                vb = c["vnb"][:, blk]
                nt = lambda a, b: lax.dot_general(a, b, (((1,), (1,)), ((), ())), preferred_element_type=F32)
                tn = lambda a, b: lax.dot_general(a, b, (((0,), (0,)), ((), ())), preferred_element_type=F32)
                dt_ref[2 * p] += nt(d0, vb)
                dt_ref[2 * p + 1] += nt(d1, vb)
                dvn_blocks.append(jnp.where(lo, tn(c["tmats"][2 * p], dgb16[:, blk]), tn(c["tmats"][2 * p + 1], dgb16[:, blk])))
            dvn = jnp.concatenate(dvn_blocks, axis=1)
            yv = c["v"] * c["rv"]
            dsg_ref[...] += jnp.sum(dvn * yv, axis=0, keepdims=True)
            dyv = dvn * prm["sgu_g"][...]
            dv = c["rv"] * (dyv - yv * group_mean_b(dyv * yv))
            dzu = du * _gelu_grad(c["zu"])
            dzv = dv * _gelu_grad(c["zv"])

            dmb = dmx[:, SW:]
            yb = c["outb"] * c["rb"]
            dnb_ref[...] += jnp.sum(dmb * yb, axis=0, keepdims=True)
            doutb = rms_bwd_full(dmb * prm["norm_b"][...], yb, c["rb"])
            lane1 = lax.broadcasted_iota(jnp.int32, (1, LANES), 1)
            dqn_blocks = [None] * (AW // LANES)
            dkn_blocks = [None] * (KVW // LANES)
            dvc_blocks = [None] * (KVW // LANES)
            dsink_vec = jnp.zeros((1, LANES), F32)
            add = lambda old, new: new if old is None else old + new
            for hq, hd in enumerate(c["heads"]):
                mb, e, ek, kb, half = hd["mb"], hd["e"], hd["ek"], hd["kb"], hd["half"]
                dr = doutb[:, LANES * mb:LANES * (mb + 1)]
                if e != ek:
                    dr = pltpu.roll(dr, HEAD_DIM, 1)
                drm = jnp.where(half, dr, 0.0).astype(BF16)
                dp = lax.dot_general(drm, hd["vblk"], (((1,), (1,)), ((), ())), preferred_element_type=F32)
                dvc_blocks[kb] = add(dvc_blocks[kb], lax.dot_general(hd["prb"], drm, (((0,), (0,)), ((), ())),
                                                                     preferred_element_type=F32))
                rowdot = jnp.sum(hd["pr"] * dp, axis=-1, keepdims=True)
                ds = hd["pr"] * (dp - rowdot)
                dsink = jnp.sum(-hd["psink"] * rowdot, axis=0, keepdims=True)
                dsink_vec = dsink_vec + jnp.where(lane1 == hq, dsink, 0.0)
                dbias_ref[hq] += ds
                dsb = (ds * c["scale"]).astype(BF16)
                dqm = jnp.dot(dsb, hd["kblk"], preferred_element_type=F32)
                dqm = jnp.where(half, dqm, 0.0)
                if e != ek:
                    dqm = pltpu.roll(dqm, HEAD_DIM, 1)
                dqn_blocks[mb] = add(dqn_blocks[mb], dqm)
                dkn_blocks[kb] = add(dkn_blocks[kb], lax.dot_general(dsb, hd["qm"], (((0,), (0,)), ((), ())),
                                                                     preferred_element_type=F32))
            dsk_ref[...] += dsink_vec
            dqn = jnp.concatenate(dqn_blocks, axis=1)
            dkn = jnp.concatenate(dkn_blocks, axis=1)
            dvc = jnp.concatenate(dvc_blocks, axis=1)
            yq = c["q"] * c["rq"]
            accq[...] += jnp.sum(dqn * yq, axis=0, keepdims=True)
            dyq = dqn * prm["q_g"][...]
            dq = c["rq"] * (dyq - yq * group_mean_b(dyq * yq))
            yk = c["kcat"] * c["rk"]
            acck[...] += jnp.sum(dkn * yk, axis=0, keepdims=True)
            dyk = dkn * prm["k_g"][...]
            dk = c["rk"] * (dyk - yk * group_mean_b(dyk * yk))

            slot = n % 2
            hold[slot, :, :SW] = dzu
            hold[slot, :, SW:2 * SW] = dzv
            hold[slot, :, 2 * SW:] = dq
            tmpkv[:, :KVW] = dk[:BLOCK]
            tmpkv[:, KVW:] = dvc[:BLOCK]
            newkv[:, :KVW] = dk[BLOCK:]
            newkv[:, KVW:] = dvc[BLOCK:]

        @pl.when(n >= 1)
        def _():
            dz_ref[:, :QW] = hold[(n - 1) % 2].astype(dz_ref.dtype)

        @pl.when((n >= 1) & (n < nb))
        def _():
            dz_ref[:, QW:] = (carry[...] + tmpkv[...]).astype(dz_ref.dtype)

        @pl.when(n == nb)
        def _():
            dz_ref[:, QW:] = carry[...].astype(dz_ref.dtype)
            row = lax.broadcasted_iota(jnp.int32, (BLOCK, BLOCK), 0)
            col = lax.broadcasted_iota(jnp.int32, (BLOCK, BLOCK), 1)
            for h in range(H):
                dt_ref[h] = jnp.where(row >= col, dt_ref[h], 0.0)
            dsb_ref[...] = _dot3(accb[...], gs_ref[...])
            dqg_ref[...] = _dot3(accq[...], gmq_ref[...])
            dkg_ref[...] = _dot3(acck[...], gmk_ref[...])

        @pl.when(n < nb)
        def _():
            carry[...] = newkv[...]

    kvblk = dm.koff // (2 * KVW)
    clamp = lambda n: jnp.minimum(n, nb - 1)
    full = lambda shape: pl.BlockSpec(shape, lambda n: (0,) * len(shape))
    in_specs = [pl.BlockSpec((BLOCK, IN), lambda n: (clamp(n), 0)),
                pl.BlockSpec((BLOCK, 2 * KVW), lambda n: (jnp.maximum(clamp(n) - 1, 0), kvblk)),
                pl.BlockSpec((BLOCK, SW + AW), lambda n: (clamp(n), 0)),
                full((NQ, BLOCK, 2 * BLOCK))] + _mixer_param_specs(dm, None) + [full((SW, LANES)), full((AW, LANES)),
                                                                                full((KVW, LANES))]
    out_shape = (jax.ShapeDtypeStruct((dm.S, IN), BF16),
                 jax.ShapeDtypeStruct((1, SW), F32), jax.ShapeDtypeStruct((H, BLOCK, BLOCK), F32),
                 jax.ShapeDtypeStruct((BLOCK, LANES), F32), jax.ShapeDtypeStruct((1, SW), F32),
                 jax.ShapeDtypeStruct((1, LANES), F32), jax.ShapeDtypeStruct((1, LANES), F32),
                 jax.ShapeDtypeStruct((1, AW), F32), jax.ShapeDtypeStruct((1, LANES), F32),
                 jax.ShapeDtypeStruct((NQ, BLOCK, 2 * BLOCK), F32))
    out_specs = (pl.BlockSpec((BLOCK, IN), lambda n: (jnp.maximum(n - 1, 0), 0)),
                 full((1, SW)), full((H, BLOCK, BLOCK)), full((BLOCK, LANES)), full((1, SW)), full((1, LANES)),
                 full((1, LANES)), full((1, AW)), full((1, LANES)), full((NQ, BLOCK, 2 * BLOCK)))
    scratch = [pltpu.VMEM((2, BLOCK, QW), F32), pltpu.VMEM((BLOCK, 2 * KVW), F32), pltpu.VMEM((BLOCK, 2 * KVW), F32),
               pltpu.VMEM((BLOCK, 2 * KVW), F32), pltpu.VMEM((BLOCK, SW), F32), pltpu.VMEM((1, AW), F32),
               pltpu.VMEM((1, KVW), F32)]
    return _call(
        body, out_shape=out_shape, grid=(nb + 1,), in_specs=in_specs, out_specs=out_specs, scratch=scratch,
        sem=("arbitrary",), args=[z, z, dmixed, dbias_in, *params, *gmats], name=name, comm=comm)


def _adamw(w, gparts, m, v, *, name, tr=256):
    R, C = w.shape
    tr = _tile(R, max(8, min(tr, (1 << 18) // C)), 8)
    ng = len(gparts)
    bc1 = 1.0 - ADAM_B1 ** ADAM_STEP
    bc2 = 1.0 - ADAM_B2 ** ADAM_STEP

    def body(w_ref, *rest):
        g_refs, (m_ref, v_ref, go_ref, d_ref, mo_ref, vo_ref) = rest[:ng], rest[ng:]
        g = g_refs[0][...].astype(F32)
        for r in g_refs[1:]:
            g = g + r[...].astype(F32)
        mn = ADAM_B1 * m_ref[...] + (1.0 - ADAM_B1) * g
        vn = ADAM_B2 * v_ref[...] + (1.0 - ADAM_B2) * jnp.square(g)
        m_hat = mn / bc1
        v_hat = vn / bc2
        go_ref[...] = g
        d_ref[...] = -ADAM_LR * (m_hat / (jnp.sqrt(v_hat) + ADAM_EPS) + ADAM_WD * w_ref[...])
        mo_ref[...] = mn
        vo_ref[...] = vn

    blk = pl.BlockSpec((tr, C), lambda i: (i, 0))
    out = jax.ShapeDtypeStruct((R, C), F32)
    return pl.pallas_call(
        body, out_shape=(out, out, out, out), grid=(R // tr,),
        in_specs=[blk] * (3 + ng), out_specs=(blk, blk, blk, blk),
        compiler_params=_cparams(("parallel",)), name=name,
    )(w, *gparts, m, v)


def _sum_chip_partials(g, axis, base, width, got, chip, layer, n_layers, stacked, *, name, tr=256):
    _, R, C = got.shape
    tr = _tile(R, max(8, min(tr, (1 << 18) // C)), 8)
    assert base % width == 0 and (C == width if axis == 1 else R == width)

    def body(chip_ref, own_ref, g0_ref, g1_ref, g2_ref, *rest):
        o_ref = rest[-1]
        o_ref[...] = ((own_ref[...].astype(F32) + g0_ref[...].astype(F32)) + g1_ref[...].astype(F32)) + g2_ref[...].astype(F32)

    if axis == 1:
        own_spec = pl.BlockSpec((tr, C), lambda i, j: (i, base // width + j[0]))
    else:
        own_spec = pl.BlockSpec((tr, C), lambda i, j: ((base // width + j[0]) * (R // tr) + i, 0))
    part = lambda k: pl.BlockSpec((None, tr, C), lambda i, j: (k, i, 0))
    prev = [] if stacked is None else [stacked]
    return pl.pallas_call(
        body, out_shape=jax.ShapeDtypeStruct((n_layers * R, C), F32),
        grid_spec=pltpu.PrefetchScalarGridSpec(
            num_scalar_prefetch=1, grid=(R // tr,), in_specs=[own_spec, part(0), part(1), part(2)] + [ANY] * len(prev),
            out_specs=pl.BlockSpec((tr, C), lambda i, j: (layer * (R // tr) + i, 0))),
        input_output_aliases={5: 0} if prev else {},
        compiler_params=_cparams(("parallel",)), name=name,
    )(chip, g, got, got, got, *prev)


def _sibling_swap(groups, *, name):
    flat = [a for g in groups for a in g]
    na = len(flat)

    def body(*refs):
        srcs, outs = refs[:na], refs[na:na + len(groups)]
        send, recv = refs[na + len(groups):]
        x, y, c = _mesh_pos()
        copies, s = [], 0
        for gi, g in enumerate(groups):
            for l in range(len(g)):
                copies.append(_remote(srcs[s], outs[gi].at[l], send.at[s], recv.at[s], (x, y, 1 - c)))
                s += 1
        for cp in copies:
            cp.start()
        for cp in copies:
            cp.wait()

    return pl.pallas_call(
        body, out_shape=tuple(jax.ShapeDtypeStruct((len(g),) + g[0].shape, g[0].dtype) for g in groups),
        in_specs=[ANY] * na, out_specs=tuple([ANY] * len(groups)),
        scratch_shapes=[pltpu.SemaphoreType.DMA((na,)), pltpu.SemaphoreType.DMA((na,))],
        name=name,
    )(*flat)


def _all_gather_rows(part, *, name):
    R, C = part.shape

    def body(x_ref, out_ref, send, recv, local):
        x, y, c = _mesh_pos()
        sib = (x, y, 1 - c)
        chips = _other_chips(x, y)
        slot = lambda px, py, pc: out_ref.at[4 * px + 2 * py + pc]

        def copy(k, block, to, src=None):
            return pltpu.make_async_remote_copy(
                src_ref=slot(*block) if src is None else src, dst_ref=slot(*block),
                send_sem=send.at[k], recv_sem=recv.at[k], device_id=to, device_id_type=MESH)

        mine = pltpu.make_async_copy(x_ref, slot(x, y, c), local)
        mine.start()
        first = [copy(0, (x, y, c), sib, src=x_ref)]
        first += [copy(1 + j, (x, y, c), (*chip, c), src=x_ref) for j, chip in enumerate(chips)]
        for cp in first:
            cp.start()
        passed = [copy(4 + j, (*chip, c), sib) for j, chip in enumerate(chips)]
        for j, chip in enumerate(chips):
            copy(1 + j, (*chip, c), (x, y, c)).wait_recv()
            passed[j].start()
        copy(0, (x, y, 1 - c), (x, y, c)).wait_recv()
        for j, chip in enumerate(chips):
            copy(4 + j, (*chip, 1 - c), (x, y, c)).wait_recv()
        for cp in first + passed:
            cp.wait_send()
        mine.wait()

    return pl.pallas_call(
        body, out_shape=jax.ShapeDtypeStruct((N_DEV, R, C), part.dtype), in_specs=[ANY], out_specs=ANY,
        scratch_shapes=[pltpu.SemaphoreType.DMA((7,)), pltpu.SemaphoreType.DMA((7,)), pltpu.SemaphoreType.DMA],
        name=name,
    )(part)


def _pack(arrays):
    parts = []
    for a in arrays:
        flat = a.reshape(-1).astype(F32)
        pad = (-flat.shape[0]) % (8 * LANES)
        parts.append(jnp.pad(flat, (0, pad)))
    return jnp.concatenate(parts).reshape(-1, LANES)


def _unpack(packed, like):
    flat = packed.reshape(-1)
    out, off = [], 0
    for a in like:
        n = int(np.prod(a.shape))
        out.append(flat[off:off + n].reshape(a.shape))
        off += n + ((-n) % (8 * LANES))
    return out


def kernel(x, rel_bias, norm1_g, w_in, sgu_norm_g, sgu_w, sgu_b, q_norm_g, k_norm_g, sinks, out_norm_a, out_norm_b, w_out, norm2_g, w_gate, w_up, w_down, loss_target, m_rel_bias, m_norm1_g, m_w_in, m_sgu_norm_g, m_sgu_w, m_sgu_b, m_q_norm_g, m_k_norm_g, m_sinks, m_out_norm_a, m_out_norm_b, m_w_out, m_norm2_g, m_w_gate, m_w_up, m_w_down, v_rel_bias, v_norm1_g, v_w_in, v_sgu_norm_g, v_sgu_w, v_sgu_b, v_q_norm_g, v_k_norm_g, v_sinks, v_out_norm_a, v_out_norm_b, v_w_out, v_norm2_g, v_w_gate, v_w_up, v_w_down):
    L, D, n_in = w_in.shape
    S = x.shape[1]
    IN = N_CHIPS * n_in
    n_ff = w_gate.shape[2]
    FF = N_CHIPS * n_ff
    H = sgu_w.shape[1]
    NQ = sinks.shape[1]
    SW, AW = H * HEAD_DIM, NQ * HEAD_DIM
    KVW = (IN - 2 * SW - AW) // 2
    dm = _MixerDims(S, IN, SW, AW, KVW, H, NQ)
    assert sgu_w.shape[2] == BLOCK and q_norm_g.shape[1] == HEAD_DIM and SW + AW == D

    wb = {k: w.astype(BF16) for k, w in (("in", w_in), ("out", w_out), ("gate", w_gate), ("up", w_up), ("down", w_down))}

    def gather(l, which):
        if which == "in":
            return _Gather([wb["in"]], [(0, 0, 1, 0)], [(D, IN)], l)
        if which == "out":
            return _Gather([wb["out"]], [(0, 0, 0, 0)], [(D, D)], l)
        if which == "gu":
            return _Gather([wb["gate"], wb["up"]], [(0, 0, 1, 0), (1, 0, 1, FF)], [(D, 2 * FF)], l)
        return _Gather([wb["down"]], [(0, 0, 0, 0)], [(FF, D)], l)

    nxt = lambda l, *which: [gather(l + 1, w) for w in which] if l + 1 < L else []
    W = {}
    W[0, "in"], W[0, "out"] = _comm_only([gather(0, "in"), gather(0, "out")], name="gather_first_weights")

    gs, gmod_q = _group_consts(SW)
    _, gmod_k = _group_consts(KVW)
    pblk = _same_group()
    onehot = _bucket_onehot()
    rbt = jnp.pad(rel_bias.T, ((0, 0), (0, LANES - NUM_BUCKETS)))
    (bias,) = _matmul(rbt, onehot.T, mode="nn", out_dtype=F32, exact=True, tn=4096, name="bias_table")
    bias = bias.reshape(NQ, BLOCK, 2 * BLOCK)

    def mixer_params(l):
        return [sgu_norm_g[l].reshape(1, SW), sgu_w[l], jnp.repeat(sgu_b[l].T, HEAD_DIM, axis=1),
                out_norm_a[l].reshape(1, SW), jnp.tile(q_norm_g[l], NQ).reshape(1, AW),
                jnp.tile(k_norm_g[l], dm.NKV).reshape(1, KVW), out_norm_b[l].reshape(1, AW), sinks[l], bias, pblk]

    xs = x.reshape(S, D)
    saved = []
    for l in range(L):
        h = _rms_fwd(xs, norm1_g[l].reshape(1, D), name="norm1_fwd")
        z, *got = _matmul(h, W[l, "in"], mode="nn", out_dtype=F32, tn=1792, name="in_proj", comm=nxt(l, "in", "out"))
        if got:
            W[l + 1, "in"], W[l + 1, "out"] = got
        first = [gather(0, "gu"), gather(0, "down")] if l == 0 else []
        mixed, *got = _mixer_fwd(dm, z, mixer_params(l), name="mixer_fwd", comm=first)
        if got:
            W[0, "gu"], W[0, "down"] = got
        (x1,) = _matmul(mixed, W[l, "out"], mode="nn", out_dtype=F32, res=xs, name="out_proj")
        h2 = _rms_fwd(x1, norm2_g[l].reshape(1, D), name="norm2_fwd")
        ab, *got = _matmul(h2, W[l, "gu"], mode="nn", out_dtype=F32, name="gate_up_proj", comm=nxt(l, "gu"))
        if got:
            (W[l + 1, "gu"],) = got
        f = _swiglu_fwd(ab, name="swiglu_fwd")
        x2, *got = _matmul(f, W[l, "down"], mode="nn", out_dtype=F32, res=x1, tk=2816, name="down_proj", comm=nxt(l, "down"))
        if got:
            (W[l + 1, "down"],) = got
        saved.append((xs, h, z, mixed, x1, h2, ab, f))
        xs = x2

    dx, dxb, loss_part = _loss_head(xs, loss_target.reshape(S, D), name="loss_head")
    loss = lax.psum(loss_part[0, 0], ("x", "y", "c"))

    dbias = jnp.zeros((NQ, BLOCK, 2 * BLOCK), F32)
    n_out = D // N_CHIPS
    big_names = ("w_in", "w_out", "w_gate", "w_up", "w_down")
    own = {k: [None] * L for k in big_names}
    got = {k: [None] * L for k in big_names}
    small = {k: [None] * L for k in ("norm1_g", "sgu_norm_g", "sgu_w", "sgu_b", "q_norm_g", "k_norm_g", "sinks",
                                     "out_norm_a", "out_norm_b", "norm2_g")}
    for l in reversed(range(L)):
        xl, h, z, mixed, x1, h2, ab, f = saved[l]
        (df,) = _matmul(dxb, W[l, "down"], mode="nt", out_dtype=F32, tm=512, tn=2816, name="down_proj_dx")
        (g_d,) = _matmul_tn(f, dxb, out_dtype=BF16, tm=1408, tn=1024, tt=2048, name="down_proj_dw")
        dab = _swiglu_bwd(ab, df, name="swiglu_bwd")
        g_gu, got["w_down"][l] = _matmul_tn(h2, dab, out_dtype=BF16, tt=2048, name="gate_up_proj_dw",
                                            comm=[_Scatter([g_d], [(0, 0, 0, n_ff)])])
        dh2, got["w_gate"][l] = _matmul(dab, W[l, "gu"], mode="nt", out_dtype=F32, tn=1024, tk=2816, name="gate_up_proj_dx",
                                        comm=[_Scatter([g_gu], [(0, 1, 0, n_ff)])])
        dx1, dx1b, dg2 = _rms_bwd(x1, norm2_g[l].reshape(1, D), dh2, dx, name="norm2_bwd")
        (dmixed,) = _matmul(dx1b, W[l, "out"], mode="nt", out_dtype=F32, name="out_proj_dx")
        (g_out,) = _matmul_tn(mixed, dx1b, out_dtype=BF16, tt=2048, name="out_proj_dw")
        (dz, d_sg, d_t, d_sb, d_na, d_qg, d_kg, d_nb, d_sk, dbias, got["w_up"][l], got["w_out"][l]) = _mixer_bwd(
            dm, z, dmixed, dbias, mixer_params(l), (gs, gmod_q, gmod_k), name="mixer_bwd",
            comm=[_Scatter([g_gu, g_out], [(0, 1, FF, n_ff), (1, 0, 0, n_out)])])
        (g_in,) = _matmul_tn(h, dz, out_dtype=BF16, tn=896, tt=2048, name="in_proj_dw")
        dh, got["w_in"][l] = _matmul(dz, W[l, "in"], mode="nt", out_dtype=F32, tk=1792, name="in_proj_dx",
                                     comm=[_Scatter([g_in], [(0, 1, 0, n_in)])])
        dx, dxb, dg1 = _rms_bwd(xl, norm1_g[l].reshape(1, D), dh, dx1, name="norm1_bwd")
        own["w_in"][l], own["w_out"][l] = (g_in, 1, 0, n_in), (g_out, 0, 0, n_out)
        own["w_gate"][l], own["w_up"][l], own["w_down"][l] = (g_gu, 1, 0, n_ff), (g_gu, 1, FF, n_ff), (g_d, 0, 0, n_ff)
        small["norm1_g"][l] = dg1.reshape(D)
        small["norm2_g"][l] = dg2.reshape(D)
        small["sgu_norm_g"][l] = d_sg.reshape(H, HEAD_DIM)
        small["sgu_w"][l] = d_t
        small["sgu_b"][l] = d_sb[:, :H].T
        small["q_norm_g"][l] = d_qg[0, :HEAD_DIM]
        small["k_norm_g"][l] = d_kg[0, :HEAD_DIM]
        small["sinks"][l] = d_sk[0, :NQ]
        small["out_norm_a"][l] = d_na.reshape(SW)
        small["out_norm_b"][l] = d_nb.reshape(AW)
    grad_x = dx.reshape(1, S, D)
    (d_rb,) = _matmul(dbias.reshape(NQ, BLOCK * 2 * BLOCK), onehot, mode="nn", out_dtype=F32, exact=True, tk=4096,
                      name="rel_bias_grad")
    d_rel_bias = d_rb[:, :NUM_BUCKETS].T

    chip = (2 * lax.axis_index("x") + lax.axis_index("y")).astype(jnp.int32).reshape(1)
    sums = {}
    for nm in big_names:
        sums[nm] = None
        for l in range(L):
            g, axis, base, width = own[nm][l]
            sums[nm] = _sum_chip_partials(g, axis, base, width, got[nm][l], chip, l, L, sums[nm], name="sum_chip_partials")
    theirs = _sibling_swap([[sums[nm]] for nm in big_names], name="sibling_swap")
    big = {}
    for nm, w, m, v, sib in zip(big_names, (w_in, w_out, w_gate, w_up, w_down),
                                (m_w_in, m_w_out, m_w_gate, m_w_up, m_w_down),
                                (v_w_in, v_w_out, v_w_gate, v_w_up, v_w_down), theirs):
        two = lambda a: a.reshape(-1, a.shape[-1])
        res = _adamw(two(w), [sums[nm], two(sib)], two(m), two(v), name="adamw_" + nm)
        big[nm] = [r.reshape(w.shape) for r in res]

    names_small = ("rel_bias", "norm1_g", "sgu_norm_g", "sgu_w", "sgu_b", "q_norm_g", "k_norm_g", "sinks", "out_norm_a",
                   "out_norm_b", "norm2_g")
    w_small = dict(rel_bias=rel_bias, norm1_g=norm1_g, sgu_norm_g=sgu_norm_g, sgu_w=sgu_w, sgu_b=sgu_b, q_norm_g=q_norm_g,
                   k_norm_g=k_norm_g, sinks=sinks, out_norm_a=out_norm_a, out_norm_b=out_norm_b, norm2_g=norm2_g)
    m_small = dict(rel_bias=m_rel_bias, norm1_g=m_norm1_g, sgu_norm_g=m_sgu_norm_g, sgu_w=m_sgu_w, sgu_b=m_sgu_b,
                   q_norm_g=m_q_norm_g, k_norm_g=m_k_norm_g, sinks=m_sinks, out_norm_a=m_out_norm_a,
                   out_norm_b=m_out_norm_b, norm2_g=m_norm2_g)
    v_small = dict(rel_bias=v_rel_bias, norm1_g=v_norm1_g, sgu_norm_g=v_sgu_norm_g, sgu_w=v_sgu_w, sgu_b=v_sgu_b,
                   q_norm_g=v_q_norm_g, k_norm_g=v_k_norm_g, sinks=v_sinks, out_norm_a=v_out_norm_a,
                   out_norm_b=v_out_norm_b, norm2_g=v_norm2_g)
    g_small = {k: jnp.stack(vs) for k, vs in small.items()}
    g_small["rel_bias"] = d_rel_bias
    like = [w_small[k] for k in names_small]
    gathered = _all_gather_rows(_pack([g_small[k] for k in names_small]), name="gather_small_grads")
    res = _adamw(_pack(like), [gathered[i] for i in range(N_DEV)], _pack([m_small[k] for k in names_small]),
                 _pack([v_small[k] for k in names_small]), name="adamw_small")
    sm = {k: vals for k, vals in zip(names_small, zip(*[_unpack(r, like) for r in res]))}

    order = ("rel_bias", "norm1_g", "w_in", "sgu_norm_g", "sgu_w", "sgu_b", "q_norm_g", "k_norm_g", "sinks", "out_norm_a",
             "out_norm_b", "w_out", "norm2_g", "w_gate", "w_up", "w_down")
    pick = lambda k, i: big[k][i] if k in big else sm[k][i]
    outs = [loss, grad_x]
    for i in range(4):
        outs += [pick(k, i) for k in order]
    return tuple(outs)
```

```python
import functools
import math

import numpy as np

import jax
import jax.numpy as jnp
from jax import lax
from jax.experimental import pallas as pl
from jax.experimental.pallas import tpu as pltpu

F32 = jnp.float32
BF16 = jnp.bfloat16
MESH = pl.DeviceIdType.MESH
ANY = pl.BlockSpec(memory_space=pl.ANY)

HEAD_DIM = 64
BLOCK = 128
NUM_BUCKETS = 32
MAX_DISTANCE = 128
EPS = 1e-6
NEG_INF = -1e30
ADAM_LR, ADAM_B1, ADAM_B2, ADAM_EPS, ADAM_WD, ADAM_STEP = 0.001, 0.9, 0.999, 1e-08, 0.01, 10

LANES = 128
VMEM_LIMIT = 56 * 1024 * 1024
N_CHIPS = 4
N_DEV = 8


def _cparams(sem=None):
    return pltpu.CompilerParams(dimension_semantics=sem, vmem_limit_bytes=VMEM_LIMIT)


def _tile(dim, target, align=LANES):
    best = None
    for t in range(align, min(dim, target) + 1, align):
        if dim % t == 0:
            best = t
    return best if best is not None else dim


def _mesh_pos():
    return lax.axis_index("x"), lax.axis_index("y"), lax.axis_index("c")


def _other_chips(x, y):
    return [(1 - x, y), (x, 1 - y), (1 - x, 1 - y)]


def _slab(ref, axis, start, size, half=None):
    if axis == 1:
        rows = ref.shape[0]
        r = pl.ds(0, rows) if half is None else pl.ds(pl.multiple_of(half * (rows // 2), 16), rows // 2)
        return ref.at[r, pl.ds(pl.multiple_of(start, LANES), size)]
    if half is None:
        return ref.at[pl.ds(pl.multiple_of(start, 16), size), :]
    return ref.at[pl.ds(pl.multiple_of(start + half * (size // 2), 16), size // 2), :]


def _remote(src, dst, send_sem, recv_sem, to):
    return pltpu.make_async_remote_copy(src_ref=src, dst_ref=dst, send_sem=send_sem, recv_sem=recv_sem,
                                        device_id=to, device_id_type=MESH)


class _Gather:
    def __init__(self, shards, streams, out_shapes, layer):
        self.ins, self.streams, self.layer = list(shards), streams, layer
        self.outs = [jax.ShapeDtypeStruct(s, BF16) for s in out_shapes]
        ns = len(streams)
        self.sems = [pltpu.SemaphoreType.DMA((ns, 3))] * 4 + [pltpu.SemaphoreType.DMA((ns,))]

    def _sent(self, srcs, outs, sems):
        send, recv, _, _, local = sems
        x, y, c = _mesh_pos()
        me_j = 2 * x + y
        own, sends = [], []
        for s, (si, oi, axis, base) in enumerate(self.streams):
            src, out = srcs[si].at[self.layer], outs[oi]
            width = src.shape[axis]
            own.append(pltpu.make_async_copy(src, _slab(out, axis, base + me_j * width, width), local.at[s]))
            half_rows = src.shape[0] // 2
            mine = src.at[pl.ds(pl.multiple_of(c * half_rows, 16), half_rows), :]
            for k, (px, py) in enumerate(_other_chips(x, y)):
                sends.append(_remote(mine, _slab(out, axis, base + me_j * width, width, half=c),
                                     send.at[s, k], recv.at[s, k], (px, py, c)))
        return own, sends

    def start(self, srcs, outs, sems):
        own, sends = self._sent(srcs, outs, sems)
        for cp in own + sends:
            cp.start()

    def finish(self, srcs, outs, sems):
        send, recv, fsend, frecv, _ = sems
        x, y, c = _mesh_pos()
        sib = (x, y, 1 - c)
        forwards, fwd_arrivals = [], []
        for s, (si, oi, axis, base) in enumerate(self.streams):
            out = outs[oi]
            width = srcs[si].shape[1 + axis]
            for k, (px, py) in enumerate(_other_chips(x, y)):
                start = base + (2 * px + py) * width
                got = _slab(out, axis, start, width, half=c)
                _remote(got, got, send.at[s, k], recv.at[s, k], (px, py, c)).wait_recv()
                fwd = _remote(got, got, fsend.at[s, k], frecv.at[s, k], sib)
                fwd.start()
                forwards.append(fwd)
                theirs = _slab(out, axis, start, width, half=1 - c)
                fwd_arrivals.append(_remote(theirs, theirs, fsend.at[s, k], frecv.at[s, k], sib))
        for a in fwd_arrivals:
            a.wait_recv()
        for cp in forwards:
            cp.wait_send()
        own, sends = self._sent(srcs, outs, sems)
        for cp in sends:
            cp.wait_send()
        for cp in own:
            cp.wait()


class _Scatter:
    def __init__(self, grads, streams):
        self.ins, self.streams = list(grads), streams
        shard = lambda g, axis, width: (width, g.shape[1]) if axis == 0 else (g.shape[0], width)
        self.outs = [jax.ShapeDtypeStruct((3,) + shard(grads[gi], axis, width), BF16) for gi, axis, base, width in streams]
        self.sems = [pltpu.SemaphoreType.DMA((len(streams), 3))] * 2

    def _copies(self, srcs, outs, sems):
        send, recv = sems
        x, y, c = _mesh_pos()
        copies = []
        for s, (gi, axis, base, width) in enumerate(self.streams):
            for k, (px, py) in enumerate(_other_chips(x, y)):
                copies.append(_remote(_slab(srcs[gi], axis, base + (2 * px + py) * width, width), outs[s].at[k],
                                      send.at[s, k], recv.at[s, k], (px, py, c)))
        return copies

    def start(self, srcs, outs, sems):
        for cp in self._copies(srcs, outs, sems):
            cp.start()

    def finish(self, srcs, outs, sems):
        for cp in self._copies(srcs, outs, sems):
            cp.wait()


class _Swap:
    def __init__(self, arrays):
        self.ins = list(arrays)
        self.outs = [jax.ShapeDtypeStruct(a.shape, a.dtype) for a in arrays]
        self.sems = [pltpu.SemaphoreType.DMA((len(arrays),))] * 2

    def _copies(self, srcs, outs, sems):
        send, recv = sems
        x, y, c = _mesh_pos()
        return [_remote(srcs[s], outs[s], send.at[s], recv.at[s], (x, y, 1 - c)) for s in range(len(srcs))]

    def start(self, srcs, outs, sems):
        for cp in self._copies(srcs, outs, sems):
            cp.start()

    def finish(self, srcs, outs, sems):
        for cp in self._copies(srcs, outs, sems):
            cp.wait()


class _AllGather:
    def __init__(self, part):
        self.ins = [part]
        self.outs = [jax.ShapeDtypeStruct((N_DEV,) + part.shape, part.dtype)]
        self.sems = [pltpu.SemaphoreType.DMA((7,)), pltpu.SemaphoreType.DMA((7,)), pltpu.SemaphoreType.DMA(())]

    def _first(self, srcs, outs, sems):
        (x_ref,), (out,), (send, recv, local) = srcs, outs, sems
        x, y, c = _mesh_pos()
        mine = out.at[4 * x + 2 * y + c]
        own = pltpu.make_async_copy(x_ref, mine, local)
        sends = [_remote(x_ref, mine, send.at[0], recv.at[0], (x, y, 1 - c))]
        sends += [_remote(x_ref, mine, send.at[1 + k], recv.at[1 + k], (px, py, c)) for k, (px, py) in enumerate(_other_chips(x, y))]
        return own, sends

    def start(self, srcs, outs, sems):
        own, sends = self._first(srcs, outs, sems)
        own.start()
        for cp in sends:
            cp.start()

    def finish(self, srcs, outs, sems):
        (out,), (send, recv, local) = outs, sems
        x, y, c = _mesh_pos()
        me, sib = (x, y, c), (x, y, 1 - c)
        slot = lambda px, py, pc: out.at[4 * px + 2 * py + pc]
        passed = []
        for k, (px, py) in enumerate(_other_chips(x, y)):
            blk = slot(px, py, c)
            _remote(blk, blk, send.at[1 + k], recv.at[1 + k], me).wait_recv()
            fwd = _remote(blk, blk, send.at[4 + k], recv.at[4 + k], sib)
            fwd.start()
            passed.append(fwd)
        blk = slot(x, y, 1 - c)
        _remote(blk, blk, send.at[0], recv.at[0], me).wait_recv()
        for k, (px, py) in enumerate(_other_chips(x, y)):
            blk = slot(px, py, 1 - c)
            _remote(blk, blk, send.at[4 + k], recv.at[4 + k], me).wait_recv()
        own, sends = self._first(srcs, outs, sems)
        for cp in sends + passed:
            cp.wait_send()
        own.wait()


def _call(body, *, grid, in_specs, out_specs, out_shape, args, name, scratch=(), sem=None, comm=()):
    out_shape, out_specs = tuple(out_shape), tuple(out_specs)
    n_in, n_out, n_scr = len(in_specs), len(out_shape), len(scratch)
    c_ins = [a for u in comm for a in u.ins]
    c_outs = [o for u in comm for o in u.outs]
    c_sems = [s for u in comm for s in u.sems]

    def wrapped(*refs):
        ins, rest = refs[:n_in], refs[n_in:]
        cin, rest = rest[:len(c_ins)], rest[len(c_ins):]
        outs, rest = rest[:n_out], rest[n_out:]
        cout, rest = rest[:len(c_outs)], rest[len(c_outs):]
        scr, csem = rest[:n_scr], rest[n_scr:]

        def each(fn_name):
            i = o = s = 0
            for u in comm:
                getattr(u, fn_name)(cin[i:i + len(u.ins)], cout[o:o + len(u.outs)], csem[s:s + len(u.sems)])
                i, o, s = i + len(u.ins), o + len(u.outs), s + len(u.sems)

        if comm:
            pids = [pl.program_id(d) for d in range(len(grid))]
            first = functools.reduce(jnp.logical_and, [p == 0 for p in pids])
            last = functools.reduce(jnp.logical_and, [p == g - 1 for p, g in zip(pids, grid)])
            pl.when(first)(lambda: each("start"))
        body(*ins, *outs, *scr)
        if comm:
            pl.when(last)(lambda: each("finish"))

    if comm:
        sem = ("arbitrary",) * len(grid)
    return pl.pallas_call(
        wrapped, out_shape=out_shape + tuple(c_outs), grid=grid,
        in_specs=list(in_specs) + [ANY] * len(c_ins), out_specs=out_specs + tuple([ANY] * len(c_outs)),
        scratch_shapes=list(scratch) + c_sems, compiler_params=_cparams(sem), name=name,
    )(*args, *c_ins)


def _comm_only(comm, *, name):
    def body(tick_ref):
        tick_ref[...] = jnp.zeros(tick_ref.shape, tick_ref.dtype)

    outs = _call(body, grid=(1,), in_specs=[], out_specs=[pl.BlockSpec((8, LANES), lambda i: (0, 0))],
                 out_shape=[jax.ShapeDtypeStruct((8, LANES), F32)], args=[], name=name, comm=comm)
    return outs[1:]


def _split3(x):
    hi = x.astype(BF16)
    r1 = x - hi.astype(F32)
    mid = r1.astype(BF16)
    lo = (r1 - mid.astype(F32)).astype(BF16)
    return hi, mid, lo


def _dot3(x, g):
    hi, mid, lo = _split3(x)
    d = lambda a: jnp.dot(a, g, preferred_element_type=F32)
    return d(hi) + d(mid) + d(lo)


GROUP_TILE = 256


def _group_sum(x, pblk):
    hi = x.astype(BF16)
    lo = (x - hi.astype(F32)).astype(BF16)
    cols = []
    for b in range(x.shape[1] // GROUP_TILE):
        sl = slice(GROUP_TILE * b, GROUP_TILE * (b + 1))
        cols.append(jnp.dot(hi[:, sl], pblk, preferred_element_type=F32) + jnp.dot(lo[:, sl], pblk, preferred_element_type=F32))
    return cols[0] if len(cols) == 1 else jnp.concatenate(cols, axis=1)


def _matmul(a, b, *, mode, out_dtype, name, res=None, exact=False, tm=1024, tn=1024, tk=2048, b_col0=0, comm=()):
    M, K = a.shape
    N = b.shape[1] if mode == "nn" else b.shape[0]
    tm, tn, tk = _tile(M, tm, 8 if M < LANES else LANES), _tile(N, tn), _tile(K, tk)
    nk = K // tk
    dn = (((1,), (0,)), ((), ())) if mode == "nn" else (((1,), (1,)), ((), ()))

    def body(*refs):
        a_ref, b_ref = refs[0], refs[1]
        r_ref = refs[2] if res is not None else None
        o_ref = refs[3] if res is not None else refs[2]
        if exact:
            part = _dot3(a_ref[...], b_ref[...])
        else:
            part = lax.dot_general(a_ref[...].astype(BF16), b_ref[...].astype(BF16), dn, preferred_element_type=F32)

        def finish(total):
            if r_ref is not None:
                total = r_ref[...] + total
            o_ref[...] = total.astype(o_ref.dtype)

        if nk == 1:
            finish(part)
        else:
            acc = refs[-1]
            k = pl.program_id(2)

            @pl.when(k == 0)
            def _():
                acc[...] = part

            @pl.when(k > 0)
            def _():
                acc[...] += part

            @pl.when(k == nk - 1)
            def _():
                finish(acc[...])

    if mode == "nn":
        b_spec = pl.BlockSpec((tk, tn), lambda j, i, k: (k, j))
    else:
        assert b_col0 % tk == 0
        b_spec = pl.BlockSpec((tn, tk), lambda j, i, k: (j, k + b_col0 // tk))
    in_specs = [pl.BlockSpec((tm, tk), lambda j, i, k: (i, k)), b_spec]
    args = [a, b]
    if res is not None:
        in_specs.append(pl.BlockSpec((tm, tn), lambda j, i, k: (i, j)))
        args.append(res)
    return _call(
        body, grid=(N // tn, M // tm, nk), in_specs=in_specs,
        out_specs=[pl.BlockSpec((tm, tn), lambda j, i, k: (i, j))], out_shape=[jax.ShapeDtypeStruct((M, N), out_dtype)],
        scratch=[pltpu.VMEM((tm, tn), F32)] if nk > 1 else [], sem=("parallel", "parallel", "arbitrary"),
        args=args, name=name, comm=comm)


def _matmul_tn(a, b, *, out_dtype, name, tm=2048, tn=1024, tt=1024, comm=()):
    T, Mo = a.shape
    N = b.shape[1]
    tm, tn, tt = _tile(Mo, tm), _tile(N, tn), _tile(T, tt)
    nt = T // tt

    def body(a_ref, b_ref, o_ref, acc):
        t = pl.program_id(2)
        part = lax.dot_general(a_ref[...].astype(BF16), b_ref[...].astype(BF16), (((0,), (0,)), ((), ())),
                               preferred_element_type=F32)

        @pl.when(t == 0)
        def _():
            acc[...] = part

        @pl.when(t > 0)
        def _():
            acc[...] += part

        @pl.when(t == nt - 1)
        def _():
            o_ref[...] = acc[...].astype(o_ref.dtype)

    return _call(
        body, grid=(Mo // tm, N // tn, nt),
        in_specs=[pl.BlockSpec((tt, tm), lambda i, j, t: (t, i)), pl.BlockSpec((tt, tn), lambda i, j, t: (t, j))],
        out_specs=[pl.BlockSpec((tm, tn), lambda i, j, t: (i, j))], out_shape=[jax.ShapeDtypeStruct((Mo, N), out_dtype)],
        scratch=[pltpu.VMEM((tm, tn), F32)], sem=("parallel", "parallel", "arbitrary"), args=[a, b], name=name, comm=comm)


def _rms_fwd(x, g, *, name, tr=256):
    R, D = x.shape
    tr = _tile(R, tr, 8)

    def body(x_ref, g_ref, o_ref):
        xv = x_ref[...]
        r = lax.rsqrt(jnp.mean(xv * xv, axis=-1, keepdims=True) + EPS)
        o_ref[...] = (xv * r * g_ref[...]).astype(o_ref.dtype)

    return pl.pallas_call(
        body, out_shape=jax.ShapeDtypeStruct((R, D), BF16), grid=(R // tr,),
        in_specs=[pl.BlockSpec((tr, D), lambda i: (i, 0)), pl.BlockSpec((1, D), lambda i: (0, 0))],
        out_specs=pl.BlockSpec((tr, D), lambda i: (i, 0)),
        compiler_params=_cparams(("parallel",)), name=name,
    )(x, g)


def _rms_bwd(x, g, dh, dres, *, name, tr=256):
    R, D = x.shape
    tr = _tile(R, tr, 8)

    def body(x_ref, g_ref, dh_ref, dres_ref, dx_ref, dxb_ref, dg_ref):
        i = pl.program_id(0)
        xv = x_ref[...]
        r = lax.rsqrt(jnp.mean(xv * xv, axis=-1, keepdims=True) + EPS)
        y = xv * r
        dhv = dh_ref[...]
        dy = dhv * g_ref[...]
        dx = dres_ref[...] + r * (dy - y * jnp.mean(dy * y, axis=-1, keepdims=True))
        dx_ref[...] = dx
        dxb_ref[...] = dx.astype(BF16)
        dg = jnp.sum(dhv * y, axis=0, keepdims=True)

        @pl.when(i == 0)
        def _():
            dg_ref[...] = dg

        @pl.when(i > 0)
        def _():
            dg_ref[...] += dg

    row = pl.BlockSpec((tr, D), lambda i: (i, 0))
    vec = pl.BlockSpec((1, D), lambda i: (0, 0))
    return pl.pallas_call(
        body, out_shape=(jax.ShapeDtypeStruct((R, D), F32), jax.ShapeDtypeStruct((R, D), BF16),
                         jax.ShapeDtypeStruct((1, D), F32)), grid=(R // tr,),
        in_specs=[row, vec, row, row], out_specs=(row, row, vec),
        compiler_params=_cparams(("arbitrary",)), name=name,
    )(x, g, dh, dres)


def _gate_up_swiglu(h, w_gu, *, name, tm=1024, tn=512, comm=()):
    M, K = h.shape
    FF = w_gu.shape[1] // 2
    tm, tn = _tile(M, tm), _tile(FF, tn)

    def body(h_ref, wg_ref, wu_ref, a_ref, b_ref, f_ref):
        hv = h_ref[...]
        a = jnp.dot(hv, wg_ref[...], preferred_element_type=F32)
        b = jnp.dot(hv, wu_ref[...], preferred_element_type=F32)
        a_ref[...] = a
        b_ref[...] = b
        f_ref[...] = (a * jax.nn.sigmoid(a) * b).astype(f_ref.dtype)

    tile = pl.BlockSpec((tm, tn), lambda j, i: (i, j))
    f32 = jax.ShapeDtypeStruct((M, FF), F32)
    return _call(
        body, grid=(FF // tn, M // tm),
        in_specs=[pl.BlockSpec((tm, K), lambda j, i: (i, 0)), pl.BlockSpec((K, tn), lambda j, i: (0, j)),
                  pl.BlockSpec((K, tn), lambda j, i: (0, j + FF // tn))],
        out_specs=[tile, tile, tile], out_shape=[f32, f32, jax.ShapeDtypeStruct((M, FF), BF16)],
        sem=("parallel", "parallel"), args=[h, w_gu, w_gu], name=name, comm=comm)


def _down_dx_swiglu_bwd(dy, w_d, a, b, *, name, tm=512, tn=1408):
    M, K = dy.shape
    FF = w_d.shape[0]
    tm, tn = _tile(M, tm), _tile(FF, tn)

    def body(dy_ref, w_ref, a_ref, b_ref, da_ref, db_ref):
        df = lax.dot_general(dy_ref[...], w_ref[...], (((1,), (1,)), ((), ())), preferred_element_type=F32)
        av = a_ref[...]
        s = jax.nn.sigmoid(av)
        da_ref[...] = (df * b_ref[...] * (s * (1.0 + av * (1.0 - s)))).astype(da_ref.dtype)
        db_ref[...] = (df * (av * s)).astype(db_ref.dtype)

    tile = pl.BlockSpec((tm, tn), lambda j, i: (i, j))
    out = jax.ShapeDtypeStruct((M, FF), BF16)
    return pl.pallas_call(
        body, out_shape=(out, out), grid=(FF // tn, M // tm),
        in_specs=[pl.BlockSpec((tm, K), lambda j, i: (i, 0)), pl.BlockSpec((tn, K), lambda j, i: (j, 0)), tile, tile],
        out_specs=(tile, tile), compiler_params=_cparams(("parallel", "parallel")), name=name,
    )(dy, w_d, a, b)


def _loss_head(y, target, *, name, tr=256):
    R, D = y.shape
    tr = _tile(R, tr, 8)

    def body(y_ref, t_ref, dy_ref, dyb_ref, l_ref):
        i = pl.program_id(0)
        e = y_ref[...] - t_ref[...]
        dy = e * (1.0 / D)
        dy_ref[...] = dy
        dyb_ref[...] = dy.astype(BF16)
        part = 0.5 * jnp.sum(jnp.mean(e * e, axis=-1, keepdims=True), axis=0, keepdims=True)
        part = jnp.broadcast_to(part, (8, LANES))

        @pl.when(i == 0)
        def _():
            l_ref[...] = part

        @pl.when(i > 0)
        def _():
            l_ref[...] += part

    row = pl.BlockSpec((tr, D), lambda i: (i, 0))
    return pl.pallas_call(
        body, out_shape=(jax.ShapeDtypeStruct((R, D), F32), jax.ShapeDtypeStruct((R, D), BF16),
                         jax.ShapeDtypeStruct((8, LANES), F32)), grid=(R // tr,),
        in_specs=[row, row], out_specs=(row, row, pl.BlockSpec((8, LANES), lambda i: (0, 0))),
        compiler_params=_cparams(("arbitrary",)), name=name,
    )(y, target)


def _gelu(x):
    return 0.5 * x * (1.0 + lax.erf(x * math.sqrt(0.5)))


def _gelu_grad(x):
    return 0.5 * (1.0 + lax.erf(x * math.sqrt(0.5))) + x * jnp.exp(-0.5 * x * x) * (1.0 / math.sqrt(2.0 * math.pi))


def _group_consts(width):
    lane = np.arange(width)
    col = np.arange(LANES)
    grp = (lane[:, None] // HEAD_DIM == col[None, :]).astype(np.float32)
    mod = ((lane[:, None] % HEAD_DIM == col[None, :]) & (col[None, :] < HEAD_DIM)).astype(np.float32)
    return jnp.asarray(grp, BF16), jnp.asarray(mod, BF16)


def _same_group():
    lane = np.arange(GROUP_TILE) // HEAD_DIM
    return jnp.asarray((lane[:, None] == lane[None, :]).astype(np.float32), BF16)


def _bucket_onehot():
    qi = np.arange(BLOCK)[:, None]
    kj = np.arange(2 * BLOCK)[None, :]
    n = np.maximum(qi + BLOCK - kj, 0)
    max_exact = NUM_BUCKETS // 2
    nf = np.maximum(n, 1).astype(np.float32)
    large = max_exact + (np.log(nf / np.float32(max_exact)) / np.float32(math.log(MAX_DISTANCE / max_exact))
                         * np.float32(NUM_BUCKETS - max_exact)).astype(np.int32)
    large = np.minimum(large, NUM_BUCKETS - 1)
    bucket = jnp.asarray(np.where(n < max_exact, n, large).reshape(-1).astype(np.int32))
    return (bucket[:, None] == jnp.arange(LANES, dtype=jnp.int32)[None, :]).astype(BF16)


class _MixerDims:
    def __init__(self, S, IN, SW, AW, KVW, H, NQ):
        self.S, self.IN, self.SW, self.AW, self.KVW, self.H, self.NQ = S, IN, SW, AW, KVW, H, NQ
        self.NKV = KVW // HEAD_DIM
        self.GROUP = NQ // self.NKV
        self.nb = S // BLOCK
        self.koff = 2 * SW + AW
        self.voff = self.koff + KVW
        assert SW % LANES == 0 and AW % LANES == 0 and KVW % LANES == 0 and self.GROUP % 2 == 0
        assert self.koff % (2 * KVW) == 0 and IN == self.voff + KVW and S % BLOCK == 0


def _mixer_block(dm, n, z, kvp, prm):
    SW, AW, KVW, H, NQ = dm.SW, dm.AW, dm.KVW, dm.H, dm.NQ
    lane = lax.broadcasted_iota(jnp.int32, (BLOCK, LANES), 1)
    lo = lane < HEAD_DIM
    row = lax.broadcasted_iota(jnp.int32, (BLOCK, BLOCK), 0)
    col = lax.broadcasted_iota(jnp.int32, (BLOCK, BLOCK), 1)
    tril = row >= col
    pblk = prm["pblk"][...]
    inv = 1.0 / HEAD_DIM

    def group_rsqrt(x):
        return lax.rsqrt(_group_sum(x * x, pblk) * inv + EPS)

    zu, zv = z[:, :SW], z[:, SW:2 * SW]
    u, v = _gelu(zu), _gelu(zv)
    rv = group_rsqrt(v)
    vn = v * rv * prm["sgu_g"][...]
    vnb = vn.astype(BF16)
    tmats, gate_blocks = [], []
    for p in range(H // 2):
        blk = slice(LANES * p, LANES * (p + 1))
        t0 = jnp.where(tril, prm["sgu_w"][2 * p], 0.0).astype(BF16)
        t1 = jnp.where(tril, prm["sgu_w"][2 * p + 1], 0.0).astype(BF16)
        tmats += [t0, t1]
        g0 = jnp.dot(t0, vnb[:, blk], preferred_element_type=F32)
        g1 = jnp.dot(t1, vnb[:, blk], preferred_element_type=F32)
        gate_blocks.append(jnp.where(lo, g0, g1) + prm["sgu_bias"][:, blk])
    gate = jnp.concatenate(gate_blocks, axis=1)
    outa = u * gate
    ra = lax.rsqrt(jnp.mean(outa * outa, axis=-1, keepdims=True) + EPS)

    q = z[:, 2 * SW:2 * SW + AW]
    kcat = jnp.concatenate([kvp[:, :KVW], z[:, dm.koff:dm.koff + KVW]], axis=0)
    vcat = jnp.concatenate([kvp[:, KVW:], z[:, dm.voff:dm.voff + KVW]], axis=0)
    rq = group_rsqrt(q)
    rk = group_rsqrt(kcat)
    qn = q * rq * prm["q_g"][...]
    kn = kcat * rk * prm["k_g"][...]
    knb, vcb = kn.astype(BF16), vcat.astype(BF16)
    qi = lax.broadcasted_iota(jnp.int32, (BLOCK, 2 * BLOCK), 0)
    kj = lax.broadcasted_iota(jnp.int32, (BLOCK, 2 * BLOCK), 1)
    valid = (kj > qi) & (kj <= qi + BLOCK) & ((n > 0) | (kj >= BLOCK))
    scale = 1.0 / math.sqrt(HEAD_DIM)
    heads = []
    out_blocks = []
    for hq in range(NQ):
        mb, e = hq // 2, hq % 2
        kv = hq // dm.GROUP
        kb, ek = kv // 2, kv % 2
        qblk = qn[:, LANES * mb:LANES * (mb + 1)]
        if e != ek:
            qblk = pltpu.roll(qblk, HEAD_DIM, 1)
        half = lo if ek == 0 else jnp.logical_not(lo)
        qm = jnp.where(half, qblk, 0.0).astype(BF16)
        kblk = knb[:, LANES * kb:LANES * (kb + 1)]
        vblk = vcb[:, LANES * kb:LANES * (kb + 1)]
        s = lax.dot_general(qm, kblk, (((1,), (1,)), ((), ())), preferred_element_type=F32) * scale + prm["bias"][hq]
        s = jnp.where(valid, s, NEG_INF)
        sink = prm["sinks"][hq]
        mx = jnp.maximum(jnp.max(s, axis=-1, keepdims=True), sink)
        ex = jnp.exp(s - mx)
        den = jnp.sum(ex, axis=-1, keepdims=True) + jnp.exp(sink - mx)
        pr = ex / den
        psink = jnp.exp(sink - mx) / den
        prb = pr.astype(BF16)
        r_h = jnp.dot(prb, vblk, preferred_element_type=F32)
        if e != ek:
            r_h = pltpu.roll(r_h, HEAD_DIM, 1)
        heads.append(dict(qm=qm, kblk=kblk, vblk=vblk, pr=pr, prb=prb, psink=psink, half=half, mb=mb, e=e, ek=ek, kb=kb))
        if e == 1:
            out_blocks.append(jnp.where(lo, prev_r, r_h))
        prev_r = r_h
    outb = jnp.concatenate(out_blocks, axis=1)
    rb = lax.rsqrt(jnp.mean(outb * outb, axis=-1, keepdims=True) + EPS)
    return dict(lo=lo, tril=tril, pblk=pblk, zu=zu, zv=zv, u=u, v=v, rv=rv, vnb=vnb, tmats=tmats, gate=gate,
                outa=outa, ra=ra, q=q, kcat=kcat, rq=rq, rk=rk, heads=heads, outb=outb, rb=rb, scale=scale)


_MIXER_PARAMS = ("sgu_g", "sgu_w", "sgu_bias", "norm_a", "q_g", "k_g", "norm_b", "sinks", "bias", "pblk")


def _mixer_param_specs(dm, idx):
    SW, AW, KVW = dm.SW, dm.AW, dm.KVW
    full = lambda shape: pl.BlockSpec(shape, lambda n: (0,) * len(shape))
    return [full((1, SW)), full((dm.H, BLOCK, BLOCK)), full((BLOCK, SW)), full((1, SW)), full((1, AW)), full((1, KVW)),
            full((1, AW)), pl.BlockSpec(memory_space=pltpu.SMEM), full((dm.NQ, BLOCK, 2 * BLOCK)),
            full((GROUP_TILE, GROUP_TILE))]


def _mixer_fwd(dm, z, params, *, name, comm=()):
    nb = dm.nb

    def body(z_ref, kvp_ref, *rest):
        prm = dict(zip(_MIXER_PARAMS, rest[:len(_MIXER_PARAMS)]))
        o_ref = rest[len(_MIXER_PARAMS)]
        n = pl.program_id(0)
        c = _mixer_block(dm, n, z_ref[...], kvp_ref[...], prm)
        o_ref[:, :dm.SW] = (c["outa"] * c["ra"] * prm["norm_a"][...]).astype(o_ref.dtype)
        o_ref[:, dm.SW:] = (c["outb"] * c["rb"] * prm["norm_b"][...]).astype(o_ref.dtype)

    kvblk = dm.koff // (2 * dm.KVW)
    in_specs = [pl.BlockSpec((BLOCK, dm.IN), lambda n: (n, 0)),
                pl.BlockSpec((BLOCK, 2 * dm.KVW), lambda n: (jnp.maximum(n - 1, 0), kvblk))] + _mixer_param_specs(dm, None)
    return _call(
        body, out_shape=[jax.ShapeDtypeStruct((dm.S, dm.SW + dm.AW), BF16)], grid=(nb,),
        in_specs=in_specs, out_specs=[pl.BlockSpec((BLOCK, dm.SW + dm.AW), lambda n: (n, 0))],
        sem=("arbitrary",), args=[z, z, *params], name=name, comm=comm)


def _mixer_bwd(dm, z, dmixed, dbias_in, params, gmats, *, name, comm=()):
    SW, AW, KVW, H, NQ, nb, IN = dm.SW, dm.AW, dm.KVW, dm.H, dm.NQ, dm.nb, dm.IN
    QW = 2 * SW + AW
    NP = len(_MIXER_PARAMS)

    def body(z_ref, kvp_ref, dm_ref, dbin_ref, *rest):
        prm = dict(zip(_MIXER_PARAMS, rest[:NP]))
        gs_ref, gmq_ref, gmk_ref = rest[NP:NP + 3]
        (dz_ref, dsg_ref, dt_ref, dsb_ref, dna_ref, dqg_ref, dkg_ref, dnb_ref, dsk_ref, dbias_ref) = rest[NP + 3:NP + 13]
        hold, tmpkv, newkv, carry, accb, accq, acck = rest[NP + 13:]
        n = pl.program_id(0)

        @pl.when(n == 0)
        def _():
            for r in (dsg_ref, dt_ref, dna_ref, dnb_ref, dsk_ref, accb, accq, acck):
                r[...] = jnp.zeros(r.shape, r.dtype)
            dbias_ref[...] = dbin_ref[...]

        @pl.when(n < nb)
        def _():
            c = _mixer_block(dm, n, z_ref[...], kvp_ref[...], prm)
            lo = c["lo"]
            dmx = dm_ref[...]
            inv = 1.0 / HEAD_DIM

            def rms_bwd_full(dy_scaled, y, r):
                return r * (dy_scaled - y * jnp.mean(dy_scaled * y, axis=-1, keepdims=True))

            def group_mean_b(x):
                return _group_sum(x, c["pblk"]) * inv

            dma = dmx[:, :SW]
            ya = c["outa"] * c["ra"]
            dna_ref[...] += jnp.sum(dma * ya, axis=0, keepdims=True)
            douta = rms_bwd_full(dma * prm["norm_a"][...], ya, c["ra"])
            du = douta * c["gate"]
            dgate = douta * c["u"]
            accb[...] += dgate
            dgb16 = dgate.astype(BF16)
            dvn_blocks = []
            for p in range(H // 2):
                blk = slice(LANES * p, LANES * (p + 1))
                dg = dgate[:, blk]
                d0 = jnp.where(lo, dg, 0.0).astype(BF16)
                d1 = jnp.where(lo, 0.0, dg).astype(BF16)
                vb = c["vnb"][:, blk]
                nt = lambda a, b: lax.dot_general(a, b, (((1,), (1,)), ((), ())), preferred_element_type=F32)
                tn = lambda a, b: lax.dot_general(a, b, (((0,), (0,)), ((), ())), preferred_element_type=F32)
                dt_ref[2 * p] += nt(d0, vb)
                dt_ref[2 * p + 1] += nt(d1, vb)
                dvn_blocks.append(jnp.where(lo, tn(c["tmats"][2 * p], dgb16[:, blk]), tn(c["tmats"][2 * p + 1], dgb16[:, blk])))
            dvn = jnp.concatenate(dvn_blocks, axis=1)
            yv = c["v"] * c["rv"]
            dsg_ref[...] += jnp.sum(dvn * yv, axis=0, keepdims=True)
            dyv = dvn * prm["sgu_g"][...]
            dv = c["rv"] * (dyv - yv * group_mean_b(dyv * yv))
            dzu = du * _gelu_grad(c["zu"])
            dzv = dv * _gelu_grad(c["zv"])

            dmb = dmx[:, SW:]
            yb = c["outb"] * c["rb"]
            dnb_ref[...] += jnp.sum(dmb * yb, axis=0, keepdims=True)
            doutb = rms_bwd_full(dmb * prm["norm_b"][...], yb, c["rb"])
            lane1 = lax.broadcasted_iota(jnp.int32, (1, LANES), 1)
            dqn_blocks = [None] * (AW // LANES)
            dkn_blocks = [None] * (KVW // LANES)
            dvc_blocks = [None] * (KVW // LANES)
            dsink_vec = jnp.zeros((1, LANES), F32)
            add = lambda old, new: new if old is None else old + new
            for hq, hd in enumerate(c["heads"]):
                mb, e, ek, kb, half = hd["mb"], hd["e"], hd["ek"], hd["kb"], hd["half"]
                dr = doutb[:, LANES * mb:LANES * (mb + 1)]
                if e != ek:
                    dr = pltpu.roll(dr, HEAD_DIM, 1)
                drm = jnp.where(half, dr, 0.0).astype(BF16)
                dp = lax.dot_general(drm, hd["vblk"], (((1,), (1,)), ((), ())), preferred_element_type=F32)
                dvc_blocks[kb] = add(dvc_blocks[kb], lax.dot_general(hd["prb"], drm, (((0,), (0,)), ((), ())),
                                                                     preferred_element_type=F32))
                rowdot = jnp.sum(hd["pr"] * dp, axis=-1, keepdims=True)
                ds = hd["pr"] * (dp - rowdot)
                dsink = jnp.sum(-hd["psink"] * rowdot, axis=0, keepdims=True)
                dsink_vec = dsink_vec + jnp.where(lane1 == hq, dsink, 0.0)
                dbias_ref[hq] += ds
                dsb = (ds * c["scale"]).astype(BF16)
                dqm = jnp.dot(dsb, hd["kblk"], preferred_element_type=F32)
                dqm = jnp.where(half, dqm, 0.0)
                if e != ek:
                    dqm = pltpu.roll(dqm, HEAD_DIM, 1)
                dqn_blocks[mb] = add(dqn_blocks[mb], dqm)
                dkn_blocks[kb] = add(dkn_blocks[kb], lax.dot_general(dsb, hd["qm"], (((0,), (0,)), ((), ())),
                                                                     preferred_element_type=F32))
            dsk_ref[...] += dsink_vec
            dqn = jnp.concatenate(dqn_blocks, axis=1)
            dkn = jnp.concatenate(dkn_blocks, axis=1)
            dvc = jnp.concatenate(dvc_blocks, axis=1)
            yq = c["q"] * c["rq"]
            accq[...] += jnp.sum(dqn * yq, axis=0, keepdims=True)
            dyq = dqn * prm["q_g"][...]
            dq = c["rq"] * (dyq - yq * group_mean_b(dyq * yq))
            yk = c["kcat"] * c["rk"]
            acck[...] += jnp.sum(dkn * yk, axis=0, keepdims=True)
            dyk = dkn * prm["k_g"][...]
            dk = c["rk"] * (dyk - yk * group_mean_b(dyk * yk))

            slot = n % 2
            hold[slot, :, :SW] = dzu
            hold[slot, :, SW:2 * SW] = dzv
            hold[slot, :, 2 * SW:] = dq
            tmpkv[:, :KVW] = dk[:BLOCK]
            tmpkv[:, KVW:] = dvc[:BLOCK]
            newkv[:, :KVW] = dk[BLOCK:]
            newkv[:, KVW:] = dvc[BLOCK:]

        @pl.when(n >= 1)
        def _():
            dz_ref[:, :QW] = hold[(n - 1) % 2].astype(dz_ref.dtype)

        @pl.when((n >= 1) & (n < nb))
        def _():
            dz_ref[:, QW:] = (carry[...] + tmpkv[...]).astype(dz_ref.dtype)

        @pl.when(n == nb)
        def _():
            dz_ref[:, QW:] = carry[...].astype(dz_ref.dtype)
            row = lax.broadcasted_iota(jnp.int32, (BLOCK, BLOCK), 0)
            col = lax.broadcasted_iota(jnp.int32, (BLOCK, BLOCK), 1)
            for h in range(H):
                dt_ref[h] = jnp.where(row >= col, dt_ref[h], 0.0)
            dsb_ref[...] = _dot3(accb[...], gs_ref[...])
            dqg_ref[...] = _dot3(accq[...], gmq_ref[...])
            dkg_ref[...] = _dot3(acck[...], gmk_ref[...])

        @pl.when(n < nb)
        def _():
            carry[...] = newkv[...]

    kvblk = dm.koff // (2 * KVW)
    clamp = lambda n: jnp.minimum(n, nb - 1)
    full = lambda shape: pl.BlockSpec(shape, lambda n: (0,) * len(shape))
    in_specs = [pl.BlockSpec((BLOCK, IN), lambda n: (clamp(n), 0)),
                pl.BlockSpec((BLOCK, 2 * KVW), lambda n: (jnp.maximum(clamp(n) - 1, 0), kvblk)),
                pl.BlockSpec((BLOCK, SW + AW), lambda n: (clamp(n), 0)),
                full((NQ, BLOCK, 2 * BLOCK))] + _mixer_param_specs(dm, None) + [full((SW, LANES)), full((AW, LANES)),
                                                                                full((KVW, LANES))]
    out_shape = (jax.ShapeDtypeStruct((dm.S, IN), BF16),
                 jax.ShapeDtypeStruct((1, SW), F32), jax.ShapeDtypeStruct((H, BLOCK, BLOCK), F32),
                 jax.ShapeDtypeStruct((BLOCK, LANES), F32), jax.ShapeDtypeStruct((1, SW), F32),
                 jax.ShapeDtypeStruct((1, LANES), F32), jax.ShapeDtypeStruct((1, LANES), F32),
                 jax.ShapeDtypeStruct((1, AW), F32), jax.ShapeDtypeStruct((1, LANES), F32),
                 jax.ShapeDtypeStruct((NQ, BLOCK, 2 * BLOCK), F32))
    out_specs = (pl.BlockSpec((BLOCK, IN), lambda n: (jnp.maximum(n - 1, 0), 0)),
                 full((1, SW)), full((H, BLOCK, BLOCK)), full((BLOCK, LANES)), full((1, SW)), full((1, LANES)),
                 full((1, LANES)), full((1, AW)), full((1, LANES)), full((NQ, BLOCK, 2 * BLOCK)))
    scratch = [pltpu.VMEM((2, BLOCK, QW), F32), pltpu.VMEM((BLOCK, 2 * KVW), F32), pltpu.VMEM((BLOCK, 2 * KVW), F32),
               pltpu.VMEM((BLOCK, 2 * KVW), F32), pltpu.VMEM((BLOCK, SW), F32), pltpu.VMEM((1, AW), F32),
               pltpu.VMEM((1, KVW), F32)]
    return _call(
        body, out_shape=out_shape, grid=(nb + 1,), in_specs=in_specs, out_specs=out_specs, scratch=scratch,
        sem=("arbitrary",), args=[z, z, dmixed, dbias_in, *params, *gmats], name=name, comm=comm)


def _adamw(w, gparts, m, v, *, name, tr=256, comm=()):
    R, C = w.shape
    tr = _tile(R, max(8, min(tr, (1 << 18) // C)), 8)
    ng = len(gparts)
    bc1 = 1.0 - ADAM_B1 ** ADAM_STEP
    bc2 = 1.0 - ADAM_B2 ** ADAM_STEP

    def body(w_ref, *rest):
        g_refs, (m_ref, v_ref, go_ref, d_ref, mo_ref, vo_ref) = rest[:ng], rest[ng:]
        g = g_refs[0][...].astype(F32)
        for r in g_refs[1:]:
            g = g + r[...].astype(F32)
        mn = ADAM_B1 * m_ref[...] + (1.0 - ADAM_B1) * g
        vn = ADAM_B2 * v_ref[...] + (1.0 - ADAM_B2) * jnp.square(g)
        m_hat = mn / bc1
        v_hat = vn / bc2
        go_ref[...] = g
        d_ref[...] = -ADAM_LR * (m_hat / (jnp.sqrt(v_hat) + ADAM_EPS) + ADAM_WD * w_ref[...])
        mo_ref[...] = mn
        vo_ref[...] = vn

    blk = pl.BlockSpec((tr, C), lambda i: (i, 0))
    out = jax.ShapeDtypeStruct((R, C), F32)
    return _call(body, out_shape=[out] * 4, grid=(R // tr,), in_specs=[blk] * (3 + ng), out_specs=[blk] * 4,
                 sem=("parallel",), args=[w, *gparts, m, v], name=name, comm=comm)


def _sum_chip_partials(g, axis, base, width, got, chip, layer, n_layers, stacked, *, name, tr=256):
    _, R, C = got.shape
    tr = _tile(R, max(8, min(tr, (1 << 18) // C)), 8)
    assert base % width == 0 and (C == width if axis == 1 else R == width)

    def body(chip_ref, own_ref, g0_ref, g1_ref, g2_ref, *rest):
        o_ref = rest[-1]
        o_ref[...] = ((own_ref[...].astype(F32) + g0_ref[...].astype(F32)) + g1_ref[...].astype(F32)) + g2_ref[...].astype(F32)

    if axis == 1:
        own_spec = pl.BlockSpec((tr, C), lambda i, j: (i, base // width + j[0]))
    else:
        own_spec = pl.BlockSpec((tr, C), lambda i, j: ((base // width + j[0]) * (R // tr) + i, 0))
    part = lambda k: pl.BlockSpec((None, tr, C), lambda i, j: (k, i, 0))
    prev = [] if stacked is None else [stacked]
    return pl.pallas_call(
        body, out_shape=jax.ShapeDtypeStruct((n_layers * R, C), F32),
        grid_spec=pltpu.PrefetchScalarGridSpec(
            num_scalar_prefetch=1, grid=(R // tr,), in_specs=[own_spec, part(0), part(1), part(2)] + [ANY] * len(prev),
            out_specs=pl.BlockSpec((tr, C), lambda i, j: (layer * (R // tr) + i, 0))),
        input_output_aliases={5: 0} if prev else {},
        compiler_params=_cparams(("parallel",)), name=name,
    )(chip, g, got, got, got, *prev)


def _pack(arrays):
    parts = []
    for a in arrays:
        flat = a.reshape(-1).astype(F32)
        pad = (-flat.shape[0]) % (8 * LANES)
        parts.append(jnp.pad(flat, (0, pad)))
    return jnp.concatenate(parts).reshape(-1, LANES)


def _unpack(packed, like):
    flat = packed.reshape(-1)
    out, off = [], 0
    for a in like:
        n = int(np.prod(a.shape))
        out.append(flat[off:off + n].reshape(a.shape))
        off += n + ((-n) % (8 * LANES))
    return out


def kernel(x, rel_bias, norm1_g, w_in, sgu_norm_g, sgu_w, sgu_b, q_norm_g, k_norm_g, sinks, out_norm_a, out_norm_b, w_out, norm2_g, w_gate, w_up, w_down, loss_target, m_rel_bias, m_norm1_g, m_w_in, m_sgu_norm_g, m_sgu_w, m_sgu_b, m_q_norm_g, m_k_norm_g, m_sinks, m_out_norm_a, m_out_norm_b, m_w_out, m_norm2_g, m_w_gate, m_w_up, m_w_down, v_rel_bias, v_norm1_g, v_w_in, v_sgu_norm_g, v_sgu_w, v_sgu_b, v_q_norm_g, v_k_norm_g, v_sinks, v_out_norm_a, v_out_norm_b, v_w_out, v_norm2_g, v_w_gate, v_w_up, v_w_down):
    L, D, n_in = w_in.shape
    S = x.shape[1]
    IN = N_CHIPS * n_in
    n_ff = w_gate.shape[2]
    FF = N_CHIPS * n_ff
    H = sgu_w.shape[1]
    NQ = sinks.shape[1]
    SW, AW = H * HEAD_DIM, NQ * HEAD_DIM
    KVW = (IN - 2 * SW - AW) // 2
    dm = _MixerDims(S, IN, SW, AW, KVW, H, NQ)
    assert sgu_w.shape[2] == BLOCK and q_norm_g.shape[1] == HEAD_DIM and SW + AW == D

    wb = {k: w.astype(BF16) for k, w in (("in", w_in), ("out", w_out), ("gate", w_gate), ("up", w_up), ("down", w_down))}

    def gather(l, which):
        if which == "in":
            return _Gather([wb["in"]], [(0, 0, 1, 0)], [(D, IN)], l)
        if which == "out":
            return _Gather([wb["out"]], [(0, 0, 0, 0)], [(D, D)], l)
        if which == "gu":
            return _Gather([wb["gate"], wb["up"]], [(0, 0, 1, 0), (1, 0, 1, FF)], [(D, 2 * FF)], l)
        return _Gather([wb["down"]], [(0, 0, 0, 0)], [(FF, D)], l)

    nxt = lambda l, *which: [gather(l + 1, w) for w in which] if l + 1 < L else []
    W = {}
    W[0, "in"], W[0, "out"] = _comm_only([gather(0, "in"), gather(0, "out")], name="gather_first_weights")

    gs, gmod_q = _group_consts(SW)
    _, gmod_k = _group_consts(KVW)
    pblk = _same_group()
    onehot = _bucket_onehot()
    rbt = jnp.pad(rel_bias.T, ((0, 0), (0, LANES - NUM_BUCKETS)))
    (bias,) = _matmul(rbt, onehot.T, mode="nn", out_dtype=F32, exact=True, tn=4096, name="bias_table")
    bias = bias.reshape(NQ, BLOCK, 2 * BLOCK)

    def mixer_params(l):
        return [sgu_norm_g[l].reshape(1, SW), sgu_w[l], jnp.repeat(sgu_b[l].T, HEAD_DIM, axis=1),
                out_norm_a[l].reshape(1, SW), jnp.tile(q_norm_g[l], NQ).reshape(1, AW),
                jnp.tile(k_norm_g[l], dm.NKV).reshape(1, KVW), out_norm_b[l].reshape(1, AW), sinks[l], bias, pblk]

    xs = x.reshape(S, D)
    saved = []
    for l in range(L):
        h = _rms_fwd(xs, norm1_g[l].reshape(1, D), name="norm1_fwd")
        z, *got = _matmul(h, W[l, "in"], mode="nn", out_dtype=F32, tn=1792, name="in_proj", comm=nxt(l, "in", "out"))
        if got:
            W[l + 1, "in"], W[l + 1, "out"] = got
        first = [gather(0, "gu"), gather(0, "down")] if l == 0 else []
        mixed, *got = _mixer_fwd(dm, z, mixer_params(l), name="mixer_fwd", comm=first)
        if got:
            W[0, "gu"], W[0, "down"] = got
        (x1,) = _matmul(mixed, W[l, "out"], mode="nn", out_dtype=F32, res=xs, name="out_proj")
        h2 = _rms_fwd(x1, norm2_g[l].reshape(1, D), name="norm2_fwd")
        a, b, f, *got = _gate_up_swiglu(h2, W[l, "gu"], name="gate_up_swiglu", comm=nxt(l, "gu"))
        if got:
            (W[l + 1, "gu"],) = got
        x2, *got = _matmul(f, W[l, "down"], mode="nn", out_dtype=F32, res=x1, tk=2816, name="down_proj", comm=nxt(l, "down"))
        if got:
            (W[l + 1, "down"],) = got
        saved.append((xs, h, z, mixed, x1, h2, a, b, f))
        xs = x2

    dx, dxb, loss_part = _loss_head(xs, loss_target.reshape(S, D), name="loss_head")
    loss = lax.psum(loss_part[0, 0], ("x", "y", "c"))

    dbias = jnp.zeros((NQ, BLOCK, 2 * BLOCK), F32)
    n_out = D // N_CHIPS
    big_names = ("w_in", "w_out", "w_gate", "w_up", "w_down")
    own = {k: [None] * L for k in big_names}
    got = {k: [None] * L for k in big_names}
    small = {k: [None] * L for k in ("norm1_g", "sgu_norm_g", "sgu_w", "sgu_b", "q_norm_g", "k_norm_g", "sinks",
                                     "out_norm_a", "out_norm_b", "norm2_g")}
    for l in reversed(range(L)):
        xl, h, z, mixed, x1, h2, a, b, f = saved[l]
        da, db = _down_dx_swiglu_bwd(dxb, W[l, "down"], a, b, name="down_dx_swiglu_bwd")
        (g_d,) = _matmul_tn(f, dxb, out_dtype=BF16, tm=1408, tn=1024, tt=2048, name="down_proj_dw")
        g_gate, got["w_down"][l] = _matmul_tn(h2, da, out_dtype=BF16, tn=512, tt=2048, name="gate_proj_dw",
                                              comm=[_Scatter([g_d], [(0, 0, 0, n_ff)])])
        (g_up,) = _matmul_tn(h2, db, out_dtype=BF16, tn=512, tt=2048, name="up_proj_dw")
        dh2, got["w_gate"][l] = _matmul(da, W[l, "gu"], mode="nt", out_dtype=F32, tk=2816, name="gate_proj_dx",
                                        comm=[_Scatter([g_gate], [(0, 1, 0, n_ff)])])
        dh2, got["w_up"][l] = _matmul(db, W[l, "gu"], mode="nt", out_dtype=F32, tk=2816, b_col0=FF, res=dh2,
                                      name="up_proj_dx", comm=[_Scatter([g_up], [(0, 1, 0, n_ff)])])
        dx1, dx1b, dg2 = _rms_bwd(x1, norm2_g[l].reshape(1, D), dh2, dx, name="norm2_bwd")
        (dmixed,) = _matmul(dx1b, W[l, "out"], mode="nt", out_dtype=F32, name="out_proj_dx")
        (g_out,) = _matmul_tn(mixed, dx1b, out_dtype=BF16, tt=2048, name="out_proj_dw")
        (dz, d_sg, d_t, d_sb, d_na, d_qg, d_kg, d_nb, d_sk, dbias, got["w_out"][l]) = _mixer_bwd(
            dm, z, dmixed, dbias, mixer_params(l), (gs, gmod_q, gmod_k), name="mixer_bwd",
            comm=[_Scatter([g_out], [(0, 0, 0, n_out)])])
        (g_in,) = _matmul_tn(h, dz, out_dtype=BF16, tn=896, tt=2048, name="in_proj_dw")
        dh, got["w_in"][l] = _matmul(dz, W[l, "in"], mode="nt", out_dtype=F32, tk=1792, name="in_proj_dx",
                                     comm=[_Scatter([g_in], [(0, 1, 0, n_in)])])
        dx, dxb, dg1 = _rms_bwd(xl, norm1_g[l].reshape(1, D), dh, dx1, name="norm1_bwd")
        own["w_in"][l], own["w_out"][l] = (g_in, 1, 0, n_in), (g_out, 0, 0, n_out)
        own["w_gate"][l], own["w_up"][l], own["w_down"][l] = (g_gate, 1, 0, n_ff), (g_up, 1, 0, n_ff), (g_d, 0, 0, n_ff)
        small["norm1_g"][l] = dg1.reshape(D)
        small["norm2_g"][l] = dg2.reshape(D)
        small["sgu_norm_g"][l] = d_sg.reshape(H, HEAD_DIM)
        small["sgu_w"][l] = d_t
        small["sgu_b"][l] = d_sb[:, :H].T
        small["q_norm_g"][l] = d_qg[0, :HEAD_DIM]
        small["k_norm_g"][l] = d_kg[0, :HEAD_DIM]
        small["sinks"][l] = d_sk[0, :NQ]
        small["out_norm_a"][l] = d_na.reshape(SW)
        small["out_norm_b"][l] = d_nb.reshape(AW)
    grad_x = dx.reshape(1, S, D)
    (d_rb,) = _matmul(dbias.reshape(NQ, BLOCK * 2 * BLOCK), onehot, mode="nn", out_dtype=F32, exact=True, tk=4096,
                      name="rel_bias_grad")
    d_rel_bias = d_rb[:, :NUM_BUCKETS].T

    chip = (2 * lax.axis_index("x") + lax.axis_index("y")).astype(jnp.int32).reshape(1)
    sums = {}
    for nm in big_names:
        sums[nm] = None
        for l in range(L):
            g, axis, base, width = own[nm][l]
            sums[nm] = _sum_chip_partials(g, axis, base, width, got[nm][l], chip, l, L, sums[nm], name="sum_chip_partials")
    g_small = {k: jnp.stack(vs) for k, vs in small.items()}
    g_small["rel_bias"] = d_rel_bias
    names_small = ("rel_bias", "norm1_g", "sgu_norm_g", "sgu_w", "sgu_b", "q_norm_g", "k_norm_g", "sinks", "out_norm_a",
                   "out_norm_b", "norm2_g")
    small_unit = _AllGather(_pack([g_small[k] for k in names_small]))
    wmv = dict(w_in=(w_in, m_w_in, v_w_in), w_out=(w_out, m_w_out, v_w_out), w_gate=(w_gate, m_w_gate, v_w_gate),
               w_up=(w_up, m_w_up, v_w_up), w_down=(w_down, m_w_down, v_w_down))
    sequence = ("w_out", "w_in", "w_down", "w_gate", "w_up")
    (sib,) = _comm_only([_Swap([sums[sequence[0]]])], name="swap_first_sums")
    big = {}
    two = lambda t: t.reshape(-1, t.shape[-1])
    for i, nm in enumerate(sequence):
        w, m, v = wmv[nm]
        units = [_Swap([sums[sequence[i + 1]]])] if i + 1 < len(sequence) else []
        if nm == "w_down":
            units.append(small_unit)
        *res, = _adamw(two(w), [sums[nm], sib], two(m), two(v), name="adamw_" + nm, comm=units)
        big[nm] = [r.reshape(w.shape) for r in res[:4]]
        if i + 1 < len(sequence):
            sib = res[4]
        if nm == "w_down":
            gathered = res[5]

    w_small = dict(rel_bias=rel_bias, norm1_g=norm1_g, sgu_norm_g=sgu_norm_g, sgu_w=sgu_w, sgu_b=sgu_b, q_norm_g=q_norm_g,
                   k_norm_g=k_norm_g, sinks=sinks, out_norm_a=out_norm_a, out_norm_b=out_norm_b, norm2_g=norm2_g)
    m_small = dict(rel_bias=m_rel_bias, norm1_g=m_norm1_g, sgu_norm_g=m_sgu_norm_g, sgu_w=m_sgu_w, sgu_b=m_sgu_b,
                   q_norm_g=m_q_norm_g, k_norm_g=m_k_norm_g, sinks=m_sinks, out_norm_a=m_out_norm_a,
                   out_norm_b=m_out_norm_b, norm2_g=m_norm2_g)
    v_small = dict(rel_bias=v_rel_bias, norm1_g=v_norm1_g, sgu_norm_g=v_sgu_norm_g, sgu_w=v_sgu_w, sgu_b=v_sgu_b,
                   q_norm_g=v_q_norm_g, k_norm_g=v_k_norm_g, sinks=v_sinks, out_norm_a=v_out_norm_a,
                   out_norm_b=v_out_norm_b, norm2_g=v_norm2_g)
    like = [w_small[k] for k in names_small]
    res = _adamw(_pack(like), [gathered[i] for i in range(N_DEV)], _pack([m_small[k] for k in names_small]),
                 _pack([v_small[k] for k in names_small]), name="adamw_small")
    sm = {k: vals for k, vals in zip(names_small, zip(*[_unpack(r, like) for r in res]))}

    order = ("rel_bias", "norm1_g", "w_in", "sgu_norm_g", "sgu_w", "sgu_b", "q_norm_g", "k_norm_g", "sinks", "out_norm_a",
             "out_norm_b", "w_out", "norm2_g", "w_gate", "w_up", "w_down")
    pick = lambda k, i: big[k][i] if k in big else sm[k][i]
    outs = [loss, grad_x]
    for i in range(4):
        outs += [pick(k, i) for k in order]
    return tuple(outs)
```

```python
import functools
import math

import numpy as np

import jax
import jax.numpy as jnp
from jax import lax
from jax.experimental import pallas as pl
from jax.experimental.pallas import tpu as pltpu

F32 = jnp.float32
BF16 = jnp.bfloat16
MESH = pl.DeviceIdType.MESH
ANY = pl.BlockSpec(memory_space=pl.ANY)

HEAD_DIM = 64
BLOCK = 128
NUM_BUCKETS = 32
MAX_DISTANCE = 128
EPS = 1e-6
NEG_INF = -1e30
ADAM_LR, ADAM_B1, ADAM_B2, ADAM_EPS, ADAM_WD, ADAM_STEP = 0.001, 0.9, 0.999, 1e-08, 0.01, 10

LANES = 128
VMEM_LIMIT = 56 * 1024 * 1024
N_CHIPS = 4
N_DEV = 8


def _cparams(sem=None):
    return pltpu.CompilerParams(dimension_semantics=sem, vmem_limit_bytes=VMEM_LIMIT)


def _tile(dim, target, align=LANES):
    best = None
    for t in range(align, min(dim, target) + 1, align):
        if dim % t == 0:
            best = t
    return best if best is not None else dim


def _mesh_pos():
    return lax.axis_index("x"), lax.axis_index("y"), lax.axis_index("c")


def _other_chips(x, y):
    return [(1 - x, y), (x, 1 - y), (1 - x, 1 - y)]


def _slab(ref, axis, start, size, half=None):
    if axis == 1:
        rows = ref.shape[0]
        r = pl.ds(0, rows) if half is None else pl.ds(pl.multiple_of(half * (rows // 2), 16), rows // 2)
        return ref.at[r, pl.ds(pl.multiple_of(start, LANES), size)]
    if half is None:
        return ref.at[pl.ds(pl.multiple_of(start, 16), size), :]
    return ref.at[pl.ds(pl.multiple_of(start + half * (size // 2), 16), size // 2), :]


def _remote(src, dst, send_sem, recv_sem, to):
    return pltpu.make_async_remote_copy(src_ref=src, dst_ref=dst, send_sem=send_sem, recv_sem=recv_sem,
                                        device_id=to, device_id_type=MESH)


class _Gather:
    def __init__(self, shards, streams, out_shapes, layer):
        self.ins, self.streams, self.layer = list(shards), streams, layer
        self.outs = [jax.ShapeDtypeStruct(s, BF16) for s in out_shapes]
        ns = len(streams)
        self.sems = [pltpu.SemaphoreType.DMA((ns, 3))] * 4 + [pltpu.SemaphoreType.DMA((ns,))]

    def _sent(self, srcs, outs, sems):
        send, recv, _, _, local = sems
        x, y, c = _mesh_pos()
        me_j = 2 * x + y
        own, sends = [], []
        for s, (si, oi, axis, base, stride) in enumerate(self.streams):
            src, out = srcs[si].at[self.layer], outs[oi]
            width = src.shape[axis]
            own.append(pltpu.make_async_copy(src, _slab(out, axis, base + me_j * stride, width), local.at[s]))
            half_rows = src.shape[0] // 2
            mine = src.at[pl.ds(pl.multiple_of(c * half_rows, 16), half_rows), :]
            for k, (px, py) in enumerate(_other_chips(x, y)):
                sends.append(_remote(mine, _slab(out, axis, base + me_j * stride, width, half=c),
                                     send.at[s, k], recv.at[s, k], (px, py, c)))
        return own, sends

    def start(self, srcs, outs, sems):
        own, sends = self._sent(srcs, outs, sems)
        for cp in own + sends:
            cp.start()

    def finish(self, srcs, outs, sems):
        send, recv, fsend, frecv, _ = sems
        x, y, c = _mesh_pos()
        sib = (x, y, 1 - c)
        forwards, fwd_arrivals = [], []
        for s, (si, oi, axis, base, stride) in enumerate(self.streams):
            out = outs[oi]
            width = srcs[si].shape[1 + axis]
            for k, (px, py) in enumerate(_other_chips(x, y)):
                start = base + (2 * px + py) * stride
                got = _slab(out, axis, start, width, half=c)
                _remote(got, got, send.at[s, k], recv.at[s, k], (px, py, c)).wait_recv()
                fwd = _remote(got, got, fsend.at[s, k], frecv.at[s, k], sib)
                fwd.start()
                forwards.append(fwd)
                theirs = _slab(out, axis, start, width, half=1 - c)
                fwd_arrivals.append(_remote(theirs, theirs, fsend.at[s, k], frecv.at[s, k], sib))
        for a in fwd_arrivals:
            a.wait_recv()
        for cp in forwards:
            cp.wait_send()
        own, sends = self._sent(srcs, outs, sems)
        for cp in sends:
            cp.wait_send()
        for cp in own:
            cp.wait()


class _Scatter:
    def __init__(self, grads, streams):
        self.ins, self.streams = list(grads), streams
        shard = lambda g, axis, width: (width, g.shape[1]) if axis == 0 else (g.shape[0], width)
        self.outs = [jax.ShapeDtypeStruct((3,) + shard(grads[gi], axis, width), BF16) for gi, axis, _, width, _ in streams]
        self.sems = [pltpu.SemaphoreType.DMA((len(streams), 3))] * 2

    def _copies(self, srcs, outs, sems):
        send, recv = sems
        x, y, c = _mesh_pos()
        copies = []
        for s, (gi, axis, base, width, stride) in enumerate(self.streams):
            for k, (px, py) in enumerate(_other_chips(x, y)):
                copies.append(_remote(_slab(srcs[gi], axis, base + (2 * px + py) * stride, width), outs[s].at[k],
                                      send.at[s, k], recv.at[s, k], (px, py, c)))
        return copies

    def start(self, srcs, outs, sems):
        for cp in self._copies(srcs, outs, sems):
            cp.start()

    def finish(self, srcs, outs, sems):
        for cp in self._copies(srcs, outs, sems):
            cp.wait()


class _Swap:
    def __init__(self, arrays):
        self.ins = list(arrays)
        self.outs = [jax.ShapeDtypeStruct(a.shape, a.dtype) for a in arrays]
        self.sems = [pltpu.SemaphoreType.DMA((len(arrays),))] * 2

    def _copies(self, srcs, outs, sems):
        send, recv = sems
        x, y, c = _mesh_pos()
        return [_remote(srcs[s], outs[s], send.at[s], recv.at[s], (x, y, 1 - c)) for s in range(len(srcs))]

    def start(self, srcs, outs, sems):
        for cp in self._copies(srcs, outs, sems):
            cp.start()

    def finish(self, srcs, outs, sems):
        for cp in self._copies(srcs, outs, sems):
            cp.wait()


class _AllGather:
    def __init__(self, part):
        self.ins = [part]
        self.outs = [jax.ShapeDtypeStruct((N_DEV,) + part.shape, part.dtype)]
        self.sems = [pltpu.SemaphoreType.DMA((7,)), pltpu.SemaphoreType.DMA((7,)), pltpu.SemaphoreType.DMA(())]

    def _first(self, srcs, outs, sems):
        (x_ref,), (out,), (send, recv, local) = srcs, outs, sems
        x, y, c = _mesh_pos()
        mine = out.at[4 * x + 2 * y + c]
        own = pltpu.make_async_copy(x_ref, mine, local)
        sends = [_remote(x_ref, mine, send.at[0], recv.at[0], (x, y, 1 - c))]
        sends += [_remote(x_ref, mine, send.at[1 + k], recv.at[1 + k], (px, py, c)) for k, (px, py) in enumerate(_other_chips(x, y))]
        return own, sends

    def start(self, srcs, outs, sems):
        own, sends = self._first(srcs, outs, sems)
        own.start()
        for cp in sends:
            cp.start()

    def finish(self, srcs, outs, sems):
        (out,), (send, recv, local) = outs, sems
        x, y, c = _mesh_pos()
        me, sib = (x, y, c), (x, y, 1 - c)
        slot = lambda px, py, pc: out.at[4 * px + 2 * py + pc]
        passed = []
        for k, (px, py) in enumerate(_other_chips(x, y)):
            blk = slot(px, py, c)
            _remote(blk, blk, send.at[1 + k], recv.at[1 + k], me).wait_recv()
            fwd = _remote(blk, blk, send.at[4 + k], recv.at[4 + k], sib)
            fwd.start()
            passed.append(fwd)
        blk = slot(x, y, 1 - c)
        _remote(blk, blk, send.at[0], recv.at[0], me).wait_recv()
        for k, (px, py) in enumerate(_other_chips(x, y)):
            blk = slot(px, py, 1 - c)
            _remote(blk, blk, send.at[4 + k], recv.at[4 + k], me).wait_recv()
        own, sends = self._first(srcs, outs, sems)
        for cp in sends + passed:
            cp.wait_send()
        own.wait()


def _call(body, *, grid, in_specs, out_specs, out_shape, args, name, scratch=(), sem=None, comm=(), prefetch=None,
          aliases=None):
    out_shape, out_specs = tuple(out_shape), tuple(out_specs)
    n_in, n_out, n_scr = len(in_specs), len(out_shape), len(scratch)
    n_pre = 0 if prefetch is None else 1
    c_ins = [a for u in comm for a in u.ins]
    c_outs = [o for u in comm for o in u.outs]
    c_sems = [s for u in comm for s in u.sems]

    def wrapped(*refs):
        pre, refs = refs[:n_pre], refs[n_pre:]
        ins, rest = refs[:n_in], refs[n_in:]
        cin, rest = rest[:len(c_ins)], rest[len(c_ins):]
        outs, rest = rest[:n_out], rest[n_out:]
        cout, rest = rest[:len(c_outs)], rest[len(c_outs):]
        scr, csem = rest[:n_scr], rest[n_scr:]

        def each(fn_name):
            i = o = s = 0
            for u in comm:
                getattr(u, fn_name)(cin[i:i + len(u.ins)], cout[o:o + len(u.outs)], csem[s:s + len(u.sems)])
                i, o, s = i + len(u.ins), o + len(u.outs), s + len(u.sems)

        if comm:
            pids = [pl.program_id(d) for d in range(len(grid))]
            first = functools.reduce(jnp.logical_and, [p == 0 for p in pids])
            last = functools.reduce(jnp.logical_and, [p == g - 1 for p, g in zip(pids, grid)])
            pl.when(first)(lambda: each("start"))
        body(*pre, *ins, *outs, *scr)
        if comm:
            pl.when(last)(lambda: each("finish"))

    if comm:
        sem = ("arbitrary",) * len(grid)
    all_in = list(in_specs) + [ANY] * len(c_ins)
    all_out = out_specs + tuple([ANY] * len(c_outs))
    all_scr = list(scratch) + c_sems
    kw = {}
    if aliases:
        kw["input_output_aliases"] = {n_pre + i: o for i, o in aliases.items()}
    if prefetch is None:
        kw.update(grid=grid, in_specs=all_in, out_specs=all_out, scratch_shapes=all_scr)
        pre_args = []
    else:
        kw["grid_spec"] = pltpu.PrefetchScalarGridSpec(num_scalar_prefetch=1, grid=grid, in_specs=all_in, out_specs=all_out,
                                                       scratch_shapes=all_scr)
        pre_args = [prefetch]
    return pl.pallas_call(wrapped, out_shape=out_shape + tuple(c_outs), compiler_params=_cparams(sem), name=name, **kw)(
        *pre_args, *args, *c_ins)


def _comm_only(comm, *, name):
    def body(tick_ref):
        tick_ref[...] = jnp.zeros(tick_ref.shape, tick_ref.dtype)

    outs = _call(body, grid=(1,), in_specs=[], out_specs=[pl.BlockSpec((8, LANES), lambda i: (0, 0))],
                 out_shape=[jax.ShapeDtypeStruct((8, LANES), F32)], args=[], name=name, comm=comm)
    return outs[1:]


def _split3(x):
    hi = x.astype(BF16)
    r1 = x - hi.astype(F32)
    mid = r1.astype(BF16)
    lo = (r1 - mid.astype(F32)).astype(BF16)
    return hi, mid, lo


def _dot3(x, g):
    hi, mid, lo = _split3(x)
    d = lambda a: jnp.dot(a, g, preferred_element_type=F32)
    return d(hi) + d(mid) + d(lo)


GROUP_TILE = 256


def _group_sum(x, pblk):
    hi = x.astype(BF16)
    lo = (x - hi.astype(F32)).astype(BF16)
    cols = []
    for b in range(x.shape[1] // GROUP_TILE):
        sl = slice(GROUP_TILE * b, GROUP_TILE * (b + 1))
        cols.append(jnp.dot(hi[:, sl], pblk, preferred_element_type=F32) + jnp.dot(lo[:, sl], pblk, preferred_element_type=F32))
    return cols[0] if len(cols) == 1 else jnp.concatenate(cols, axis=1)


def _matmul(a, b, *, mode, out_dtype, name, res=None, exact=False, tm=1024, tn=1024, tk=2048, b_col0=0, comm=()):
    M, K = a.shape
    N = b.shape[1] if mode == "nn" else b.shape[0]
    tm, tn, tk = _tile(M, tm, 8 if M < LANES else LANES), _tile(N, tn), _tile(K, tk)
    nk = K // tk
    dn = (((1,), (0,)), ((), ())) if mode == "nn" else (((1,), (1,)), ((), ()))

    def body(*refs):
        a_ref, b_ref = refs[0], refs[1]
        r_ref = refs[2] if res is not None else None
        o_ref = refs[3] if res is not None else refs[2]
        if exact:
            part = _dot3(a_ref[...], b_ref[...])
        else:
            part = lax.dot_general(a_ref[...].astype(BF16), b_ref[...].astype(BF16), dn, preferred_element_type=F32)

        def finish(total):
            if r_ref is not None:
                total = r_ref[...] + total
            o_ref[...] = total.astype(o_ref.dtype)

        if nk == 1:
            finish(part)
        else:
            acc = refs[-1]
            k = pl.program_id(2)

            @pl.when(k == 0)
            def _():
                acc[...] = part

            @pl.when(k > 0)
            def _():
                acc[...] += part

            @pl.when(k == nk - 1)
            def _():
                finish(acc[...])

    if mode == "nn":
        b_spec = pl.BlockSpec((tk, tn), lambda j, i, k: (k, j))
    else:
        assert b_col0 % tk == 0
        b_spec = pl.BlockSpec((tn, tk), lambda j, i, k: (j, k + b_col0 // tk))
    in_specs = [pl.BlockSpec((tm, tk), lambda j, i, k: (i, k)), b_spec]
    args = [a, b]
    if res is not None:
        in_specs.append(pl.BlockSpec((tm, tn), lambda j, i, k: (i, j)))
        args.append(res)
    return _call(
        body, grid=(N // tn, M // tm, nk), in_specs=in_specs,
        out_specs=[pl.BlockSpec((tm, tn), lambda j, i, k: (i, j))], out_shape=[jax.ShapeDtypeStruct((M, N), out_dtype)],
        scratch=[pltpu.VMEM((tm, tn), F32)] if nk > 1 else [], sem=("parallel", "parallel", "arbitrary"),
        args=args, name=name, comm=comm)


def _matmul_tn(a, b, *, out_dtype, name, tm=2048, tn=1024, tt=1024, comm=()):
    T, Mo = a.shape
    N = b.shape[1]
    tm, tn, tt = _tile(Mo, tm), _tile(N, tn), _tile(T, tt)
    nt = T // tt

    def body(a_ref, b_ref, o_ref, acc):
        t = pl.program_id(2)
        part = lax.dot_general(a_ref[...].astype(BF16), b_ref[...].astype(BF16), (((0,), (0,)), ((), ())),
                               preferred_element_type=F32)

        @pl.when(t == 0)
        def _():
            acc[...] = part

        @pl.when(t > 0)
        def _():
            acc[...] += part

        @pl.when(t == nt - 1)
        def _():
            o_ref[...] = acc[...].astype(o_ref.dtype)

    return _call(
        body, grid=(Mo // tm, N // tn, nt),
        in_specs=[pl.BlockSpec((tt, tm), lambda i, j, t: (t, i)), pl.BlockSpec((tt, tn), lambda i, j, t: (t, j))],
        out_specs=[pl.BlockSpec((tm, tn), lambda i, j, t: (i, j))], out_shape=[jax.ShapeDtypeStruct((Mo, N), out_dtype)],
        scratch=[pltpu.VMEM((tm, tn), F32)], sem=("parallel", "parallel", "arbitrary"), args=[a, b], name=name, comm=comm)


def _rms_fwd(x, g, *, name, tr=256):
    R, D = x.shape
    tr = _tile(R, tr, 8)

    def body(x_ref, g_ref, o_ref):
        xv = x_ref[...]
        r = lax.rsqrt(jnp.mean(xv * xv, axis=-1, keepdims=True) + EPS)
        o_ref[...] = (xv * r * g_ref[...]).astype(o_ref.dtype)

    return pl.pallas_call(
        body, out_shape=jax.ShapeDtypeStruct((R, D), BF16), grid=(R // tr,),
        in_specs=[pl.BlockSpec((tr, D), lambda i: (i, 0)), pl.BlockSpec((1, D), lambda i: (0, 0))],
        out_specs=pl.BlockSpec((tr, D), lambda i: (i, 0)),
        compiler_params=_cparams(("parallel",)), name=name,
    )(x, g)


def _rms_bwd(x, g, dh, dres, *, name, tr=256):
    R, D = x.shape
    tr = _tile(R, tr, 8)

    def body(x_ref, g_ref, dh_ref, dres_ref, dx_ref, dxb_ref, dg_ref):
        i = pl.program_id(0)
        xv = x_ref[...]
        r = lax.rsqrt(jnp.mean(xv * xv, axis=-1, keepdims=True) + EPS)
        y = xv * r
        dhv = dh_ref[...]
        dy = dhv * g_ref[...]
        dx = dres_ref[...] + r * (dy - y * jnp.mean(dy * y, axis=-1, keepdims=True))
        dx_ref[...] = dx
        dxb_ref[...] = dx.astype(BF16)
        dg = jnp.sum(dhv * y, axis=0, keepdims=True)

        @pl.when(i == 0)
        def _():
            dg_ref[...] = dg

        @pl.when(i > 0)
        def _():
            dg_ref[...] += dg

    row = pl.BlockSpec((tr, D), lambda i: (i, 0))
    vec = pl.BlockSpec((1, D), lambda i: (0, 0))
    return pl.pallas_call(
        body, out_shape=(jax.ShapeDtypeStruct((R, D), F32), jax.ShapeDtypeStruct((R, D), BF16),
                         jax.ShapeDtypeStruct((1, D), F32)), grid=(R // tr,),
        in_specs=[row, vec, row, row], out_specs=(row, row, vec),
        compiler_params=_cparams(("arbitrary",)), name=name,
    )(x, g, dh, dres)


def _gate_up_swiglu(h, w_gu, n_ff, *, name, tm=512, comm=()):
    M, K = h.shape
    nsh = w_gu.shape[1] // (2 * n_ff)
    tm = _tile(M, tm)

    def body(h_ref, w_ref, ab_ref, f_ref):
        ab = jnp.dot(h_ref[...], w_ref[...], preferred_element_type=F32)
        ab_ref[...] = ab
        a, b = ab[:, :n_ff], ab[:, n_ff:]
        f_ref[...] = (a * jax.nn.sigmoid(a) * b).astype(f_ref.dtype)

    return _call(
        body, grid=(nsh, M // tm),
        in_specs=[pl.BlockSpec((tm, K), lambda j, i: (i, 0)), pl.BlockSpec((K, 2 * n_ff), lambda j, i: (0, j))],
        out_specs=[pl.BlockSpec((tm, 2 * n_ff), lambda j, i: (i, j)), pl.BlockSpec((tm, n_ff), lambda j, i: (i, j))],
        out_shape=[jax.ShapeDtypeStruct((M, 2 * nsh * n_ff), F32), jax.ShapeDtypeStruct((M, nsh * n_ff), BF16)],
        sem=("parallel", "parallel"), args=[h, w_gu], name=name, comm=comm)


def _down_dx_swiglu_bwd(dy, w_d, ab, n_ff, *, name, tm=512, comm=()):
    M, K = dy.shape
    nsh = w_d.shape[0] // n_ff
    tm = _tile(M, tm)

    def body(dy_ref, w_ref, ab_ref, dab_ref):
        df = lax.dot_general(dy_ref[...], w_ref[...], (((1,), (1,)), ((), ())), preferred_element_type=F32)
        av = ab_ref[:, :n_ff]
        s = jax.nn.sigmoid(av)
        dab_ref[:, :n_ff] = (df * ab_ref[:, n_ff:] * (s * (1.0 + av * (1.0 - s)))).astype(dab_ref.dtype)
        dab_ref[:, n_ff:] = (df * (av * s)).astype(dab_ref.dtype)

    pair = pl.BlockSpec((tm, 2 * n_ff), lambda j, i: (i, j))
    return _call(
        body, grid=(nsh, M // tm),
        in_specs=[pl.BlockSpec((tm, K), lambda j, i: (i, 0)), pl.BlockSpec((n_ff, K), lambda j, i: (j, 0)), pair],
        out_specs=[pair], out_shape=[jax.ShapeDtypeStruct(ab.shape, BF16)],
        sem=("parallel", "parallel"), args=[dy, w_d, ab], name=name, comm=comm)


def _loss_head(y, target, *, name, tr=256):
    R, D = y.shape
    tr = _tile(R, tr, 8)

    def body(y_ref, t_ref, dy_ref, dyb_ref, l_ref):
        i = pl.program_id(0)
        e = y_ref[...] - t_ref[...]
        dy = e * (1.0 / D)
        dy_ref[...] = dy
        dyb_ref[...] = dy.astype(BF16)
        part = 0.5 * jnp.sum(jnp.mean(e * e, axis=-1, keepdims=True), axis=0, keepdims=True)
        part = jnp.broadcast_to(part, (8, LANES))

        @pl.when(i == 0)
        def _():
            l_ref[...] = part

        @pl.when(i > 0)
        def _():
            l_ref[...] += part

    row = pl.BlockSpec((tr, D), lambda i: (i, 0))
    return pl.pallas_call(
        body, out_shape=(jax.ShapeDtypeStruct((R, D), F32), jax.ShapeDtypeStruct((R, D), BF16),
                         jax.ShapeDtypeStruct((8, LANES), F32)), grid=(R // tr,),
        in_specs=[row, row], out_specs=(row, row, pl.BlockSpec((8, LANES), lambda i: (0, 0))),
        compiler_params=_cparams(("arbitrary",)), name=name,
    )(y, target)


def _gelu(x):
    return 0.5 * x * (1.0 + lax.erf(x * math.sqrt(0.5)))


def _gelu_grad(x):
    return 0.5 * (1.0 + lax.erf(x * math.sqrt(0.5))) + x * jnp.exp(-0.5 * x * x) * (1.0 / math.sqrt(2.0 * math.pi))


def _group_consts(width):
    lane = np.arange(width)
    col = np.arange(LANES)
    grp = (lane[:, None] // HEAD_DIM == col[None, :]).astype(np.float32)
    mod = ((lane[:, None] % HEAD_DIM == col[None, :]) & (col[None, :] < HEAD_DIM)).astype(np.float32)
    return jnp.asarray(grp, BF16), jnp.asarray(mod, BF16)


def _same_group():
    lane = np.arange(GROUP_TILE) // HEAD_DIM
    return jnp.asarray((lane[:, None] == lane[None, :]).astype(np.float32), BF16)


def _bucket_onehot():
    qi = np.arange(BLOCK)[:, None]
    kj = np.arange(2 * BLOCK)[None, :]
    n = np.maximum(qi + BLOCK - kj, 0)
    max_exact = NUM_BUCKETS // 2
    nf = np.maximum(n, 1).astype(np.float32)
    large = max_exact + (np.log(nf / np.float32(max_exact)) / np.float32(math.log(MAX_DISTANCE / max_exact))
                         * np.float32(NUM_BUCKETS - max_exact)).astype(np.int32)
    large = np.minimum(large, NUM_BUCKETS - 1)
    bucket = jnp.asarray(np.where(n < max_exact, n, large).reshape(-1).astype(np.int32))
    return (bucket[:, None] == jnp.arange(LANES, dtype=jnp.int32)[None, :]).astype(BF16)


class _MixerDims:
    def __init__(self, S, IN, SW, AW, KVW, H, NQ):
        self.S, self.IN, self.SW, self.AW, self.KVW, self.H, self.NQ = S, IN, SW, AW, KVW, H, NQ
        self.NKV = KVW // HEAD_DIM
        self.GROUP = NQ // self.NKV
        self.nb = S // BLOCK
        self.koff = 2 * SW + AW
        self.voff = self.koff + KVW
        assert SW % LANES == 0 and AW % LANES == 0 and KVW % LANES == 0 and self.GROUP % 2 == 0
        assert self.koff % (2 * KVW) == 0 and IN == self.voff + KVW and S % BLOCK == 0


def _mixer_block(dm, n, z, kvp, prm):
    SW, AW, KVW, H, NQ = dm.SW, dm.AW, dm.KVW, dm.H, dm.NQ
    lane = lax.broadcasted_iota(jnp.int32, (BLOCK, LANES), 1)
    lo = lane < HEAD_DIM
    row = lax.broadcasted_iota(jnp.int32, (BLOCK, BLOCK), 0)
    col = lax.broadcasted_iota(jnp.int32, (BLOCK, BLOCK), 1)
    tril = row >= col
    pblk = prm["pblk"][...]
    inv = 1.0 / HEAD_DIM

    def group_rsqrt(x):
        return lax.rsqrt(_group_sum(x * x, pblk) * inv + EPS)

    zu, zv = z[:, :SW], z[:, SW:2 * SW]
    u, v = _gelu(zu), _gelu(zv)
    rv = group_rsqrt(v)
    vn = v * rv * prm["sgu_g"][...]
    vnb = vn.astype(BF16)
    tmats, gate_blocks = [], []
    for p in range(H // 2):
        blk = slice(LANES * p, LANES * (p + 1))
        t0 = jnp.where(tril, prm["sgu_w"][2 * p], 0.0).astype(BF16)
        t1 = jnp.where(tril, prm["sgu_w"][2 * p + 1], 0.0).astype(BF16)
        tmats += [t0, t1]
        g0 = jnp.dot(t0, vnb[:, blk], preferred_element_type=F32)
        g1 = jnp.dot(t1, vnb[:, blk], preferred_element_type=F32)
        gate_blocks.append(jnp.where(lo, g0, g1) + prm["sgu_bias"][:, blk])
    gate = jnp.concatenate(gate_blocks, axis=1)
    outa = u * gate
    ra = lax.rsqrt(jnp.mean(outa * outa, axis=-1, keepdims=True) + EPS)

    q = z[:, 2 * SW:2 * SW + AW]
    kcat = jnp.concatenate([kvp[:, :KVW], z[:, dm.koff:dm.koff + KVW]], axis=0)
    vcat = jnp.concatenate([kvp[:, KVW:], z[:, dm.voff:dm.voff + KVW]], axis=0)
    rq = group_rsqrt(q)
    rk = group_rsqrt(kcat)
    qn = q * rq * prm["q_g"][...]
    kn = kcat * rk * prm["k_g"][...]
    knb, vcb = kn.astype(BF16), vcat.astype(BF16)
    qi = lax.broadcasted_iota(jnp.int32, (BLOCK, 2 * BLOCK), 0)
    kj = lax.broadcasted_iota(jnp.int32, (BLOCK, 2 * BLOCK), 1)
    valid = (kj > qi) & (kj <= qi + BLOCK) & ((n > 0) | (kj >= BLOCK))
    scale = 1.0 / math.sqrt(HEAD_DIM)
    heads = []
    out_blocks = []
    for hq in range(NQ):
        mb, e = hq // 2, hq % 2
        kv = hq // dm.GROUP
        kb, ek = kv // 2, kv % 2
        qblk = qn[:, LANES * mb:LANES * (mb + 1)]
        if e != ek:
            qblk = pltpu.roll(qblk, HEAD_DIM, 1)
        half = lo if ek == 0 else jnp.logical_not(lo)
        qm = jnp.where(half, qblk, 0.0).astype(BF16)
        kblk = knb[:, LANES * kb:LANES * (kb + 1)]
        vblk = vcb[:, LANES * kb:LANES * (kb + 1)]
        s = lax.dot_general(qm, kblk, (((1,), (1,)), ((), ())), preferred_element_type=F32) * scale + prm["bias"][hq]
        s = jnp.where(valid, s, NEG_INF)
        sink = prm["sinks"][hq]
        mx = jnp.maximum(jnp.max(s, axis=-1, keepdims=True), sink)
        ex = jnp.exp(s - mx)
        den = jnp.sum(ex, axis=-1, keepdims=True) + jnp.exp(sink - mx)
        pr = ex / den
        psink = jnp.exp(sink - mx) / den
        prb = pr.astype(BF16)
        r_h = jnp.dot(prb, vblk, preferred_element_type=F32)
        if e != ek:
            r_h = pltpu.roll(r_h, HEAD_DIM, 1)
        heads.append(dict(qm=qm, kblk=kblk, vblk=vblk, pr=pr, prb=prb, psink=psink, half=half, mb=mb, e=e, ek=ek, kb=kb))
        if e == 1:
            out_blocks.append(jnp.where(lo, prev_r, r_h))
        prev_r = r_h
    outb = jnp.concatenate(out_blocks, axis=1)
    rb = lax.rsqrt(jnp.mean(outb * outb, axis=-1, keepdims=True) + EPS)
    return dict(lo=lo, tril=tril, pblk=pblk, zu=zu, zv=zv, u=u, v=v, rv=rv, vnb=vnb, tmats=tmats, gate=gate,
                outa=outa, ra=ra, q=q, kcat=kcat, rq=rq, rk=rk, heads=heads, outb=outb, rb=rb, scale=scale)


_MIXER_PARAMS = ("sgu_g", "sgu_w", "sgu_bias", "norm_a", "q_g", "k_g", "norm_b", "sinks", "bias", "pblk")


def _mixer_param_specs(dm, idx):
    SW, AW, KVW = dm.SW, dm.AW, dm.KVW
    full = lambda shape: pl.BlockSpec(shape, lambda n: (0,) * len(shape))
    return [full((1, SW)), full((dm.H, BLOCK, BLOCK)), full((BLOCK, SW)), full((1, SW)), full((1, AW)), full((1, KVW)),
            full((1, AW)), pl.BlockSpec(memory_space=pltpu.SMEM), full((dm.NQ, BLOCK, 2 * BLOCK)),
            full((GROUP_TILE, GROUP_TILE))]


def _mixer_fwd(dm, z, params, *, name, comm=()):
    nb = dm.nb

    def body(z_ref, kvp_ref, *rest):
        prm = dict(zip(_MIXER_PARAMS, rest[:len(_MIXER_PARAMS)]))
        o_ref = rest[len(_MIXER_PARAMS)]
        n = pl.program_id(0)
        c = _mixer_block(dm, n, z_ref[...], kvp_ref[...], prm)
        o_ref[:, :dm.SW] = (c["outa"] * c["ra"] * prm["norm_a"][...]).astype(o_ref.dtype)
        o_ref[:, dm.SW:] = (c["outb"] * c["rb"] * prm["norm_b"][...]).astype(o_ref.dtype)

    kvblk = dm.koff // (2 * dm.KVW)
    in_specs = [pl.BlockSpec((BLOCK, dm.IN), lambda n: (n, 0)),
                pl.BlockSpec((BLOCK, 2 * dm.KVW), lambda n: (jnp.maximum(n - 1, 0), kvblk))] + _mixer_param_specs(dm, None)
    return _call(
        body, out_shape=[jax.ShapeDtypeStruct((dm.S, dm.SW + dm.AW), BF16)], grid=(nb,),
        in_specs=in_specs, out_specs=[pl.BlockSpec((BLOCK, dm.SW + dm.AW), lambda n: (n, 0))],
        sem=("arbitrary",), args=[z, z, *params], name=name, comm=comm)


def _mixer_bwd(dm, z, dmixed, dbias_in, params, gmats, *, name, comm=()):
    SW, AW, KVW, H, NQ, nb, IN = dm.SW, dm.AW, dm.KVW, dm.H, dm.NQ, dm.nb, dm.IN
    QW = 2 * SW + AW
    NP = len(_MIXER_PARAMS)

    def body(z_ref, kvp_ref, dm_ref, dbin_ref, *rest):
        prm = dict(zip(_MIXER_PARAMS, rest[:NP]))
        gs_ref, gmq_ref, gmk_ref = rest[NP:NP + 3]
        (dz_ref, dsg_ref, dt_ref, dsb_ref, dna_ref, dqg_ref, dkg_ref, dnb_ref, dsk_ref, dbias_ref) = rest[NP + 3:NP + 13]
        hold, tmpkv, newkv, carry, accb, accq, acck = rest[NP + 13:]
        n = pl.program_id(0)

        @pl.when(n == 0)
        def _():
            for r in (dsg_ref, dt_ref, dna_ref, dnb_ref, dsk_ref, accb, accq, acck):
                r[...] = jnp.zeros(r.shape, r.dtype)
            dbias_ref[...] = dbin_ref[...]

        @pl.when(n < nb)
        def _():
            c = _mixer_block(dm, n, z_ref[...], kvp_ref[...], prm)
            lo = c["lo"]
            dmx = dm_ref[...]
            inv = 1.0 / HEAD_DIM

            def rms_bwd_full(dy_scaled, y, r):
                return r * (dy_scaled - y * jnp.mean(dy_scaled * y, axis=-1, keepdims=True))

            def group_mean_b(x):
                return _group_sum(x, c["pblk"]) * inv

            dma = dmx[:, :SW]
            ya = c["outa"] * c["ra"]
            dna_ref[...] += jnp.sum(dma * ya, axis=0, keepdims=True)
            douta = rms_bwd_full(dma * prm["norm_a"][...], ya, c["ra"])
            du = douta * c["gate"]
            dgate = douta * c["u"]
            accb[...] += dgate
            dgb16 = dgate.astype(BF16)
            dvn_blocks = []
            for p in range(H // 2):
                blk = slice(LANES * p, LANES * (p + 1))
                dg = dgate[:, blk]
                d0 = jnp.where(lo, dg, 0.0).astype(BF16)
                d1 = jnp.where(lo, 0.0, dg).astype(BF16)
                vb = c["vnb"][:, blk]
                nt = lambda a, b: lax.dot_general(a, b, (((1,), (1,)), ((), ())), preferred_element_type=F32)
                tn = lambda a, b: lax.dot_general(a, b, (((0,), (0,)), ((), ())), preferred_element_type=F32)
                dt_ref[2 * p] += nt(d0, vb)
                dt_ref[2 * p + 1] += nt(d1, vb)
                dvn_blocks.append(jnp.where(lo, tn(c["tmats"][2 * p], dgb16[:, blk]), tn(c["tmats"][2 * p + 1], dgb16[:, blk])))
            dvn = jnp.concatenate(dvn_blocks, axis=1)
            yv = c["v"] * c["rv"]
            dsg_ref[...] += jnp.sum(dvn * yv, axis=0, keepdims=True)
            dyv = dvn * prm["sgu_g"][...]
            dv = c["rv"] * (dyv - yv * group_mean_b(dyv * yv))
            dzu = du * _gelu_grad(c["zu"])
            dzv = dv * _gelu_grad(c["zv"])

            dmb = dmx[:, SW:]
            yb = c["outb"] * c["rb"]
            dnb_ref[...] += jnp.sum(dmb * yb, axis=0, keepdims=True)
            doutb = rms_bwd_full(dmb * prm["norm_b"][...], yb, c["rb"])
            lane1 = lax.broadcasted_iota(jnp.int32, (1, LANES), 1)
            dqn_blocks = [None] * (AW // LANES)
            dkn_blocks = [None] * (KVW // LANES)
            dvc_blocks = [None] * (KVW // LANES)
            dsink_vec = jnp.zeros((1, LANES), F32)
            add = lambda old, new: new if old is None else old + new
            for hq, hd in enumerate(c["heads"]):
                mb, e, ek, kb, half = hd["mb"], hd["e"], hd["ek"], hd["kb"], hd["half"]
                dr = doutb[:, LANES * mb:LANES * (mb + 1)]
                if e != ek:
                    dr = pltpu.roll(dr, HEAD_DIM, 1)
                drm = jnp.where(half, dr, 0.0).astype(BF16)
                dp = lax.dot_general(drm, hd["vblk"], (((1,), (1,)), ((), ())), preferred_element_type=F32)
                dvc_blocks[kb] = add(dvc_blocks[kb], lax.dot_general(hd["prb"], drm, (((0,), (0,)), ((), ())),
                                                                     preferred_element_type=F32))
                rowdot = jnp.sum(hd["pr"] * dp, axis=-1, keepdims=True)
                ds = hd["pr"] * (dp - rowdot)
                dsink = jnp.sum(-hd["psink"] * rowdot, axis=0, keepdims=True)
                dsink_vec = dsink_vec + jnp.where(lane1 == hq, dsink, 0.0)
                dbias_ref[hq] += ds
                dsb = (ds * c["scale"]).astype(BF16)
                dqm = jnp.dot(dsb, hd["kblk"], preferred_element_type=F32)
                dqm = jnp.where(half, dqm, 0.0)
                if e != ek:
                    dqm = pltpu.roll(dqm, HEAD_DIM, 1)
                dqn_blocks[mb] = add(dqn_blocks[mb], dqm)
                dkn_blocks[kb] = add(dkn_blocks[kb], lax.dot_general(dsb, hd["qm"], (((0,), (0,)), ((), ())),
                                                                     preferred_element_type=F32))
            dsk_ref[...] += dsink_vec
            dqn = jnp.concatenate(dqn_blocks, axis=1)
            dkn = jnp.concatenate(dkn_blocks, axis=1)
            dvc = jnp.concatenate(dvc_blocks, axis=1)
            yq = c["q"] * c["rq"]
            accq[...] += jnp.sum(dqn * yq, axis=0, keepdims=True)
            dyq = dqn * prm["q_g"][...]
            dq = c["rq"] * (dyq - yq * group_mean_b(dyq * yq))
            yk = c["kcat"] * c["rk"]
            acck[...] += jnp.sum(dkn * yk, axis=0, keepdims=True)
            dyk = dkn * prm["k_g"][...]
            dk = c["rk"] * (dyk - yk * group_mean_b(dyk * yk))

            slot = n % 2
            hold[slot, :, :SW] = dzu
            hold[slot, :, SW:2 * SW] = dzv
            hold[slot, :, 2 * SW:] = dq
            tmpkv[:, :KVW] = dk[:BLOCK]
            tmpkv[:, KVW:] = dvc[:BLOCK]
            newkv[:, :KVW] = dk[BLOCK:]
            newkv[:, KVW:] = dvc[BLOCK:]

        @pl.when(n >= 1)
        def _():
            dz_ref[:, :QW] = hold[(n - 1) % 2].astype(dz_ref.dtype)

        @pl.when((n >= 1) & (n < nb))
        def _():
            dz_ref[:, QW:] = (carry[...] + tmpkv[...]).astype(dz_ref.dtype)

        @pl.when(n == nb)
        def _():
            dz_ref[:, QW:] = carry[...].astype(dz_ref.dtype)
            row = lax.broadcasted_iota(jnp.int32, (BLOCK, BLOCK), 0)
            col = lax.broadcasted_iota(jnp.int32, (BLOCK, BLOCK), 1)
            for h in range(H):
                dt_ref[h] = jnp.where(row >= col, dt_ref[h], 0.0)
            dsb_ref[...] = _dot3(accb[...], gs_ref[...])
            dqg_ref[...] = _dot3(accq[...], gmq_ref[...])
            dkg_ref[...] = _dot3(acck[...], gmk_ref[...])

        @pl.when(n < nb)
        def _():
            carry[...] = newkv[...]

    kvblk = dm.koff // (2 * KVW)
    clamp = lambda n: jnp.minimum(n, nb - 1)
    full = lambda shape: pl.BlockSpec(shape, lambda n: (0,) * len(shape))
    in_specs = [pl.BlockSpec((BLOCK, IN), lambda n: (clamp(n), 0)),
                pl.BlockSpec((BLOCK, 2 * KVW), lambda n: (jnp.maximum(clamp(n) - 1, 0), kvblk)),
                pl.BlockSpec((BLOCK, SW + AW), lambda n: (clamp(n), 0)),
                full((NQ, BLOCK, 2 * BLOCK))] + _mixer_param_specs(dm, None) + [full((SW, LANES)), full((AW, LANES)),
                                                                                full((KVW, LANES))]
    out_shape = (jax.ShapeDtypeStruct((dm.S, IN), BF16),
                 jax.ShapeDtypeStruct((1, SW), F32), jax.ShapeDtypeStruct((H, BLOCK, BLOCK), F32),
                 jax.ShapeDtypeStruct((BLOCK, LANES), F32), jax.ShapeDtypeStruct((1, SW), F32),
                 jax.ShapeDtypeStruct((1, LANES), F32), jax.ShapeDtypeStruct((1, LANES), F32),
                 jax.ShapeDtypeStruct((1, AW), F32), jax.ShapeDtypeStruct((1, LANES), F32),
                 jax.ShapeDtypeStruct((NQ, BLOCK, 2 * BLOCK), F32))
    out_specs = (pl.BlockSpec((BLOCK, IN), lambda n: (jnp.maximum(n - 1, 0), 0)),
                 full((1, SW)), full((H, BLOCK, BLOCK)), full((BLOCK, LANES)), full((1, SW)), full((1, LANES)),
                 full((1, LANES)), full((1, AW)), full((1, LANES)), full((NQ, BLOCK, 2 * BLOCK)))
    scratch = [pltpu.VMEM((2, BLOCK, QW), F32), pltpu.VMEM((BLOCK, 2 * KVW), F32), pltpu.VMEM((BLOCK, 2 * KVW), F32),
               pltpu.VMEM((BLOCK, 2 * KVW), F32), pltpu.VMEM((BLOCK, SW), F32), pltpu.VMEM((1, AW), F32),
               pltpu.VMEM((1, KVW), F32)]
    return _call(
        body, out_shape=out_shape, grid=(nb + 1,), in_specs=in_specs, out_specs=out_specs, scratch=scratch,
        sem=("arbitrary",), args=[z, z, dmixed, dbias_in, *params, *gmats], name=name, comm=comm)


def _adamw(w, gparts, m, v, *, name, layer=0, stacked=None, tr=256):
    R, C = gparts[0].shape
    tr = _tile(R, max(8, min(tr, (1 << 18) // C)), 8)
    ng = len(gparts)
    bc1 = 1.0 - ADAM_B1 ** ADAM_STEP
    bc2 = 1.0 - ADAM_B2 ** ADAM_STEP

    def body(w_ref, *rest):
        g_refs, (m_ref, v_ref), (go_ref, d_ref, mo_ref, vo_ref) = rest[:ng], rest[ng:ng + 2], rest[-4:]
        g = g_refs[0][...].astype(F32)
        for r in g_refs[1:]:
            g = g + r[...].astype(F32)
        mn = ADAM_B1 * m_ref[...] + (1.0 - ADAM_B1) * g
        vn = ADAM_B2 * v_ref[...] + (1.0 - ADAM_B2) * jnp.square(g)
        m_hat = mn / bc1
        v_hat = vn / bc2
        go_ref[...] = g
        d_ref[...] = -ADAM_LR * (m_hat / (jnp.sqrt(v_hat) + ADAM_EPS) + ADAM_WD * w_ref[...])
        mo_ref[...] = mn
        vo_ref[...] = vn

    here = pl.BlockSpec((tr, C), lambda i: (layer * (R // tr) + i, 0))
    blk = pl.BlockSpec((tr, C), lambda i: (i, 0))
    prev = [] if stacked is None else list(stacked)
    return _call(body, out_shape=[jax.ShapeDtypeStruct(w.shape, F32)] * 4, grid=(R // tr,),
                 in_specs=[here] + [blk] * ng + [here, here] + [ANY] * len(prev), out_specs=[here] * 4,
                 sem=("parallel",), args=[w, *gparts, m, v, *prev], name=name,
                 aliases={3 + ng + q: q for q in range(len(prev))})


def _sum_chip_partials(g, axis, base, width, stride, got, chip, *, name, tr=256, comm=()):
    _, R, C = got.shape
    tr = _tile(R, max(8, min(tr, (1 << 18) // C)), 8)
    assert base % width == 0 and stride % width == 0 and (C == width if axis == 1 else R == width)

    def body(chip_ref, own_ref, g0_ref, g1_ref, g2_ref, o_ref):
        o_ref[...] = ((own_ref[...].astype(F32) + g0_ref[...].astype(F32)) + g1_ref[...].astype(F32)) + g2_ref[...].astype(F32)

    band = lambda j: base // width + j[0] * (stride // width)
    if axis == 1:
        own_spec = pl.BlockSpec((tr, C), lambda i, j: (i, band(j)))
    else:
        own_spec = pl.BlockSpec((tr, C), lambda i, j: (band(j) * (R // tr) + i, 0))
    part = lambda k: pl.BlockSpec((None, tr, C), lambda i, j: (k, i, 0))
    return _call(body, out_shape=[jax.ShapeDtypeStruct((R, C), F32)], grid=(R // tr,),
                 in_specs=[own_spec, part(0), part(1), part(2)], out_specs=[pl.BlockSpec((tr, C), lambda i, j: (i, 0))],
                 sem=("parallel",), args=[g, got, got, got], name=name, prefetch=chip, comm=comm)


def _pack(arrays):
    parts = []
    for a in arrays:
        flat = a.reshape(-1).astype(F32)
        pad = (-flat.shape[0]) % (8 * LANES)
        parts.append(jnp.pad(flat, (0, pad)))
    return jnp.concatenate(parts).reshape(-1, LANES)


def _unpack(packed, like):
    flat = packed.reshape(-1)
    out, off = [], 0
    for a in like:
        n = int(np.prod(a.shape))
        out.append(flat[off:off + n].reshape(a.shape))
        off += n + ((-n) % (8 * LANES))
    return out


def kernel(x, rel_bias, norm1_g, w_in, sgu_norm_g, sgu_w, sgu_b, q_norm_g, k_norm_g, sinks, out_norm_a, out_norm_b, w_out, norm2_g, w_gate, w_up, w_down, loss_target, m_rel_bias, m_norm1_g, m_w_in, m_sgu_norm_g, m_sgu_w, m_sgu_b, m_q_norm_g, m_k_norm_g, m_sinks, m_out_norm_a, m_out_norm_b, m_w_out, m_norm2_g, m_w_gate, m_w_up, m_w_down, v_rel_bias, v_norm1_g, v_w_in, v_sgu_norm_g, v_sgu_w, v_sgu_b, v_q_norm_g, v_k_norm_g, v_sinks, v_out_norm_a, v_out_norm_b, v_w_out, v_norm2_g, v_w_gate, v_w_up, v_w_down):
    L, D, n_in = w_in.shape
    S = x.shape[1]
    IN = N_CHIPS * n_in
    n_ff = w_gate.shape[2]
    FF = N_CHIPS * n_ff
    H = sgu_w.shape[1]
    NQ = sinks.shape[1]
    SW, AW = H * HEAD_DIM, NQ * HEAD_DIM
    KVW = (IN - 2 * SW - AW) // 2
    dm = _MixerDims(S, IN, SW, AW, KVW, H, NQ)
    assert sgu_w.shape[2] == BLOCK and q_norm_g.shape[1] == HEAD_DIM and SW + AW == D

    wb = {k: w.astype(BF16) for k, w in (("in", w_in), ("out", w_out), ("gate", w_gate), ("up", w_up), ("down", w_down))}

    def gather(l, which):
        if which == "in":
            return _Gather([wb["in"]], [(0, 0, 1, 0, n_in)], [(D, IN)], l)
        if which == "out":
            return _Gather([wb["out"]], [(0, 0, 0, 0, D // N_CHIPS)], [(D, D)], l)
        if which == "gu":
            return _Gather([wb["gate"], wb["up"]], [(0, 0, 1, 0, 2 * n_ff), (1, 0, 1, n_ff, 2 * n_ff)], [(D, 2 * FF)], l)
        return _Gather([wb["down"]], [(0, 0, 0, 0, n_ff)], [(FF, D)], l)

    nxt = lambda l, *which: [gather(l + 1, w) for w in which] if l + 1 < L else []
    W = {}
    (W[0, "in"],) = _comm_only([gather(0, "in")], name="gather_first_weights")

    gs, gmod_q = _group_consts(SW)
    _, gmod_k = _group_consts(KVW)
    pblk = _same_group()
    onehot = _bucket_onehot()
    rbt = jnp.pad(rel_bias.T, ((0, 0), (0, LANES - NUM_BUCKETS)))
    (bias,) = _matmul(rbt, onehot.T, mode="nn", out_dtype=F32, exact=True, tn=4096, name="bias_table")
    bias = bias.reshape(NQ, BLOCK, 2 * BLOCK)

    def mixer_params(l):
        return [sgu_norm_g[l].reshape(1, SW), sgu_w[l], jnp.repeat(sgu_b[l].T, HEAD_DIM, axis=1),
                out_norm_a[l].reshape(1, SW), jnp.tile(q_norm_g[l], NQ).reshape(1, AW),
                jnp.tile(k_norm_g[l], dm.NKV).reshape(1, KVW), out_norm_b[l].reshape(1, AW), sinks[l], bias, pblk]

    xs = x.reshape(S, D)
    saved = []
    for l in range(L):
        h = _rms_fwd(xs, norm1_g[l].reshape(1, D), name="norm1_fwd")
        units = nxt(l, "in", "out") + ([gather(0, "out")] if l == 0 else [])
        z, *got = _matmul(h, W[l, "in"], mode="nn", out_dtype=F32, tn=1792, name="in_proj", comm=units)
        if l == 0:
            W[0, "out"] = got.pop()
        if got:
            W[l + 1, "in"], W[l + 1, "out"] = got
        first = [gather(0, "gu"), gather(0, "down")] if l == 0 else []
        mixed, *got = _mixer_fwd(dm, z, mixer_params(l), name="mixer_fwd", comm=first)
        if got:
            W[0, "gu"], W[0, "down"] = got
        (x1,) = _matmul(mixed, W[l, "out"], mode="nn", out_dtype=F32, res=xs, name="out_proj")
        h2 = _rms_fwd(x1, norm2_g[l].reshape(1, D), name="norm2_fwd")
        ab, f, *got = _gate_up_swiglu(h2, W[l, "gu"], n_ff, name="gate_up_swiglu", comm=nxt(l, "gu"))
        if got:
            (W[l + 1, "gu"],) = got
        x2, *got = _matmul(f, W[l, "down"], mode="nn", out_dtype=F32, res=x1, tk=2816, name="down_proj", comm=nxt(l, "down"))
        if got:
            (W[l + 1, "down"],) = got
        saved.append((xs, h, z, mixed, x1, h2, ab, f))
        xs = x2

    dx, dxb, loss_part = _loss_head(xs, loss_target.reshape(S, D), name="loss_head")
    loss = lax.psum(loss_part[0, 0], ("x", "y", "c"))

    dbias = jnp.zeros((NQ, BLOCK, 2 * BLOCK), F32)
    n_out = D // N_CHIPS
    big_names = ("w_in", "w_out", "w_gate", "w_up", "w_down")
    own = {k: [None] * L for k in big_names}
    got = {k: [None] * L for k in big_names}
    names_layer = ("norm1_g", "sgu_norm_g", "sgu_w", "sgu_b", "q_norm_g", "k_norm_g", "sinks", "out_norm_a", "out_norm_b",
                   "norm2_g")
    small = {k: [None] * L for k in names_layer}
    small_unit = lambda layers: _AllGather(_pack([jnp.stack([small[k][i] for i in layers]) for k in names_layer]))
    for l in reversed(range(L)):
        xl, h, z, mixed, x1, h2, ab, f = saved[l]
        early = [small_unit(range(1, L))] if (l == 0 and L > 1) else []
        dab, *gathered_early = _down_dx_swiglu_bwd(dxb, W[l, "down"], ab, n_ff, name="down_dx_swiglu_bwd", comm=early)
        (g_d,) = _matmul_tn(f, dxb, out_dtype=BF16, tm=1408, tn=1024, tt=2048, name="down_proj_dw")
        g_gu, got["w_down"][l] = _matmul_tn(h2, dab, out_dtype=BF16, tt=2048, name="gate_up_proj_dw",
                                            comm=[_Scatter([g_d], [(0, 0, 0, n_ff, n_ff)])])
        dh2, got["w_gate"][l] = _matmul(dab, W[l, "gu"], mode="nt", out_dtype=F32, tk=2816, name="gate_up_proj_dx",
                                        comm=[_Scatter([g_gu], [(0, 1, 0, n_ff, 2 * n_ff)])])
        dx1, dx1b, dg2 = _rms_bwd(x1, norm2_g[l].reshape(1, D), dh2, dx, name="norm2_bwd")
        (dmixed,) = _matmul(dx1b, W[l, "out"], mode="nt", out_dtype=F32, name="out_proj_dx")
        (g_out,) = _matmul_tn(mixed, dx1b, out_dtype=BF16, tt=2048, name="out_proj_dw")
        (dz, d_sg, d_t, d_sb, d_na, d_qg, d_kg, d_nb, d_sk, dbias, got["w_up"][l], got["w_out"][l]) = _mixer_bwd(
            dm, z, dmixed, dbias, mixer_params(l), (gs, gmod_q, gmod_k), name="mixer_bwd",
            comm=[_Scatter([g_gu, g_out], [(0, 1, n_ff, n_ff, 2 * n_ff), (1, 0, 0, n_out, n_out)])])
        (g_in,) = _matmul_tn(h, dz, out_dtype=BF16, tn=896, tt=2048, name="in_proj_dw")
        dh, got["w_in"][l] = _matmul(dz, W[l, "in"], mode="nt", out_dtype=F32, tk=1792, name="in_proj_dx",
                                     comm=[_Scatter([g_in], [(0, 1, 0, n_in, n_in)])])
        dx, dxb, dg1 = _rms_bwd(xl, norm1_g[l].reshape(1, D), dh, dx1, name="norm1_bwd")
        own["w_in"][l], own["w_out"][l] = (g_in, 1, 0, n_in, n_in), (g_out, 0, 0, n_out, n_out)
        own["w_gate"][l], own["w_up"][l] = (g_gu, 1, 0, n_ff, 2 * n_ff), (g_gu, 1, n_ff, n_ff, 2 * n_ff)
        own["w_down"][l] = (g_d, 0, 0, n_ff, n_ff)
        small["norm1_g"][l] = dg1.reshape(D)
        small["norm2_g"][l] = dg2.reshape(D)
        small["sgu_norm_g"][l] = d_sg.reshape(H, HEAD_DIM)
        small["sgu_w"][l] = d_t
        small["sgu_b"][l] = d_sb[:, :H].T
        small["q_norm_g"][l] = d_qg[0, :HEAD_DIM]
        small["k_norm_g"][l] = d_kg[0, :HEAD_DIM]
        small["sinks"][l] = d_sk[0, :NQ]
        small["out_norm_a"][l] = d_na.reshape(SW)
        small["out_norm_b"][l] = d_nb.reshape(AW)
    grad_x = dx.reshape(1, S, D)
    (d_rb,) = _matmul(dbias.reshape(NQ, BLOCK * 2 * BLOCK), onehot, mode="nn", out_dtype=F32, exact=True, tk=4096,
                      name="rel_bias_grad")
    d_rel_bias = d_rb[:, :NUM_BUCKETS].T

    chip = (2 * lax.axis_index("x") + lax.axis_index("y")).astype(jnp.int32).reshape(1)
    jobs = [(nm, l) for nm in big_names for l in range(L)]
    late = _AllGather(_pack([small[k][0][None] for k in names_layer] + [d_rel_bias]))
    mine, sib = {}, {}
    for i, (nm, l) in enumerate(jobs):
        g, axis, base, width, stride = own[nm][l]
        units = ([late] if i == 0 else []) + ([_Swap([mine[jobs[i - 1]]])] if i > 0 else [])
        mine[nm, l], *rest = _sum_chip_partials(g, axis, base, width, stride, got[nm][l], chip, name="sum_chip_partials",
                                                comm=units)
        if i == 0:
            gathered_late = rest.pop(0)
        if i > 0:
            (sib[jobs[i - 1]],) = rest
    (sib[jobs[-1]],) = _comm_only([_Swap([mine[jobs[-1]]])], name="swap_last_sums")
    wmv = dict(w_in=(w_in, m_w_in, v_w_in), w_out=(w_out, m_w_out, v_w_out), w_gate=(w_gate, m_w_gate, v_w_gate),
               w_up=(w_up, m_w_up, v_w_up), w_down=(w_down, m_w_down, v_w_down))
    big = {}
    two = lambda t: t.reshape(-1, t.shape[-1])
    for nm in big_names:
        w, m, v = wmv[nm]
        res = None
        for l in range(L):
            res = _adamw(two(w), [mine[nm, l], sib[nm, l]], two(m), two(v), layer=l, stacked=res, name="adamw_" + nm)
        big[nm] = [r.reshape(w.shape) for r in res]

    w_small = dict(rel_bias=rel_bias, norm1_g=norm1_g, sgu_norm_g=sgu_norm_g, sgu_w=sgu_w, sgu_b=sgu_b, q_norm_g=q_norm_g,
                   k_norm_g=k_norm_g, sinks=sinks, out_norm_a=out_norm_a, out_norm_b=out_norm_b, norm2_g=norm2_g)
    m_small = dict(rel_bias=m_rel_bias, norm1_g=m_norm1_g, sgu_norm_g=m_sgu_norm_g, sgu_w=m_sgu_w, sgu_b=m_sgu_b,
                   q_norm_g=m_q_norm_g, k_norm_g=m_k_norm_g, sinks=m_sinks, out_norm_a=m_out_norm_a,
                   out_norm_b=m_out_norm_b, norm2_g=m_norm2_g)
    v_small = dict(rel_bias=v_rel_bias, norm1_g=v_norm1_g, sgu_norm_g=v_sgu_norm_g, sgu_w=v_sgu_w, sgu_b=v_sgu_b,
                   q_norm_g=v_q_norm_g, k_norm_g=v_k_norm_g, sinks=v_sinks, out_norm_a=v_out_norm_a,
                   out_norm_b=v_out_norm_b, norm2_g=v_norm2_g)

    def adamw_small(gathered, pick, name):
        like = pick(w_small)
        res = _adamw(_pack(like), [gathered[i] for i in range(N_DEV)], _pack(pick(m_small)), _pack(pick(v_small)), name=name)
        return [_unpack(r, like) for r in res]

    first_layer = adamw_small(gathered_late, lambda d: [d[k][:1] for k in names_layer] + [d["rel_bias"]], "adamw_small_late")
    if L > 1:
        others = adamw_small(gathered_early[0], lambda d: [d[k][1:] for k in names_layer], "adamw_small_early")
    sm = {"rel_bias": [r[-1] for r in first_layer]}
    for j, k in enumerate(names_layer):
        sm[k] = [jnp.concatenate([first_layer[q][j]] + ([others[q][j]] if L > 1 else [])) for q in range(4)]

    order = ("rel_bias", "norm1_g", "w_in", "sgu_norm_g", "sgu_w", "sgu_b", "q_norm_g", "k_norm_g", "sinks", "out_norm_a",
             "out_norm_b", "w_out", "norm2_g", "w_gate", "w_up", "w_down")
    pick = lambda k, i: big[k][i] if k in big else sm[k][i]
    outs = [loss, grad_x]
    for i in range(4):
        outs += [pick(k, i) for k in order]
    return tuple(outs)
```

```python
import functools
import math

import numpy as np

import jax
import jax.numpy as jnp
from jax import lax
from jax.experimental import pallas as pl
from jax.experimental.pallas import tpu as pltpu

F32 = jnp.float32
BF16 = jnp.bfloat16
MESH = pl.DeviceIdType.MESH
ANY = pl.BlockSpec(memory_space=pl.ANY)

HEAD_DIM = 64
BLOCK = 128
NUM_BUCKETS = 32
MAX_DISTANCE = 128
EPS = 1e-6
NEG_INF = -1e30
ADAM_LR, ADAM_B1, ADAM_B2, ADAM_EPS, ADAM_WD, ADAM_STEP = 0.001, 0.9, 0.999, 1e-08, 0.01, 10

LANES = 128
VMEM_LIMIT = 56 * 1024 * 1024
N_CHIPS = 4
N_DEV = 8
PACK_ROWS = 256


def _cparams(sem=None):
    return pltpu.CompilerParams(dimension_semantics=sem, vmem_limit_bytes=VMEM_LIMIT)


def _tile(dim, target, align=LANES):
    best = None
    for t in range(align, min(dim, target) + 1, align):
        if dim % t == 0:
            best = t
    return best if best is not None else dim


def _mesh_pos():
    return lax.axis_index("x"), lax.axis_index("y"), lax.axis_index("c")


def _other_chips(x, y):
    return [(1 - x, y), (x, 1 - y), (1 - x, 1 - y)]


def _slab(ref, axis, start, size, half=None):
    if axis == 1:
        rows = ref.shape[0]
        r = pl.ds(0, rows) if half is None else pl.ds(pl.multiple_of(half * (rows // 2), 16), rows // 2)
        return ref.at[r, pl.ds(pl.multiple_of(start, LANES), size)]
    if half is None:
        return ref.at[pl.ds(pl.multiple_of(start, 16), size), :]
    return ref.at[pl.ds(pl.multiple_of(start + half * (size // 2), 16), size // 2), :]


def _remote(src, dst, send_sem, recv_sem, to):
    return pltpu.make_async_remote_copy(src_ref=src, dst_ref=dst, send_sem=send_sem, recv_sem=recv_sem,
                                        device_id=to, device_id_type=MESH)


class _Gather:
    def __init__(self, shards, streams, out_shapes, layer):
        self.ins, self.streams, self.layer = list(shards), streams, layer
        self.outs = [jax.ShapeDtypeStruct(s, BF16) for s in out_shapes]
        ns = len(streams)
        self.sems = [pltpu.SemaphoreType.DMA((ns, 3))] * 4 + [pltpu.SemaphoreType.DMA((ns,))]

    def _sent(self, srcs, outs, sems):
        send, recv, _, _, local = sems
        x, y, c = _mesh_pos()
        me_j = 2 * x + y
        own, sends = [], []
        for s, (si, oi, axis, base, stride) in enumerate(self.streams):
            src, out = srcs[si].at[self.layer], outs[oi]
            width = src.shape[axis]
            own.append(pltpu.make_async_copy(src, _slab(out, axis, base + me_j * stride, width), local.at[s]))
            half_rows = src.shape[0] // 2
            mine = src.at[pl.ds(pl.multiple_of(c * half_rows, 16), half_rows), :]
            for k, (px, py) in enumerate(_other_chips(x, y)):
                sends.append(_remote(mine, _slab(out, axis, base + me_j * stride, width, half=c),
                                     send.at[s, k], recv.at[s, k], (px, py, c)))
        return own, sends

    def start(self, srcs, outs, sems):
        own, sends = self._sent(srcs, outs, sems)
        for cp in own + sends:
            cp.start()

    def finish(self, srcs, outs, sems):
        send, recv, fsend, frecv, _ = sems
        x, y, c = _mesh_pos()
        sib = (x, y, 1 - c)
        forwards, fwd_arrivals = [], []
        for s, (si, oi, axis, base, stride) in enumerate(self.streams):
            out = outs[oi]
            width = srcs[si].shape[1 + axis]
            for k, (px, py) in enumerate(_other_chips(x, y)):
                start = base + (2 * px + py) * stride
                got = _slab(out, axis, start, width, half=c)
                _remote(got, got, send.at[s, k], recv.at[s, k], (px, py, c)).wait_recv()
                fwd = _remote(got, got, fsend.at[s, k], frecv.at[s, k], sib)
                fwd.start()
                forwards.append(fwd)
                theirs = _slab(out, axis, start, width, half=1 - c)
                fwd_arrivals.append(_remote(theirs, theirs, fsend.at[s, k], frecv.at[s, k], sib))
        for a in fwd_arrivals:
            a.wait_recv()
        for cp in forwards:
            cp.wait_send()
        own, sends = self._sent(srcs, outs, sems)
        for cp in sends:
            cp.wait_send()
        for cp in own:
            cp.wait()


class _Scatter:
    def __init__(self, grads, streams):
        self.ins, self.streams = list(grads), streams
        shard = lambda g, axis, width: (width, g.shape[1]) if axis == 0 else (g.shape[0], width)
        self.outs = [jax.ShapeDtypeStruct((3,) + shard(grads[gi], axis, width), BF16) for gi, axis, _, width, _ in streams]
        self.sems = [pltpu.SemaphoreType.DMA((len(streams), 3))] * 2

    def _copies(self, srcs, outs, sems):
        send, recv = sems
        x, y, c = _mesh_pos()
        copies = []
        for s, (gi, axis, base, width, stride) in enumerate(self.streams):
            for k, (px, py) in enumerate(_other_chips(x, y)):
                copies.append(_remote(_slab(srcs[gi], axis, base + (2 * px + py) * stride, width), outs[s].at[k],
                                      send.at[s, k], recv.at[s, k], (px, py, c)))
        return copies

    def start(self, srcs, outs, sems):
        for cp in self._copies(srcs, outs, sems):
            cp.start()

    def finish(self, srcs, outs, sems):
        for cp in self._copies(srcs, outs, sems):
            cp.wait()


class _Swap:
    def __init__(self, arrays):
        self.ins = list(arrays)
        self.outs = [jax.ShapeDtypeStruct(a.shape, a.dtype) for a in arrays]
        self.sems = [pltpu.SemaphoreType.DMA((len(arrays),))] * 2

    def _copies(self, srcs, outs, sems):
        send, recv = sems
        x, y, c = _mesh_pos()
        return [_remote(srcs[s], outs[s], send.at[s], recv.at[s], (x, y, 1 - c)) for s in range(len(srcs))]

    def start(self, srcs, outs, sems):
        for cp in self._copies(srcs, outs, sems):
            cp.start()

    def finish(self, srcs, outs, sems):
        for cp in self._copies(srcs, outs, sems):
            cp.wait()


class _AllGather:
    def __init__(self, part):
        self.ins = [part]
        self.outs = [jax.ShapeDtypeStruct((N_DEV,) + part.shape, part.dtype)]
        self.sems = [pltpu.SemaphoreType.DMA((7,)), pltpu.SemaphoreType.DMA((7,)), pltpu.SemaphoreType.DMA(())]

    def _first(self, srcs, outs, sems):
        (x_ref,), (out,), (send, recv, local) = srcs, outs, sems
        x, y, c = _mesh_pos()
        mine = out.at[4 * x + 2 * y + c]
        own = pltpu.make_async_copy(x_ref, mine, local)
        sends = [_remote(x_ref, mine, send.at[0], recv.at[0], (x, y, 1 - c))]
        sends += [_remote(x_ref, mine, send.at[1 + k], recv.at[1 + k], (px, py, c)) for k, (px, py) in enumerate(_other_chips(x, y))]
        return own, sends

    def start(self, srcs, outs, sems):
        own, sends = self._first(srcs, outs, sems)
        own.start()
        for cp in sends:
            cp.start()

    def finish(self, srcs, outs, sems):
        (out,), (send, recv, local) = outs, sems
        x, y, c = _mesh_pos()
        me, sib = (x, y, c), (x, y, 1 - c)
        slot = lambda px, py, pc: out.at[4 * px + 2 * py + pc]
        passed = []
        for k, (px, py) in enumerate(_other_chips(x, y)):
            blk = slot(px, py, c)
            _remote(blk, blk, send.at[1 + k], recv.at[1 + k], me).wait_recv()
            fwd = _remote(blk, blk, send.at[4 + k], recv.at[4 + k], sib)
            fwd.start()
            passed.append(fwd)
        blk = slot(x, y, 1 - c)
        _remote(blk, blk, send.at[0], recv.at[0], me).wait_recv()
        for k, (px, py) in enumerate(_other_chips(x, y)):
            blk = slot(px, py, 1 - c)
            _remote(blk, blk, send.at[4 + k], recv.at[4 + k], me).wait_recv()
        own, sends = self._first(srcs, outs, sems)
        for cp in sends + passed:
            cp.wait_send()
        own.wait()


def _call(body, *, grid, in_specs, out_specs, out_shape, args, name, scratch=(), sem=None, comm=(), prefetch=None,
          aliases=None):
    out_shape, out_specs = tuple(out_shape), tuple(out_specs)
    n_in, n_out, n_scr = len(in_specs), len(out_shape), len(scratch)
    n_pre = 0 if prefetch is None else 1
    c_ins = [a for u in comm for a in u.ins]
    c_outs = [o for u in comm for o in u.outs]
    c_sems = [s for u in comm for s in u.sems]

    def wrapped(*refs):
        pre, refs = refs[:n_pre], refs[n_pre:]
        ins, rest = refs[:n_in], refs[n_in:]
        cin, rest = rest[:len(c_ins)], rest[len(c_ins):]
        outs, rest = rest[:n_out], rest[n_out:]
        cout, rest = rest[:len(c_outs)], rest[len(c_outs):]
        scr, csem = rest[:n_scr], rest[n_scr:]

        def each(fn_name):
            i = o = s = 0
            for u in comm:
                getattr(u, fn_name)(cin[i:i + len(u.ins)], cout[o:o + len(u.outs)], csem[s:s + len(u.sems)])
                i, o, s = i + len(u.ins), o + len(u.outs), s + len(u.sems)

        if comm:
            pids = [pl.program_id(d) for d in range(len(grid))]
            first = functools.reduce(jnp.logical_and, [p == 0 for p in pids])
            last = functools.reduce(jnp.logical_and, [p == g - 1 for p, g in zip(pids, grid)])
            pl.when(first)(lambda: each("start"))
        body(*pre, *ins, *outs, *scr)
        if comm:
            pl.when(last)(lambda: each("finish"))

    if comm:
        sem = ("arbitrary",) * len(grid)
    all_in = list(in_specs) + [ANY] * len(c_ins)
    all_out = out_specs + tuple([ANY] * len(c_outs))
    all_scr = list(scratch) + c_sems
    kw = {}
    if aliases:
        kw["input_output_aliases"] = {n_pre + i: o for i, o in aliases.items()}
    if prefetch is None:
        kw.update(grid=grid, in_specs=all_in, out_specs=all_out, scratch_shapes=all_scr)
        pre_args = []
    else:
        kw["grid_spec"] = pltpu.PrefetchScalarGridSpec(num_scalar_prefetch=1, grid=grid, in_specs=all_in, out_specs=all_out,
                                                       scratch_shapes=all_scr)
        pre_args = [prefetch]
    return pl.pallas_call(wrapped, out_shape=out_shape + tuple(c_outs), compiler_params=_cparams(sem), name=name, **kw)(
        *pre_args, *args, *c_ins)


def _comm_only(comm, *, name):
    def body(tick_ref):
        tick_ref[...] = jnp.zeros(tick_ref.shape, tick_ref.dtype)

    outs = _call(body, grid=(1,), in_specs=[], out_specs=[pl.BlockSpec((8, LANES), lambda i: (0, 0))],
                 out_shape=[jax.ShapeDtypeStruct((8, LANES), F32)], args=[], name=name, comm=comm)
    return outs[1:]


def _split3(x):
    hi = x.astype(BF16)
    r1 = x - hi.astype(F32)
    mid = r1.astype(BF16)
    lo = (r1 - mid.astype(F32)).astype(BF16)
    return hi, mid, lo


def _dot3(x, g):
    hi, mid, lo = _split3(x)
    d = lambda a: jnp.dot(a, g, preferred_element_type=F32)
    return d(hi) + d(mid) + d(lo)


GROUP_TILE = 256


def _group_sum(x, pblk):
    hi = x.astype(BF16)
    lo = (x - hi.astype(F32)).astype(BF16)
    cols = []
    for b in range(x.shape[1] // GROUP_TILE):
        sl = slice(GROUP_TILE * b, GROUP_TILE * (b + 1))
        cols.append(jnp.dot(hi[:, sl], pblk, preferred_element_type=F32) + jnp.dot(lo[:, sl], pblk, preferred_element_type=F32))
    return cols[0] if len(cols) == 1 else jnp.concatenate(cols, axis=1)


def _matmul(a, b, *, mode, out_dtype, name, res=None, exact=False, tm=1024, tn=1024, tk=2048, b_col0=0, comm=()):
    M, K = a.shape
    N = b.shape[1] if mode == "nn" else b.shape[0]
    tm, tn, tk = _tile(M, tm, 8 if M < LANES else LANES), _tile(N, tn), _tile(K, tk)
    nk = K // tk
    dn = (((1,), (0,)), ((), ())) if mode == "nn" else (((1,), (1,)), ((), ()))

    def body(*refs):
        a_ref, b_ref = refs[0], refs[1]
        r_ref = refs[2] if res is not None else None
        o_ref = refs[3] if res is not None else refs[2]
        if exact:
            part = _dot3(a_ref[...], b_ref[...])
        else:
            part = lax.dot_general(a_ref[...].astype(BF16), b_ref[...].astype(BF16), dn, preferred_element_type=F32)

        def finish(total):
            if r_ref is not None:
                total = r_ref[...] + total
            o_ref[...] = total.astype(o_ref.dtype)

        if nk == 1:
            finish(part)
        else:
            acc = refs[-1]
            k = pl.program_id(2)

            @pl.when(k == 0)
            def _():
                acc[...] = part

            @pl.when(k > 0)
            def _():
                acc[...] += part

            @pl.when(k == nk - 1)
            def _():
                finish(acc[...])

    if mode == "nn":
        b_spec = pl.BlockSpec((tk, tn), lambda j, i, k: (k, j))
    else:
        assert b_col0 % tk == 0
        b_spec = pl.BlockSpec((tn, tk), lambda j, i, k: (j, k + b_col0 // tk))
    in_specs = [pl.BlockSpec((tm, tk), lambda j, i, k: (i, k)), b_spec]
    args = [a, b]
    if res is not None:
        in_specs.append(pl.BlockSpec((tm, tn), lambda j, i, k: (i, j)))
        args.append(res)
    return _call(
        body, grid=(N // tn, M // tm, nk), in_specs=in_specs,
        out_specs=[pl.BlockSpec((tm, tn), lambda j, i, k: (i, j))], out_shape=[jax.ShapeDtypeStruct((M, N), out_dtype)],
        scratch=[pltpu.VMEM((tm, tn), F32)] if nk > 1 else [], sem=("parallel", "parallel", "arbitrary"),
        args=args, name=name, comm=comm)


def _matmul_tn(a, b, *, out_dtype, name, tm=2048, tn=1024, tt=1024, comm=()):
    T, Mo = a.shape
    N = b.shape[1]
    tm, tn, tt = _tile(Mo, tm), _tile(N, tn), _tile(T, tt)
    nt = T // tt

    def body(a_ref, b_ref, o_ref, acc):
        t = pl.program_id(2)
        part = lax.dot_general(a_ref[...].astype(BF16), b_ref[...].astype(BF16), (((0,), (0,)), ((), ())),
                               preferred_element_type=F32)

        @pl.when(t == 0)
        def _():
            acc[...] = part

        @pl.when(t > 0)
        def _():
            acc[...] += part

        @pl.when(t == nt - 1)
        def _():
            o_ref[...] = acc[...].astype(o_ref.dtype)

    return _call(
        body, grid=(Mo // tm, N // tn, nt),
        in_specs=[pl.BlockSpec((tt, tm), lambda i, j, t: (t, i)), pl.BlockSpec((tt, tn), lambda i, j, t: (t, j))],
        out_specs=[pl.BlockSpec((tm, tn), lambda i, j, t: (i, j))], out_shape=[jax.ShapeDtypeStruct((Mo, N), out_dtype)],
        scratch=[pltpu.VMEM((tm, tn), F32)], sem=("parallel", "parallel", "arbitrary"), args=[a, b], name=name, comm=comm)


def _rms_fwd(x, g, *, name, tr=256):
    R, D = x.shape
    tr = _tile(R, tr, 8)

    def body(x_ref, g_ref, o_ref):
        xv = x_ref[...]
        r = lax.rsqrt(jnp.mean(xv * xv, axis=-1, keepdims=True) + EPS)
        o_ref[...] = (xv * r * g_ref[...]).astype(o_ref.dtype)

    return pl.pallas_call(
        body, out_shape=jax.ShapeDtypeStruct((R, D), BF16), grid=(R // tr,),
        in_specs=[pl.BlockSpec((tr, D), lambda i: (i, 0)), pl.BlockSpec((1, D), lambda i: (0, 0))],
        out_specs=pl.BlockSpec((tr, D), lambda i: (i, 0)),
        compiler_params=_cparams(("parallel",)), name=name,
    )(x, g)


def _rms_bwd(x, g, dh, dres, *, name, tr=256):
    R, D = x.shape
    tr = _tile(R, tr, 8)

    def body(x_ref, g_ref, dh_ref, dres_ref, dx_ref, dxb_ref, dg_ref):
        i = pl.program_id(0)
        xv = x_ref[...]
        r = lax.rsqrt(jnp.mean(xv * xv, axis=-1, keepdims=True) + EPS)
        y = xv * r
        dhv = dh_ref[...]
        dy = dhv * g_ref[...]
        dx = dres_ref[...] + r * (dy - y * jnp.mean(dy * y, axis=-1, keepdims=True))
        dx_ref[...] = dx
        dxb_ref[...] = dx.astype(BF16)
        dg = jnp.sum(dhv * y, axis=0, keepdims=True)

        @pl.when(i == 0)
        def _():
            dg_ref[...] = dg

        @pl.when(i > 0)
        def _():
            dg_ref[...] += dg

    row = pl.BlockSpec((tr, D), lambda i: (i, 0))
    vec = pl.BlockSpec((1, D), lambda i: (0, 0))
    return pl.pallas_call(
        body, out_shape=(jax.ShapeDtypeStruct((R, D), F32), jax.ShapeDtypeStruct((R, D), BF16),
                         jax.ShapeDtypeStruct((1, D), F32)), grid=(R // tr,),
        in_specs=[row, vec, row, row], out_specs=(row, row, vec),
        compiler_params=_cparams(("arbitrary",)), name=name,
    )(x, g, dh, dres)


def _gate_up_swiglu(h, w_gu, n_ff, *, name, tm=512, comm=()):
    M, K = h.shape
    nsh = w_gu.shape[1] // (2 * n_ff)
    tm = _tile(M, tm)

    def body(h_ref, w_ref, ab_ref, f_ref):
        ab = jnp.dot(h_ref[...], w_ref[...], preferred_element_type=F32)
        ab_ref[...] = ab.astype(ab_ref.dtype)
        a, b = ab[:, :n_ff], ab[:, n_ff:]
        f_ref[...] = (a * jax.nn.sigmoid(a) * b).astype(f_ref.dtype)

    return _call(
        body, grid=(nsh, M // tm),
        in_specs=[pl.BlockSpec((tm, K), lambda j, i: (i, 0)), pl.BlockSpec((K, 2 * n_ff), lambda j, i: (0, j))],
        out_specs=[pl.BlockSpec((tm, 2 * n_ff), lambda j, i: (i, j)), pl.BlockSpec((tm, n_ff), lambda j, i: (i, j))],
        out_shape=[jax.ShapeDtypeStruct((M, 2 * nsh * n_ff), BF16), jax.ShapeDtypeStruct((M, nsh * n_ff), BF16)],
        sem=("parallel", "parallel"), args=[h, w_gu], name=name, comm=comm)


def _down_dx_swiglu_bwd(dy, w_d, ab, n_ff, *, name, tm=512, comm=()):
    M, K = dy.shape
    nsh = w_d.shape[0] // n_ff
    tm = _tile(M, tm)

    def body(dy_ref, w_ref, ab_ref, dab_ref):
        df = lax.dot_general(dy_ref[...], w_ref[...], (((1,), (1,)), ((), ())), preferred_element_type=F32)
        av = ab_ref[:, :n_ff].astype(F32)
        s = jax.nn.sigmoid(av)
        dab_ref[:, :n_ff] = (df * ab_ref[:, n_ff:].astype(F32) * (s * (1.0 + av * (1.0 - s)))).astype(dab_ref.dtype)
        dab_ref[:, n_ff:] = (df * (av * s)).astype(dab_ref.dtype)

    pair = pl.BlockSpec((tm, 2 * n_ff), lambda j, i: (i, j))
    return _call(
        body, grid=(nsh, M // tm),
        in_specs=[pl.BlockSpec((tm, K), lambda j, i: (i, 0)), pl.BlockSpec((n_ff, K), lambda j, i: (j, 0)), pair],
        out_specs=[pair], out_shape=[jax.ShapeDtypeStruct(ab.shape, BF16)],
        sem=("parallel", "parallel"), args=[dy, w_d, ab], name=name, comm=comm)


def _loss_head(y, target, *, name, tr=256):
    R, D = y.shape
    tr = _tile(R, tr, 8)

    def body(y_ref, t_ref, dy_ref, dyb_ref, l_ref):
        i = pl.program_id(0)
        e = y_ref[...] - t_ref[...]
        dy = e * (1.0 / D)
        dy_ref[...] = dy
        dyb_ref[...] = dy.astype(BF16)
        part = 0.5 * jnp.sum(jnp.mean(e * e, axis=-1, keepdims=True), axis=0, keepdims=True)
        part = jnp.broadcast_to(part, (8, LANES))

        @pl.when(i == 0)
        def _():
            l_ref[...] = part

        @pl.when(i > 0)
        def _():
            l_ref[...] += part

    row = pl.BlockSpec((tr, D), lambda i: (i, 0))
    return pl.pallas_call(
        body, out_shape=(jax.ShapeDtypeStruct((R, D), F32), jax.ShapeDtypeStruct((R, D), BF16),
                         jax.ShapeDtypeStruct((8, LANES), F32)), grid=(R // tr,),
        in_specs=[row, row], out_specs=(row, row, pl.BlockSpec((8, LANES), lambda i: (0, 0))),
        compiler_params=_cparams(("arbitrary",)), name=name,
    )(y, target)


def _gelu(x):
    return 0.5 * x * (1.0 + lax.erf(x * math.sqrt(0.5)))


def _gelu_grad(x):
    return 0.5 * (1.0 + lax.erf(x * math.sqrt(0.5))) + x * jnp.exp(-0.5 * x * x) * (1.0 / math.sqrt(2.0 * math.pi))


def _group_consts(width):
    lane = np.arange(width)
    col = np.arange(LANES)
    grp = (lane[:, None] // HEAD_DIM == col[None, :]).astype(np.float32)
    mod = ((lane[:, None] % HEAD_DIM == col[None, :]) & (col[None, :] < HEAD_DIM)).astype(np.float32)
    return jnp.asarray(grp, BF16), jnp.asarray(mod, BF16)


def _same_group():
    lane = np.arange(GROUP_TILE) // HEAD_DIM
    return jnp.asarray((lane[:, None] == lane[None, :]).astype(np.float32), BF16)


def _bucket_onehot():
    qi = np.arange(BLOCK)[:, None]
    kj = np.arange(2 * BLOCK)[None, :]
    n = np.maximum(qi + BLOCK - kj, 0)
    max_exact = NUM_BUCKETS // 2
    nf = np.maximum(n, 1).astype(np.float32)
    large = max_exact + (np.log(nf / np.float32(max_exact)) / np.float32(math.log(MAX_DISTANCE / max_exact))
                         * np.float32(NUM_BUCKETS - max_exact)).astype(np.int32)
    large = np.minimum(large, NUM_BUCKETS - 1)
    bucket = jnp.asarray(np.where(n < max_exact, n, large).reshape(-1).astype(np.int32))
    return (bucket[:, None] == jnp.arange(LANES, dtype=jnp.int32)[None, :]).astype(BF16)


class _MixerDims:
    def __init__(self, S, IN, SW, AW, KVW, H, NQ):
        self.S, self.IN, self.SW, self.AW, self.KVW, self.H, self.NQ = S, IN, SW, AW, KVW, H, NQ
        self.NKV = KVW // HEAD_DIM
        self.GROUP = NQ // self.NKV
        self.nb = S // BLOCK
        self.koff = 2 * SW + AW
        self.voff = self.koff + KVW
        assert SW % LANES == 0 and AW % LANES == 0 and KVW % LANES == 0 and self.GROUP % 2 == 0
        assert self.koff % (2 * KVW) == 0 and IN == self.voff + KVW and S % BLOCK == 0


def _mixer_block(dm, n, z, kvp, prm):
    SW, AW, KVW, H, NQ = dm.SW, dm.AW, dm.KVW, dm.H, dm.NQ
    lane = lax.broadcasted_iota(jnp.int32, (BLOCK, LANES), 1)
    lo = lane < HEAD_DIM
    row = lax.broadcasted_iota(jnp.int32, (BLOCK, BLOCK), 0)
    col = lax.broadcasted_iota(jnp.int32, (BLOCK, BLOCK), 1)
    tril = row >= col
    pblk = prm["pblk"][...]
    inv = 1.0 / HEAD_DIM

    def group_rsqrt(x):
        return lax.rsqrt(_group_sum(x * x, pblk) * inv + EPS)

    zu, zv = z[:, :SW], z[:, SW:2 * SW]
    u, v = _gelu(zu), _gelu(zv)
    rv = group_rsqrt(v)
    vn = v * rv * prm["sgu_g"][...]
    vnb = vn.astype(BF16)
    tmats, gate_blocks = [], []
    for p in range(H // 2):
        blk = slice(LANES * p, LANES * (p + 1))
        t0 = jnp.where(tril, prm["sgu_w"][2 * p], 0.0).astype(BF16)
        t1 = jnp.where(tril, prm["sgu_w"][2 * p + 1], 0.0).astype(BF16)
        tmats += [t0, t1]
        g0 = jnp.dot(t0, vnb[:, blk], preferred_element_type=F32)
        g1 = jnp.dot(t1, vnb[:, blk], preferred_element_type=F32)
        gate_blocks.append(jnp.where(lo, g0, g1) + prm["sgu_bias"][:, blk])
    gate = jnp.concatenate(gate_blocks, axis=1)
    outa = u * gate
    ra = lax.rsqrt(jnp.mean(outa * outa, axis=-1, keepdims=True) + EPS)

    q = z[:, 2 * SW:2 * SW + AW]
    kcat = jnp.concatenate([kvp[:, :KVW], z[:, dm.koff:dm.koff + KVW]], axis=0)
    vcat = jnp.concatenate([kvp[:, KVW:], z[:, dm.voff:dm.voff + KVW]], axis=0)
    rq = group_rsqrt(q)
    rk = group_rsqrt(kcat)
    qn = q * rq * prm["q_g"][...]
    kn = kcat * rk * prm["k_g"][...]
    knb, vcb = kn.astype(BF16), vcat.astype(BF16)
    qi = lax.broadcasted_iota(jnp.int32, (BLOCK, 2 * BLOCK), 0)
    kj = lax.broadcasted_iota(jnp.int32, (BLOCK, 2 * BLOCK), 1)
    valid = (kj > qi) & (kj <= qi + BLOCK) & ((n > 0) | (kj >= BLOCK))
    scale = 1.0 / math.sqrt(HEAD_DIM)
    heads = []
    out_blocks = []
    for hq in range(NQ):
        mb, e = hq // 2, hq % 2
        kv = hq // dm.GROUP
        kb, ek = kv // 2, kv % 2
        qblk = qn[:, LANES * mb:LANES * (mb + 1)]
        if e != ek:
            qblk = pltpu.roll(qblk, HEAD_DIM, 1)
        half = lo if ek == 0 else jnp.logical_not(lo)
        qm = jnp.where(half, qblk, 0.0).astype(BF16)
        kblk = knb[:, LANES * kb:LANES * (kb + 1)]
        vblk = vcb[:, LANES * kb:LANES * (kb + 1)]
        s = lax.dot_general(qm, kblk, (((1,), (1,)), ((), ())), preferred_element_type=F32) * scale + prm["bias"][hq]
        s = jnp.where(valid, s, NEG_INF)
        sink = prm["sinks"][hq]
        mx = jnp.maximum(jnp.max(s, axis=-1, keepdims=True), sink)
        ex = jnp.exp(s - mx)
        den = jnp.sum(ex, axis=-1, keepdims=True) + jnp.exp(sink - mx)
        inv_den = 1.0 / den
        pr = ex * inv_den
        psink = jnp.exp(sink - mx) * inv_den
        prb = pr.astype(BF16)
        r_h = jnp.dot(prb, vblk, preferred_element_type=F32)
        if e != ek:
            r_h = pltpu.roll(r_h, HEAD_DIM, 1)
        heads.append(dict(qm=qm, kblk=kblk, vblk=vblk, pr=pr, prb=prb, psink=psink, half=half, mb=mb, e=e, ek=ek, kb=kb))
        if e == 1:
            out_blocks.append(jnp.where(lo, prev_r, r_h))
        prev_r = r_h
    outb = jnp.concatenate(out_blocks, axis=1)
    rb = lax.rsqrt(jnp.mean(outb * outb, axis=-1, keepdims=True) + EPS)
    return dict(lo=lo, tril=tril, pblk=pblk, zu=zu, zv=zv, u=u, v=v, rv=rv, vnb=vnb, tmats=tmats, gate=gate,
                outa=outa, ra=ra, q=q, kcat=kcat, rq=rq, rk=rk, heads=heads, outb=outb, rb=rb, scale=scale)


_MIXER_PARAMS = ("sgu_g", "sgu_w", "sgu_bias", "norm_a", "q_g", "k_g", "norm_b", "sinks", "bias", "pblk")


def _mixer_param_specs(dm, idx):
    SW, AW, KVW = dm.SW, dm.AW, dm.KVW
    full = lambda shape: pl.BlockSpec(shape, lambda n: (0,) * len(shape))
    return [full((1, SW)), full((dm.H, BLOCK, BLOCK)), full((BLOCK, SW)), full((1, SW)), full((1, AW)), full((1, KVW)),
            full((1, AW)), pl.BlockSpec(memory_space=pltpu.SMEM), full((dm.NQ, BLOCK, 2 * BLOCK)),
            full((GROUP_TILE, GROUP_TILE))]


def _mixer_fwd(dm, z, params, *, name, comm=()):
    nb = dm.nb

    def body(z_ref, kvp_ref, *rest):
        prm = dict(zip(_MIXER_PARAMS, rest[:len(_MIXER_PARAMS)]))
        o_ref = rest[len(_MIXER_PARAMS)]
        n = pl.program_id(0)
        c = _mixer_block(dm, n, z_ref[...], kvp_ref[...], prm)
        o_ref[:, :dm.SW] = (c["outa"] * c["ra"] * prm["norm_a"][...]).astype(o_ref.dtype)
        o_ref[:, dm.SW:] = (c["outb"] * c["rb"] * prm["norm_b"][...]).astype(o_ref.dtype)

    kvblk = dm.koff // (2 * dm.KVW)
    in_specs = [pl.BlockSpec((BLOCK, dm.IN), lambda n: (n, 0)),
                pl.BlockSpec((BLOCK, 2 * dm.KVW), lambda n: (jnp.maximum(n - 1, 0), kvblk))] + _mixer_param_specs(dm, None)
    return _call(
        body, out_shape=[jax.ShapeDtypeStruct((dm.S, dm.SW + dm.AW), BF16)], grid=(nb,),
        in_specs=in_specs, out_specs=[pl.BlockSpec((BLOCK, dm.SW + dm.AW), lambda n: (n, 0))],
        sem=("arbitrary",), args=[z, z, *params], name=name, comm=comm)


def _mixer_bwd(dm, z, dmixed, dbias_in, params, gmats, *, name, comm=()):
    SW, AW, KVW, H, NQ, nb, IN = dm.SW, dm.AW, dm.KVW, dm.H, dm.NQ, dm.nb, dm.IN
    QW = 2 * SW + AW
    NP = len(_MIXER_PARAMS)

    def body(z_ref, kvp_ref, dm_ref, dbin_ref, *rest):
        prm = dict(zip(_MIXER_PARAMS, rest[:NP]))
        gs_ref, gmq_ref, gmk_ref = rest[NP:NP + 3]
        (dz_ref, dsg_ref, dt_ref, dsb_ref, dna_ref, dqg_ref, dkg_ref, dnb_ref, dsk_ref, dbias_ref) = rest[NP + 3:NP + 13]
        hold, tmpkv, newkv, carry, accb, accq, acck = rest[NP + 13:]
        n = pl.program_id(0)

        @pl.when(n == 0)
        def _():
            for r in (dsg_ref, dt_ref, dna_ref, dnb_ref, dsk_ref, accb, accq, acck):
                r[...] = jnp.zeros(r.shape, r.dtype)
            dbias_ref[...] = dbin_ref[...]

        @pl.when(n < nb)
        def _():
            c = _mixer_block(dm, n, z_ref[...], kvp_ref[...], prm)
            lo = c["lo"]
            dmx = dm_ref[...]
            inv = 1.0 / HEAD_DIM

            def rms_bwd_full(dy_scaled, y, r):
                return r * (dy_scaled - y * jnp.mean(dy_scaled * y, axis=-1, keepdims=True))

            def group_mean_b(x):
                return _group_sum(x, c["pblk"]) * inv

            dma = dmx[:, :SW]
            ya = c["outa"] * c["ra"]
            dna_ref[...] += jnp.sum(dma * ya, axis=0, keepdims=True)
            douta = rms_bwd_full(dma * prm["norm_a"][...], ya, c["ra"])
            du = douta * c["gate"]
            dgate = douta * c["u"]
            accb[...] += dgate
            dgb16 = dgate.astype(BF16)
            dvn_blocks = []
            for p in range(H // 2):
                blk = slice(LANES * p, LANES * (p + 1))
                dg = dgate[:, blk]
                d0 = jnp.where(lo, dg, 0.0).astype(BF16)
                d1 = jnp.where(lo, 0.0, dg).astype(BF16)
                vb = c["vnb"][:, blk]
                nt = lambda a, b: lax.dot_general(a, b, (((1,), (1,)), ((), ())), preferred_element_type=F32)
                tn = lambda a, b: lax.dot_general(a, b, (((0,), (0,)), ((), ())), preferred_element_type=F32)
                dt_ref[2 * p] += nt(d0, vb)
                dt_ref[2 * p + 1] += nt(d1, vb)
                dvn_blocks.append(jnp.where(lo, tn(c["tmats"][2 * p], dgb16[:, blk]), tn(c["tmats"][2 * p + 1], dgb16[:, blk])))
            dvn = jnp.concatenate(dvn_blocks, axis=1)
            yv = c["v"] * c["rv"]
            dsg_ref[...] += jnp.sum(dvn * yv, axis=0, keepdims=True)
            dyv = dvn * prm["sgu_g"][...]
            dv = c["rv"] * (dyv - yv * group_mean_b(dyv * yv))
            dzu = du * _gelu_grad(c["zu"])
            dzv = dv * _gelu_grad(c["zv"])

            dmb = dmx[:, SW:]
            yb = c["outb"] * c["rb"]
            dnb_ref[...] += jnp.sum(dmb * yb, axis=0, keepdims=True)
            doutb = rms_bwd_full(dmb * prm["norm_b"][...], yb, c["rb"])
            lane1 = lax.broadcasted_iota(jnp.int32, (1, LANES), 1)
            dqn_blocks = [None] * (AW // LANES)
            dkn_blocks = [None] * (KVW // LANES)
            dvc_blocks = [None] * (KVW // LANES)
            dsink_vec = jnp.zeros((1, LANES), F32)
            add = lambda old, new: new if old is None else old + new
            for hq, hd in enumerate(c["heads"]):
                mb, e, ek, kb, half = hd["mb"], hd["e"], hd["ek"], hd["kb"], hd["half"]
                dr = doutb[:, LANES * mb:LANES * (mb + 1)]
                if e != ek:
                    dr = pltpu.roll(dr, HEAD_DIM, 1)
                drm = jnp.where(half, dr, 0.0).astype(BF16)
                dp = lax.dot_general(drm, hd["vblk"], (((1,), (1,)), ((), ())), preferred_element_type=F32)
                dvc_blocks[kb] = add(dvc_blocks[kb], lax.dot_general(hd["prb"], drm, (((0,), (0,)), ((), ())),
                                                                     preferred_element_type=F32))
                rowdot = jnp.sum(hd["pr"] * dp, axis=-1, keepdims=True)
                ds = hd["pr"] * (dp - rowdot)
                dsink = jnp.sum(-hd["psink"] * rowdot, axis=0, keepdims=True)
                dsink_vec = dsink_vec + jnp.where(lane1 == hq, dsink, 0.0)
                dbias_ref[hq] += ds
                dsb = (ds * c["scale"]).astype(BF16)
                dqm = jnp.dot(dsb, hd["kblk"], preferred_element_type=F32)
                dqm = jnp.where(half, dqm, 0.0)
                if e != ek:
                    dqm = pltpu.roll(dqm, HEAD_DIM, 1)
                dqn_blocks[mb] = add(dqn_blocks[mb], dqm)
                dkn_blocks[kb] = add(dkn_blocks[kb], lax.dot_general(dsb, hd["qm"], (((0,), (0,)), ((), ())),
                                                                     preferred_element_type=F32))
            dsk_ref[...] += dsink_vec
            dqn = jnp.concatenate(dqn_blocks, axis=1)
            dkn = jnp.concatenate(dkn_blocks, axis=1)
            dvc = jnp.concatenate(dvc_blocks, axis=1)
            yq = c["q"] * c["rq"]
            accq[...] += jnp.sum(dqn * yq, axis=0, keepdims=True)
            dyq = dqn * prm["q_g"][...]
            dq = c["rq"] * (dyq - yq * group_mean_b(dyq * yq))
            yk = c["kcat"] * c["rk"]
            acck[...] += jnp.sum(dkn * yk, axis=0, keepdims=True)
            dyk = dkn * prm["k_g"][...]
            dk = c["rk"] * (dyk - yk * group_mean_b(dyk * yk))

            slot = n % 2
            hold[slot, :, :SW] = dzu
            hold[slot, :, SW:2 * SW] = dzv
            hold[slot, :, 2 * SW:] = dq
            tmpkv[:, :KVW] = dk[:BLOCK]
            tmpkv[:, KVW:] = dvc[:BLOCK]
            newkv[:, :KVW] = dk[BLOCK:]
            newkv[:, KVW:] = dvc[BLOCK:]

        @pl.when(n >= 1)
        def _():
            dz_ref[:, :QW] = hold[(n - 1) % 2].astype(dz_ref.dtype)

        @pl.when((n >= 1) & (n < nb))
        def _():
            dz_ref[:, QW:] = (carry[...] + tmpkv[...]).astype(dz_ref.dtype)

        @pl.when(n == nb)
        def _():
            dz_ref[:, QW:] = carry[...].astype(dz_ref.dtype)
            row = lax.broadcasted_iota(jnp.int32, (BLOCK, BLOCK), 0)
            col = lax.broadcasted_iota(jnp.int32, (BLOCK, BLOCK), 1)
            for h in range(H):
                dt_ref[h] = jnp.where(row >= col, dt_ref[h], 0.0)
            dsb_ref[...] = _dot3(accb[...], gs_ref[...])
            dqg_ref[...] = _dot3(accq[...], gmq_ref[...])
            dkg_ref[...] = _dot3(acck[...], gmk_ref[...])

        @pl.when(n < nb)
        def _():
            carry[...] = newkv[...]

    kvblk = dm.koff // (2 * KVW)
    clamp = lambda n: jnp.minimum(n, nb - 1)
    full = lambda shape: pl.BlockSpec(shape, lambda n: (0,) * len(shape))
    in_specs = [pl.BlockSpec((BLOCK, IN), lambda n: (clamp(n), 0)),
                pl.BlockSpec((BLOCK, 2 * KVW), lambda n: (jnp.maximum(clamp(n) - 1, 0), kvblk)),
                pl.BlockSpec((BLOCK, SW + AW), lambda n: (clamp(n), 0)),
                full((NQ, BLOCK, 2 * BLOCK))] + _mixer_param_specs(dm, None) + [full((SW, LANES)), full((AW, LANES)),
                                                                                full((KVW, LANES))]
    out_shape = (jax.ShapeDtypeStruct((dm.S, IN), BF16),
                 jax.ShapeDtypeStruct((1, SW), F32), jax.ShapeDtypeStruct((H, BLOCK, BLOCK), F32),
                 jax.ShapeDtypeStruct((BLOCK, LANES), F32), jax.ShapeDtypeStruct((1, SW), F32),
                 jax.ShapeDtypeStruct((1, LANES), F32), jax.ShapeDtypeStruct((1, LANES), F32),
                 jax.ShapeDtypeStruct((1, AW), F32), jax.ShapeDtypeStruct((1, LANES), F32),
                 jax.ShapeDtypeStruct((NQ, BLOCK, 2 * BLOCK), F32))
    out_specs = (pl.BlockSpec((BLOCK, IN), lambda n: (jnp.maximum(n - 1, 0), 0)),
                 full((1, SW)), full((H, BLOCK, BLOCK)), full((BLOCK, LANES)), full((1, SW)), full((1, LANES)),
                 full((1, LANES)), full((1, AW)), full((1, LANES)), full((NQ, BLOCK, 2 * BLOCK)))
    scratch = [pltpu.VMEM((2, BLOCK, QW), F32), pltpu.VMEM((BLOCK, 2 * KVW), F32), pltpu.VMEM((BLOCK, 2 * KVW), F32),
               pltpu.VMEM((BLOCK, 2 * KVW), F32), pltpu.VMEM((BLOCK, SW), F32), pltpu.VMEM((1, AW), F32),
               pltpu.VMEM((1, KVW), F32)]
    return _call(
        body, out_shape=out_shape, grid=(nb + 1,), in_specs=in_specs, out_specs=out_specs, scratch=scratch,
        sem=("arbitrary",), args=[z, z, dmixed, dbias_in, *params, *gmats], name=name, comm=comm)


def _adamw(w, gparts, m, v, *, name, layer=0, stacked=None, tr=256):
    R, C = gparts[0].shape
    tr = _tile(R, max(8, min(tr, (1 << 18) // C)), 8)
    ng = len(gparts)
    bc1 = 1.0 - ADAM_B1 ** ADAM_STEP
    bc2 = 1.0 - ADAM_B2 ** ADAM_STEP

    def body(w_ref, *rest):
        g_refs, (m_ref, v_ref), (go_ref, d_ref, mo_ref, vo_ref) = rest[:ng], rest[ng:ng + 2], rest[-4:]
        g = g_refs[0][...].astype(F32)
        for r in g_refs[1:]:
            g = g + r[...].astype(F32)
        mn = ADAM_B1 * m_ref[...] + (1.0 - ADAM_B1) * g
        vn = ADAM_B2 * v_ref[...] + (1.0 - ADAM_B2) * jnp.square(g)
        m_hat = mn / bc1
        v_hat = vn / bc2
        go_ref[...] = g
        d_ref[...] = -ADAM_LR * (m_hat / (jnp.sqrt(v_hat) + ADAM_EPS) + ADAM_WD * w_ref[...])
        mo_ref[...] = mn
        vo_ref[...] = vn

    here = pl.BlockSpec((tr, C), lambda i: (layer * (R // tr) + i, 0))
    blk = pl.BlockSpec((tr, C), lambda i: (i, 0))
    prev = [] if stacked is None else list(stacked)
    return _call(body, out_shape=[jax.ShapeDtypeStruct(w.shape, F32)] * 4, grid=(R // tr,),
                 in_specs=[here] + [blk] * ng + [here, here] + [ANY] * len(prev), out_specs=[here] * 4,
                 sem=("parallel",), args=[w, *gparts, m, v, *prev], name=name,
                 aliases={3 + ng + q: q for q in range(len(prev))})


def _sum_chip_partials(g, axis, base, width, stride, got, chip, *, name, tr=256, comm=()):
    _, R, C = got.shape
    tr = _tile(R, max(8, min(tr, (1 << 18) // C)), 8)
    assert base % width == 0 and stride % width == 0 and (C == width if axis == 1 else R == width)

    def body(chip_ref, own_ref, g0_ref, g1_ref, g2_ref, o_ref):
        o_ref[...] = ((own_ref[...].astype(F32) + g0_ref[...].astype(F32)) + g1_ref[...].astype(F32)) + g2_ref[...].astype(F32)

    band = lambda j: base // width + j[0] * (stride // width)
    if axis == 1:
        own_spec = pl.BlockSpec((tr, C), lambda i, j: (i, band(j)))
    else:
        own_spec = pl.BlockSpec((tr, C), lambda i, j: (band(j) * (R // tr) + i, 0))
    part = lambda k: pl.BlockSpec((None, tr, C), lambda i, j: (k, i, 0))
    return _call(body, out_shape=[jax.ShapeDtypeStruct((R, C), F32)], grid=(R // tr,),
                 in_specs=[own_spec, part(0), part(1), part(2)], out_specs=[pl.BlockSpec((tr, C), lambda i, j: (i, 0))],
                 sem=("parallel",), args=[g, got, got, got], name=name, prefetch=chip, comm=comm)


def _pack(arrays):
    parts, total = [], 0
    for a in arrays:
        flat = a.reshape(-1).astype(F32)
        pad = (-flat.shape[0]) % (8 * LANES)
        parts.append(jnp.pad(flat, (0, pad)))
        total += flat.shape[0] + pad
    parts.append(jnp.zeros(((-total) % (PACK_ROWS * LANES),), F32))
    return jnp.concatenate(parts).reshape(-1, LANES)


def _unpack(packed, like):
    flat = packed.reshape(-1)
    out, off = [], 0
    for a in like:
        n = int(np.prod(a.shape))
        out.append(flat[off:off + n].reshape(a.shape))
        off += n + ((-n) % (8 * LANES))
    return out


def kernel(x, rel_bias, norm1_g, w_in, sgu_norm_g, sgu_w, sgu_b, q_norm_g, k_norm_g, sinks, out_norm_a, out_norm_b, w_out, norm2_g, w_gate, w_up, w_down, loss_target, m_rel_bias, m_norm1_g, m_w_in, m_sgu_norm_g, m_sgu_w, m_sgu_b, m_q_norm_g, m_k_norm_g, m_sinks, m_out_norm_a, m_out_norm_b, m_w_out, m_norm2_g, m_w_gate, m_w_up, m_w_down, v_rel_bias, v_norm1_g, v_w_in, v_sgu_norm_g, v_sgu_w, v_sgu_b, v_q_norm_g, v_k_norm_g, v_sinks, v_out_norm_a, v_out_norm_b, v_w_out, v_norm2_g, v_w_gate, v_w_up, v_w_down):
    L, D, n_in = w_in.shape
    S = x.shape[1]
    IN = N_CHIPS * n_in
    n_ff = w_gate.shape[2]
    FF = N_CHIPS * n_ff
    H = sgu_w.shape[1]
    NQ = sinks.shape[1]
    SW, AW = H * HEAD_DIM, NQ * HEAD_DIM
    KVW = (IN - 2 * SW - AW) // 2
    dm = _MixerDims(S, IN, SW, AW, KVW, H, NQ)
    assert sgu_w.shape[2] == BLOCK and q_norm_g.shape[1] == HEAD_DIM and SW + AW == D

    wb = {k: w.astype(BF16) for k, w in (("in", w_in), ("out", w_out), ("gate", w_gate), ("up", w_up), ("down", w_down))}

    def gather(l, which):
        if which == "in":
            return _Gather([wb["in"]], [(0, 0, 1, 0, n_in)], [(D, IN)], l)
        if which == "out":
            return _Gather([wb["out"]], [(0, 0, 0, 0, D // N_CHIPS)], [(D, D)], l)
        if which == "gu":
            return _Gather([wb["gate"], wb["up"]], [(0, 0, 1, 0, 2 * n_ff), (1, 0, 1, n_ff, 2 * n_ff)], [(D, 2 * FF)], l)
        return _Gather([wb["down"]], [(0, 0, 0, 0, n_ff)], [(FF, D)], l)

    nxt = lambda l, *which: [gather(l + 1, w) for w in which] if l + 1 < L else []
    W = {}
    W[0, "in"], W[0, "out"] = _comm_only([gather(0, "in"), gather(0, "out")], name="gather_first_weights")

    gs, gmod_q = _group_consts(SW)
    _, gmod_k = _group_consts(KVW)
    pblk = _same_group()
    onehot = _bucket_onehot()
    rbt = jnp.pad(rel_bias.T, ((0, 0), (0, LANES - NUM_BUCKETS)))
    (bias,) = _matmul(rbt, onehot.T, mode="nn", out_dtype=F32, exact=True, tn=4096, name="bias_table")
    bias = bias.reshape(NQ, BLOCK, 2 * BLOCK)

    def mixer_params(l):
        return [sgu_norm_g[l].reshape(1, SW), sgu_w[l], jnp.repeat(sgu_b[l].T, HEAD_DIM, axis=1),
                out_norm_a[l].reshape(1, SW), jnp.tile(q_norm_g[l], NQ).reshape(1, AW),
                jnp.tile(k_norm_g[l], dm.NKV).reshape(1, KVW), out_norm_b[l].reshape(1, AW), sinks[l], bias, pblk]

    xs = x.reshape(S, D)
    saved = []
    for l in range(L):
        h = _rms_fwd(xs, norm1_g[l].reshape(1, D), name="norm1_fwd")
        z, *got = _matmul(h, W[l, "in"], mode="nn", out_dtype=F32, tn=1792, name="in_proj",
                          comm=[gather(0, "gu")] if l == 0 else [])
        if l == 0:
            (W[0, "gu"],) = got
        units = ([gather(0, "down")] if l == 0 else []) + nxt(l, "in", "out")
        mixed, *got = _mixer_fwd(dm, z, mixer_params(l), name="mixer_fwd", comm=units)
        if l == 0:
            W[0, "down"] = got.pop(0)
        if got:
            W[l + 1, "in"], W[l + 1, "out"] = got
        (x1,) = _matmul(mixed, W[l, "out"], mode="nn", out_dtype=F32, res=xs, name="out_proj")
        h2 = _rms_fwd(x1, norm2_g[l].reshape(1, D), name="norm2_fwd")
        ab, f, *got = _gate_up_swiglu(h2, W[l, "gu"], n_ff, name="gate_up_swiglu", comm=nxt(l, "gu"))
        if got:
            (W[l + 1, "gu"],) = got
        x2, *got = _matmul(f, W[l, "down"], mode="nn", out_dtype=F32, res=x1, tk=2816, name="down_proj", comm=nxt(l, "down"))
        if got:
            (W[l + 1, "down"],) = got
        saved.append((xs, h, z, mixed, x1, h2, ab, f))
        xs = x2

    dx, dxb, loss_part = _loss_head(xs, loss_target.reshape(S, D), name="loss_head")
    loss = lax.psum(loss_part[0, 0], ("x", "y", "c"))

    dbias = jnp.zeros((NQ, BLOCK, 2 * BLOCK), F32)
    n_out = D // N_CHIPS
    big_names = ("w_in", "w_out", "w_gate", "w_up", "w_down")
    own = {k: [None] * L for k in big_names}
    got = {k: [None] * L for k in big_names}
    names_layer = ("norm1_g", "sgu_norm_g", "sgu_w", "sgu_b", "q_norm_g", "k_norm_g", "sinks", "out_norm_a", "out_norm_b",
                   "norm2_g")
    small = {k: [None] * L for k in names_layer}
    small_unit = lambda layers: _AllGather(_pack([jnp.stack([small[k][i] for i in layers]) for k in names_layer]))
    for l in reversed(range(L)):
        xl, h, z, mixed, x1, h2, ab, f = saved[l]
        early = [small_unit(range(1, L))] if (l == 0 and L > 1) else []
        dab, *gathered_early = _down_dx_swiglu_bwd(dxb, W[l, "down"], ab, n_ff, name="down_dx_swiglu_bwd", comm=early)
        (g_d,) = _matmul_tn(f, dxb, out_dtype=BF16, tm=1408, tn=1024, tt=2048, name="down_proj_dw")
        g_gu, got["w_down"][l] = _matmul_tn(h2, dab, out_dtype=BF16, tt=2048, name="gate_up_proj_dw",
                                            comm=[_Scatter([g_d], [(0, 0, 0, n_ff, n_ff)])])
        dh2, got["w_gate"][l] = _matmul(dab, W[l, "gu"], mode="nt", out_dtype=F32, tk=2816, name="gate_up_proj_dx",
                                        comm=[_Scatter([g_gu], [(0, 1, 0, n_ff, 2 * n_ff)])])
        dx1, dx1b, dg2 = _rms_bwd(x1, norm2_g[l].reshape(1, D), dh2, dx, name="norm2_bwd")
        (dmixed,) = _matmul(dx1b, W[l, "out"], mode="nt", out_dtype=F32, name="out_proj_dx")
        (g_out,) = _matmul_tn(mixed, dx1b, out_dtype=BF16, tt=2048, name="out_proj_dw")
        (dz, d_sg, d_t, d_sb, d_na, d_qg, d_kg, d_nb, d_sk, dbias, got["w_up"][l], got["w_out"][l]) = _mixer_bwd(
            dm, z, dmixed, dbias, mixer_params(l), (gs, gmod_q, gmod_k), name="mixer_bwd",
            comm=[_Scatter([g_gu, g_out], [(0, 1, n_ff, n_ff, 2 * n_ff), (1, 0, 0, n_out, n_out)])])
        (g_in,) = _matmul_tn(h, dz, out_dtype=BF16, tn=896, tt=2048, name="in_proj_dw")
        dh, got["w_in"][l] = _matmul(dz, W[l, "in"], mode="nt", out_dtype=F32, tk=1792, name="in_proj_dx",
                                     comm=[_Scatter([g_in], [(0, 1, 0, n_in, n_in)])])
        dx, dxb, dg1 = _rms_bwd(xl, norm1_g[l].reshape(1, D), dh, dx1, name="norm1_bwd")
        own["w_in"][l], own["w_out"][l] = (g_in, 1, 0, n_in, n_in), (g_out, 0, 0, n_out, n_out)
        own["w_gate"][l], own["w_up"][l] = (g_gu, 1, 0, n_ff, 2 * n_ff), (g_gu, 1, n_ff, n_ff, 2 * n_ff)
        own["w_down"][l] = (g_d, 0, 0, n_ff, n_ff)
        small["norm1_g"][l] = dg1.reshape(D)
        small["norm2_g"][l] = dg2.reshape(D)
        small["sgu_norm_g"][l] = d_sg.reshape(H, HEAD_DIM)
        small["sgu_w"][l] = d_t
        small["sgu_b"][l] = d_sb[:, :H].T
        small["q_norm_g"][l] = d_qg[0, :HEAD_DIM]
        small["k_norm_g"][l] = d_kg[0, :HEAD_DIM]
        small["sinks"][l] = d_sk[0, :NQ]
        small["out_norm_a"][l] = d_na.reshape(SW)
        small["out_norm_b"][l] = d_nb.reshape(AW)
    grad_x = dx.reshape(1, S, D)
    (d_rb,) = _matmul(dbias.reshape(NQ, BLOCK * 2 * BLOCK), onehot, mode="nn", out_dtype=F32, exact=True, tk=4096,
                      name="rel_bias_grad")
    d_rel_bias = d_rb[:, :NUM_BUCKETS].T

    chip = (2 * lax.axis_index("x") + lax.axis_index("y")).astype(jnp.int32).reshape(1)
    jobs = [(nm, l) for nm in big_names for l in range(L)]
    late = _AllGather(_pack([small[k][0][None] for k in names_layer] + [d_rel_bias]))
    mine, sib = {}, {}
    for i, (nm, l) in enumerate(jobs):
        g, axis, base, width, stride = own[nm][l]
        units = ([late] if i == 0 else []) + ([_Swap([mine[jobs[i - 1]]])] if i > 0 else [])
        mine[nm, l], *rest = _sum_chip_partials(g, axis, base, width, stride, got[nm][l], chip, name="sum_chip_partials",
                                                comm=units)
        if i == 0:
            gathered_late = rest.pop(0)
        if i > 0:
            (sib[jobs[i - 1]],) = rest
    (sib[jobs[-1]],) = _comm_only([_Swap([mine[jobs[-1]]])], name="swap_last_sums")
    wmv = dict(w_in=(w_in, m_w_in, v_w_in), w_out=(w_out, m_w_out, v_w_out), w_gate=(w_gate, m_w_gate, v_w_gate),
               w_up=(w_up, m_w_up, v_w_up), w_down=(w_down, m_w_down, v_w_down))
    big = {}
    two = lambda t: t.reshape(-1, t.shape[-1])
    for nm in big_names:
        w, m, v = wmv[nm]
        res = None
        for l in range(L):
            res = _adamw(two(w), [mine[nm, l], sib[nm, l]], two(m), two(v), layer=l, stacked=res, name="adamw_" + nm)
        big[nm] = [r.reshape(w.shape) for r in res]

    w_small = dict(rel_bias=rel_bias, norm1_g=norm1_g, sgu_norm_g=sgu_norm_g, sgu_w=sgu_w, sgu_b=sgu_b, q_norm_g=q_norm_g,
                   k_norm_g=k_norm_g, sinks=sinks, out_norm_a=out_norm_a, out_norm_b=out_norm_b, norm2_g=norm2_g)
    m_small = dict(rel_bias=m_rel_bias, norm1_g=m_norm1_g, sgu_norm_g=m_sgu_norm_g, sgu_w=m_sgu_w, sgu_b=m_sgu_b,
                   q_norm_g=m_q_norm_g, k_norm_g=m_k_norm_g, sinks=m_sinks, out_norm_a=m_out_norm_a,
                   out_norm_b=m_out_norm_b, norm2_g=m_norm2_g)
    v_small = dict(rel_bias=v_rel_bias, norm1_g=v_norm1_g, sgu_norm_g=v_sgu_norm_g, sgu_w=v_sgu_w, sgu_b=v_sgu_b,
                   q_norm_g=v_q_norm_g, k_norm_g=v_k_norm_g, sinks=v_sinks, out_norm_a=v_out_norm_a,
                   out_norm_b=v_out_norm_b, norm2_g=v_norm2_g)

    def adamw_small(gathered, pick, name):
        like = pick(w_small)
        res = _adamw(_pack(like), [gathered[i] for i in range(N_DEV)], _pack(pick(m_small)), _pack(pick(v_small)), name=name)
        return [_unpack(r, like) for r in res]

    first_layer = adamw_small(gathered_late, lambda d: [d[k][:1] for k in names_layer] + [d["rel_bias"]], "adamw_small_late")
    if L > 1:
        others = adamw_small(gathered_early[0], lambda d: [d[k][1:] for k in names_layer], "adamw_small_early")
    sm = {"rel_bias": [r[-1] for r in first_layer]}
    for j, k in enumerate(names_layer):
        sm[k] = [jnp.concatenate([first_layer[q][j]] + ([others[q][j]] if L > 1 else [])) for q in range(4)]

    order = ("rel_bias", "norm1_g", "w_in", "sgu_norm_g", "sgu_w", "sgu_b", "q_norm_g", "k_norm_g", "sinks", "out_norm_a",
             "out_norm_b", "w_out", "norm2_g", "w_gate", "w_up", "w_down")
    pick = lambda k, i: big[k][i] if k in big else sm[k][i]
    outs = [loss, grad_x]
    for i in range(4):
        outs += [pick(k, i) for k in order]
    return tuple(outs)
```

```python
import functools
import math

import numpy as np

import jax
import jax.numpy as jnp
from jax import lax
from jax.experimental import pallas as pl
from jax.experimental.pallas import tpu as pltpu

F32 = jnp.float32
BF16 = jnp.bfloat16
MESH = pl.DeviceIdType.MESH
ANY = pl.BlockSpec(memory_space=pl.ANY)

HEAD_DIM = 64
BLOCK = 128
NUM_BUCKETS = 32
MAX_DISTANCE = 128
EPS = 1e-6
NEG_INF = -1e30
ADAM_LR, ADAM_B1, ADAM_B2, ADAM_EPS, ADAM_WD, ADAM_STEP = 0.001, 0.9, 0.999, 1e-08, 0.01, 10

LANES = 128
VMEM_LIMIT = 56 * 1024 * 1024
N_CHIPS = 4
N_DEV = 8
PACK_ROWS = 256
ROW_CHUNK = 256


def _cparams(sem=None):
    return pltpu.CompilerParams(dimension_semantics=sem, vmem_limit_bytes=VMEM_LIMIT)


def _tile(dim, target, align=LANES):
    best = None
    for t in range(align, min(dim, target) + 1, align):
        if dim % t == 0:
            best = t
    return best if best is not None else dim


def _mesh_pos():
    return lax.axis_index("x"), lax.axis_index("y"), lax.axis_index("c")


def _other_chips(x, y):
    return [(1 - x, y), (x, 1 - y), (1 - x, 1 - y)]


def _slab(ref, axis, start, size, half=None):
    if axis == 1:
        rows = ref.shape[0]
        r = pl.ds(0, rows) if half is None else pl.ds(pl.multiple_of(half * (rows // 2), 16), rows // 2)
        return ref.at[r, pl.ds(pl.multiple_of(start, LANES), size)]
    if half is None:
        return ref.at[pl.ds(pl.multiple_of(start, 16), size), :]
    return ref.at[pl.ds(pl.multiple_of(start + half * (size // 2), 16), size // 2), :]


def _remote(src, dst, send_sem, recv_sem, to):
    return pltpu.make_async_remote_copy(src_ref=src, dst_ref=dst, send_sem=send_sem, recv_sem=recv_sem,
                                        device_id=to, device_id_type=MESH)


class _Gather:
    def __init__(self, shards, streams, out_shapes, layer):
        self.ins, self.streams, self.layer = list(shards), streams, layer
        self.outs = [jax.ShapeDtypeStruct(s, BF16) for s in out_shapes]
        ns = len(streams)
        self.sems = [pltpu.SemaphoreType.DMA((ns, 3))] * 4 + [pltpu.SemaphoreType.DMA((ns,))]

    def _sent(self, srcs, outs, sems):
        send, recv, _, _, local = sems
        x, y, c = _mesh_pos()
        me_j = 2 * x + y
        own, sends = [], []
        for s, (si, oi, axis, base, stride) in enumerate(self.streams):
            src, out = srcs[si].at[self.layer], outs[oi]
            width = src.shape[axis]
            own.append(pltpu.make_async_copy(src, _slab(out, axis, base + me_j * stride, width), local.at[s]))
            half_rows = src.shape[0] // 2
            mine = src.at[pl.ds(pl.multiple_of(c * half_rows, 16), half_rows), :]
            for k, (px, py) in enumerate(_other_chips(x, y)):
                sends.append(_remote(mine, _slab(out, axis, base + me_j * stride, width, half=c),
                                     send.at[s, k], recv.at[s, k], (px, py, c)))
        return own, sends

    def start(self, srcs, outs, sems):
        own, sends = self._sent(srcs, outs, sems)
        for cp in own + sends:
            cp.start()

    def finish(self, srcs, outs, sems):
        send, recv, fsend, frecv, _ = sems
        x, y, c = _mesh_pos()
        sib = (x, y, 1 - c)
        forwards, fwd_arrivals = [], []
        for s, (si, oi, axis, base, stride) in enumerate(self.streams):
            out = outs[oi]
            width = srcs[si].shape[1 + axis]
            for k, (px, py) in enumerate(_other_chips(x, y)):
                start = base + (2 * px + py) * stride
                got = _slab(out, axis, start, width, half=c)
                _remote(got, got, send.at[s, k], recv.at[s, k], (px, py, c)).wait_recv()
                fwd = _remote(got, got, fsend.at[s, k], frecv.at[s, k], sib)
                fwd.start()
                forwards.append(fwd)
                theirs = _slab(out, axis, start, width, half=1 - c)
                fwd_arrivals.append(_remote(theirs, theirs, fsend.at[s, k], frecv.at[s, k], sib))
        for a in fwd_arrivals:
            a.wait_recv()
        for cp in forwards:
            cp.wait_send()
        own, sends = self._sent(srcs, outs, sems)
        for cp in sends:
            cp.wait_send()
        for cp in own:
            cp.wait()


class _Scatter:
    def __init__(self, grads, streams):
        self.ins, self.streams = list(grads), streams
        shard = lambda g, axis, width: (width, g.shape[1]) if axis == 0 else (g.shape[0], width)
        self.outs = [jax.ShapeDtypeStruct((3,) + shard(grads[gi], axis, width), BF16) for gi, axis, _, width, _ in streams]
        self.sems = [pltpu.SemaphoreType.DMA((len(streams), 3))] * 2

    def _copies(self, srcs, outs, sems):
        send, recv = sems
        x, y, c = _mesh_pos()
        copies = []
        for s, (gi, axis, base, width, stride) in enumerate(self.streams):
            for k, (px, py) in enumerate(_other_chips(x, y)):
                copies.append(_remote(_slab(srcs[gi], axis, base + (2 * px + py) * stride, width), outs[s].at[k],
                                      send.at[s, k], recv.at[s, k], (px, py, c)))
        return copies

    def start(self, srcs, outs, sems):
        for cp in self._copies(srcs, outs, sems):
            cp.start()

    def finish(self, srcs, outs, sems):
        for cp in self._copies(srcs, outs, sems):
            cp.wait()


class _Swap:
    def __init__(self, arrays):
        self.ins = list(arrays)
        self.outs = [jax.ShapeDtypeStruct(a.shape, a.dtype) for a in arrays]
        self.sems = [pltpu.SemaphoreType.DMA((len(arrays),))] * 2

    def _copies(self, srcs, outs, sems):
        send, recv = sems
        x, y, c = _mesh_pos()
        return [_remote(srcs[s], outs[s], send.at[s], recv.at[s], (x, y, 1 - c)) for s in range(len(srcs))]

    def start(self, srcs, outs, sems):
        for cp in self._copies(srcs, outs, sems):
            cp.start()

    def finish(self, srcs, outs, sems):
        for cp in self._copies(srcs, outs, sems):
            cp.wait()


class _AllGather:
    def __init__(self, part):
        self.ins = [part]
        self.outs = [jax.ShapeDtypeStruct((N_DEV,) + part.shape, part.dtype)]
        self.sems = [pltpu.SemaphoreType.DMA((7,)), pltpu.SemaphoreType.DMA((7,)), pltpu.SemaphoreType.DMA(())]

    def _first(self, srcs, outs, sems):
        (x_ref,), (out,), (send, recv, local) = srcs, outs, sems
        x, y, c = _mesh_pos()
        mine = out.at[4 * x + 2 * y + c]
        own = pltpu.make_async_copy(x_ref, mine, local)
        sends = [_remote(x_ref, mine, send.at[0], recv.at[0], (x, y, 1 - c))]
        sends += [_remote(x_ref, mine, send.at[1 + k], recv.at[1 + k], (px, py, c)) for k, (px, py) in enumerate(_other_chips(x, y))]
        return own, sends

    def start(self, srcs, outs, sems):
        own, sends = self._first(srcs, outs, sems)
        own.start()
        for cp in sends:
            cp.start()

    def finish(self, srcs, outs, sems):
        (out,), (send, recv, local) = outs, sems
        x, y, c = _mesh_pos()
        me, sib = (x, y, c), (x, y, 1 - c)
        slot = lambda px, py, pc: out.at[4 * px + 2 * py + pc]
        passed = []
        for k, (px, py) in enumerate(_other_chips(x, y)):
            blk = slot(px, py, c)
            _remote(blk, blk, send.at[1 + k], recv.at[1 + k], me).wait_recv()
            fwd = _remote(blk, blk, send.at[4 + k], recv.at[4 + k], sib)
            fwd.start()
            passed.append(fwd)
        blk = slot(x, y, 1 - c)
        _remote(blk, blk, send.at[0], recv.at[0], me).wait_recv()
        for k, (px, py) in enumerate(_other_chips(x, y)):
            blk = slot(px, py, 1 - c)
            _remote(blk, blk, send.at[4 + k], recv.at[4 + k], me).wait_recv()
        own, sends = self._first(srcs, outs, sems)
        for cp in sends + passed:
            cp.wait_send()
        own.wait()


def _call(body, *, grid, in_specs, out_specs, out_shape, args, name, scratch=(), sem=None, comm=(), prefetch=None,
          aliases=None):
    out_shape, out_specs = tuple(out_shape), tuple(out_specs)
    n_in, n_out, n_scr = len(in_specs), len(out_shape), len(scratch)
    n_pre = 0 if prefetch is None else 1
    c_ins = [a for u in comm for a in u.ins]
    c_outs = [o for u in comm for o in u.outs]
    c_sems = [s for u in comm for s in u.sems]

    def wrapped(*refs):
        pre, refs = refs[:n_pre], refs[n_pre:]
        ins, rest = refs[:n_in], refs[n_in:]
        cin, rest = rest[:len(c_ins)], rest[len(c_ins):]
        outs, rest = rest[:n_out], rest[n_out:]
        cout, rest = rest[:len(c_outs)], rest[len(c_outs):]
        scr, csem = rest[:n_scr], rest[n_scr:]

        def each(fn_name):
            i = o = s = 0
            for u in comm:
                getattr(u, fn_name)(cin[i:i + len(u.ins)], cout[o:o + len(u.outs)], csem[s:s + len(u.sems)])
                i, o, s = i + len(u.ins), o + len(u.outs), s + len(u.sems)

        if comm:
            pids = [pl.program_id(d) for d in range(len(grid))]
            first = functools.reduce(jnp.logical_and, [p == 0 for p in pids])
            last = functools.reduce(jnp.logical_and, [p == g - 1 for p, g in zip(pids, grid)])
            pl.when(first)(lambda: each("start"))
        body(*pre, *ins, *outs, *scr)
        if comm:
            pl.when(last)(lambda: each("finish"))

    if comm:
        sem = ("arbitrary",) * len(grid)
    all_in = list(in_specs) + [ANY] * len(c_ins)
    all_out = out_specs + tuple([ANY] * len(c_outs))
    all_scr = list(scratch) + c_sems
    kw = {}
    if aliases:
        kw["input_output_aliases"] = {n_pre + i: o for i, o in aliases.items()}
    if prefetch is None:
        kw.update(grid=grid, in_specs=all_in, out_specs=all_out, scratch_shapes=all_scr)
        pre_args = []
    else:
        kw["grid_spec"] = pltpu.PrefetchScalarGridSpec(num_scalar_prefetch=1, grid=grid, in_specs=all_in, out_specs=all_out,
                                                       scratch_shapes=all_scr)
        pre_args = [prefetch]
    return pl.pallas_call(wrapped, out_shape=out_shape + tuple(c_outs), compiler_params=_cparams(sem), name=name, **kw)(
        *pre_args, *args, *c_ins)


def _comm_only(comm, *, name):
    def body(tick_ref):
        tick_ref[...] = jnp.zeros(tick_ref.shape, tick_ref.dtype)

    outs = _call(body, grid=(1,), in_specs=[], out_specs=[pl.BlockSpec((8, LANES), lambda i: (0, 0))],
                 out_shape=[jax.ShapeDtypeStruct((8, LANES), F32)], args=[], name=name, comm=comm)
    return outs[1:]


def _split3(x):
    hi = x.astype(BF16)
    r1 = x - hi.astype(F32)
    mid = r1.astype(BF16)
    lo = (r1 - mid.astype(F32)).astype(BF16)
    return hi, mid, lo


def _dot3(x, g):
    hi, mid, lo = _split3(x)
    d = lambda a: jnp.dot(a, g, preferred_element_type=F32)
    return d(hi) + d(mid) + d(lo)


GROUP_TILE = 256


def _group_sum(x, pblk):
    hi = x.astype(BF16)
    lo = (x - hi.astype(F32)).astype(BF16)
    cols = []
    for b in range(x.shape[1] // GROUP_TILE):
        sl = slice(GROUP_TILE * b, GROUP_TILE * (b + 1))
        cols.append(jnp.dot(hi[:, sl], pblk, preferred_element_type=F32) + jnp.dot(lo[:, sl], pblk, preferred_element_type=F32))
    return cols[0] if len(cols) == 1 else jnp.concatenate(cols, axis=1)


def _matmul(a, b, *, mode, out_dtype, name, res=None, exact=False, tm=1024, tn=1024, tk=2048, b_col0=0, comm=()):
    M, K = a.shape
    N = b.shape[1] if mode == "nn" else b.shape[0]
    tm, tn, tk = _tile(M, tm, 8 if M < LANES else LANES), _tile(N, tn), _tile(K, tk)
    nk = K // tk
    dn = (((1,), (0,)), ((), ())) if mode == "nn" else (((1,), (1,)), ((), ()))

    def body(*refs):
        a_ref, b_ref = refs[0], refs[1]
        r_ref = refs[2] if res is not None else None
        o_ref = refs[3] if res is not None else refs[2]
        if exact:
            part = _dot3(a_ref[...], b_ref[...])
        else:
            part = lax.dot_general(a_ref[...].astype(BF16), b_ref[...].astype(BF16), dn, preferred_element_type=F32)

        def finish(total):
            if r_ref is not None:
                total = r_ref[...] + total
            o_ref[...] = total.astype(o_ref.dtype)

        if nk == 1:
            finish(part)
        else:
            acc = refs[-1]
            k = pl.program_id(2)

            @pl.when(k == 0)
            def _():
                acc[...] = part

            @pl.when(k > 0)
            def _():
                acc[...] += part

            @pl.when(k == nk - 1)
            def _():
                finish(acc[...])

    if mode == "nn":
        b_spec = pl.BlockSpec((tk, tn), lambda j, i, k: (k, j))
    else:
        assert b_col0 % tk == 0
        b_spec = pl.BlockSpec((tn, tk), lambda j, i, k: (j, k + b_col0 // tk))
    in_specs = [pl.BlockSpec((tm, tk), lambda j, i, k: (i, k)), b_spec]
    args = [a, b]
    if res is not None:
        in_specs.append(pl.BlockSpec((tm, tn), lambda j, i, k: (i, j)))
        args.append(res)
    return _call(
        body, grid=(N // tn, M // tm, nk), in_specs=in_specs,
        out_specs=[pl.BlockSpec((tm, tn), lambda j, i, k: (i, j))], out_shape=[jax.ShapeDtypeStruct((M, N), out_dtype)],
        scratch=[pltpu.VMEM((tm, tn), F32)] if nk > 1 else [], sem=("parallel", "parallel", "arbitrary"),
        args=args, name=name, comm=comm)


def _matmul_tn(a, b, *, out_dtype, name, tm=2048, tn=1024, tt=1024, comm=()):
    T, Mo = a.shape
    N = b.shape[1]
    tm, tn, tt = _tile(Mo, tm), _tile(N, tn), _tile(T, tt)
    nt = T // tt

    def body(a_ref, b_ref, o_ref, acc):
        t = pl.program_id(2)
        part = lax.dot_general(a_ref[...].astype(BF16), b_ref[...].astype(BF16), (((0,), (0,)), ((), ())),
                               preferred_element_type=F32)

        @pl.when(t == 0)
        def _():
            acc[...] = part

        @pl.when(t > 0)
        def _():
            acc[...] += part

        @pl.when(t == nt - 1)
        def _():
            o_ref[...] = acc[...].astype(o_ref.dtype)

    return _call(
        body, grid=(Mo // tm, N // tn, nt),
        in_specs=[pl.BlockSpec((tt, tm), lambda i, j, t: (t, i)), pl.BlockSpec((tt, tn), lambda i, j, t: (t, j))],
        out_specs=[pl.BlockSpec((tm, tn), lambda i, j, t: (i, j))], out_shape=[jax.ShapeDtypeStruct((Mo, N), out_dtype)],
        scratch=[pltpu.VMEM((tm, tn), F32)], sem=("parallel", "parallel", "arbitrary"), args=[a, b], name=name, comm=comm)


def _rms_fwd(x, g, *, name, tr=256):
    R, D = x.shape
    tr = _tile(R, tr, 8)

    def body(x_ref, g_ref, o_ref):
        xv = x_ref[...]
        r = lax.rsqrt(jnp.mean(xv * xv, axis=-1, keepdims=True) + EPS)
        o_ref[...] = (xv * r * g_ref[...]).astype(o_ref.dtype)

    return pl.pallas_call(
        body, out_shape=jax.ShapeDtypeStruct((R, D), BF16), grid=(R // tr,),
        in_specs=[pl.BlockSpec((tr, D), lambda i: (i, 0)), pl.BlockSpec((1, D), lambda i: (0, 0))],
        out_specs=pl.BlockSpec((tr, D), lambda i: (i, 0)),
        compiler_params=_cparams(("parallel",)), name=name,
    )(x, g)


def _rms_bwd(x, g, dh, dres, *, name, tr=256):
    R, D = x.shape
    tr = _tile(R, tr, 8)

    def body(x_ref, g_ref, dh_ref, dres_ref, dx_ref, dxb_ref, dg_ref):
        i = pl.program_id(0)
        xv = x_ref[...]
        r = lax.rsqrt(jnp.mean(xv * xv, axis=-1, keepdims=True) + EPS)
        y = xv * r
        dhv = dh_ref[...]
        dy = dhv * g_ref[...]
        dx = dres_ref[...] + r * (dy - y * jnp.mean(dy * y, axis=-1, keepdims=True))
        dx_ref[...] = dx
        dxb_ref[...] = dx.astype(BF16)
        dg = jnp.sum(dhv * y, axis=0, keepdims=True)

        @pl.when(i == 0)
        def _():
            dg_ref[...] = dg

        @pl.when(i > 0)
        def _():
            dg_ref[...] += dg

    row = pl.BlockSpec((tr, D), lambda i: (i, 0))
    vec = pl.BlockSpec((1, D), lambda i: (0, 0))
    return pl.pallas_call(
        body, out_shape=(jax.ShapeDtypeStruct((R, D), F32), jax.ShapeDtypeStruct((R, D), BF16),
                         jax.ShapeDtypeStruct((1, D), F32)), grid=(R // tr,),
        in_specs=[row, vec, row, row], out_specs=(row, row, vec),
        compiler_params=_cparams(("arbitrary",)), name=name,
    )(x, g, dh, dres)


def _gate_up_swiglu(h, w_gu, n_ff, *, name, tm=512, comm=()):
    M, K = h.shape
    nsh = w_gu.shape[1] // (2 * n_ff)
    tm = _tile(M, tm)

    def body(h_ref, w_ref, ab_ref, f_ref):
        for c in range(tm // _tile(tm, ROW_CHUNK)):
            rows = pl.ds(c * _tile(tm, ROW_CHUNK), _tile(tm, ROW_CHUNK))
            ab = jnp.dot(h_ref[rows, :], w_ref[...], preferred_element_type=F32)
            ab_ref[rows, :] = ab.astype(ab_ref.dtype)
            a, b = ab[:, :n_ff], ab[:, n_ff:]
            f_ref[rows, :] = (a * jax.nn.sigmoid(a) * b).astype(f_ref.dtype)

    return _call(
        body, grid=(nsh, M // tm),
        in_specs=[pl.BlockSpec((tm, K), lambda j, i: (i, 0)), pl.BlockSpec((K, 2 * n_ff), lambda j, i: (0, j))],
        out_specs=[pl.BlockSpec((tm, 2 * n_ff), lambda j, i: (i, j)), pl.BlockSpec((tm, n_ff), lambda j, i: (i, j))],
        out_shape=[jax.ShapeDtypeStruct((M, 2 * nsh * n_ff), BF16), jax.ShapeDtypeStruct((M, nsh * n_ff), BF16)],
        sem=("parallel", "parallel"), args=[h, w_gu], name=name, comm=comm)


def _down_dx_swiglu_bwd(dy, w_d, ab, n_ff, *, name, tm=1024, comm=()):
    M, K = dy.shape
    nsh = w_d.shape[0] // n_ff
    tm = _tile(M, tm)

    def body(dy_ref, w_ref, ab_ref, dab_ref):
        for c in range(tm // _tile(tm, ROW_CHUNK)):
            rows = pl.ds(c * _tile(tm, ROW_CHUNK), _tile(tm, ROW_CHUNK))
            df = lax.dot_general(dy_ref[rows, :], w_ref[...], (((1,), (1,)), ((), ())), preferred_element_type=F32)
            av = ab_ref[rows, :n_ff].astype(F32)
            s = jax.nn.sigmoid(av)
            dab_ref[rows, :n_ff] = (df * ab_ref[rows, n_ff:].astype(F32) * (s * (1.0 + av * (1.0 - s)))).astype(dab_ref.dtype)
            dab_ref[rows, n_ff:] = (df * (av * s)).astype(dab_ref.dtype)

    pair = pl.BlockSpec((tm, 2 * n_ff), lambda j, i: (i, j))
    return _call(
        body, grid=(nsh, M // tm),
        in_specs=[pl.BlockSpec((tm, K), lambda j, i: (i, 0)), pl.BlockSpec((n_ff, K), lambda j, i: (j, 0)), pair],
        out_specs=[pair], out_shape=[jax.ShapeDtypeStruct(ab.shape, BF16)],
        sem=("parallel", "parallel"), args=[dy, w_d, ab], name=name, comm=comm)


def _loss_head(y, target, *, name, tr=256):
    R, D = y.shape
    tr = _tile(R, tr, 8)

    def body(y_ref, t_ref, dy_ref, dyb_ref, l_ref):
        i = pl.program_id(0)
        e = y_ref[...] - t_ref[...]
        dy = e * (1.0 / D)
        dy_ref[...] = dy
        dyb_ref[...] = dy.astype(BF16)
        part = 0.5 * jnp.sum(jnp.mean(e * e, axis=-1, keepdims=True), axis=0, keepdims=True)
        part = jnp.broadcast_to(part, (8, LANES))

        @pl.when(i == 0)
        def _():
            l_ref[...] = part

        @pl.when(i > 0)
        def _():
            l_ref[...] += part

    row = pl.BlockSpec((tr, D), lambda i: (i, 0))
    return pl.pallas_call(
        body, out_shape=(jax.ShapeDtypeStruct((R, D), F32), jax.ShapeDtypeStruct((R, D), BF16),
                         jax.ShapeDtypeStruct((8, LANES), F32)), grid=(R // tr,),
        in_specs=[row, row], out_specs=(row, row, pl.BlockSpec((8, LANES), lambda i: (0, 0))),
        compiler_params=_cparams(("arbitrary",)), name=name,
    )(y, target)


def _gelu(x):
    return 0.5 * x * (1.0 + lax.erf(x * math.sqrt(0.5)))


def _gelu_grad(x):
    return 0.5 * (1.0 + lax.erf(x * math.sqrt(0.5))) + x * jnp.exp(-0.5 * x * x) * (1.0 / math.sqrt(2.0 * math.pi))


def _group_consts(width):
    lane = np.arange(width)
    col = np.arange(LANES)
    grp = (lane[:, None] // HEAD_DIM == col[None, :]).astype(np.float32)
    mod = ((lane[:, None] % HEAD_DIM == col[None, :]) & (col[None, :] < HEAD_DIM)).astype(np.float32)
    return jnp.asarray(grp, BF16), jnp.asarray(mod, BF16)


def _same_group():
    lane = np.arange(GROUP_TILE) // HEAD_DIM
    return jnp.asarray((lane[:, None] == lane[None, :]).astype(np.float32), BF16)


def _bucket_onehot():
    qi = np.arange(BLOCK)[:, None]
    kj = np.arange(2 * BLOCK)[None, :]
    n = np.maximum(qi + BLOCK - kj, 0)
    max_exact = NUM_BUCKETS // 2
    nf = np.maximum(n, 1).astype(np.float32)
    large = max_exact + (np.log(nf / np.float32(max_exact)) / np.float32(math.log(MAX_DISTANCE / max_exact))
                         * np.float32(NUM_BUCKETS - max_exact)).astype(np.int32)
    large = np.minimum(large, NUM_BUCKETS - 1)
    bucket = jnp.asarray(np.where(n < max_exact, n, large).reshape(-1).astype(np.int32))
    return (bucket[:, None] == jnp.arange(LANES, dtype=jnp.int32)[None, :]).astype(BF16)


class _MixerDims:
    def __init__(self, S, IN, SW, AW, KVW, H, NQ):
        self.S, self.IN, self.SW, self.AW, self.KVW, self.H, self.NQ = S, IN, SW, AW, KVW, H, NQ
        self.NKV = KVW // HEAD_DIM
        self.GROUP = NQ // self.NKV
        self.nb = S // BLOCK
        self.koff = 2 * SW + AW
        self.voff = self.koff + KVW
        assert SW % LANES == 0 and AW % LANES == 0 and KVW % LANES == 0 and self.GROUP % 2 == 0
        assert self.koff % (2 * KVW) == 0 and IN == self.voff + KVW and S % BLOCK == 0


def _mixer_block(dm, n, z, kvp, prm):
    SW, AW, KVW, H, NQ = dm.SW, dm.AW, dm.KVW, dm.H, dm.NQ
    lane = lax.broadcasted_iota(jnp.int32, (BLOCK, LANES), 1)
    lo = lane < HEAD_DIM
    row = lax.broadcasted_iota(jnp.int32, (BLOCK, BLOCK), 0)
    col = lax.broadcasted_iota(jnp.int32, (BLOCK, BLOCK), 1)
    tril = row >= col
    pblk = prm["pblk"][...]
    inv = 1.0 / HEAD_DIM

    def group_rsqrt(x):
        return lax.rsqrt(_group_sum(x * x, pblk) * inv + EPS)

    zu, zv = z[:, :SW], z[:, SW:2 * SW]
    u, v = _gelu(zu), _gelu(zv)
    rv = group_rsqrt(v)
    vn = v * rv * prm["sgu_g"][...]
    vnb = vn.astype(BF16)
    tmats, gate_blocks = [], []
    for p in range(H // 2):
        blk = slice(LANES * p, LANES * (p + 1))
        t0 = jnp.where(tril, prm["sgu_w"][2 * p], 0.0).astype(BF16)
        t1 = jnp.where(tril, prm["sgu_w"][2 * p + 1], 0.0).astype(BF16)
        tmats += [t0, t1]
        g0 = jnp.dot(t0, vnb[:, blk], preferred_element_type=F32)
        g1 = jnp.dot(t1, vnb[:, blk], preferred_element_type=F32)
        gate_blocks.append(jnp.where(lo, g0, g1) + prm["sgu_bias"][:, blk])
    gate = jnp.concatenate(gate_blocks, axis=1)
    outa = u * gate
    ra = lax.rsqrt(jnp.mean(outa * outa, axis=-1, keepdims=True) + EPS)

    q = z[:, 2 * SW:2 * SW + AW]
    kcat = jnp.concatenate([kvp[:, :KVW], z[:, dm.koff:dm.koff + KVW]], axis=0)
    vcat = jnp.concatenate([kvp[:, KVW:], z[:, dm.voff:dm.voff + KVW]], axis=0)
    rq = group_rsqrt(q)
    rk = group_rsqrt(kcat)
    qn = q * rq * prm["q_g"][...]
    kn = kcat * rk * prm["k_g"][...]
    knb, vcb = kn.astype(BF16), vcat.astype(BF16)
    qi = lax.broadcasted_iota(jnp.int32, (BLOCK, 2 * BLOCK), 0)
    kj = lax.broadcasted_iota(jnp.int32, (BLOCK, 2 * BLOCK), 1)
    valid = (kj > qi) & (kj <= qi + BLOCK) & ((n > 0) | (kj >= BLOCK))
    scale = 1.0 / math.sqrt(HEAD_DIM)
    heads = []
    out_blocks = []
    for hq in range(NQ):
        mb, e = hq // 2, hq % 2
        kv = hq // dm.GROUP
        kb, ek = kv // 2, kv % 2
        qblk = qn[:, LANES * mb:LANES * (mb + 1)]
        if e != ek:
            qblk = pltpu.roll(qblk, HEAD_DIM, 1)
        half = lo if ek == 0 else jnp.logical_not(lo)
        qm = jnp.where(half, qblk, 0.0).astype(BF16)
        kblk = knb[:, LANES * kb:LANES * (kb + 1)]
        vblk = vcb[:, LANES * kb:LANES * (kb + 1)]
        s = lax.dot_general(qm, kblk, (((1,), (1,)), ((), ())), preferred_element_type=F32) * scale + prm["bias"][hq]
        s = jnp.where(valid, s, NEG_INF)
        sink = prm["sinks"][hq]
        mx = jnp.maximum(jnp.max(s, axis=-1, keepdims=True), sink)
        ex = jnp.exp(s - mx)
        den = jnp.sum(ex, axis=-1, keepdims=True) + jnp.exp(sink - mx)
        inv_den = 1.0 / den
        pr = ex * inv_den
        psink = jnp.exp(sink - mx) * inv_den
        prb = pr.astype(BF16)
        r_h = jnp.dot(prb, vblk, preferred_element_type=F32)
        if e != ek:
            r_h = pltpu.roll(r_h, HEAD_DIM, 1)
        heads.append(dict(qm=qm, kblk=kblk, vblk=vblk, pr=pr, prb=prb, psink=psink, half=half, mb=mb, e=e, ek=ek, kb=kb))
        if e == 1:
            out_blocks.append(jnp.where(lo, prev_r, r_h))
        prev_r = r_h
    outb = jnp.concatenate(out_blocks, axis=1)
    rb = lax.rsqrt(jnp.mean(outb * outb, axis=-1, keepdims=True) + EPS)
    return dict(lo=lo, tril=tril, pblk=pblk, zu=zu, zv=zv, u=u, v=v, rv=rv, vnb=vnb, tmats=tmats, gate=gate,
                outa=outa, ra=ra, q=q, kcat=kcat, rq=rq, rk=rk, heads=heads, outb=outb, rb=rb, scale=scale)


_MIXER_PARAMS = ("sgu_g", "sgu_w", "sgu_bias", "norm_a", "q_g", "k_g", "norm_b", "sinks", "bias", "pblk")


def _mixer_param_specs(dm, idx):
    SW, AW, KVW = dm.SW, dm.AW, dm.KVW
    full = lambda shape: pl.BlockSpec(shape, lambda n: (0,) * len(shape))
    return [full((1, SW)), full((dm.H, BLOCK, BLOCK)), full((BLOCK, SW)), full((1, SW)), full((1, AW)), full((1, KVW)),
            full((1, AW)), pl.BlockSpec(memory_space=pltpu.SMEM), full((dm.NQ, BLOCK, 2 * BLOCK)),
            full((GROUP_TILE, GROUP_TILE))]


def _mixer_fwd(dm, z, params, *, name, comm=()):
    nb = dm.nb

    def body(z_ref, kvp_ref, *rest):
        prm = dict(zip(_MIXER_PARAMS, rest[:len(_MIXER_PARAMS)]))
        o_ref = rest[len(_MIXER_PARAMS)]
        n = pl.program_id(0)
        c = _mixer_block(dm, n, z_ref[...], kvp_ref[...], prm)
        o_ref[:, :dm.SW] = (c["outa"] * c["ra"] * prm["norm_a"][...]).astype(o_ref.dtype)
        o_ref[:, dm.SW:] = (c["outb"] * c["rb"] * prm["norm_b"][...]).astype(o_ref.dtype)

    kvblk = dm.koff // (2 * dm.KVW)
    in_specs = [pl.BlockSpec((BLOCK, dm.IN), lambda n: (n, 0)),
                pl.BlockSpec((BLOCK, 2 * dm.KVW), lambda n: (jnp.maximum(n - 1, 0), kvblk))] + _mixer_param_specs(dm, None)
    return _call(
        body, out_shape=[jax.ShapeDtypeStruct((dm.S, dm.SW + dm.AW), BF16)], grid=(nb,),
        in_specs=in_specs, out_specs=[pl.BlockSpec((BLOCK, dm.SW + dm.AW), lambda n: (n, 0))],
        sem=("arbitrary",), args=[z, z, *params], name=name, comm=comm)


def _mixer_bwd(dm, z, dmixed, dbias_in, params, gmats, *, name, comm=()):
    SW, AW, KVW, H, NQ, nb, IN = dm.SW, dm.AW, dm.KVW, dm.H, dm.NQ, dm.nb, dm.IN
    QW = 2 * SW + AW
    NP = len(_MIXER_PARAMS)

    def body(z_ref, kvp_ref, dm_ref, dbin_ref, *rest):
        prm = dict(zip(_MIXER_PARAMS, rest[:NP]))
        gs_ref, gmq_ref, gmk_ref = rest[NP:NP + 3]
        (dz_ref, dsg_ref, dt_ref, dsb_ref, dna_ref, dqg_ref, dkg_ref, dnb_ref, dsk_ref, dbias_ref) = rest[NP + 3:NP + 13]
        hold, tmpkv, newkv, carry, accb, accq, acck = rest[NP + 13:]
        n = pl.program_id(0)

        @pl.when(n == 0)
        def _():
            for r in (dsg_ref, dt_ref, dna_ref, dnb_ref, dsk_ref, accb, accq, acck):
                r[...] = jnp.zeros(r.shape, r.dtype)
            dbias_ref[...] = dbin_ref[...]

        @pl.when(n < nb)
        def _():
            c = _mixer_block(dm, n, z_ref[...], kvp_ref[...], prm)
            lo = c["lo"]
            dmx = dm_ref[...]
            inv = 1.0 / HEAD_DIM

            def rms_bwd_full(dy_scaled, y, r):
                return r * (dy_scaled - y * jnp.mean(dy_scaled * y, axis=-1, keepdims=True))

            def group_mean_b(x):
                return _group_sum(x, c["pblk"]) * inv

            dma = dmx[:, :SW]
            ya = c["outa"] * c["ra"]
            dna_ref[...] += jnp.sum(dma * ya, axis=0, keepdims=True)
            douta = rms_bwd_full(dma * prm["norm_a"][...], ya, c["ra"])
            du = douta * c["gate"]
            dgate = douta * c["u"]
            accb[...] += dgate
            dgb16 = dgate.astype(BF16)
            dvn_blocks = []
            for p in range(H // 2):
                blk = slice(LANES * p, LANES * (p + 1))
                dg = dgate[:, blk]
                d0 = jnp.where(lo, dg, 0.0).astype(BF16)
                d1 = jnp.where(lo, 0.0, dg).astype(BF16)
                vb = c["vnb"][:, blk]
                nt = lambda a, b: lax.dot_general(a, b, (((1,), (1,)), ((), ())), preferred_element_type=F32)
                tn = lambda a, b: lax.dot_general(a, b, (((0,), (0,)), ((), ())), preferred_element_type=F32)
                dt_ref[2 * p] += nt(d0, vb)
                dt_ref[2 * p + 1] += nt(d1, vb)
                dvn_blocks.append(jnp.where(lo, tn(c["tmats"][2 * p], dgb16[:, blk]), tn(c["tmats"][2 * p + 1], dgb16[:, blk])))
            dvn = jnp.concatenate(dvn_blocks, axis=1)
            yv = c["v"] * c["rv"]
            dsg_ref[...] += jnp.sum(dvn * yv, axis=0, keepdims=True)
            dyv = dvn * prm["sgu_g"][...]
            dv = c["rv"] * (dyv - yv * group_mean_b(dyv * yv))
            dzu = du * _gelu_grad(c["zu"])
            dzv = dv * _gelu_grad(c["zv"])

            dmb = dmx[:, SW:]
            yb = c["outb"] * c["rb"]
            dnb_ref[...] += jnp.sum(dmb * yb, axis=0, keepdims=True)
            doutb = rms_bwd_full(dmb * prm["norm_b"][...], yb, c["rb"])
            lane1 = lax.broadcasted_iota(jnp.int32, (1, LANES), 1)
            dqn_blocks = [None] * (AW // LANES)
            dkn_blocks = [None] * (KVW // LANES)
            dvc_blocks = [None] * (KVW // LANES)
            dsink_vec = jnp.zeros((1, LANES), F32)
            add = lambda old, new: new if old is None else old + new
            for hq, hd in enumerate(c["heads"]):
                mb, e, ek, kb, half = hd["mb"], hd["e"], hd["ek"], hd["kb"], hd["half"]
                dr = doutb[:, LANES * mb:LANES * (mb + 1)]
                if e != ek:
                    dr = pltpu.roll(dr, HEAD_DIM, 1)
                drm = jnp.where(half, dr, 0.0).astype(BF16)
                dp = lax.dot_general(drm, hd["vblk"], (((1,), (1,)), ((), ())), preferred_element_type=F32)
                dvc_blocks[kb] = add(dvc_blocks[kb], lax.dot_general(hd["prb"], drm, (((0,), (0,)), ((), ())),
                                                                     preferred_element_type=F32))
                rowdot = jnp.sum(hd["pr"] * dp, axis=-1, keepdims=True)
                ds = hd["pr"] * (dp - rowdot)
                dsink = jnp.sum(-hd["psink"] * rowdot, axis=0, keepdims=True)
                dsink_vec = dsink_vec + jnp.where(lane1 == hq, dsink, 0.0)
                dbias_ref[hq] += ds
                dsb = (ds * c["scale"]).astype(BF16)
                dqm = jnp.dot(dsb, hd["kblk"], preferred_element_type=F32)
                dqm = jnp.where(half, dqm, 0.0)
                if e != ek:
                    dqm = pltpu.roll(dqm, HEAD_DIM, 1)
                dqn_blocks[mb] = add(dqn_blocks[mb], dqm)
                dkn_blocks[kb] = add(dkn_blocks[kb], lax.dot_general(dsb, hd["qm"], (((0,), (0,)), ((), ())),
                                                                     preferred_element_type=F32))
            dsk_ref[...] += dsink_vec
            dqn = jnp.concatenate(dqn_blocks, axis=1)
            dkn = jnp.concatenate(dkn_blocks, axis=1)
            dvc = jnp.concatenate(dvc_blocks, axis=1)
            yq = c["q"] * c["rq"]
            accq[...] += jnp.sum(dqn * yq, axis=0, keepdims=True)
            dyq = dqn * prm["q_g"][...]
            dq = c["rq"] * (dyq - yq * group_mean_b(dyq * yq))
            yk = c["kcat"] * c["rk"]
            acck[...] += jnp.sum(dkn * yk, axis=0, keepdims=True)
            dyk = dkn * prm["k_g"][...]
            dk = c["rk"] * (dyk - yk * group_mean_b(dyk * yk))

            slot = n % 2
            hold[slot, :, :SW] = dzu
            hold[slot, :, SW:2 * SW] = dzv
            hold[slot, :, 2 * SW:] = dq
            tmpkv[:, :KVW] = dk[:BLOCK]
            tmpkv[:, KVW:] = dvc[:BLOCK]
            newkv[:, :KVW] = dk[BLOCK:]
            newkv[:, KVW:] = dvc[BLOCK:]

        @pl.when(n >= 1)
        def _():
            dz_ref[:, :QW] = hold[(n - 1) % 2].astype(dz_ref.dtype)

        @pl.when((n >= 1) & (n < nb))
        def _():
            dz_ref[:, QW:] = (carry[...] + tmpkv[...]).astype(dz_ref.dtype)

        @pl.when(n == nb)
        def _():
            dz_ref[:, QW:] = carry[...].astype(dz_ref.dtype)
            row = lax.broadcasted_iota(jnp.int32, (BLOCK, BLOCK), 0)
            col = lax.broadcasted_iota(jnp.int32, (BLOCK, BLOCK), 1)
            for h in range(H):
                dt_ref[h] = jnp.where(row >= col, dt_ref[h], 0.0)
            dsb_ref[...] = _dot3(accb[...], gs_ref[...])
            dqg_ref[...] = _dot3(accq[...], gmq_ref[...])
            dkg_ref[...] = _dot3(acck[...], gmk_ref[...])

        @pl.when(n < nb)
        def _():
            carry[...] = newkv[...]

    kvblk = dm.koff // (2 * KVW)
    clamp = lambda n: jnp.minimum(n, nb - 1)
    full = lambda shape: pl.BlockSpec(shape, lambda n: (0,) * len(shape))
    in_specs = [pl.BlockSpec((BLOCK, IN), lambda n: (clamp(n), 0)),
                pl.BlockSpec((BLOCK, 2 * KVW), lambda n: (jnp.maximum(clamp(n) - 1, 0), kvblk)),
                pl.BlockSpec((BLOCK, SW + AW), lambda n: (clamp(n), 0)),
                full((NQ, BLOCK, 2 * BLOCK))] + _mixer_param_specs(dm, None) + [full((SW, LANES)), full((AW, LANES)),
                                                                                full((KVW, LANES))]
    out_shape = (jax.ShapeDtypeStruct((dm.S, IN), BF16),
                 jax.ShapeDtypeStruct((1, SW), F32), jax.ShapeDtypeStruct((H, BLOCK, BLOCK), F32),
                 jax.ShapeDtypeStruct((BLOCK, LANES), F32), jax.ShapeDtypeStruct((1, SW), F32),
                 jax.ShapeDtypeStruct((1, LANES), F32), jax.ShapeDtypeStruct((1, LANES), F32),
                 jax.ShapeDtypeStruct((1, AW), F32), jax.ShapeDtypeStruct((1, LANES), F32),
                 jax.ShapeDtypeStruct((NQ, BLOCK, 2 * BLOCK), F32))
    out_specs = (pl.BlockSpec((BLOCK, IN), lambda n: (jnp.maximum(n - 1, 0), 0)),
                 full((1, SW)), full((H, BLOCK, BLOCK)), full((BLOCK, LANES)), full((1, SW)), full((1, LANES)),
                 full((1, LANES)), full((1, AW)), full((1, LANES)), full((NQ, BLOCK, 2 * BLOCK)))
    scratch = [pltpu.VMEM((2, BLOCK, QW), F32), pltpu.VMEM((BLOCK, 2 * KVW), F32), pltpu.VMEM((BLOCK, 2 * KVW), F32),
               pltpu.VMEM((BLOCK, 2 * KVW), F32), pltpu.VMEM((BLOCK, SW), F32), pltpu.VMEM((1, AW), F32),
               pltpu.VMEM((1, KVW), F32)]
    return _call(
        body, out_shape=out_shape, grid=(nb + 1,), in_specs=in_specs, out_specs=out_specs, scratch=scratch,
        sem=("arbitrary",), args=[z, z, dmixed, dbias_in, *params, *gmats], name=name, comm=comm)


def _adamw(w, gparts, m, v, *, name, layer=0, stacked=None, tr=256):
    R, C = gparts[0].shape
    tr = _tile(R, max(8, min(tr, (1 << 18) // C)), 8)
    ng = len(gparts)
    bc1 = 1.0 - ADAM_B1 ** ADAM_STEP
    bc2 = 1.0 - ADAM_B2 ** ADAM_STEP

    def body(w_ref, *rest):
        g_refs, (m_ref, v_ref), (go_ref, d_ref, mo_ref, vo_ref) = rest[:ng], rest[ng:ng + 2], rest[-4:]
        g = g_refs[0][...].astype(F32)
        for r in g_refs[1:]:
            g = g + r[...].astype(F32)
        mn = ADAM_B1 * m_ref[...] + (1.0 - ADAM_B1) * g
        vn = ADAM_B2 * v_ref[...] + (1.0 - ADAM_B2) * jnp.square(g)
        m_hat = mn / bc1
        v_hat = vn / bc2
        go_ref[...] = g
        d_ref[...] = -ADAM_LR * (m_hat / (jnp.sqrt(v_hat) + ADAM_EPS) + ADAM_WD * w_ref[...])
        mo_ref[...] = mn
        vo_ref[...] = vn

    here = pl.BlockSpec((tr, C), lambda i: (layer * (R // tr) + i, 0))
    blk = pl.BlockSpec((tr, C), lambda i: (i, 0))
    prev = [] if stacked is None else list(stacked)
    return _call(body, out_shape=[jax.ShapeDtypeStruct(w.shape, F32)] * 4, grid=(R // tr,),
                 in_specs=[here] + [blk] * ng + [here, here] + [ANY] * len(prev), out_specs=[here] * 4,
                 sem=("parallel",), args=[w, *gparts, m, v, *prev], name=name,
                 aliases={3 + ng + q: q for q in range(len(prev))})


def _sum_chip_partials(g, axis, base, width, stride, got, chip, *, name, tr=256, comm=()):
    _, R, C = got.shape
    tr = _tile(R, max(8, min(tr, (1 << 18) // C)), 8)
    assert base % width == 0 and stride % width == 0 and (C == width if axis == 1 else R == width)

    def body(chip_ref, own_ref, g0_ref, g1_ref, g2_ref, o_ref):
        o_ref[...] = ((own_ref[...].astype(F32) + g0_ref[...].astype(F32)) + g1_ref[...].astype(F32)) + g2_ref[...].astype(F32)

    band = lambda j: base // width + j[0] * (stride // width)
    if axis == 1:
        own_spec = pl.BlockSpec((tr, C), lambda i, j: (i, band(j)))
    else:
        own_spec = pl.BlockSpec((tr, C), lambda i, j: (band(j) * (R // tr) + i, 0))
    part = lambda k: pl.BlockSpec((None, tr, C), lambda i, j: (k, i, 0))
    return _call(body, out_shape=[jax.ShapeDtypeStruct((R, C), F32)], grid=(R // tr,),
                 in_specs=[own_spec, part(0), part(1), part(2)], out_specs=[pl.BlockSpec((tr, C), lambda i, j: (i, 0))],
                 sem=("parallel",), args=[g, got, got, got], name=name, prefetch=chip, comm=comm)


def _pack(arrays):
    parts, total = [], 0
    for a in arrays:
        flat = a.reshape(-1).astype(F32)
        pad = (-flat.shape[0]) % (8 * LANES)
        parts.append(jnp.pad(flat, (0, pad)))
        total += flat.shape[0] + pad
    parts.append(jnp.zeros(((-total) % (PACK_ROWS * LANES),), F32))
    return jnp.concatenate(parts).reshape(-1, LANES)


def _unpack(packed, like):
    flat = packed.reshape(-1)
    out, off = [], 0
    for a in like:
        n = int(np.prod(a.shape))
        out.append(flat[off:off + n].reshape(a.shape))
        off += n + ((-n) % (8 * LANES))
    return out


def kernel(x, rel_bias, norm1_g, w_in, sgu_norm_g, sgu_w, sgu_b, q_norm_g, k_norm_g, sinks, out_norm_a, out_norm_b, w_out, norm2_g, w_gate, w_up, w_down, loss_target, m_rel_bias, m_norm1_g, m_w_in, m_sgu_norm_g, m_sgu_w, m_sgu_b, m_q_norm_g, m_k_norm_g, m_sinks, m_out_norm_a, m_out_norm_b, m_w_out, m_norm2_g, m_w_gate, m_w_up, m_w_down, v_rel_bias, v_norm1_g, v_w_in, v_sgu_norm_g, v_sgu_w, v_sgu_b, v_q_norm_g, v_k_norm_g, v_sinks, v_out_norm_a, v_out_norm_b, v_w_out, v_norm2_g, v_w_gate, v_w_up, v_w_down):
    L, D, n_in = w_in.shape
    S = x.shape[1]
    IN = N_CHIPS * n_in
    n_ff = w_gate.shape[2]
    FF = N_CHIPS * n_ff
    H = sgu_w.shape[1]
    NQ = sinks.shape[1]
    SW, AW = H * HEAD_DIM, NQ * HEAD_DIM
    KVW = (IN - 2 * SW - AW) // 2
    dm = _MixerDims(S, IN, SW, AW, KVW, H, NQ)
    assert sgu_w.shape[2] == BLOCK and q_norm_g.shape[1] == HEAD_DIM and SW + AW == D

    wb = {k: w.astype(BF16) for k, w in (("in", w_in), ("out", w_out), ("gate", w_gate), ("up", w_up), ("down", w_down))}

    def gather(l, which):
        if which == "in":
            return _Gather([wb["in"]], [(0, 0, 1, 0, n_in)], [(D, IN)], l)
        if which == "out":
            return _Gather([wb["out"]], [(0, 0, 0, 0, D // N_CHIPS)], [(D, D)], l)
        if which == "gu":
            return _Gather([wb["gate"], wb["up"]], [(0, 0, 1, 0, 2 * n_ff), (1, 0, 1, n_ff, 2 * n_ff)], [(D, 2 * FF)], l)
        return _Gather([wb["down"]], [(0, 0, 0, 0, n_ff)], [(FF, D)], l)

    nxt = lambda l, *which: [gather(l + 1, w) for w in which] if l + 1 < L else []
    W = {}
    (W[0, "in"],) = _comm_only([gather(0, "in")], name="gather_first_weights")

    gs, gmod_q = _group_consts(SW)
    _, gmod_k = _group_consts(KVW)
    pblk = _same_group()
    onehot = _bucket_onehot()
    rbt = jnp.pad(rel_bias.T, ((0, 0), (0, LANES - NUM_BUCKETS)))
    (bias,) = _matmul(rbt, onehot.T, mode="nn", out_dtype=F32, exact=True, tn=4096, name="bias_table")
    bias = bias.reshape(NQ, BLOCK, 2 * BLOCK)

    def mixer_params(l):
        return [sgu_norm_g[l].reshape(1, SW), sgu_w[l], jnp.repeat(sgu_b[l].T, HEAD_DIM, axis=1),
                out_norm_a[l].reshape(1, SW), jnp.tile(q_norm_g[l], NQ).reshape(1, AW),
                jnp.tile(k_norm_g[l], dm.NKV).reshape(1, KVW), out_norm_b[l].reshape(1, AW), sinks[l], bias, pblk]

    xs = x.reshape(S, D)
    saved = []
    for l in range(L):
        h = _rms_fwd(xs, norm1_g[l].reshape(1, D), name="norm1_fwd")
        z, *got = _matmul(h, W[l, "in"], mode="nn", out_dtype=F32, tn=1792, name="in_proj",
                          comm=[gather(0, "out"), gather(0, "down")] if l == 0 else [])
        if l == 0:
            W[0, "out"], W[0, "down"] = got
        mixed, *got = _mixer_fwd(dm, z, mixer_params(l), name="mixer_fwd",
                                 comm=[gather(0, "gu")] if l == 0 else nxt(l, "in", "out"))
        if l == 0:
            (W[0, "gu"],) = got
        elif got:
            W[l + 1, "in"], W[l + 1, "out"] = got
        (x1,) = _matmul(mixed, W[l, "out"], mode="nn", out_dtype=F32, res=xs, name="out_proj")
        h2 = _rms_fwd(x1, norm2_g[l].reshape(1, D), name="norm2_fwd")
        ab, f, *got = _gate_up_swiglu(h2, W[l, "gu"], n_ff, name="gate_up_swiglu",
                                      comm=nxt(l, "gu") + (nxt(l, "in", "out") if l == 0 else []))
        if got:
            W[l + 1, "gu"] = got.pop(0)
        if got:
            W[l + 1, "in"], W[l + 1, "out"] = got
        x2, *got = _matmul(f, W[l, "down"], mode="nn", out_dtype=F32, res=x1, tk=2816, name="down_proj", comm=nxt(l, "down"))
        if got:
            (W[l + 1, "down"],) = got
        saved.append((xs, h, z, mixed, x1, h2, ab, f))
        xs = x2

    dx, dxb, loss_part = _loss_head(xs, loss_target.reshape(S, D), name="loss_head")
    loss = lax.psum(loss_part[0, 0], ("x", "y", "c"))

    dbias = jnp.zeros((NQ, BLOCK, 2 * BLOCK), F32)
    n_out = D // N_CHIPS
    big_names = ("w_in", "w_out", "w_gate", "w_up", "w_down")
    own = {k: [None] * L for k in big_names}
    got = {k: [None] * L for k in big_names}
    names_layer = ("norm1_g", "sgu_norm_g", "sgu_w", "sgu_b", "q_norm_g", "k_norm_g", "sinks", "out_norm_a", "out_norm_b",
                   "norm2_g")
    small = {k: [None] * L for k in names_layer}
    small_unit = lambda layers: _AllGather(_pack([jnp.stack([small[k][i] for i in layers]) for k in names_layer]))
    for l in reversed(range(L)):
        xl, h, z, mixed, x1, h2, ab, f = saved[l]
        early = [small_unit(range(1, L))] if (l == 0 and L > 1) else []
        dab, *gathered_early = _down_dx_swiglu_bwd(dxb, W[l, "down"], ab, n_ff, name="down_dx_swiglu_bwd", comm=early)
        (g_d,) = _matmul_tn(f, dxb, out_dtype=BF16, tm=1408, tn=1024, tt=2048, name="down_proj_dw")
        g_gu, got["w_down"][l] = _matmul_tn(h2, dab, out_dtype=BF16, tt=2048, name="gate_up_proj_dw",
                                            comm=[_Scatter([g_d], [(0, 0, 0, n_ff, n_ff)])])
        dh2, got["w_gate"][l] = _matmul(dab, W[l, "gu"], mode="nt", out_dtype=F32, tk=2816, name="gate_up_proj_dx",
                                        comm=[_Scatter([g_gu], [(0, 1, 0, n_ff, 2 * n_ff)])])
        dx1, dx1b, dg2 = _rms_bwd(x1, norm2_g[l].reshape(1, D), dh2, dx, name="norm2_bwd")
        (dmixed,) = _matmul(dx1b, W[l, "out"], mode="nt", out_dtype=F32, name="out_proj_dx")
        (g_out,) = _matmul_tn(mixed, dx1b, out_dtype=BF16, tt=2048, name="out_proj_dw")
        (dz, d_sg, d_t, d_sb, d_na, d_qg, d_kg, d_nb, d_sk, dbias, got["w_up"][l], got["w_out"][l]) = _mixer_bwd(
            dm, z, dmixed, dbias, mixer_params(l), (gs, gmod_q, gmod_k), name="mixer_bwd",
            comm=[_Scatter([g_gu, g_out], [(0, 1, n_ff, n_ff, 2 * n_ff), (1, 0, 0, n_out, n_out)])])
        (g_in,) = _matmul_tn(h, dz, out_dtype=BF16, tn=896, tt=2048, name="in_proj_dw")
        dh, got["w_in"][l] = _matmul(dz, W[l, "in"], mode="nt", out_dtype=F32, tk=1792, name="in_proj_dx",
                                     comm=[_Scatter([g_in], [(0, 1, 0, n_in, n_in)])])
        dx, dxb, dg1 = _rms_bwd(xl, norm1_g[l].reshape(1, D), dh, dx1, name="norm1_bwd")
        own["w_in"][l], own["w_out"][l] = (g_in, 1, 0, n_in, n_in), (g_out, 0, 0, n_out, n_out)
        own["w_gate"][l], own["w_up"][l] = (g_gu, 1, 0, n_ff, 2 * n_ff), (g_gu, 1, n_ff, n_ff, 2 * n_ff)
        own["w_down"][l] = (g_d, 0, 0, n_ff, n_ff)
        small["norm1_g"][l] = dg1.reshape(D)
        small["norm2_g"][l] = dg2.reshape(D)
        small["sgu_norm_g"][l] = d_sg.reshape(H, HEAD_DIM)
        small["sgu_w"][l] = d_t
        small["sgu_b"][l] = d_sb[:, :H].T
        small["q_norm_g"][l] = d_qg[0, :HEAD_DIM]
        small["k_norm_g"][l] = d_kg[0, :HEAD_DIM]
        small["sinks"][l] = d_sk[0, :NQ]
        small["out_norm_a"][l] = d_na.reshape(SW)
        small["out_norm_b"][l] = d_nb.reshape(AW)
    grad_x = dx.reshape(1, S, D)
    (d_rb,) = _matmul(dbias.reshape(NQ, BLOCK * 2 * BLOCK), onehot, mode="nn", out_dtype=F32, exact=True, tk=4096,
                      name="rel_bias_grad")
    d_rel_bias = d_rb[:, :NUM_BUCKETS].T

    chip = (2 * lax.axis_index("x") + lax.axis_index("y")).astype(jnp.int32).reshape(1)
    jobs = [(nm, l) for nm in big_names for l in range(L)]
    late = _AllGather(_pack([small[k][0][None] for k in names_layer] + [d_rel_bias]))
    mine, sib = {}, {}
    for i, (nm, l) in enumerate(jobs):
        g, axis, base, width, stride = own[nm][l]
        units = ([late] if i == 0 else []) + ([_Swap([mine[jobs[i - 1]]])] if i > 0 else [])
        mine[nm, l], *rest = _sum_chip_partials(g, axis, base, width, stride, got[nm][l], chip, name="sum_chip_partials",
                                                comm=units)
        if i == 0:
            gathered_late = rest.pop(0)
        if i > 0:
            (sib[jobs[i - 1]],) = rest
    (sib[jobs[-1]],) = _comm_only([_Swap([mine[jobs[-1]]])], name="swap_last_sums")
    wmv = dict(w_in=(w_in, m_w_in, v_w_in), w_out=(w_out, m_w_out, v_w_out), w_gate=(w_gate, m_w_gate, v_w_gate),
               w_up=(w_up, m_w_up, v_w_up), w_down=(w_down, m_w_down, v_w_down))
    big = {}
    two = lambda t: t.reshape(-1, t.shape[-1])
    for nm in big_names:
        w, m, v = wmv[nm]
        res = None
        for l in range(L):
            res = _adamw(two(w), [mine[nm, l], sib[nm, l]], two(m), two(v), layer=l, stacked=res, name="adamw_" + nm)
        big[nm] = [r.reshape(w.shape) for r in res]

    w_small = dict(rel_bias=rel_bias, norm1_g=norm1_g, sgu_norm_g=sgu_norm_g, sgu_w=sgu_w, sgu_b=sgu_b, q_norm_g=q_norm_g,
                   k_norm_g=k_norm_g, sinks=sinks, out_norm_a=out_norm_a, out_norm_b=out_norm_b, norm2_g=norm2_g)
    m_small = dict(rel_bias=m_rel_bias, norm1_g=m_norm1_g, sgu_norm_g=m_sgu_norm_g, sgu_w=m_sgu_w, sgu_b=m_sgu_b,
                   q_norm_g=m_q_norm_g, k_norm_g=m_k_norm_g, sinks=m_sinks, out_norm_a=m_out_norm_a,
                   out_norm_b=m_out_norm_b, norm2_g=m_norm2_g)
    v_small = dict(rel_bias=v_rel_bias, norm1_g=v_norm1_g, sgu_norm_g=v_sgu_norm_g, sgu_w=v_sgu_w, sgu_b=v_sgu_b,
                   q_norm_g=v_q_norm_g, k_norm_g=v_k_norm_g, sinks=v_sinks, out_norm_a=v_out_norm_a,
                   out_norm_b=v_out_norm_b, norm2_g=v_norm2_g)

    def adamw_small(gathered, pick, name):
        like = pick(w_small)
        res = _adamw(_pack(like), [gathered[i] for i in range(N_DEV)], _pack(pick(m_small)), _pack(pick(v_small)), name=name)
        return [_unpack(r, like) for r in res]

    first_layer = adamw_small(gathered_late, lambda d: [d[k][:1] for k in names_layer] + [d["rel_bias"]], "adamw_small_late")
    if L > 1:
        others = adamw_small(gathered_early[0], lambda d: [d[k][1:] for k in names_layer], "adamw_small_early")
    sm = {"rel_bias": [r[-1] for r in first_layer]}
    for j, k in enumerate(names_layer):
        sm[k] = [jnp.concatenate([first_layer[q][j]] + ([others[q][j]] if L > 1 else [])) for q in range(4)]

    order = ("rel_bias", "norm1_g", "w_in", "sgu_norm_g", "sgu_w", "sgu_b", "q_norm_g", "k_norm_g", "sinks", "out_norm_a",
             "out_norm_b", "w_out", "norm2_g", "w_gate", "w_up", "w_down")
    pick = lambda k, i: big[k][i] if k in big else sm[k][i]
    outs = [loss, grad_x]
    for i in range(4):
        outs += [pick(k, i) for k in order]
    return tuple(outs)
```

```python
import functools
import math

import numpy as np

import jax
import jax.numpy as jnp
from jax import lax
from jax.experimental import pallas as pl
from jax.experimental.pallas import tpu as pltpu

F32 = jnp.float32
BF16 = jnp.bfloat16
MESH = pl.DeviceIdType.MESH
ANY = pl.BlockSpec(memory_space=pl.ANY)

HEAD_DIM = 64
BLOCK = 128
NUM_BUCKETS = 32
MAX_DISTANCE = 128
EPS = 1e-6
NEG_INF = -1e30
ADAM_LR, ADAM_B1, ADAM_B2, ADAM_EPS, ADAM_WD, ADAM_STEP = 0.001, 0.9, 0.999, 1e-08, 0.01, 10

LANES = 128
VMEM_LIMIT = 56 * 1024 * 1024
N_CHIPS = 4
N_DEV = 8
PACK_ROWS = 256
ROW_CHUNK = 256


def _cparams(sem=None):
    return pltpu.CompilerParams(dimension_semantics=sem, vmem_limit_bytes=VMEM_LIMIT)


def _tile(dim, target, align=LANES):
    best = None
    for t in range(align, min(dim, target) + 1, align):
        if dim % t == 0:
            best = t
    return best if best is not None else dim


def _mesh_pos():
    return lax.axis_index("x"), lax.axis_index("y"), lax.axis_index("c")


def _other_chips(x, y):
    return [(1 - x, y), (x, 1 - y), (1 - x, 1 - y)]


def _slab(ref, axis, start, size, half=None):
    if axis == 1:
        rows = ref.shape[0]
        r = pl.ds(0, rows) if half is None else pl.ds(pl.multiple_of(half * (rows // 2), 16), rows // 2)
        return ref.at[r, pl.ds(pl.multiple_of(start, LANES), size)]
    if half is None:
        return ref.at[pl.ds(pl.multiple_of(start, 16), size), :]
    return ref.at[pl.ds(pl.multiple_of(start + half * (size // 2), 16), size // 2), :]


def _remote(src, dst, send_sem, recv_sem, to):
    return pltpu.make_async_remote_copy(src_ref=src, dst_ref=dst, send_sem=send_sem, recv_sem=recv_sem,
                                        device_id=to, device_id_type=MESH)


class _Gather:
    def __init__(self, shards, streams, out_shapes, layer):
        self.ins, self.streams, self.layer = list(shards), streams, layer
        self.outs = [jax.ShapeDtypeStruct(s, BF16) for s in out_shapes]
        ns = len(streams)
        self.sems = [pltpu.SemaphoreType.DMA((ns, 3))] * 4 + [pltpu.SemaphoreType.DMA((ns,))]

    def _sent(self, srcs, outs, sems):
        send, recv, _, _, local = sems
        x, y, c = _mesh_pos()
        me_j = 2 * x + y
        own, sends = [], []
        for s, (si, oi, axis, base, stride) in enumerate(self.streams):
            src, out = srcs[si].at[self.layer], outs[oi]
            width = src.shape[axis]
            own.append(pltpu.make_async_copy(src, _slab(out, axis, base + me_j * stride, width), local.at[s]))
            half_rows = src.shape[0] // 2
            mine = src.at[pl.ds(pl.multiple_of(c * half_rows, 16), half_rows), :]
            for k, (px, py) in enumerate(_other_chips(x, y)):
                sends.append(_remote(mine, _slab(out, axis, base + me_j * stride, width, half=c),
                                     send.at[s, k], recv.at[s, k], (px, py, c)))
        return own, sends

    def start(self, srcs, outs, sems):
        own, sends = self._sent(srcs, outs, sems)
        for cp in own + sends:
            cp.start()

    def finish(self, srcs, outs, sems):
        send, recv, fsend, frecv, _ = sems
        x, y, c = _mesh_pos()
        sib = (x, y, 1 - c)
        forwards, fwd_arrivals = [], []
        for s, (si, oi, axis, base, stride) in enumerate(self.streams):
            out = outs[oi]
            width = srcs[si].shape[1 + axis]
            for k, (px, py) in enumerate(_other_chips(x, y)):
                start = base + (2 * px + py) * stride
                got = _slab(out, axis, start, width, half=c)
                _remote(got, got, send.at[s, k], recv.at[s, k], (px, py, c)).wait_recv()
                fwd = _remote(got, got, fsend.at[s, k], frecv.at[s, k], sib)
                fwd.start()
                forwards.append(fwd)
                theirs = _slab(out, axis, start, width, half=1 - c)
                fwd_arrivals.append(_remote(theirs, theirs, fsend.at[s, k], frecv.at[s, k], sib))
        for a in fwd_arrivals:
            a.wait_recv()
        for cp in forwards:
            cp.wait_send()
        own, sends = self._sent(srcs, outs, sems)
        for cp in sends:
            cp.wait_send()
        for cp in own:
            cp.wait()


class _Scatter:
    def __init__(self, grads, streams):
        self.ins, self.streams = list(grads), streams
        shard = lambda g, axis, width: (width, g.shape[1]) if axis == 0 else (g.shape[0], width)
        self.outs = [jax.ShapeDtypeStruct((3,) + shard(grads[gi], axis, width), BF16) for gi, axis, _, width, _ in streams]
        self.sems = [pltpu.SemaphoreType.DMA((len(streams), 3))] * 2

    def _copies(self, srcs, outs, sems):
        send, recv = sems
        x, y, c = _mesh_pos()
        copies = []
        for s, (gi, axis, base, width, stride) in enumerate(self.streams):
            for k, (px, py) in enumerate(_other_chips(x, y)):
                copies.append(_remote(_slab(srcs[gi], axis, base + (2 * px + py) * stride, width), outs[s].at[k],
                                      send.at[s, k], recv.at[s, k], (px, py, c)))
        return copies

    def start(self, srcs, outs, sems):
        for cp in self._copies(srcs, outs, sems):
            cp.start()

    def finish(self, srcs, outs, sems):
        for cp in self._copies(srcs, outs, sems):
            cp.wait()


class _Swap:
    def __init__(self, arrays):
        self.ins = list(arrays)
        self.outs = [jax.ShapeDtypeStruct(a.shape, a.dtype) for a in arrays]
        self.sems = [pltpu.SemaphoreType.DMA((len(arrays),))] * 2

    def _copies(self, srcs, outs, sems):
        send, recv = sems
        x, y, c = _mesh_pos()
        return [_remote(srcs[s], outs[s], send.at[s], recv.at[s], (x, y, 1 - c)) for s in range(len(srcs))]

    def start(self, srcs, outs, sems):
        for cp in self._copies(srcs, outs, sems):
            cp.start()

    def finish(self, srcs, outs, sems):
        for cp in self._copies(srcs, outs, sems):
            cp.wait()


class _AllGather:
    def __init__(self, part):
        self.ins = [part]
        self.outs = [jax.ShapeDtypeStruct((N_DEV,) + part.shape, part.dtype)]
        self.sems = [pltpu.SemaphoreType.DMA((7,)), pltpu.SemaphoreType.DMA((7,)), pltpu.SemaphoreType.DMA(())]

    def _first(self, srcs, outs, sems):
        (x_ref,), (out,), (send, recv, local) = srcs, outs, sems
        x, y, c = _mesh_pos()
        mine = out.at[4 * x + 2 * y + c]
        own = pltpu.make_async_copy(x_ref, mine, local)
        sends = [_remote(x_ref, mine, send.at[0], recv.at[0], (x, y, 1 - c))]
        sends += [_remote(x_ref, mine, send.at[1 + k], recv.at[1 + k], (px, py, c)) for k, (px, py) in enumerate(_other_chips(x, y))]
        return own, sends

    def start(self, srcs, outs, sems):
        own, sends = self._first(srcs, outs, sems)
        own.start()
        for cp in sends:
            cp.start()

    def finish(self, srcs, outs, sems):
        (out,), (send, recv, local) = outs, sems
        x, y, c = _mesh_pos()
        me, sib = (x, y, c), (x, y, 1 - c)
        slot = lambda px, py, pc: out.at[4 * px + 2 * py + pc]
        passed = []
        for k, (px, py) in enumerate(_other_chips(x, y)):
            blk = slot(px, py, c)
            _remote(blk, blk, send.at[1 + k], recv.at[1 + k], me).wait_recv()
            fwd = _remote(blk, blk, send.at[4 + k], recv.at[4 + k], sib)
            fwd.start()
            passed.append(fwd)
        blk = slot(x, y, 1 - c)
        _remote(blk, blk, send.at[0], recv.at[0], me).wait_recv()
        for k, (px, py) in enumerate(_other_chips(x, y)):
            blk = slot(px, py, 1 - c)
            _remote(blk, blk, send.at[4 + k], recv.at[4 + k], me).wait_recv()
        own, sends = self._first(srcs, outs, sems)
        for cp in sends + passed:
            cp.wait_send()
        own.wait()


def _call(body, *, grid, in_specs, out_specs, out_shape, args, name, scratch=(), sem=None, comm=(), prefetch=None,
          aliases=None):
    out_shape, out_specs = tuple(out_shape), tuple(out_specs)
    n_in, n_out, n_scr = len(in_specs), len(out_shape), len(scratch)
    n_pre = 0 if prefetch is None else 1
    c_ins = [a for u in comm for a in u.ins]
    c_outs = [o for u in comm for o in u.outs]
    c_sems = [s for u in comm for s in u.sems]

    def wrapped(*refs):
        pre, refs = refs[:n_pre], refs[n_pre:]
        ins, rest = refs[:n_in], refs[n_in:]
        cin, rest = rest[:len(c_ins)], rest[len(c_ins):]
        outs, rest = rest[:n_out], rest[n_out:]
        cout, rest = rest[:len(c_outs)], rest[len(c_outs):]
        scr, csem = rest[:n_scr], rest[n_scr:]

        def each(fn_name):
            i = o = s = 0
            for u in comm:
                getattr(u, fn_name)(cin[i:i + len(u.ins)], cout[o:o + len(u.outs)], csem[s:s + len(u.sems)])
                i, o, s = i + len(u.ins), o + len(u.outs), s + len(u.sems)

        if comm:
            pids = [pl.program_id(d) for d in range(len(grid))]
            first = functools.reduce(jnp.logical_and, [p == 0 for p in pids])
            last = functools.reduce(jnp.logical_and, [p == g - 1 for p, g in zip(pids, grid)])
            pl.when(first)(lambda: each("start"))
        body(*pre, *ins, *outs, *scr)
        if comm:
            pl.when(last)(lambda: each("finish"))

    if comm:
        sem = ("arbitrary",) * len(grid)
    all_in = list(in_specs) + [ANY] * len(c_ins)
    all_out = out_specs + tuple([ANY] * len(c_outs))
    all_scr = list(scratch) + c_sems
    kw = {}
    if aliases:
        kw["input_output_aliases"] = {n_pre + i: o for i, o in aliases.items()}
    if prefetch is None:
        kw.update(grid=grid, in_specs=all_in, out_specs=all_out, scratch_shapes=all_scr)
        pre_args = []
    else:
        kw["grid_spec"] = pltpu.PrefetchScalarGridSpec(num_scalar_prefetch=1, grid=grid, in_specs=all_in, out_specs=all_out,
                                                       scratch_shapes=all_scr)
        pre_args = [prefetch]
    return pl.pallas_call(wrapped, out_shape=out_shape + tuple(c_outs), compiler_params=_cparams(sem), name=name, **kw)(
        *pre_args, *args, *c_ins)


def _comm_only(comm, *, name):
    def body(tick_ref):
        tick_ref[...] = jnp.zeros(tick_ref.shape, tick_ref.dtype)

    outs = _call(body, grid=(1,), in_specs=[], out_specs=[pl.BlockSpec((8, LANES), lambda i: (0, 0))],
                 out_shape=[jax.ShapeDtypeStruct((8, LANES), F32)], args=[], name=name, comm=comm)
    return outs[1:]


def _split3(x):
    hi = x.astype(BF16)
    r1 = x - hi.astype(F32)
    mid = r1.astype(BF16)
    lo = (r1 - mid.astype(F32)).astype(BF16)
    return hi, mid, lo


def _dot3(x, g):
    hi, mid, lo = _split3(x)
    d = lambda a: jnp.dot(a, g, preferred_element_type=F32)
    return d(hi) + d(mid) + d(lo)


GROUP_TILE = 256


def _group_sum(x, pblk):
    hi = x.astype(BF16)
    lo = (x - hi.astype(F32)).astype(BF16)
    cols = []
    for b in range(x.shape[1] // GROUP_TILE):
        sl = slice(GROUP_TILE * b, GROUP_TILE * (b + 1))
        cols.append(jnp.dot(hi[:, sl], pblk, preferred_element_type=F32) + jnp.dot(lo[:, sl], pblk, preferred_element_type=F32))
    return cols[0] if len(cols) == 1 else jnp.concatenate(cols, axis=1)


def _matmul(a, b, *, mode, out_dtype, name, res=None, exact=False, tm=1024, tn=1024, tk=2048, b_col0=0, comm=()):
    M, K = a.shape
    N = b.shape[1] if mode == "nn" else b.shape[0]
    tm, tn, tk = _tile(M, tm, 8 if M < LANES else LANES), _tile(N, tn), _tile(K, tk)
    nk = K // tk
    dn = (((1,), (0,)), ((), ())) if mode == "nn" else (((1,), (1,)), ((), ()))

    def body(*refs):
        a_ref, b_ref = refs[0], refs[1]
        r_ref = refs[2] if res is not None else None
        o_ref = refs[3] if res is not None else refs[2]
        if exact:
            part = _dot3(a_ref[...], b_ref[...])
        else:
            part = lax.dot_general(a_ref[...].astype(BF16), b_ref[...].astype(BF16), dn, preferred_element_type=F32)

        def finish(total):
            if r_ref is not None:
                total = r_ref[...] + total
            o_ref[...] = total.astype(o_ref.dtype)

        if nk == 1:
            finish(part)
        else:
            acc = refs[-1]
            k = pl.program_id(2)

            @pl.when(k == 0)
            def _():
                acc[...] = part

            @pl.when(k > 0)
            def _():
                acc[...] += part

            @pl.when(k == nk - 1)
            def _():
                finish(acc[...])

    if mode == "nn":
        b_spec = pl.BlockSpec((tk, tn), lambda j, i, k: (k, j))
    else:
        assert b_col0 % tk == 0
        b_spec = pl.BlockSpec((tn, tk), lambda j, i, k: (j, k + b_col0 // tk))
    in_specs = [pl.BlockSpec((tm, tk), lambda j, i, k: (i, k)), b_spec]
    args = [a, b]
    if res is not None:
        in_specs.append(pl.BlockSpec((tm, tn), lambda j, i, k: (i, j)))
        args.append(res)
    return _call(
        body, grid=(N // tn, M // tm, nk), in_specs=in_specs,
        out_specs=[pl.BlockSpec((tm, tn), lambda j, i, k: (i, j))], out_shape=[jax.ShapeDtypeStruct((M, N), out_dtype)],
        scratch=[pltpu.VMEM((tm, tn), F32)] if nk > 1 else [], sem=("parallel", "parallel", "arbitrary"),
        args=args, name=name, comm=comm)


def _matmul_tn(a, b, *, out_dtype, name, tm=2048, tn=1024, tt=1024, comm=()):
    T, Mo = a.shape
    N = b.shape[1]
    tm, tn, tt = _tile(Mo, tm), _tile(N, tn), _tile(T, tt)
    nt = T // tt

    def body(a_ref, b_ref, o_ref, acc):
        t = pl.program_id(2)
        part = lax.dot_general(a_ref[...].astype(BF16), b_ref[...].astype(BF16), (((0,), (0,)), ((), ())),
                               preferred_element_type=F32)

        @pl.when(t == 0)
        def _():
            acc[...] = part

        @pl.when(t > 0)
        def _():
            acc[...] += part

        @pl.when(t == nt - 1)
        def _():
            o_ref[...] = acc[...].astype(o_ref.dtype)

    return _call(
        body, grid=(Mo // tm, N // tn, nt),
        in_specs=[pl.BlockSpec((tt, tm), lambda i, j, t: (t, i)), pl.BlockSpec((tt, tn), lambda i, j, t: (t, j))],
        out_specs=[pl.BlockSpec((tm, tn), lambda i, j, t: (i, j))], out_shape=[jax.ShapeDtypeStruct((Mo, N), out_dtype)],
        scratch=[pltpu.VMEM((tm, tn), F32)], sem=("parallel", "parallel", "arbitrary"), args=[a, b], name=name, comm=comm)


def _rms_fwd(x, g, *, name, tr=256):
    R, D = x.shape
    tr = _tile(R, tr, 8)

    def body(x_ref, g_ref, o_ref):
        xv = x_ref[...]
        r = lax.rsqrt(jnp.mean(xv * xv, axis=-1, keepdims=True) + EPS)
        o_ref[...] = (xv * r * g_ref[...]).astype(o_ref.dtype)

    return pl.pallas_call(
        body, out_shape=jax.ShapeDtypeStruct((R, D), BF16), grid=(R // tr,),
        in_specs=[pl.BlockSpec((tr, D), lambda i: (i, 0)), pl.BlockSpec((1, D), lambda i: (0, 0))],
        out_specs=pl.BlockSpec((tr, D), lambda i: (i, 0)),
        compiler_params=_cparams(("parallel",)), name=name,
    )(x, g)


def _rms_bwd(x, g, dh, dres, *, name, tr=256):
    R, D = x.shape
    tr = _tile(R, tr, 8)

    def body(x_ref, g_ref, dh_ref, dres_ref, dx_ref, dxb_ref, dg_ref):
        i = pl.program_id(0)
        xv = x_ref[...]
        r = lax.rsqrt(jnp.mean(xv * xv, axis=-1, keepdims=True) + EPS)
        y = xv * r
        dhv = dh_ref[...]
        dy = dhv * g_ref[...]
        dx = dres_ref[...] + r * (dy - y * jnp.mean(dy * y, axis=-1, keepdims=True))
        dx_ref[...] = dx
        dxb_ref[...] = dx.astype(BF16)
        dg = jnp.sum(dhv * y, axis=0, keepdims=True)

        @pl.when(i == 0)
        def _():
            dg_ref[...] = dg

        @pl.when(i > 0)
        def _():
            dg_ref[...] += dg

    row = pl.BlockSpec((tr, D), lambda i: (i, 0))
    vec = pl.BlockSpec((1, D), lambda i: (0, 0))
    return pl.pallas_call(
        body, out_shape=(jax.ShapeDtypeStruct((R, D), F32), jax.ShapeDtypeStruct((R, D), BF16),
                         jax.ShapeDtypeStruct((1, D), F32)), grid=(R // tr,),
        in_specs=[row, vec, row, row], out_specs=(row, row, vec),
        compiler_params=_cparams(("arbitrary",)), name=name,
    )(x, g, dh, dres)


def _sigmoid(x):
    return 0.5 * (jnp.tanh(0.5 * x) + 1.0)


def _gate_up_swiglu(h, w_gu, n_ff, *, name, tm=512, comm=()):
    M, K = h.shape
    nsh = w_gu.shape[1] // (2 * n_ff)
    tm = _tile(M, tm)

    def body(h_ref, w_ref, ab_ref, f_ref):
        for c in range(tm // _tile(tm, ROW_CHUNK)):
            rows = pl.ds(c * _tile(tm, ROW_CHUNK), _tile(tm, ROW_CHUNK))
            ab = jnp.dot(h_ref[rows, :], w_ref[...], preferred_element_type=F32)
            ab_ref[rows, :] = ab.astype(ab_ref.dtype)
            a, b = ab[:, :n_ff], ab[:, n_ff:]
            f_ref[rows, :] = (a * _sigmoid(a) * b).astype(f_ref.dtype)

    return _call(
        body, grid=(nsh, M // tm),
        in_specs=[pl.BlockSpec((tm, K), lambda j, i: (i, 0)), pl.BlockSpec((K, 2 * n_ff), lambda j, i: (0, j))],
        out_specs=[pl.BlockSpec((tm, 2 * n_ff), lambda j, i: (i, j)), pl.BlockSpec((tm, n_ff), lambda j, i: (i, j))],
        out_shape=[jax.ShapeDtypeStruct((M, 2 * nsh * n_ff), BF16), jax.ShapeDtypeStruct((M, nsh * n_ff), BF16)],
        sem=("parallel", "parallel"), args=[h, w_gu], name=name, comm=comm)


def _down_dx_swiglu_bwd(dy, w_d, ab, n_ff, *, name, tm=1024, comm=()):
    M, K = dy.shape
    nsh = w_d.shape[0] // n_ff
    tm = _tile(M, tm)

    def body(dy_ref, w_ref, ab_ref, dab_ref):
        for c in range(tm // _tile(tm, ROW_CHUNK)):
            rows = pl.ds(c * _tile(tm, ROW_CHUNK), _tile(tm, ROW_CHUNK))
            df = lax.dot_general(dy_ref[rows, :], w_ref[...], (((1,), (1,)), ((), ())), preferred_element_type=F32)
            av = ab_ref[rows, :n_ff].astype(F32)
            s = _sigmoid(av)
            dab_ref[rows, :n_ff] = (df * ab_ref[rows, n_ff:].astype(F32) * (s * (1.0 + av * (1.0 - s)))).astype(dab_ref.dtype)
            dab_ref[rows, n_ff:] = (df * (av * s)).astype(dab_ref.dtype)

    pair = pl.BlockSpec((tm, 2 * n_ff), lambda j, i: (i, j))
    return _call(
        body, grid=(nsh, M // tm),
        in_specs=[pl.BlockSpec((tm, K), lambda j, i: (i, 0)), pl.BlockSpec((n_ff, K), lambda j, i: (j, 0)), pair],
        out_specs=[pair], out_shape=[jax.ShapeDtypeStruct(ab.shape, BF16)],
        sem=("parallel", "parallel"), args=[dy, w_d, ab], name=name, comm=comm)


def _loss_head(y, target, *, name, tr=256):
    R, D = y.shape
    tr = _tile(R, tr, 8)

    def body(y_ref, t_ref, dy_ref, dyb_ref, l_ref):
        i = pl.program_id(0)
        e = y_ref[...] - t_ref[...]
        dy = e * (1.0 / D)
        dy_ref[...] = dy
        dyb_ref[...] = dy.astype(BF16)
        part = 0.5 * jnp.sum(jnp.mean(e * e, axis=-1, keepdims=True), axis=0, keepdims=True)
        part = jnp.broadcast_to(part, (8, LANES))

        @pl.when(i == 0)
        def _():
            l_ref[...] = part

        @pl.when(i > 0)
        def _():
            l_ref[...] += part

    row = pl.BlockSpec((tr, D), lambda i: (i, 0))
    return pl.pallas_call(
        body, out_shape=(jax.ShapeDtypeStruct((R, D), F32), jax.ShapeDtypeStruct((R, D), BF16),
                         jax.ShapeDtypeStruct((8, LANES), F32)), grid=(R // tr,),
        in_specs=[row, row], out_specs=(row, row, pl.BlockSpec((8, LANES), lambda i: (0, 0))),
        compiler_params=_cparams(("arbitrary",)), name=name,
    )(y, target)


def _gelu(x):
    return 0.5 * x * (1.0 + lax.erf(x * math.sqrt(0.5)))


def _gelu_grad(x):
    return 0.5 * (1.0 + lax.erf(x * math.sqrt(0.5))) + x * jnp.exp(-0.5 * x * x) * (1.0 / math.sqrt(2.0 * math.pi))


def _group_consts(width):
    lane = np.arange(width)
    col = np.arange(LANES)
    grp = (lane[:, None] // HEAD_DIM == col[None, :]).astype(np.float32)
    mod = ((lane[:, None] % HEAD_DIM == col[None, :]) & (col[None, :] < HEAD_DIM)).astype(np.float32)
    return jnp.asarray(grp, BF16), jnp.asarray(mod, BF16)


def _same_group():
    lane = np.arange(GROUP_TILE) // HEAD_DIM
    return jnp.asarray((lane[:, None] == lane[None, :]).astype(np.float32), BF16)


def _bucket_onehot():
    qi = np.arange(BLOCK)[:, None]
    kj = np.arange(2 * BLOCK)[None, :]
    n = np.maximum(qi + BLOCK - kj, 0)
    max_exact = NUM_BUCKETS // 2
    nf = np.maximum(n, 1).astype(np.float32)
    large = max_exact + (np.log(nf / np.float32(max_exact)) / np.float32(math.log(MAX_DISTANCE / max_exact))
                         * np.float32(NUM_BUCKETS - max_exact)).astype(np.int32)
    large = np.minimum(large, NUM_BUCKETS - 1)
    bucket = jnp.asarray(np.where(n < max_exact, n, large).reshape(-1).astype(np.int32))
    return (bucket[:, None] == jnp.arange(LANES, dtype=jnp.int32)[None, :]).astype(BF16)


class _MixerDims:
    def __init__(self, S, IN, SW, AW, KVW, H, NQ):
        self.S, self.IN, self.SW, self.AW, self.KVW, self.H, self.NQ = S, IN, SW, AW, KVW, H, NQ
        self.NKV = KVW // HEAD_DIM
        self.GROUP = NQ // self.NKV
        self.nb = S // BLOCK
        self.koff = 2 * SW + AW
        self.voff = self.koff + KVW
        assert SW % LANES == 0 and AW % LANES == 0 and KVW % LANES == 0 and self.GROUP % 2 == 0
        assert self.koff % (2 * KVW) == 0 and IN == self.voff + KVW and S % BLOCK == 0


def _mixer_block(dm, n, z, kvp, prm):
    SW, AW, KVW, H, NQ = dm.SW, dm.AW, dm.KVW, dm.H, dm.NQ
    lane = lax.broadcasted_iota(jnp.int32, (BLOCK, LANES), 1)
    lo = lane < HEAD_DIM
    row = lax.broadcasted_iota(jnp.int32, (BLOCK, BLOCK), 0)
    col = lax.broadcasted_iota(jnp.int32, (BLOCK, BLOCK), 1)
    tril = row >= col
    pblk = prm["pblk"][...]
    inv = 1.0 / HEAD_DIM

    def group_rsqrt(x):
        return lax.rsqrt(_group_sum(x * x, pblk) * inv + EPS)

    zu, zv = z[:, :SW], z[:, SW:2 * SW]
    u, v = _gelu(zu), _gelu(zv)
    rv = group_rsqrt(v)
    vn = v * rv * prm["sgu_g"][...]
    vnb = vn.astype(BF16)
    tmats, gate_blocks = [], []
    for p in range(H // 2):
        blk = slice(LANES * p, LANES * (p + 1))
        t0 = jnp.where(tril, prm["sgu_w"][2 * p], 0.0).astype(BF16)
        t1 = jnp.where(tril, prm["sgu_w"][2 * p + 1], 0.0).astype(BF16)
        tmats += [t0, t1]
        g0 = jnp.dot(t0, vnb[:, blk], preferred_element_type=F32)
        g1 = jnp.dot(t1, vnb[:, blk], preferred_element_type=F32)
        gate_blocks.append(jnp.where(lo, g0, g1) + prm["sgu_bias"][:, blk])
    gate = jnp.concatenate(gate_blocks, axis=1)
    outa = u * gate
    ra = lax.rsqrt(jnp.mean(outa * outa, axis=-1, keepdims=True) + EPS)

    q = z[:, 2 * SW:2 * SW + AW]
    kcat = jnp.concatenate([kvp[:, :KVW], z[:, dm.koff:dm.koff + KVW]], axis=0)
    vcat = jnp.concatenate([kvp[:, KVW:], z[:, dm.voff:dm.voff + KVW]], axis=0)
    rq = group_rsqrt(q)
    rk = group_rsqrt(kcat)
    qn = q * rq * prm["q_g"][...]
    kn = kcat * rk * prm["k_g"][...]
    knb, vcb = kn.astype(BF16), vcat.astype(BF16)
    qi = lax.broadcasted_iota(jnp.int32, (BLOCK, 2 * BLOCK), 0)
    kj = lax.broadcasted_iota(jnp.int32, (BLOCK, 2 * BLOCK), 1)
    valid = (kj > qi) & (kj <= qi + BLOCK) & ((n > 0) | (kj >= BLOCK))
    scale = 1.0 / math.sqrt(HEAD_DIM)
    heads = []
    out_blocks = []
    for hq in range(NQ):
        mb, e = hq // 2, hq % 2
        kv = hq // dm.GROUP
        kb, ek = kv // 2, kv % 2
        qblk = qn[:, LANES * mb:LANES * (mb + 1)]
        if e != ek:
            qblk = pltpu.roll(qblk, HEAD_DIM, 1)
        half = lo if ek == 0 else jnp.logical_not(lo)
        qm = jnp.where(half, qblk, 0.0).astype(BF16)
        kblk = knb[:, LANES * kb:LANES * (kb + 1)]
        vblk = vcb[:, LANES * kb:LANES * (kb + 1)]
        s = lax.dot_general(qm, kblk, (((1,), (1,)), ((), ())), preferred_element_type=F32) * scale + prm["bias"][hq]
        s = jnp.where(valid, s, NEG_INF)
        sink = prm["sinks"][hq]
        mx = jnp.maximum(jnp.max(s, axis=-1, keepdims=True), sink)
        ex = jnp.exp(s - mx)
        den = jnp.sum(ex, axis=-1, keepdims=True) + jnp.exp(sink - mx)
        inv_den = 1.0 / den
        pr = ex * inv_den
        psink = jnp.exp(sink - mx) * inv_den
        prb = pr.astype(BF16)
        r_h = jnp.dot(prb, vblk, preferred_element_type=F32)
        if e != ek:
            r_h = pltpu.roll(r_h, HEAD_DIM, 1)
        heads.append(dict(qm=qm, kblk=kblk, vblk=vblk, pr=pr, prb=prb, psink=psink, half=half, mb=mb, e=e, ek=ek, kb=kb))
        if e == 1:
            out_blocks.append(jnp.where(lo, prev_r, r_h))
        prev_r = r_h
    outb = jnp.concatenate(out_blocks, axis=1)
    rb = lax.rsqrt(jnp.mean(outb * outb, axis=-1, keepdims=True) + EPS)
    return dict(lo=lo, tril=tril, pblk=pblk, zu=zu, zv=zv, u=u, v=v, rv=rv, vnb=vnb, tmats=tmats, gate=gate,
                outa=outa, ra=ra, q=q, kcat=kcat, rq=rq, rk=rk, heads=heads, outb=outb, rb=rb, scale=scale)


_MIXER_PARAMS = ("sgu_g", "sgu_w", "sgu_bias", "norm_a", "q_g", "k_g", "norm_b", "sinks", "bias", "pblk")


def _mixer_param_specs(dm, idx):
    SW, AW, KVW = dm.SW, dm.AW, dm.KVW
    full = lambda shape: pl.BlockSpec(shape, lambda n: (0,) * len(shape))
    return [full((1, SW)), full((dm.H, BLOCK, BLOCK)), full((BLOCK, SW)), full((1, SW)), full((1, AW)), full((1, KVW)),
            full((1, AW)), pl.BlockSpec(memory_space=pltpu.SMEM), full((dm.NQ, BLOCK, 2 * BLOCK)),
            full((GROUP_TILE, GROUP_TILE))]


def _mixer_fwd(dm, z, params, *, name, comm=()):
    nb = dm.nb

    def body(z_ref, kvp_ref, *rest):
        prm = dict(zip(_MIXER_PARAMS, rest[:len(_MIXER_PARAMS)]))
        o_ref = rest[len(_MIXER_PARAMS)]
        n = pl.program_id(0)
        c = _mixer_block(dm, n, z_ref[...], kvp_ref[...], prm)
        o_ref[:, :dm.SW] = (c["outa"] * c["ra"] * prm["norm_a"][...]).astype(o_ref.dtype)
        o_ref[:, dm.SW:] = (c["outb"] * c["rb"] * prm["norm_b"][...]).astype(o_ref.dtype)

    kvblk = dm.koff // (2 * dm.KVW)
    in_specs = [pl.BlockSpec((BLOCK, dm.IN), lambda n: (n, 0)),
                pl.BlockSpec((BLOCK, 2 * dm.KVW), lambda n: (jnp.maximum(n - 1, 0), kvblk))] + _mixer_param_specs(dm, None)
    return _call(
        body, out_shape=[jax.ShapeDtypeStruct((dm.S, dm.SW + dm.AW), BF16)], grid=(nb,),
        in_specs=in_specs, out_specs=[pl.BlockSpec((BLOCK, dm.SW + dm.AW), lambda n: (n, 0))],
        sem=("arbitrary",), args=[z, z, *params], name=name, comm=comm)


def _mixer_bwd(dm, z, dmixed, dbias_in, params, gmats, *, name, comm=()):
    SW, AW, KVW, H, NQ, nb, IN = dm.SW, dm.AW, dm.KVW, dm.H, dm.NQ, dm.nb, dm.IN
    QW = 2 * SW + AW
    NP = len(_MIXER_PARAMS)

    def body(z_ref, kvp_ref, dm_ref, dbin_ref, *rest):
        prm = dict(zip(_MIXER_PARAMS, rest[:NP]))
        gs_ref, gmq_ref, gmk_ref = rest[NP:NP + 3]
        (dz_ref, dsg_ref, dt_ref, dsb_ref, dna_ref, dqg_ref, dkg_ref, dnb_ref, dsk_ref, dbias_ref) = rest[NP + 3:NP + 13]
        hold, tmpkv, newkv, carry, accb, accq, acck = rest[NP + 13:]
        n = pl.program_id(0)

        @pl.when(n == 0)
        def _():
            for r in (dsg_ref, dt_ref, dna_ref, dnb_ref, dsk_ref, accb, accq, acck):
                r[...] = jnp.zeros(r.shape, r.dtype)
            dbias_ref[...] = dbin_ref[...]

        @pl.when(n < nb)
        def _():
            c = _mixer_block(dm, n, z_ref[...], kvp_ref[...], prm)
            lo = c["lo"]
            dmx = dm_ref[...]
            inv = 1.0 / HEAD_DIM

            def rms_bwd_full(dy_scaled, y, r):
                return r * (dy_scaled - y * jnp.mean(dy_scaled * y, axis=-1, keepdims=True))

            def group_mean_b(x):
                return _group_sum(x, c["pblk"]) * inv

            dma = dmx[:, :SW]
            ya = c["outa"] * c["ra"]
            dna_ref[...] += jnp.sum(dma * ya, axis=0, keepdims=True)
            douta = rms_bwd_full(dma * prm["norm_a"][...], ya, c["ra"])
            du = douta * c["gate"]
            dgate = douta * c["u"]
            accb[...] += dgate
            dgb16 = dgate.astype(BF16)
            dvn_blocks = []
            for p in range(H // 2):
                blk = slice(LANES * p, LANES * (p + 1))
                dg = dgate[:, blk]
                d0 = jnp.where(lo, dg, 0.0).astype(BF16)
                d1 = jnp.where(lo, 0.0, dg).astype(BF16)
                vb = c["vnb"][:, blk]
                nt = lambda a, b: lax.dot_general(a, b, (((1,), (1,)), ((), ())), preferred_element_type=F32)
                tn = lambda a, b: lax.dot_general(a, b, (((0,), (0,)), ((), ())), preferred_element_type=F32)
                dt_ref[2 * p] += nt(d0, vb)
                dt_ref[2 * p + 1] += nt(d1, vb)
                dvn_blocks.append(jnp.where(lo, tn(c["tmats"][2 * p], dgb16[:, blk]), tn(c["tmats"][2 * p + 1], dgb16[:, blk])))
            dvn = jnp.concatenate(dvn_blocks, axis=1)
            yv = c["v"] * c["rv"]
            dsg_ref[...] += jnp.sum(dvn * yv, axis=0, keepdims=True)
            dyv = dvn * prm["sgu_g"][...]
            dv = c["rv"] * (dyv - yv * group_mean_b(dyv * yv))
            dzu = du * _gelu_grad(c["zu"])
            dzv = dv * _gelu_grad(c["zv"])

            dmb = dmx[:, SW:]
            yb = c["outb"] * c["rb"]
            dnb_ref[...] += jnp.sum(dmb * yb, axis=0, keepdims=True)
            doutb = rms_bwd_full(dmb * prm["norm_b"][...], yb, c["rb"])
            lane1 = lax.broadcasted_iota(jnp.int32, (1, LANES), 1)
            dqn_blocks = [None] * (AW // LANES)
            dkn_blocks = [None] * (KVW // LANES)
            dvc_blocks = [None] * (KVW // LANES)
            dsink_vec = jnp.zeros((1, LANES), F32)
            add = lambda old, new: new if old is None else old + new
            for hq, hd in enumerate(c["heads"]):
                mb, e, ek, kb, half = hd["mb"], hd["e"], hd["ek"], hd["kb"], hd["half"]
                dr = doutb[:, LANES * mb:LANES * (mb + 1)]
                if e != ek:
                    dr = pltpu.roll(dr, HEAD_DIM, 1)
                drm = jnp.where(half, dr, 0.0).astype(BF16)
                dp = lax.dot_general(drm, hd["vblk"], (((1,), (1,)), ((), ())), preferred_element_type=F32)
                dvc_blocks[kb] = add(dvc_blocks[kb], lax.dot_general(hd["prb"], drm, (((0,), (0,)), ((), ())),
                                                                     preferred_element_type=F32))
                rowdot = jnp.sum(hd["pr"] * dp, axis=-1, keepdims=True)
                ds = hd["pr"] * (dp - rowdot)
                dsink = jnp.sum(-hd["psink"] * rowdot, axis=0, keepdims=True)
                dsink_vec = dsink_vec + jnp.where(lane1 == hq, dsink, 0.0)
                dbias_ref[hq] += ds
                dsb = (ds * c["scale"]).astype(BF16)
                dqm = jnp.dot(dsb, hd["kblk"], preferred_element_type=F32)
                dqm = jnp.where(half, dqm, 0.0)
                if e != ek:
                    dqm = pltpu.roll(dqm, HEAD_DIM, 1)
                dqn_blocks[mb] = add(dqn_blocks[mb], dqm)
                dkn_blocks[kb] = add(dkn_blocks[kb], lax.dot_general(dsb, hd["qm"], (((0,), (0,)), ((), ())),
                                                                     preferred_element_type=F32))
            dsk_ref[...] += dsink_vec
            dqn = jnp.concatenate(dqn_blocks, axis=1)
            dkn = jnp.concatenate(dkn_blocks, axis=1)
            dvc = jnp.concatenate(dvc_blocks, axis=1)
            yq = c["q"] * c["rq"]
            accq[...] += jnp.sum(dqn * yq, axis=0, keepdims=True)
            dyq = dqn * prm["q_g"][...]
            dq = c["rq"] * (dyq - yq * group_mean_b(dyq * yq))
            yk = c["kcat"] * c["rk"]
            acck[...] += jnp.sum(dkn * yk, axis=0, keepdims=True)
            dyk = dkn * prm["k_g"][...]
            dk = c["rk"] * (dyk - yk * group_mean_b(dyk * yk))

            slot = n % 2
            hold[slot, :, :SW] = dzu
            hold[slot, :, SW:2 * SW] = dzv
            hold[slot, :, 2 * SW:] = dq
            tmpkv[:, :KVW] = dk[:BLOCK]
            tmpkv[:, KVW:] = dvc[:BLOCK]
            newkv[:, :KVW] = dk[BLOCK:]
            newkv[:, KVW:] = dvc[BLOCK:]

        @pl.when(n >= 1)
        def _():
            dz_ref[:, :QW] = hold[(n - 1) % 2].astype(dz_ref.dtype)

        @pl.when((n >= 1) & (n < nb))
        def _():
            dz_ref[:, QW:] = (carry[...] + tmpkv[...]).astype(dz_ref.dtype)

        @pl.when(n == nb)
        def _():
            dz_ref[:, QW:] = carry[...].astype(dz_ref.dtype)
            row = lax.broadcasted_iota(jnp.int32, (BLOCK, BLOCK), 0)
            col = lax.broadcasted_iota(jnp.int32, (BLOCK, BLOCK), 1)
            for h in range(H):
                dt_ref[h] = jnp.where(row >= col, dt_ref[h], 0.0)
            dsb_ref[...] = _dot3(accb[...], gs_ref[...])
            dqg_ref[...] = _dot3(accq[...], gmq_ref[...])
            dkg_ref[...] = _dot3(acck[...], gmk_ref[...])

        @pl.when(n < nb)
        def _():
            carry[...] = newkv[...]

    kvblk = dm.koff // (2 * KVW)
    clamp = lambda n: jnp.minimum(n, nb - 1)
    full = lambda shape: pl.BlockSpec(shape, lambda n: (0,) * len(shape))
    in_specs = [pl.BlockSpec((BLOCK, IN), lambda n: (clamp(n), 0)),
                pl.BlockSpec((BLOCK, 2 * KVW), lambda n: (jnp.maximum(clamp(n) - 1, 0), kvblk)),
                pl.BlockSpec((BLOCK, SW + AW), lambda n: (clamp(n), 0)),
                full((NQ, BLOCK, 2 * BLOCK))] + _mixer_param_specs(dm, None) + [full((SW, LANES)), full((AW, LANES)),
                                                                                full((KVW, LANES))]
    out_shape = (jax.ShapeDtypeStruct((dm.S, IN), BF16),
                 jax.ShapeDtypeStruct((1, SW), F32), jax.ShapeDtypeStruct((H, BLOCK, BLOCK), F32),
                 jax.ShapeDtypeStruct((BLOCK, LANES), F32), jax.ShapeDtypeStruct((1, SW), F32),
                 jax.ShapeDtypeStruct((1, LANES), F32), jax.ShapeDtypeStruct((1, LANES), F32),
                 jax.ShapeDtypeStruct((1, AW), F32), jax.ShapeDtypeStruct((1, LANES), F32),
                 jax.ShapeDtypeStruct((NQ, BLOCK, 2 * BLOCK), F32))
    out_specs = (pl.BlockSpec((BLOCK, IN), lambda n: (jnp.maximum(n - 1, 0), 0)),
                 full((1, SW)), full((H, BLOCK, BLOCK)), full((BLOCK, LANES)), full((1, SW)), full((1, LANES)),
                 full((1, LANES)), full((1, AW)), full((1, LANES)), full((NQ, BLOCK, 2 * BLOCK)))
    scratch = [pltpu.VMEM((2, BLOCK, QW), F32), pltpu.VMEM((BLOCK, 2 * KVW), F32), pltpu.VMEM((BLOCK, 2 * KVW), F32),
               pltpu.VMEM((BLOCK, 2 * KVW), F32), pltpu.VMEM((BLOCK, SW), F32), pltpu.VMEM((1, AW), F32),
               pltpu.VMEM((1, KVW), F32)]
    return _call(
        body, out_shape=out_shape, grid=(nb + 1,), in_specs=in_specs, out_specs=out_specs, scratch=scratch,
        sem=("arbitrary",), args=[z, z, dmixed, dbias_in, *params, *gmats], name=name, comm=comm)


def _adamw(w, gparts, m, v, *, name, layer=0, stacked=None, tr=256):
    R, C = gparts[0].shape
    tr = _tile(R, max(8, min(tr, (1 << 18) // C)), 8)
    ng = len(gparts)
    bc1 = 1.0 - ADAM_B1 ** ADAM_STEP
    bc2 = 1.0 - ADAM_B2 ** ADAM_STEP

    def body(w_ref, *rest):
        g_refs, (m_ref, v_ref), (go_ref, d_ref, mo_ref, vo_ref) = rest[:ng], rest[ng:ng + 2], rest[-4:]
        g = g_refs[0][...].astype(F32)
        for r in g_refs[1:]:
            g = g + r[...].astype(F32)
        mn = ADAM_B1 * m_ref[...] + (1.0 - ADAM_B1) * g
        vn = ADAM_B2 * v_ref[...] + (1.0 - ADAM_B2) * jnp.square(g)
        m_hat = mn / bc1
        v_hat = vn / bc2
        go_ref[...] = g
        d_ref[...] = -ADAM_LR * (m_hat / (jnp.sqrt(v_hat) + ADAM_EPS) + ADAM_WD * w_ref[...])
        mo_ref[...] = mn
        vo_ref[...] = vn

    here = pl.BlockSpec((tr, C), lambda i: (layer * (R // tr) + i, 0))
    blk = pl.BlockSpec((tr, C), lambda i: (i, 0))
    prev = [] if stacked is None else list(stacked)
    return _call(body, out_shape=[jax.ShapeDtypeStruct(w.shape, F32)] * 4, grid=(R // tr,),
                 in_specs=[here] + [blk] * ng + [here, here] + [ANY] * len(prev), out_specs=[here] * 4,
                 sem=("parallel",), args=[w, *gparts, m, v, *prev], name=name,
                 aliases={3 + ng + q: q for q in range(len(prev))})


def _sum_chip_partials(g, axis, base, width, stride, got, chip, *, name, tr=256, comm=()):
    _, R, C = got.shape
    tr = _tile(R, max(8, min(tr, (1 << 18) // C)), 8)
    assert base % width == 0 and stride % width == 0 and (C == width if axis == 1 else R == width)

    def body(chip_ref, own_ref, g0_ref, g1_ref, g2_ref, o_ref):
        o_ref[...] = ((own_ref[...].astype(F32) + g0_ref[...].astype(F32)) + g1_ref[...].astype(F32)) + g2_ref[...].astype(F32)

    band = lambda j: base // width + j[0] * (stride // width)
    if axis == 1:
        own_spec = pl.BlockSpec((tr, C), lambda i, j: (i, band(j)))
    else:
        own_spec = pl.BlockSpec((tr, C), lambda i, j: (band(j) * (R // tr) + i, 0))
    part = lambda k: pl.BlockSpec((None, tr, C), lambda i, j: (k, i, 0))
    return _call(body, out_shape=[jax.ShapeDtypeStruct((R, C), F32)], grid=(R // tr,),
                 in_specs=[own_spec, part(0), part(1), part(2)], out_specs=[pl.BlockSpec((tr, C), lambda i, j: (i, 0))],
                 sem=("parallel",), args=[g, got, got, got], name=name, prefetch=chip, comm=comm)


def _pack(arrays):
    parts, total = [], 0
    for a in arrays:
        flat = a.reshape(-1).astype(F32)
        pad = (-flat.shape[0]) % (8 * LANES)
        parts.append(jnp.pad(flat, (0, pad)))
        total += flat.shape[0] + pad
    parts.append(jnp.zeros(((-total) % (PACK_ROWS * LANES),), F32))
    return jnp.concatenate(parts).reshape(-1, LANES)


def _unpack(packed, like):
    flat = packed.reshape(-1)
    out, off = [], 0
    for a in like:
        n = int(np.prod(a.shape))
        out.append(flat[off:off + n].reshape(a.shape))
        off += n + ((-n) % (8 * LANES))
    return out


def kernel(x, rel_bias, norm1_g, w_in, sgu_norm_g, sgu_w, sgu_b, q_norm_g, k_norm_g, sinks, out_norm_a, out_norm_b, w_out, norm2_g, w_gate, w_up, w_down, loss_target, m_rel_bias, m_norm1_g, m_w_in, m_sgu_norm_g, m_sgu_w, m_sgu_b, m_q_norm_g, m_k_norm_g, m_sinks, m_out_norm_a, m_out_norm_b, m_w_out, m_norm2_g, m_w_gate, m_w_up, m_w_down, v_rel_bias, v_norm1_g, v_w_in, v_sgu_norm_g, v_sgu_w, v_sgu_b, v_q_norm_g, v_k_norm_g, v_sinks, v_out_norm_a, v_out_norm_b, v_w_out, v_norm2_g, v_w_gate, v_w_up, v_w_down):
    L, D, n_in = w_in.shape
    S = x.shape[1]
    IN = N_CHIPS * n_in
    n_ff = w_gate.shape[2]
    FF = N_CHIPS * n_ff
    H = sgu_w.shape[1]
    NQ = sinks.shape[1]
    SW, AW = H * HEAD_DIM, NQ * HEAD_DIM
    KVW = (IN - 2 * SW - AW) // 2
    dm = _MixerDims(S, IN, SW, AW, KVW, H, NQ)
    assert sgu_w.shape[2] == BLOCK and q_norm_g.shape[1] == HEAD_DIM and SW + AW == D

    wb = {k: w.astype(BF16) for k, w in (("in", w_in), ("out", w_out), ("gate", w_gate), ("up", w_up), ("down", w_down))}

    def gather(l, which):
        if which == "in":
            return _Gather([wb["in"]], [(0, 0, 1, 0, n_in)], [(D, IN)], l)
        if which == "out":
            return _Gather([wb["out"]], [(0, 0, 0, 0, D // N_CHIPS)], [(D, D)], l)
        if which == "gu":
            return _Gather([wb["gate"], wb["up"]], [(0, 0, 1, 0, 2 * n_ff), (1, 0, 1, n_ff, 2 * n_ff)], [(D, 2 * FF)], l)
        return _Gather([wb["down"]], [(0, 0, 0, 0, n_ff)], [(FF, D)], l)

    nxt = lambda l, *which: [gather(l + 1, w) for w in which] if l + 1 < L else []
    W = {}
    (W[0, "in"],) = _comm_only([gather(0, "in")], name="gather_first_weights")

    gs, gmod_q = _group_consts(SW)
    _, gmod_k = _group_consts(KVW)
    pblk = _same_group()
    onehot = _bucket_onehot()
    rbt = jnp.pad(rel_bias.T, ((0, 0), (0, LANES - NUM_BUCKETS)))
    (bias,) = _matmul(rbt, onehot.T, mode="nn", out_dtype=F32, exact=True, tn=4096, name="bias_table")
    bias = bias.reshape(NQ, BLOCK, 2 * BLOCK)

    def mixer_params(l):
        return [sgu_norm_g[l].reshape(1, SW), sgu_w[l], jnp.repeat(sgu_b[l].T, HEAD_DIM, axis=1),
                out_norm_a[l].reshape(1, SW), jnp.tile(q_norm_g[l], NQ).reshape(1, AW),
                jnp.tile(k_norm_g[l], dm.NKV).reshape(1, KVW), out_norm_b[l].reshape(1, AW), sinks[l], bias, pblk]

    xs = x.reshape(S, D)
    saved = []
    for l in range(L):
        h = _rms_fwd(xs, norm1_g[l].reshape(1, D), name="norm1_fwd")
        z, *got = _matmul(h, W[l, "in"], mode="nn", out_dtype=F32, tn=1792, name="in_proj",
                          comm=[gather(0, "out"), gather(0, "down")] if l == 0 else [])
        if l == 0:
            W[0, "out"], W[0, "down"] = got
        mixed, *got = _mixer_fwd(dm, z, mixer_params(l), name="mixer_fwd",
                                 comm=[gather(0, "gu")] if l == 0 else nxt(l, "in", "out"))
        if l == 0:
            (W[0, "gu"],) = got
        elif got:
            W[l + 1, "in"], W[l + 1, "out"] = got
        (x1,) = _matmul(mixed, W[l, "out"], mode="nn", out_dtype=F32, res=xs, name="out_proj")
        h2 = _rms_fwd(x1, norm2_g[l].reshape(1, D), name="norm2_fwd")
        ab, f, *got = _gate_up_swiglu(h2, W[l, "gu"], n_ff, name="gate_up_swiglu",
                                      comm=nxt(l, "gu") + (nxt(l, "in") if l == 0 else []))
        if got:
            W[l + 1, "gu"] = got.pop(0)
        if got:
            (W[l + 1, "in"],) = got
        x2, *got = _matmul(f, W[l, "down"], mode="nn", out_dtype=F32, res=x1, tm=512, tk=5632, name="down_proj",
                           comm=nxt(l, "down") + (nxt(l, "out") if l == 0 else []))
        if got:
            W[l + 1, "down"] = got.pop(0)
        if got:
            (W[l + 1, "out"],) = got
        saved.append((xs, h, z, mixed, x1, h2, ab, f))
        xs = x2

    dx, dxb, loss_part = _loss_head(xs, loss_target.reshape(S, D), name="loss_head")
    loss = lax.psum(loss_part[0, 0], ("x", "y", "c"))

    dbias = jnp.zeros((NQ, BLOCK, 2 * BLOCK), F32)
    n_out = D // N_CHIPS
    big_names = ("w_in", "w_out", "w_gate", "w_up", "w_down")
    own = {k: [None] * L for k in big_names}
    got = {k: [None] * L for k in big_names}
    names_layer = ("norm1_g", "sgu_norm_g", "sgu_w", "sgu_b", "q_norm_g", "k_norm_g", "sinks", "out_norm_a", "out_norm_b",
                   "norm2_g")
    small = {k: [None] * L for k in names_layer}
    small_unit = lambda layers: _AllGather(_pack([jnp.stack([small[k][i] for i in layers]) for k in names_layer]))
    for l in reversed(range(L)):
        xl, h, z, mixed, x1, h2, ab, f = saved[l]
        early = [small_unit(range(1, L))] if (l == 0 and L > 1) else []
        dab, *gathered_early = _down_dx_swiglu_bwd(dxb, W[l, "down"], ab, n_ff, name="down_dx_swiglu_bwd", comm=early)
        (g_d,) = _matmul_tn(f, dxb, out_dtype=BF16, tm=1408, tn=1024, tt=2048, name="down_proj_dw")
        g_gu, got["w_down"][l] = _matmul_tn(h2, dab, out_dtype=BF16, tt=2048, name="gate_up_proj_dw",
                                            comm=[_Scatter([g_d], [(0, 0, 0, n_ff, n_ff)])])
        dh2, got["w_gate"][l] = _matmul(dab, W[l, "gu"], mode="nt", out_dtype=F32, tk=2816, name="gate_up_proj_dx",
                                        comm=[_Scatter([g_gu], [(0, 1, 0, n_ff, 2 * n_ff)])])
        dx1, dx1b, dg2 = _rms_bwd(x1, norm2_g[l].reshape(1, D), dh2, dx, name="norm2_bwd")
        (dmixed,) = _matmul(dx1b, W[l, "out"], mode="nt", out_dtype=F32, name="out_proj_dx")
        (g_out,) = _matmul_tn(mixed, dx1b, out_dtype=BF16, tt=2048, name="out_proj_dw")
        (dz, d_sg, d_t, d_sb, d_na, d_qg, d_kg, d_nb, d_sk, dbias, got["w_up"][l], got["w_out"][l]) = _mixer_bwd(
            dm, z, dmixed, dbias, mixer_params(l), (gs, gmod_q, gmod_k), name="mixer_bwd",
            comm=[_Scatter([g_gu, g_out], [(0, 1, n_ff, n_ff, 2 * n_ff), (1, 0, 0, n_out, n_out)])])
        (g_in,) = _matmul_tn(h, dz, out_dtype=BF16, tn=896, tt=2048, name="in_proj_dw")
        dh, got["w_in"][l] = _matmul(dz, W[l, "in"], mode="nt", out_dtype=F32, tk=3584, name="in_proj_dx",
                                     comm=[_Scatter([g_in], [(0, 1, 0, n_in, n_in)])])
        dx, dxb, dg1 = _rms_bwd(xl, norm1_g[l].reshape(1, D), dh, dx1, name="norm1_bwd")
        own["w_in"][l], own["w_out"][l] = (g_in, 1, 0, n_in, n_in), (g_out, 0, 0, n_out, n_out)
        own["w_gate"][l], own["w_up"][l] = (g_gu, 1, 0, n_ff, 2 * n_ff), (g_gu, 1, n_ff, n_ff, 2 * n_ff)
        own["w_down"][l] = (g_d, 0, 0, n_ff, n_ff)
        small["norm1_g"][l] = dg1.reshape(D)
        small["norm2_g"][l] = dg2.reshape(D)
        small["sgu_norm_g"][l] = d_sg.reshape(H, HEAD_DIM)
        small["sgu_w"][l] = d_t
        small["sgu_b"][l] = d_sb[:, :H].T
        small["q_norm_g"][l] = d_qg[0, :HEAD_DIM]
        small["k_norm_g"][l] = d_kg[0, :HEAD_DIM]
        small["sinks"][l] = d_sk[0, :NQ]
        small["out_norm_a"][l] = d_na.reshape(SW)
        small["out_norm_b"][l] = d_nb.reshape(AW)
    grad_x = dx.reshape(1, S, D)
    (d_rb,) = _matmul(dbias.reshape(NQ, BLOCK * 2 * BLOCK), onehot, mode="nn", out_dtype=F32, exact=True, tk=4096,
                      name="rel_bias_grad")
    d_rel_bias = d_rb[:, :NUM_BUCKETS].T

    chip = (2 * lax.axis_index("x") + lax.axis_index("y")).astype(jnp.int32).reshape(1)
    jobs = [(nm, l) for nm in big_names for l in range(L)]
    late = _AllGather(_pack([small[k][0][None] for k in names_layer] + [d_rel_bias]))
    mine, sib = {}, {}
    for i, (nm, l) in enumerate(jobs):
        g, axis, base, width, stride = own[nm][l]
        units = ([late] if i == 0 else []) + ([_Swap([mine[jobs[i - 1]]])] if i > 0 else [])
        mine[nm, l], *rest = _sum_chip_partials(g, axis, base, width, stride, got[nm][l], chip, name="sum_chip_partials",
                                                comm=units)
        if i == 0:
            gathered_late = rest.pop(0)
        if i > 0:
            (sib[jobs[i - 1]],) = rest
    (sib[jobs[-1]],) = _comm_only([_Swap([mine[jobs[-1]]])], name="swap_last_sums")
    wmv = dict(w_in=(w_in, m_w_in, v_w_in), w_out=(w_out, m_w_out, v_w_out), w_gate=(w_gate, m_w_gate, v_w_gate),
               w_up=(w_up, m_w_up, v_w_up), w_down=(w_down, m_w_down, v_w_down))
    big = {}
    two = lambda t: t.reshape(-1, t.shape[-1])
    for nm in big_names:
        w, m, v = wmv[nm]
        res = None
        for l in range(L):
            res = _adamw(two(w), [mine[nm, l], sib[nm, l]], two(m), two(v), layer=l, stacked=res, name="adamw_" + nm)
        big[nm] = [r.reshape(w.shape) for r in res]

    w_small = dict(rel_bias=rel_bias, norm1_g=norm1_g, sgu_norm_g=sgu_norm_g, sgu_w=sgu_w, sgu_b=sgu_b, q_norm_g=q_norm_g,
                   k_norm_g=k_norm_g, sinks=sinks, out_norm_a=out_norm_a, out_norm_b=out_norm_b, norm2_g=norm2_g)
    m_small = dict(rel_bias=m_rel_bias, norm1_g=m_norm1_g, sgu_norm_g=m_sgu_norm_g, sgu_w=m_sgu_w, sgu_b=m_sgu_b,
                   q_norm_g=m_q_norm_g, k_norm_g=m_k_norm_g, sinks=m_sinks, out_norm_a=m_out_norm_a,
                   out_norm_b=m_out_norm_b, norm2_g=m_norm2_g)
    v_small = dict(rel_bias=v_rel_bias, norm1_g=v_norm1_g, sgu_norm_g=v_sgu_norm_g, sgu_w=v_sgu_w, sgu_b=v_sgu_b,
                   q_norm_g=v_q_norm_g, k_norm_g=v_k_norm_g, sinks=v_sinks, out_norm_a=v_out_norm_a,
                   out_norm_b=v_out_norm_b, norm2_g=v_norm2_g)

    def adamw_small(gathered, pick, name):
        like = pick(w_small)
        res = _adamw(_pack(like), [gathered[i] for i in range(N_DEV)], _pack(pick(m_small)), _pack(pick(v_small)), name=name)
        return [_unpack(r, like) for r in res]

    first_layer = adamw_small(gathered_late, lambda d: [d[k][:1] for k in names_layer] + [d["rel_bias"]], "adamw_small_late")
    if L > 1:
        others = adamw_small(gathered_early[0], lambda d: [d[k][1:] for k in names_layer], "adamw_small_early")
    sm = {"rel_bias": [r[-1] for r in first_layer]}
    for j, k in enumerate(names_layer):
        sm[k] = [jnp.concatenate([first_layer[q][j]] + ([others[q][j]] if L > 1 else [])) for q in range(4)]

    order = ("rel_bias", "norm1_g", "w_in", "sgu_norm_g", "sgu_w", "sgu_b", "q_norm_g", "k_norm_g", "sinks", "out_norm_a",
             "out_norm_b", "w_out", "norm2_g", "w_gate", "w_up", "w_down")
    pick = lambda k, i: big[k][i] if k in big else sm[k][i]
    outs = [loss, grad_x]
    for i in range(4):
        outs += [pick(k, i) for k in order]
    return tuple(outs)
```

```python
import functools
import math

import numpy as np

import jax
import jax.numpy as jnp
from jax import lax
from jax.experimental import pallas as pl
from jax.experimental.pallas import tpu as pltpu

F32 = jnp.float32
BF16 = jnp.bfloat16
MESH = pl.DeviceIdType.MESH
ANY = pl.BlockSpec(memory_space=pl.ANY)

HEAD_DIM = 64
BLOCK = 128
NUM_BUCKETS = 32
MAX_DISTANCE = 128
EPS = 1e-6
NEG_INF = -1e30
ADAM_LR, ADAM_B1, ADAM_B2, ADAM_EPS, ADAM_WD, ADAM_STEP = 0.001, 0.9, 0.999, 1e-08, 0.01, 10

LANES = 128
VMEM_LIMIT = 56 * 1024 * 1024
N_CHIPS = 4
N_DEV = 8
PACK_ROWS = 256
ROW_CHUNK = 256


def _cparams(sem=None):
    return pltpu.CompilerParams(dimension_semantics=sem, vmem_limit_bytes=VMEM_LIMIT)


def _tile(dim, target, align=LANES):
    best = None
    for t in range(align, min(dim, target) + 1, align):
        if dim % t == 0:
            best = t
    return best if best is not None else dim


def _mesh_pos():
    return lax.axis_index("x"), lax.axis_index("y"), lax.axis_index("c")


def _other_chips(x, y):
    return [(1 - x, y), (x, 1 - y), (1 - x, 1 - y)]


def _slab(ref, axis, start, size, half=None):
    if axis == 1:
        rows = ref.shape[0]
        r = pl.ds(0, rows) if half is None else pl.ds(pl.multiple_of(half * (rows // 2), 16), rows // 2)
        return ref.at[r, pl.ds(pl.multiple_of(start, LANES), size)]
    if half is None:
        return ref.at[pl.ds(pl.multiple_of(start, 16), size), :]
    return ref.at[pl.ds(pl.multiple_of(start + half * (size // 2), 16), size // 2), :]


def _remote(src, dst, send_sem, recv_sem, to):
    return pltpu.make_async_remote_copy(src_ref=src, dst_ref=dst, send_sem=send_sem, recv_sem=recv_sem,
                                        device_id=to, device_id_type=MESH)


class _Gather:
    def __init__(self, shards, streams, out_shapes, layer):
        self.ins, self.streams, self.layer = list(shards), streams, layer
        self.outs = [jax.ShapeDtypeStruct(s, BF16) for s in out_shapes]
        ns = len(streams)
        self.sems = [pltpu.SemaphoreType.DMA((ns, 3))] * 4 + [pltpu.SemaphoreType.DMA((ns,))]

    def _sent(self, srcs, outs, sems):
        send, recv, _, _, local = sems
        x, y, c = _mesh_pos()
        me_j = 2 * x + y
        own, sends = [], []
        for s, (si, oi, axis, base, stride) in enumerate(self.streams):
            src, out = srcs[si].at[self.layer], outs[oi]
            width = src.shape[axis]
            own.append(pltpu.make_async_copy(src, _slab(out, axis, base + me_j * stride, width), local.at[s]))
            half_rows = src.shape[0] // 2
            mine = src.at[pl.ds(pl.multiple_of(c * half_rows, 16), half_rows), :]
            for k, (px, py) in enumerate(_other_chips(x, y)):
                sends.append(_remote(mine, _slab(out, axis, base + me_j * stride, width, half=c),
                                     send.at[s, k], recv.at[s, k], (px, py, c)))
        return own, sends

    def start(self, srcs, outs, sems):
        own, sends = self._sent(srcs, outs, sems)
        for cp in own + sends:
            cp.start()

    def finish(self, srcs, outs, sems):
        send, recv, fsend, frecv, _ = sems
        x, y, c = _mesh_pos()
        sib = (x, y, 1 - c)
        forwards, fwd_arrivals = [], []
        for s, (si, oi, axis, base, stride) in enumerate(self.streams):
            out = outs[oi]
            width = srcs[si].shape[1 + axis]
            for k, (px, py) in enumerate(_other_chips(x, y)):
                start = base + (2 * px + py) * stride
                got = _slab(out, axis, start, width, half=c)
                _remote(got, got, send.at[s, k], recv.at[s, k], (px, py, c)).wait_recv()
                fwd = _remote(got, got, fsend.at[s, k], frecv.at[s, k], sib)
                fwd.start()
                forwards.append(fwd)
                theirs = _slab(out, axis, start, width, half=1 - c)
                fwd_arrivals.append(_remote(theirs, theirs, fsend.at[s, k], frecv.at[s, k], sib))
        for a in fwd_arrivals:
            a.wait_recv()
        for cp in forwards:
            cp.wait_send()
        own, sends = self._sent(srcs, outs, sems)
        for cp in sends:
            cp.wait_send()
        for cp in own:
            cp.wait()


class _Scatter:
    def __init__(self, grads, streams):
        self.ins, self.streams = list(grads), streams
        shard = lambda g, axis, width: (width, g.shape[1]) if axis == 0 else (g.shape[0], width)
        self.outs = [jax.ShapeDtypeStruct((3,) + shard(grads[gi], axis, width), BF16) for gi, axis, _, width, _ in streams]
        self.sems = [pltpu.SemaphoreType.DMA((len(streams), 3))] * 2

    def _copies(self, srcs, outs, sems):
        send, recv = sems
        x, y, c = _mesh_pos()
        copies = []
        for s, (gi, axis, base, width, stride) in enumerate(self.streams):
            for k, (px, py) in enumerate(_other_chips(x, y)):
                copies.append(_remote(_slab(srcs[gi], axis, base + (2 * px + py) * stride, width), outs[s].at[k],
                                      send.at[s, k], recv.at[s, k], (px, py, c)))
        return copies

    def start(self, srcs, outs, sems):
        for cp in self._copies(srcs, outs, sems):
            cp.start()

    def finish(self, srcs, outs, sems):
        for cp in self._copies(srcs, outs, sems):
            cp.wait()


class _Swap:
    def __init__(self, arrays):
        self.ins = list(arrays)
        self.outs = [jax.ShapeDtypeStruct(a.shape, a.dtype) for a in arrays]
        self.sems = [pltpu.SemaphoreType.DMA((len(arrays),))] * 2

    def _copies(self, srcs, outs, sems):
        send, recv = sems
        x, y, c = _mesh_pos()
        return [_remote(srcs[s], outs[s], send.at[s], recv.at[s], (x, y, 1 - c)) for s in range(len(srcs))]

    def start(self, srcs, outs, sems):
        for cp in self._copies(srcs, outs, sems):
            cp.start()

    def finish(self, srcs, outs, sems):
        for cp in self._copies(srcs, outs, sems):
            cp.wait()


class _AllGather:
    def __init__(self, part):
        self.ins = [part]
        self.outs = [jax.ShapeDtypeStruct((N_DEV,) + part.shape, part.dtype)]
        self.sems = [pltpu.SemaphoreType.DMA((7,)), pltpu.SemaphoreType.DMA((7,)), pltpu.SemaphoreType.DMA(())]

    def _first(self, srcs, outs, sems):
        (x_ref,), (out,), (send, recv, local) = srcs, outs, sems
        x, y, c = _mesh_pos()
        mine = out.at[4 * x + 2 * y + c]
        own = pltpu.make_async_copy(x_ref, mine, local)
        sends = [_remote(x_ref, mine, send.at[0], recv.at[0], (x, y, 1 - c))]
        sends += [_remote(x_ref, mine, send.at[1 + k], recv.at[1 + k], (px, py, c)) for k, (px, py) in enumerate(_other_chips(x, y))]
        return own, sends

    def start(self, srcs, outs, sems):
        own, sends = self._first(srcs, outs, sems)
        own.start()
        for cp in sends:
            cp.start()

    def finish(self, srcs, outs, sems):
        (out,), (send, recv, local) = outs, sems
        x, y, c = _mesh_pos()
        me, sib = (x, y, c), (x, y, 1 - c)
        slot = lambda px, py, pc: out.at[4 * px + 2 * py + pc]
        passed = []
        for k, (px, py) in enumerate(_other_chips(x, y)):
            blk = slot(px, py, c)
            _remote(blk, blk, send.at[1 + k], recv.at[1 + k], me).wait_recv()
            fwd = _remote(blk, blk, send.at[4 + k], recv.at[4 + k], sib)
            fwd.start()
            passed.append(fwd)
        blk = slot(x, y, 1 - c)
        _remote(blk, blk, send.at[0], recv.at[0], me).wait_recv()
        for k, (px, py) in enumerate(_other_chips(x, y)):
            blk = slot(px, py, 1 - c)
            _remote(blk, blk, send.at[4 + k], recv.at[4 + k], me).wait_recv()
        own, sends = self._first(srcs, outs, sems)
        for cp in sends + passed:
            cp.wait_send()
        own.wait()


def _call(body, *, grid, in_specs, out_specs, out_shape, args, name, scratch=(), sem=None, comm=(), prefetch=None,
          aliases=None):
    out_shape, out_specs = tuple(out_shape), tuple(out_specs)
    n_in, n_out, n_scr = len(in_specs), len(out_shape), len(scratch)
    n_pre = 0 if prefetch is None else 1
    c_ins = [a for u in comm for a in u.ins]
    c_outs = [o for u in comm for o in u.outs]
    c_sems = [s for u in comm for s in u.sems]

    def wrapped(*refs):
        pre, refs = refs[:n_pre], refs[n_pre:]
        ins, rest = refs[:n_in], refs[n_in:]
        cin, rest = rest[:len(c_ins)], rest[len(c_ins):]
        outs, rest = rest[:n_out], rest[n_out:]
        cout, rest = rest[:len(c_outs)], rest[len(c_outs):]
        scr, csem = rest[:n_scr], rest[n_scr:]

        def each(fn_name):
            i = o = s = 0
            for u in comm:
                getattr(u, fn_name)(cin[i:i + len(u.ins)], cout[o:o + len(u.outs)], csem[s:s + len(u.sems)])
                i, o, s = i + len(u.ins), o + len(u.outs), s + len(u.sems)

        if comm:
            pids = [pl.program_id(d) for d in range(len(grid))]
            first = functools.reduce(jnp.logical_and, [p == 0 for p in pids])
            last = functools.reduce(jnp.logical_and, [p == g - 1 for p, g in zip(pids, grid)])
            pl.when(first)(lambda: each("start"))
        body(*pre, *ins, *outs, *scr)
        if comm:
            pl.when(last)(lambda: each("finish"))

    if comm:
        sem = ("arbitrary",) * len(grid)
    all_in = list(in_specs) + [ANY] * len(c_ins)
    all_out = out_specs + tuple([ANY] * len(c_outs))
    all_scr = list(scratch) + c_sems
    kw = {}
    if aliases:
        kw["input_output_aliases"] = {n_pre + i: o for i, o in aliases.items()}
    if prefetch is None:
        kw.update(grid=grid, in_specs=all_in, out_specs=all_out, scratch_shapes=all_scr)
        pre_args = []
    else:
        kw["grid_spec"] = pltpu.PrefetchScalarGridSpec(num_scalar_prefetch=1, grid=grid, in_specs=all_in, out_specs=all_out,
                                                       scratch_shapes=all_scr)
        pre_args = [prefetch]
    return pl.pallas_call(wrapped, out_shape=out_shape + tuple(c_outs), compiler_params=_cparams(sem), name=name, **kw)(
        *pre_args, *args, *c_ins)


def _comm_only(comm, *, name):
    def body(tick_ref):
        tick_ref[...] = jnp.zeros(tick_ref.shape, tick_ref.dtype)

    outs = _call(body, grid=(1,), in_specs=[], out_specs=[pl.BlockSpec((8, LANES), lambda i: (0, 0))],
                 out_shape=[jax.ShapeDtypeStruct((8, LANES), F32)], args=[], name=name, comm=comm)
    return outs[1:]


def _split3(x):
    hi = x.astype(BF16)
    r1 = x - hi.astype(F32)
    mid = r1.astype(BF16)
    lo = (r1 - mid.astype(F32)).astype(BF16)
    return hi, mid, lo


def _dot3(x, g):
    hi, mid, lo = _split3(x)
    d = lambda a: jnp.dot(a, g, preferred_element_type=F32)
    return d(hi) + d(mid) + d(lo)


GROUP_TILE = 256


def _group_sum(x, pblk):
    hi = x.astype(BF16)
    lo = (x - hi.astype(F32)).astype(BF16)
    cols = []
    for b in range(x.shape[1] // GROUP_TILE):
        sl = slice(GROUP_TILE * b, GROUP_TILE * (b + 1))
        cols.append(jnp.dot(hi[:, sl], pblk, preferred_element_type=F32) + jnp.dot(lo[:, sl], pblk, preferred_element_type=F32))
    return cols[0] if len(cols) == 1 else jnp.concatenate(cols, axis=1)


def _matmul(a, b, *, mode, out_dtype, name, res=None, exact=False, tm=1024, tn=1024, tk=2048, b_col0=0, comm=()):
    M, K = a.shape
    N = b.shape[1] if mode == "nn" else b.shape[0]
    tm, tn, tk = _tile(M, tm, 8 if M < LANES else LANES), _tile(N, tn), _tile(K, tk)
    nk = K // tk
    dn = (((1,), (0,)), ((), ())) if mode == "nn" else (((1,), (1,)), ((), ()))

    def body(*refs):
        a_ref, b_ref = refs[0], refs[1]
        r_ref = refs[2] if res is not None else None
        o_ref = refs[3] if res is not None else refs[2]
        if exact:
            part = _dot3(a_ref[...], b_ref[...])
        else:
            part = lax.dot_general(a_ref[...].astype(BF16), b_ref[...].astype(BF16), dn, preferred_element_type=F32)

        def finish(total):
            if r_ref is not None:
                total = r_ref[...] + total
            o_ref[...] = total.astype(o_ref.dtype)

        if nk == 1:
            finish(part)
        else:
            acc = refs[-1]
            k = pl.program_id(2)

            @pl.when(k == 0)
            def _():
                acc[...] = part

            @pl.when(k > 0)
            def _():
                acc[...] += part

            @pl.when(k == nk - 1)
            def _():
                finish(acc[...])

    if mode == "nn":
        b_spec = pl.BlockSpec((tk, tn), lambda j, i, k: (k, j))
    else:
        assert b_col0 % tk == 0
        b_spec = pl.BlockSpec((tn, tk), lambda j, i, k: (j, k + b_col0 // tk))
    in_specs = [pl.BlockSpec((tm, tk), lambda j, i, k: (i, k)), b_spec]
    args = [a, b]
    if res is not None:
        in_specs.append(pl.BlockSpec((tm, tn), lambda j, i, k: (i, j)))
        args.append(res)
    return _call(
        body, grid=(N // tn, M // tm, nk), in_specs=in_specs,
        out_specs=[pl.BlockSpec((tm, tn), lambda j, i, k: (i, j))], out_shape=[jax.ShapeDtypeStruct((M, N), out_dtype)],
        scratch=[pltpu.VMEM((tm, tn), F32)] if nk > 1 else [], sem=("parallel", "parallel", "arbitrary"),
        args=args, name=name, comm=comm)


def _matmul_tn(a, b, *, out_dtype, name, tm=2048, tn=1024, tt=1024, comm=()):
    T, Mo = a.shape
    N = b.shape[1]
    tm, tn, tt = _tile(Mo, tm), _tile(N, tn), _tile(T, tt)
    nt = T // tt

    def body(a_ref, b_ref, o_ref, acc):
        t = pl.program_id(2)
        part = lax.dot_general(a_ref[...].astype(BF16), b_ref[...].astype(BF16), (((0,), (0,)), ((), ())),
                               preferred_element_type=F32)

        @pl.when(t == 0)
        def _():
            acc[...] = part

        @pl.when(t > 0)
        def _():
            acc[...] += part

        @pl.when(t == nt - 1)
        def _():
            o_ref[...] = acc[...].astype(o_ref.dtype)

    return _call(
        body, grid=(Mo // tm, N // tn, nt),
        in_specs=[pl.BlockSpec((tt, tm), lambda i, j, t: (t, i)), pl.BlockSpec((tt, tn), lambda i, j, t: (t, j))],
        out_specs=[pl.BlockSpec((tm, tn), lambda i, j, t: (i, j))], out_shape=[jax.ShapeDtypeStruct((Mo, N), out_dtype)],
        scratch=[pltpu.VMEM((tm, tn), F32)], sem=("parallel", "parallel", "arbitrary"), args=[a, b], name=name, comm=comm)


def _rms_fwd(x, g, *, name, tr=512):
    R, D = x.shape
    tr = _tile(R, tr, 8)

    def body(x_ref, g_ref, o_ref):
        xv = x_ref[...]
        r = lax.rsqrt(jnp.mean(xv * xv, axis=-1, keepdims=True) + EPS)
        o_ref[...] = (xv * r * g_ref[...]).astype(o_ref.dtype)

    return pl.pallas_call(
        body, out_shape=jax.ShapeDtypeStruct((R, D), BF16), grid=(R // tr,),
        in_specs=[pl.BlockSpec((tr, D), lambda i: (i, 0)), pl.BlockSpec((1, D), lambda i: (0, 0))],
        out_specs=pl.BlockSpec((tr, D), lambda i: (i, 0)),
        compiler_params=_cparams(("parallel",)), name=name,
    )(x, g)


def _rms_bwd(x, g, dh, dres, *, name, tr=512):
    R, D = x.shape
    tr = _tile(R, tr, 8)

    def body(x_ref, g_ref, dh_ref, dres_ref, dx_ref, dxb_ref, dg_ref):
        i = pl.program_id(0)
        xv = x_ref[...]
        r = lax.rsqrt(jnp.mean(xv * xv, axis=-1, keepdims=True) + EPS)
        y = xv * r
        dhv = dh_ref[...]
        dy = dhv * g_ref[...]
        dx = dres_ref[...] + r * (dy - y * jnp.mean(dy * y, axis=-1, keepdims=True))
        dx_ref[...] = dx
        dxb_ref[...] = dx.astype(BF16)
        dg = jnp.sum(dhv * y, axis=0, keepdims=True)

        @pl.when(i == 0)
        def _():
            dg_ref[...] = dg

        @pl.when(i > 0)
        def _():
            dg_ref[...] += dg

    row = pl.BlockSpec((tr, D), lambda i: (i, 0))
    vec = pl.BlockSpec((1, D), lambda i: (0, 0))
    return pl.pallas_call(
        body, out_shape=(jax.ShapeDtypeStruct((R, D), F32), jax.ShapeDtypeStruct((R, D), BF16),
                         jax.ShapeDtypeStruct((1, D), F32)), grid=(R // tr,),
        in_specs=[row, vec, row, row], out_specs=(row, row, vec),
        compiler_params=_cparams(("arbitrary",)), name=name,
    )(x, g, dh, dres)


def _sigmoid(x):
    return 0.5 * (jnp.tanh(0.5 * x) + 1.0)


def _gate_up_swiglu(h, w_gu, n_ff, *, name, tm=512, comm=()):
    M, K = h.shape
    nsh = w_gu.shape[1] // (2 * n_ff)
    tm = _tile(M, tm)

    def body(h_ref, w_ref, ab_ref, f_ref):
        for c in range(tm // _tile(tm, ROW_CHUNK)):
            rows = pl.ds(c * _tile(tm, ROW_CHUNK), _tile(tm, ROW_CHUNK))
            ab = jnp.dot(h_ref[rows, :], w_ref[...], preferred_element_type=F32)
            ab_ref[rows, :] = ab.astype(ab_ref.dtype)
            a, b = ab[:, :n_ff], ab[:, n_ff:]
            f_ref[rows, :] = (a * _sigmoid(a) * b).astype(f_ref.dtype)

    return _call(
        body, grid=(nsh, M // tm),
        in_specs=[pl.BlockSpec((tm, K), lambda j, i: (i, 0)), pl.BlockSpec((K, 2 * n_ff), lambda j, i: (0, j))],
        out_specs=[pl.BlockSpec((tm, 2 * n_ff), lambda j, i: (i, j)), pl.BlockSpec((tm, n_ff), lambda j, i: (i, j))],
        out_shape=[jax.ShapeDtypeStruct((M, 2 * nsh * n_ff), BF16), jax.ShapeDtypeStruct((M, nsh * n_ff), BF16)],
        sem=("parallel", "parallel"), args=[h, w_gu], name=name, comm=comm)


def _down_dx_swiglu_bwd(dy, w_d, ab, n_ff, *, name, tm=512, shards=2, comm=()):
    M, K = dy.shape
    nsh = w_d.shape[0] // n_ff
    tm = _tile(M, tm)
    shards = shards if nsh % shards == 0 else 1

    def body(dy_ref, w_ref, ab_ref, dab_ref):
        for c in range(tm // _tile(tm, ROW_CHUNK)):
            rows = pl.ds(c * _tile(tm, ROW_CHUNK), _tile(tm, ROW_CHUNK))
            df_all = lax.dot_general(dy_ref[rows, :], w_ref[...], (((1,), (1,)), ((), ())), preferred_element_type=F32)
            for s in range(shards):
                df = df_all[:, s * n_ff:(s + 1) * n_ff]
                a_cols = slice(2 * s * n_ff, (2 * s + 1) * n_ff)
                b_cols = slice((2 * s + 1) * n_ff, (2 * s + 2) * n_ff)
                av = ab_ref[rows, a_cols].astype(F32)
                sg = _sigmoid(av)
                dab_ref[rows, a_cols] = (df * ab_ref[rows, b_cols].astype(F32) * (sg * (1.0 + av * (1.0 - sg)))).astype(dab_ref.dtype)
                dab_ref[rows, b_cols] = (df * (av * sg)).astype(dab_ref.dtype)

    cols = pl.BlockSpec((tm, 2 * shards * n_ff), lambda j, i: (i, j))
    return _call(
        body, grid=(nsh // shards, M // tm),
        in_specs=[pl.BlockSpec((tm, K), lambda j, i: (i, 0)), pl.BlockSpec((shards * n_ff, K), lambda j, i: (j, 0)), cols],
        out_specs=[cols], out_shape=[jax.ShapeDtypeStruct(ab.shape, BF16)],
        sem=("parallel", "parallel"), args=[dy, w_d, ab], name=name, comm=comm)


def _loss_head(y, target, *, name, tr=256):
    R, D = y.shape
    tr = _tile(R, tr, 8)

    def body(y_ref, t_ref, dy_ref, dyb_ref, l_ref):
        i = pl.program_id(0)
        e = y_ref[...] - t_ref[...]
        dy = e * (1.0 / D)
        dy_ref[...] = dy
        dyb_ref[...] = dy.astype(BF16)
        part = 0.5 * jnp.sum(jnp.mean(e * e, axis=-1, keepdims=True), axis=0, keepdims=True)
        part = jnp.broadcast_to(part, (8, LANES))

        @pl.when(i == 0)
        def _():
            l_ref[...] = part

        @pl.when(i > 0)
        def _():
            l_ref[...] += part

    row = pl.BlockSpec((tr, D), lambda i: (i, 0))
    return pl.pallas_call(
        body, out_shape=(jax.ShapeDtypeStruct((R, D), F32), jax.ShapeDtypeStruct((R, D), BF16),
                         jax.ShapeDtypeStruct((8, LANES), F32)), grid=(R // tr,),
        in_specs=[row, row], out_specs=(row, row, pl.BlockSpec((8, LANES), lambda i: (0, 0))),
        compiler_params=_cparams(("arbitrary",)), name=name,
    )(y, target)


def _gelu(x):
    return 0.5 * x * (1.0 + lax.erf(x * math.sqrt(0.5)))


def _gelu_grad(x):
    return 0.5 * (1.0 + lax.erf(x * math.sqrt(0.5))) + x * jnp.exp(-0.5 * x * x) * (1.0 / math.sqrt(2.0 * math.pi))


def _group_consts(width):
    lane = np.arange(width)
    col = np.arange(LANES)
    grp = (lane[:, None] // HEAD_DIM == col[None, :]).astype(np.float32)
    mod = ((lane[:, None] % HEAD_DIM == col[None, :]) & (col[None, :] < HEAD_DIM)).astype(np.float32)
    return jnp.asarray(grp, BF16), jnp.asarray(mod, BF16)


def _same_group():
    lane = np.arange(GROUP_TILE) // HEAD_DIM
    return jnp.asarray((lane[:, None] == lane[None, :]).astype(np.float32), BF16)


def _bucket_onehot():
    qi = np.arange(BLOCK)[:, None]
    kj = np.arange(2 * BLOCK)[None, :]
    n = np.maximum(qi + BLOCK - kj, 0)
    max_exact = NUM_BUCKETS // 2
    nf = np.maximum(n, 1).astype(np.float32)
    large = max_exact + (np.log(nf / np.float32(max_exact)) / np.float32(math.log(MAX_DISTANCE / max_exact))
                         * np.float32(NUM_BUCKETS - max_exact)).astype(np.int32)
    large = np.minimum(large, NUM_BUCKETS - 1)
    bucket = jnp.asarray(np.where(n < max_exact, n, large).reshape(-1).astype(np.int32))
    return (bucket[:, None] == jnp.arange(LANES, dtype=jnp.int32)[None, :]).astype(BF16)


class _MixerDims:
    def __init__(self, S, IN, SW, AW, KVW, H, NQ):
        self.S, self.IN, self.SW, self.AW, self.KVW, self.H, self.NQ = S, IN, SW, AW, KVW, H, NQ
        self.NKV = KVW // HEAD_DIM
        self.GROUP = NQ // self.NKV
        self.nb = S // BLOCK
        self.koff = 2 * SW + AW
        self.voff = self.koff + KVW
        assert SW % LANES == 0 and AW % LANES == 0 and KVW % LANES == 0 and self.GROUP % 2 == 0
        assert self.koff % (2 * KVW) == 0 and IN == self.voff + KVW and S % BLOCK == 0


def _mixer_block(dm, n, z, kvp, prm):
    SW, AW, KVW, H, NQ = dm.SW, dm.AW, dm.KVW, dm.H, dm.NQ
    lane = lax.broadcasted_iota(jnp.int32, (BLOCK, LANES), 1)
    lo = lane < HEAD_DIM
    row = lax.broadcasted_iota(jnp.int32, (BLOCK, BLOCK), 0)
    col = lax.broadcasted_iota(jnp.int32, (BLOCK, BLOCK), 1)
    tril = row >= col
    pblk = prm["pblk"][...]
    inv = 1.0 / HEAD_DIM

    def group_rsqrt(x):
        return lax.rsqrt(_group_sum(x * x, pblk) * inv + EPS)

    zu, zv = z[:, :SW], z[:, SW:2 * SW]
    u, v = _gelu(zu), _gelu(zv)
    rv = group_rsqrt(v)
    vn = v * rv * prm["sgu_g"][...]
    vnb = vn.astype(BF16)
    tmats, gate_blocks = [], []
    for p in range(H // 2):
        blk = slice(LANES * p, LANES * (p + 1))
        t0 = jnp.where(tril, prm["sgu_w"][2 * p], 0.0).astype(BF16)
        t1 = jnp.where(tril, prm["sgu_w"][2 * p + 1], 0.0).astype(BF16)
        tmats += [t0, t1]
        g0 = jnp.dot(t0, vnb[:, blk], preferred_element_type=F32)
        g1 = jnp.dot(t1, vnb[:, blk], preferred_element_type=F32)
        gate_blocks.append(jnp.where(lo, g0, g1) + prm["sgu_bias"][:, blk])
    gate = jnp.concatenate(gate_blocks, axis=1)
    outa = u * gate
    ra = lax.rsqrt(jnp.mean(outa * outa, axis=-1, keepdims=True) + EPS)

    q = z[:, 2 * SW:2 * SW + AW]
    kcat = jnp.concatenate([kvp[:, :KVW], z[:, dm.koff:dm.koff + KVW]], axis=0)
    vcat = jnp.concatenate([kvp[:, KVW:], z[:, dm.voff:dm.voff + KVW]], axis=0)
    rq = group_rsqrt(q)
    rk = group_rsqrt(kcat)
    qn = q * rq * prm["q_g"][...]
    kn = kcat * rk * prm["k_g"][...]
    knb, vcb = kn.astype(BF16), vcat.astype(BF16)
    qi = lax.broadcasted_iota(jnp.int32, (BLOCK, 2 * BLOCK), 0)
    kj = lax.broadcasted_iota(jnp.int32, (BLOCK, 2 * BLOCK), 1)
    valid = (kj > qi) & (kj <= qi + BLOCK) & ((n > 0) | (kj >= BLOCK))
    scale = 1.0 / math.sqrt(HEAD_DIM)
    heads = []
    out_blocks = []
    for hq in range(NQ):
        mb, e = hq // 2, hq % 2
        kv = hq // dm.GROUP
        kb, ek = kv // 2, kv % 2
        qblk = qn[:, LANES * mb:LANES * (mb + 1)]
        if e != ek:
            qblk = pltpu.roll(qblk, HEAD_DIM, 1)
        half = lo if ek == 0 else jnp.logical_not(lo)
        qm = jnp.where(half, qblk, 0.0).astype(BF16)
        kblk = knb[:, LANES * kb:LANES * (kb + 1)]
        vblk = vcb[:, LANES * kb:LANES * (kb + 1)]
        s = lax.dot_general(qm, kblk, (((1,), (1,)), ((), ())), preferred_element_type=F32) * scale + prm["bias"][hq]
        s = jnp.where(valid, s, NEG_INF)
        sink = prm["sinks"][hq]
        mx = jnp.maximum(jnp.max(s, axis=-1, keepdims=True), sink)
        ex = jnp.exp(s - mx)
        den = jnp.sum(ex, axis=-1, keepdims=True) + jnp.exp(sink - mx)
        inv_den = 1.0 / den
        pr = ex * inv_den
        psink = jnp.exp(sink - mx) * inv_den
        prb = pr.astype(BF16)
        r_h = jnp.dot(prb, vblk, preferred_element_type=F32)
        if e != ek:
            r_h = pltpu.roll(r_h, HEAD_DIM, 1)
        heads.append(dict(qm=qm, kblk=kblk, vblk=vblk, pr=pr, prb=prb, psink=psink, half=half, mb=mb, e=e, ek=ek, kb=kb))
        if e == 1:
            out_blocks.append(jnp.where(lo, prev_r, r_h))
        prev_r = r_h
    outb = jnp.concatenate(out_blocks, axis=1)
    rb = lax.rsqrt(jnp.mean(outb * outb, axis=-1, keepdims=True) + EPS)
    return dict(lo=lo, tril=tril, pblk=pblk, zu=zu, zv=zv, u=u, v=v, rv=rv, vnb=vnb, tmats=tmats, gate=gate,
                outa=outa, ra=ra, q=q, kcat=kcat, rq=rq, rk=rk, heads=heads, outb=outb, rb=rb, scale=scale)


_MIXER_PARAMS = ("sgu_g", "sgu_w", "sgu_bias", "norm_a", "q_g", "k_g", "norm_b", "sinks", "bias", "pblk")


def _mixer_param_specs(dm, idx):
    SW, AW, KVW = dm.SW, dm.AW, dm.KVW
    full = lambda shape: pl.BlockSpec(shape, lambda n: (0,) * len(shape))
    return [full((1, SW)), full((dm.H, BLOCK, BLOCK)), full((BLOCK, SW)), full((1, SW)), full((1, AW)), full((1, KVW)),
            full((1, AW)), pl.BlockSpec(memory_space=pltpu.SMEM), full((dm.NQ, BLOCK, 2 * BLOCK)),
            full((GROUP_TILE, GROUP_TILE))]


def _mixer_fwd(dm, z, params, *, name, comm=()):
    nb = dm.nb

    def body(z_ref, kvp_ref, *rest):
        prm = dict(zip(_MIXER_PARAMS, rest[:len(_MIXER_PARAMS)]))
        o_ref = rest[len(_MIXER_PARAMS)]
        n = pl.program_id(0)
        c = _mixer_block(dm, n, z_ref[...], kvp_ref[...], prm)
        o_ref[:, :dm.SW] = (c["outa"] * c["ra"] * prm["norm_a"][...]).astype(o_ref.dtype)
        o_ref[:, dm.SW:] = (c["outb"] * c["rb"] * prm["norm_b"][...]).astype(o_ref.dtype)

    kvblk = dm.koff // (2 * dm.KVW)
    in_specs = [pl.BlockSpec((BLOCK, dm.IN), lambda n: (n, 0)),
                pl.BlockSpec((BLOCK, 2 * dm.KVW), lambda n: (jnp.maximum(n - 1, 0), kvblk))] + _mixer_param_specs(dm, None)
    return _call(
        body, out_shape=[jax.ShapeDtypeStruct((dm.S, dm.SW + dm.AW), BF16)], grid=(nb,),
        in_specs=in_specs, out_specs=[pl.BlockSpec((BLOCK, dm.SW + dm.AW), lambda n: (n, 0))],
        sem=("arbitrary",), args=[z, z, *params], name=name, comm=comm)


def _mixer_bwd(dm, z, dmixed, dbias_in, params, gmats, *, name, comm=()):
    SW, AW, KVW, H, NQ, nb, IN = dm.SW, dm.AW, dm.KVW, dm.H, dm.NQ, dm.nb, dm.IN
    QW = 2 * SW + AW
    NP = len(_MIXER_PARAMS)

    def body(z_ref, kvp_ref, dm_ref, dbin_ref, *rest):
        prm = dict(zip(_MIXER_PARAMS, rest[:NP]))
        gs_ref, gmq_ref, gmk_ref = rest[NP:NP + 3]
        (dz_ref, dsg_ref, dt_ref, dsb_ref, dna_ref, dqg_ref, dkg_ref, dnb_ref, dsk_ref, dbias_ref) = rest[NP + 3:NP + 13]
        hold, tmpkv, newkv, carry, accb, accq, acck = rest[NP + 13:]
        n = pl.program_id(0)

        @pl.when(n == 0)
        def _():
            for r in (dsg_ref, dt_ref, dna_ref, dnb_ref, dsk_ref, accb, accq, acck):
                r[...] = jnp.zeros(r.shape, r.dtype)
            dbias_ref[...] = dbin_ref[...]

        @pl.when(n < nb)
        def _():
            c = _mixer_block(dm, n, z_ref[...], kvp_ref[...], prm)
            lo = c["lo"]
            dmx = dm_ref[...]
            inv = 1.0 / HEAD_DIM

            def rms_bwd_full(dy_scaled, y, r):
                return r * (dy_scaled - y * jnp.mean(dy_scaled * y, axis=-1, keepdims=True))

            def group_mean_b(x):
                return _group_sum(x, c["pblk"]) * inv

            dma = dmx[:, :SW]
            ya = c["outa"] * c["ra"]
            dna_ref[...] += jnp.sum(dma * ya, axis=0, keepdims=True)
            douta = rms_bwd_full(dma * prm["norm_a"][...], ya, c["ra"])
            du = douta * c["gate"]
            dgate = douta * c["u"]
            accb[...] += dgate
            dgb16 = dgate.astype(BF16)
            dvn_blocks = []
            for p in range(H // 2):
                blk = slice(LANES * p, LANES * (p + 1))
                dg = dgate[:, blk]
                d0 = jnp.where(lo, dg, 0.0).astype(BF16)
                d1 = jnp.where(lo, 0.0, dg).astype(BF16)
                vb = c["vnb"][:, blk]
                nt = lambda a, b: lax.dot_general(a, b, (((1,), (1,)), ((), ())), preferred_element_type=F32)
                tn = lambda a, b: lax.dot_general(a, b, (((0,), (0,)), ((), ())), preferred_element_type=F32)
                dt_ref[2 * p] += nt(d0, vb)
                dt_ref[2 * p + 1] += nt(d1, vb)
                dvn_blocks.append(jnp.where(lo, tn(c["tmats"][2 * p], dgb16[:, blk]), tn(c["tmats"][2 * p + 1], dgb16[:, blk])))
            dvn = jnp.concatenate(dvn_blocks, axis=1)
            yv = c["v"] * c["rv"]
            dsg_ref[...] += jnp.sum(dvn * yv, axis=0, keepdims=True)
            dyv = dvn * prm["sgu_g"][...]
            dv = c["rv"] * (dyv - yv * group_mean_b(dyv * yv))
            dzu = du * _gelu_grad(c["zu"])
            dzv = dv * _gelu_grad(c["zv"])

            dmb = dmx[:, SW:]
            yb = c["outb"] * c["rb"]
            dnb_ref[...] += jnp.sum(dmb * yb, axis=0, keepdims=True)
            doutb = rms_bwd_full(dmb * prm["norm_b"][...], yb, c["rb"])
            lane1 = lax.broadcasted_iota(jnp.int32, (1, LANES), 1)
            dqn_blocks = [None] * (AW // LANES)
            dkn_blocks = [None] * (KVW // LANES)
            dvc_blocks = [None] * (KVW // LANES)
            dsink_vec = jnp.zeros((1, LANES), F32)
            add = lambda old, new: new if old is None else old + new
            for hq, hd in enumerate(c["heads"]):
                mb, e, ek, kb, half = hd["mb"], hd["e"], hd["ek"], hd["kb"], hd["half"]
                dr = doutb[:, LANES * mb:LANES * (mb + 1)]
                if e != ek:
                    dr = pltpu.roll(dr, HEAD_DIM, 1)
                drm = jnp.where(half, dr, 0.0).astype(BF16)
                dp = lax.dot_general(drm, hd["vblk"], (((1,), (1,)), ((), ())), preferred_element_type=F32)
                dvc_blocks[kb] = add(dvc_blocks[kb], lax.dot_general(hd["prb"], drm, (((0,), (0,)), ((), ())),
                                                                     preferred_element_type=F32))
                rowdot = jnp.sum(hd["pr"] * dp, axis=-1, keepdims=True)
                ds = hd["pr"] * (dp - rowdot)
                dsink = jnp.sum(-hd["psink"] * rowdot, axis=0, keepdims=True)
                dsink_vec = dsink_vec + jnp.where(lane1 == hq, dsink, 0.0)
                dbias_ref[hq] += ds
                dsb = (ds * c["scale"]).astype(BF16)
                dqm = jnp.dot(dsb, hd["kblk"], preferred_element_type=F32)
                dqm = jnp.where(half, dqm, 0.0)
                if e != ek:
                    dqm = pltpu.roll(dqm, HEAD_DIM, 1)
                dqn_blocks[mb] = add(dqn_blocks[mb], dqm)
                dkn_blocks[kb] = add(dkn_blocks[kb], lax.dot_general(dsb, hd["qm"], (((0,), (0,)), ((), ())),
                                                                     preferred_element_type=F32))
            dsk_ref[...] += dsink_vec
            dqn = jnp.concatenate(dqn_blocks, axis=1)
            dkn = jnp.concatenate(dkn_blocks, axis=1)
            dvc = jnp.concatenate(dvc_blocks, axis=1)
            yq = c["q"] * c["rq"]
            accq[...] += jnp.sum(dqn * yq, axis=0, keepdims=True)
            dyq = dqn * prm["q_g"][...]
            dq = c["rq"] * (dyq - yq * group_mean_b(dyq * yq))
            yk = c["kcat"] * c["rk"]
            acck[...] += jnp.sum(dkn * yk, axis=0, keepdims=True)
            dyk = dkn * prm["k_g"][...]
            dk = c["rk"] * (dyk - yk * group_mean_b(dyk * yk))

            slot = n % 2
            hold[slot, :, :SW] = dzu
            hold[slot, :, SW:2 * SW] = dzv
            hold[slot, :, 2 * SW:] = dq
            tmpkv[:, :KVW] = dk[:BLOCK]
            tmpkv[:, KVW:] = dvc[:BLOCK]
            newkv[:, :KVW] = dk[BLOCK:]
            newkv[:, KVW:] = dvc[BLOCK:]

        @pl.when(n >= 1)
        def _():
            dz_ref[:, :QW] = hold[(n - 1) % 2].astype(dz_ref.dtype)

        @pl.when((n >= 1) & (n < nb))
        def _():
            dz_ref[:, QW:] = (carry[...] + tmpkv[...]).astype(dz_ref.dtype)

        @pl.when(n == nb)
        def _():
            dz_ref[:, QW:] = carry[...].astype(dz_ref.dtype)
            row = lax.broadcasted_iota(jnp.int32, (BLOCK, BLOCK), 0)
            col = lax.broadcasted_iota(jnp.int32, (BLOCK, BLOCK), 1)
            for h in range(H):
                dt_ref[h] = jnp.where(row >= col, dt_ref[h], 0.0)
            dsb_ref[...] = _dot3(accb[...], gs_ref[...])
            dqg_ref[...] = _dot3(accq[...], gmq_ref[...])
            dkg_ref[...] = _dot3(acck[...], gmk_ref[...])

        @pl.when(n < nb)
        def _():
            carry[...] = newkv[...]

    kvblk = dm.koff // (2 * KVW)
    clamp = lambda n: jnp.minimum(n, nb - 1)
    full = lambda shape: pl.BlockSpec(shape, lambda n: (0,) * len(shape))
    in_specs = [pl.BlockSpec((BLOCK, IN), lambda n: (clamp(n), 0)),
                pl.BlockSpec((BLOCK, 2 * KVW), lambda n: (jnp.maximum(clamp(n) - 1, 0), kvblk)),
                pl.BlockSpec((BLOCK, SW + AW), lambda n: (clamp(n), 0)),
                full((NQ, BLOCK, 2 * BLOCK))] + _mixer_param_specs(dm, None) + [full((SW, LANES)), full((AW, LANES)),
                                                                                full((KVW, LANES))]
    out_shape = (jax.ShapeDtypeStruct((dm.S, IN), BF16),
                 jax.ShapeDtypeStruct((1, SW), F32), jax.ShapeDtypeStruct((H, BLOCK, BLOCK), F32),
                 jax.ShapeDtypeStruct((BLOCK, LANES), F32), jax.ShapeDtypeStruct((1, SW), F32),
                 jax.ShapeDtypeStruct((1, LANES), F32), jax.ShapeDtypeStruct((1, LANES), F32),
                 jax.ShapeDtypeStruct((1, AW), F32), jax.ShapeDtypeStruct((1, LANES), F32),
                 jax.ShapeDtypeStruct((NQ, BLOCK, 2 * BLOCK), F32))
    out_specs = (pl.BlockSpec((BLOCK, IN), lambda n: (jnp.maximum(n - 1, 0), 0)),
                 full((1, SW)), full((H, BLOCK, BLOCK)), full((BLOCK, LANES)), full((1, SW)), full((1, LANES)),
                 full((1, LANES)), full((1, AW)), full((1, LANES)), full((NQ, BLOCK, 2 * BLOCK)))
    scratch = [pltpu.VMEM((2, BLOCK, QW), F32), pltpu.VMEM((BLOCK, 2 * KVW), F32), pltpu.VMEM((BLOCK, 2 * KVW), F32),
               pltpu.VMEM((BLOCK, 2 * KVW), F32), pltpu.VMEM((BLOCK, SW), F32), pltpu.VMEM((1, AW), F32),
               pltpu.VMEM((1, KVW), F32)]
    return _call(
        body, out_shape=out_shape, grid=(nb + 1,), in_specs=in_specs, out_specs=out_specs, scratch=scratch,
        sem=("arbitrary",), args=[z, z, dmixed, dbias_in, *params, *gmats], name=name, comm=comm)


def _adamw(w, gparts, m, v, *, name, layer=0, stacked=None, tr=256):
    R, C = gparts[0].shape
    tr = _tile(R, max(8, min(tr, (1 << 18) // C)), 8)
    ng = len(gparts)
    bc1 = 1.0 - ADAM_B1 ** ADAM_STEP
    bc2 = 1.0 - ADAM_B2 ** ADAM_STEP

    def body(w_ref, *rest):
        g_refs, (m_ref, v_ref), (go_ref, d_ref, mo_ref, vo_ref) = rest[:ng], rest[ng:ng + 2], rest[-4:]
        g = g_refs[0][...].astype(F32)
        for r in g_refs[1:]:
            g = g + r[...].astype(F32)
        mn = ADAM_B1 * m_ref[...] + (1.0 - ADAM_B1) * g
        vn = ADAM_B2 * v_ref[...] + (1.0 - ADAM_B2) * jnp.square(g)
        m_hat = mn / bc1
        v_hat = vn / bc2
        go_ref[...] = g
        d_ref[...] = -ADAM_LR * (m_hat / (jnp.sqrt(v_hat) + ADAM_EPS) + ADAM_WD * w_ref[...])
        mo_ref[...] = mn
        vo_ref[...] = vn

    here = pl.BlockSpec((tr, C), lambda i: (layer * (R // tr) + i, 0))
    blk = pl.BlockSpec((tr, C), lambda i: (i, 0))
    prev = [] if stacked is None else list(stacked)
    return _call(body, out_shape=[jax.ShapeDtypeStruct(w.shape, F32)] * 4, grid=(R // tr,),
                 in_specs=[here] + [blk] * ng + [here, here] + [ANY] * len(prev), out_specs=[here] * 4,
                 sem=("parallel",), args=[w, *gparts, m, v, *prev], name=name,
                 aliases={3 + ng + q: q for q in range(len(prev))})


def _sum_chip_partials(g, axis, base, width, stride, got, chip, *, name, tr=256, comm=()):
    _, R, C = got.shape
    tr = _tile(R, max(8, min(tr, (1 << 18) // C)), 8)
    assert base % width == 0 and stride % width == 0 and (C == width if axis == 1 else R == width)

    def body(chip_ref, own_ref, g0_ref, g1_ref, g2_ref, o_ref):
        o_ref[...] = ((own_ref[...].astype(F32) + g0_ref[...].astype(F32)) + g1_ref[...].astype(F32)) + g2_ref[...].astype(F32)

    band = lambda j: base // width + j[0] * (stride // width)
    if axis == 1:
        own_spec = pl.BlockSpec((tr, C), lambda i, j: (i, band(j)))
    else:
        own_spec = pl.BlockSpec((tr, C), lambda i, j: (band(j) * (R // tr) + i, 0))
    part = lambda k: pl.BlockSpec((None, tr, C), lambda i, j: (k, i, 0))
    return _call(body, out_shape=[jax.ShapeDtypeStruct((R, C), F32)], grid=(R // tr,),
                 in_specs=[own_spec, part(0), part(1), part(2)], out_specs=[pl.BlockSpec((tr, C), lambda i, j: (i, 0))],
                 sem=("parallel",), args=[g, got, got, got], name=name, prefetch=chip, comm=comm)


def _pack(arrays):
    parts, total = [], 0
    for a in arrays:
        flat = a.reshape(-1).astype(F32)
        pad = (-flat.shape[0]) % (8 * LANES)
        parts.append(jnp.pad(flat, (0, pad)))
        total += flat.shape[0] + pad
    parts.append(jnp.zeros(((-total) % (PACK_ROWS * LANES),), F32))
    return jnp.concatenate(parts).reshape(-1, LANES)


def _unpack(packed, like):
    flat = packed.reshape(-1)
    out, off = [], 0
    for a in like:
        n = int(np.prod(a.shape))
        out.append(flat[off:off + n].reshape(a.shape))
        off += n + ((-n) % (8 * LANES))
    return out


def kernel(x, rel_bias, norm1_g, w_in, sgu_norm_g, sgu_w, sgu_b, q_norm_g, k_norm_g, sinks, out_norm_a, out_norm_b, w_out, norm2_g, w_gate, w_up, w_down, loss_target, m_rel_bias, m_norm1_g, m_w_in, m_sgu_norm_g, m_sgu_w, m_sgu_b, m_q_norm_g, m_k_norm_g, m_sinks, m_out_norm_a, m_out_norm_b, m_w_out, m_norm2_g, m_w_gate, m_w_up, m_w_down, v_rel_bias, v_norm1_g, v_w_in, v_sgu_norm_g, v_sgu_w, v_sgu_b, v_q_norm_g, v_k_norm_g, v_sinks, v_out_norm_a, v_out_norm_b, v_w_out, v_norm2_g, v_w_gate, v_w_up, v_w_down):
    L, D, n_in = w_in.shape
    S = x.shape[1]
    IN = N_CHIPS * n_in
    n_ff = w_gate.shape[2]
    FF = N_CHIPS * n_ff
    H = sgu_w.shape[1]
    NQ = sinks.shape[1]
    SW, AW = H * HEAD_DIM, NQ * HEAD_DIM
    KVW = (IN - 2 * SW - AW) // 2
    dm = _MixerDims(S, IN, SW, AW, KVW, H, NQ)
    assert sgu_w.shape[2] == BLOCK and q_norm_g.shape[1] == HEAD_DIM and SW + AW == D

    wb = {k: w.astype(BF16) for k, w in (("in", w_in), ("out", w_out), ("gate", w_gate), ("up", w_up), ("down", w_down))}

    def gather(l, which):
        if which == "in":
            return _Gather([wb["in"]], [(0, 0, 1, 0, n_in)], [(D, IN)], l)
        if which == "out":
            return _Gather([wb["out"]], [(0, 0, 0, 0, D // N_CHIPS)], [(D, D)], l)
        if which == "gu":
            return _Gather([wb["gate"], wb["up"]], [(0, 0, 1, 0, 2 * n_ff), (1, 0, 1, n_ff, 2 * n_ff)], [(D, 2 * FF)], l)
        return _Gather([wb["down"]], [(0, 0, 0, 0, n_ff)], [(FF, D)], l)

    nxt = lambda l, *which: [gather(l + 1, w) for w in which] if l + 1 < L else []
    W = {}
    (W[0, "in"],) = _comm_only([gather(0, "in")], name="gather_first_weights")

    gs, gmod_q = _group_consts(SW)
    _, gmod_k = _group_consts(KVW)
    pblk = _same_group()
    onehot = _bucket_onehot()
    rbt = jnp.pad(rel_bias.T, ((0, 0), (0, LANES - NUM_BUCKETS)))
    (bias,) = _matmul(rbt, onehot.T, mode="nn", out_dtype=F32, exact=True, tn=4096, name="bias_table")
    bias = bias.reshape(NQ, BLOCK, 2 * BLOCK)

    def mixer_params(l):
        return [sgu_norm_g[l].reshape(1, SW), sgu_w[l], jnp.repeat(sgu_b[l].T, HEAD_DIM, axis=1),
                out_norm_a[l].reshape(1, SW), jnp.tile(q_norm_g[l], NQ).reshape(1, AW),
                jnp.tile(k_norm_g[l], dm.NKV).reshape(1, KVW), out_norm_b[l].reshape(1, AW), sinks[l], bias, pblk]

    xs = x.reshape(S, D)
    saved = []
    for l in range(L):
        h = _rms_fwd(xs, norm1_g[l].reshape(1, D), name="norm1_fwd")
        z, *got = _matmul(h, W[l, "in"], mode="nn", out_dtype=F32, tn=1792, name="in_proj",
                          comm=[gather(0, "out"), gather(0, "down")] if l == 0 else [])
        if l == 0:
            W[0, "out"], W[0, "down"] = got
        mixed, *got = _mixer_fwd(dm, z, mixer_params(l), name="mixer_fwd",
                                 comm=[gather(0, "gu")] if l == 0 else nxt(l, "in", "out"))
        if l == 0:
            (W[0, "gu"],) = got
        elif got:
            W[l + 1, "in"], W[l + 1, "out"] = got
        (x1,) = _matmul(mixed, W[l, "out"], mode="nn", out_dtype=F32, res=xs, name="out_proj")
        h2 = _rms_fwd(x1, norm2_g[l].reshape(1, D), name="norm2_fwd")
        ab, f, *got = _gate_up_swiglu(h2, W[l, "gu"], n_ff, name="gate_up_swiglu",
                                      comm=nxt(l, "gu") + (nxt(l, "in") if l == 0 else []))
        if got:
            W[l + 1, "gu"] = got.pop(0)
        if got:
            (W[l + 1, "in"],) = got
        x2, *got = _matmul(f, W[l, "down"], mode="nn", out_dtype=F32, res=x1, tm=512, tk=5632, name="down_proj",
                           comm=nxt(l, "down") + (nxt(l, "out") if l == 0 else []))
        if got:
            W[l + 1, "down"] = got.pop(0)
        if got:
            (W[l + 1, "out"],) = got
        saved.append((xs, h, z, mixed, x1, h2, ab, f))
        xs = x2

    dx, dxb, loss_part = _loss_head(xs, loss_target.reshape(S, D), name="loss_head")
    loss = lax.psum(loss_part[0, 0], ("x", "y", "c"))

    dbias = jnp.zeros((NQ, BLOCK, 2 * BLOCK), F32)
    n_out = D // N_CHIPS
    big_names = ("w_in", "w_out", "w_gate", "w_up", "w_down")
    own = {k: [None] * L for k in big_names}
    got = {k: [None] * L for k in big_names}
    names_layer = ("norm1_g", "sgu_norm_g", "sgu_w", "sgu_b", "q_norm_g", "k_norm_g", "sinks", "out_norm_a", "out_norm_b",
                   "norm2_g")
    small = {k: [None] * L for k in names_layer}
    small_unit = lambda layers: _AllGather(_pack([jnp.stack([small[k][i] for i in layers]) for k in names_layer]))
    for l in reversed(range(L)):
        xl, h, z, mixed, x1, h2, ab, f = saved[l]
        early = [small_unit(range(1, L))] if (l == 0 and L > 1) else []
        dab, *gathered_early = _down_dx_swiglu_bwd(dxb, W[l, "down"], ab, n_ff, name="down_dx_swiglu_bwd", comm=early)
        (g_d,) = _matmul_tn(f, dxb, out_dtype=BF16, tm=1408, tn=1024, tt=2048, name="down_proj_dw")
        g_gu, got["w_down"][l] = _matmul_tn(h2, dab, out_dtype=BF16, tt=2048, name="gate_up_proj_dw",
                                            comm=[_Scatter([g_d], [(0, 0, 0, n_ff, n_ff)])])
        dh2, got["w_gate"][l] = _matmul(dab, W[l, "gu"], mode="nt", out_dtype=F32, tk=2816, name="gate_up_proj_dx",
                                        comm=[_Scatter([g_gu], [(0, 1, 0, n_ff, 2 * n_ff)])])
        dx1, dx1b, dg2 = _rms_bwd(x1, norm2_g[l].reshape(1, D), dh2, dx, name="norm2_bwd")
        (dmixed,) = _matmul(dx1b, W[l, "out"], mode="nt", out_dtype=F32, name="out_proj_dx")
        (g_out,) = _matmul_tn(mixed, dx1b, out_dtype=BF16, tt=2048, name="out_proj_dw")
        (dz, d_sg, d_t, d_sb, d_na, d_qg, d_kg, d_nb, d_sk, dbias, got["w_up"][l], got["w_out"][l]) = _mixer_bwd(
            dm, z, dmixed, dbias, mixer_params(l), (gs, gmod_q, gmod_k), name="mixer_bwd",
            comm=[_Scatter([g_gu, g_out], [(0, 1, n_ff, n_ff, 2 * n_ff), (1, 0, 0, n_out, n_out)])])
        (g_in,) = _matmul_tn(h, dz, out_dtype=BF16, tn=896, tt=2048, name="in_proj_dw")
        dh, got["w_in"][l] = _matmul(dz, W[l, "in"], mode="nt", out_dtype=F32, tk=3584, name="in_proj_dx",
                                     comm=[_Scatter([g_in], [(0, 1, 0, n_in, n_in)])])
        dx, dxb, dg1 = _rms_bwd(xl, norm1_g[l].reshape(1, D), dh, dx1, name="norm1_bwd")
        own["w_in"][l], own["w_out"][l] = (g_in, 1, 0, n_in, n_in), (g_out, 0, 0, n_out, n_out)
        own["w_gate"][l], own["w_up"][l] = (g_gu, 1, 0, n_ff, 2 * n_ff), (g_gu, 1, n_ff, n_ff, 2 * n_ff)
        own["w_down"][l] = (g_d, 0, 0, n_ff, n_ff)
        small["norm1_g"][l] = dg1.reshape(D)
        small["norm2_g"][l] = dg2.reshape(D)
        small["sgu_norm_g"][l] = d_sg.reshape(H, HEAD_DIM)
        small["sgu_w"][l] = d_t
        small["sgu_b"][l] = d_sb[:, :H].T
        small["q_norm_g"][l] = d_qg[0, :HEAD_DIM]
        small["k_norm_g"][l] = d_kg[0, :HEAD_DIM]
        small["sinks"][l] = d_sk[0, :NQ]
        small["out_norm_a"][l] = d_na.reshape(SW)
        small["out_norm_b"][l] = d_nb.reshape(AW)
    grad_x = dx.reshape(1, S, D)
    (d_rb,) = _matmul(dbias.reshape(NQ, BLOCK * 2 * BLOCK), onehot, mode="nn", out_dtype=F32, exact=True, tk=4096,
                      name="rel_bias_grad")
    d_rel_bias = d_rb[:, :NUM_BUCKETS].T

    chip = (2 * lax.axis_index("x") + lax.axis_index("y")).astype(jnp.int32).reshape(1)
    jobs = [(nm, l) for nm in big_names for l in range(L)]
    late = _AllGather(_pack([small[k][0][None] for k in names_layer] + [d_rel_bias]))
    mine, sib = {}, {}
    for i, (nm, l) in enumerate(jobs):
        g, axis, base, width, stride = own[nm][l]
        units = ([late] if i == 0 else []) + ([_Swap([mine[jobs[i - 1]]])] if i > 0 else [])
        mine[nm, l], *rest = _sum_chip_partials(g, axis, base, width, stride, got[nm][l], chip, name="sum_chip_partials",
                                                comm=units)
        if i == 0:
            gathered_late = rest.pop(0)
        if i > 0:
            (sib[jobs[i - 1]],) = rest
    (sib[jobs[-1]],) = _comm_only([_Swap([mine[jobs[-1]]])], name="swap_last_sums")
    wmv = dict(w_in=(w_in, m_w_in, v_w_in), w_out=(w_out, m_w_out, v_w_out), w_gate=(w_gate, m_w_gate, v_w_gate),
               w_up=(w_up, m_w_up, v_w_up), w_down=(w_down, m_w_down, v_w_down))
    big = {}
    two = lambda t: t.reshape(-1, t.shape[-1])
    for nm in big_names:
        w, m, v = wmv[nm]
        res = None
        for l in range(L):
            res = _adamw(two(w), [mine[nm, l], sib[nm, l]], two(m), two(v), layer=l, stacked=res, name="adamw_" + nm)
        big[nm] = [r.reshape(w.shape) for r in res]

    w_small = dict(rel_bias=rel_bias, norm1_g=norm1_g, sgu_norm_g=sgu_norm_g, sgu_w=sgu_w, sgu_b=sgu_b, q_norm_g=q_norm_g,
                   k_norm_g=k_norm_g, sinks=sinks, out_norm_a=out_norm_a, out_norm_b=out_norm_b, norm2_g=norm2_g)
    m_small = dict(rel_bias=m_rel_bias, norm1_g=m_norm1_g, sgu_norm_g=m_sgu_norm_g, sgu_w=m_sgu_w, sgu_b=m_sgu_b,
                   q_norm_g=m_q_norm_g, k_norm_g=m_k_norm_g, sinks=m_sinks, out_norm_a=m_out_norm_a,
                   out_norm_b=m_out_norm_b, norm2_g=m_norm2_g)
    v_small = dict(rel_bias=v_rel_bias, norm1_g=v_norm1_g, sgu_norm_g=v_sgu_norm_g, sgu_w=v_sgu_w, sgu_b=v_sgu_b,
                   q_norm_g=v_q_norm_g, k_norm_g=v_k_norm_g, sinks=v_sinks, out_norm_a=v_out_norm_a,
                   out_norm_b=v_out_norm_b, norm2_g=v_norm2_g)

    def adamw_small(gathered, pick, name):
        like = pick(w_small)
        res = _adamw(_pack(like), [gathered[i] for i in range(N_DEV)], _pack(pick(m_small)), _pack(pick(v_small)), name=name)
        return [_unpack(r, like) for r in res]

    first_layer = adamw_small(gathered_late, lambda d: [d[k][:1] for k in names_layer] + [d["rel_bias"]], "adamw_small_late")
    if L > 1:
        others = adamw_small(gathered_early[0], lambda d: [d[k][1:] for k in names_layer], "adamw_small_early")
    sm = {"rel_bias": [r[-1] for r in first_layer]}
    for j, k in enumerate(names_layer):
        sm[k] = [jnp.concatenate([first_layer[q][j]] + ([others[q][j]] if L > 1 else [])) for q in range(4)]

    order = ("rel_bias", "norm1_g", "w_in", "sgu_norm_g", "sgu_w", "sgu_b", "q_norm_g", "k_norm_g", "sinks", "out_norm_a",
             "out_norm_b", "w_out", "norm2_g", "w_gate", "w_up", "w_down")
    pick = lambda k, i: big[k][i] if k in big else sm[k][i]
    outs = [loss, grad_x]
    for i in range(4):
        outs += [pick(k, i) for k in order]
    return tuple(outs)
```

```python
import functools
import math

import numpy as np

import jax
import jax.numpy as jnp
from jax import lax
from jax.experimental import pallas as pl
from jax.experimental.pallas import tpu as pltpu

F32 = jnp.float32
BF16 = jnp.bfloat16
MESH = pl.DeviceIdType.MESH
ANY = pl.BlockSpec(memory_space=pl.ANY)

HEAD_DIM = 64
BLOCK = 128
NUM_BUCKETS = 32
MAX_DISTANCE = 128
EPS = 1e-6
NEG_INF = -1e30
ADAM_LR, ADAM_B1, ADAM_B2, ADAM_EPS, ADAM_WD, ADAM_STEP = 0.001, 0.9, 0.999, 1e-08, 0.01, 10

LANES = 128
VMEM_LIMIT = 56 * 1024 * 1024
N_CHIPS = 4
N_DEV = 8
PACK_ROWS = 256
ROW_CHUNK = 256


def _cparams(sem=None):
    return pltpu.CompilerParams(dimension_semantics=sem, vmem_limit_bytes=VMEM_LIMIT)


def _tile(dim, target, align=LANES):
    best = None
    for t in range(align, min(dim, target) + 1, align):
        if dim % t == 0:
            best = t
    return best if best is not None else dim


def _mesh_pos():
    return lax.axis_index("x"), lax.axis_index("y"), lax.axis_index("c")


def _other_chips(x, y):
    return [(1 - x, y), (x, 1 - y), (1 - x, 1 - y)]


def _slab(ref, axis, start, size, half=None):
    if axis == 1:
        rows = ref.shape[0]
        r = pl.ds(0, rows) if half is None else pl.ds(pl.multiple_of(half * (rows // 2), 16), rows // 2)
        return ref.at[r, pl.ds(pl.multiple_of(start, LANES), size)]
    if half is None:
        return ref.at[pl.ds(pl.multiple_of(start, 16), size), :]
    return ref.at[pl.ds(pl.multiple_of(start + half * (size // 2), 16), size // 2), :]


def _remote(src, dst, send_sem, recv_sem, to):
    return pltpu.make_async_remote_copy(src_ref=src, dst_ref=dst, send_sem=send_sem, recv_sem=recv_sem,
                                        device_id=to, device_id_type=MESH)


class _Gather:
    def __init__(self, shards, streams, out_shapes, layer):
        self.ins, self.streams, self.layer = list(shards), streams, layer
        self.outs = [jax.ShapeDtypeStruct(s, BF16) for s in out_shapes]
        ns = len(streams)
        self.sems = [pltpu.SemaphoreType.DMA((ns, 3))] * 4 + [pltpu.SemaphoreType.DMA((ns,))]

    def _sent(self, srcs, outs, sems):
        send, recv, _, _, local = sems
        x, y, c = _mesh_pos()
        me_j = 2 * x + y
        own, sends = [], []
        for s, (si, oi, axis, base, stride) in enumerate(self.streams):
            src, out = srcs[si].at[self.layer], outs[oi]
            width = src.shape[axis]
            own.append(pltpu.make_async_copy(src, _slab(out, axis, base + me_j * stride, width), local.at[s]))
            half_rows = src.shape[0] // 2
            mine = src.at[pl.ds(pl.multiple_of(c * half_rows, 16), half_rows), :]
            for k, (px, py) in enumerate(_other_chips(x, y)):
                sends.append(_remote(mine, _slab(out, axis, base + me_j * stride, width, half=c),
                                     send.at[s, k], recv.at[s, k], (px, py, c)))
        return own, sends

    def start(self, srcs, outs, sems):
        own, sends = self._sent(srcs, outs, sems)
        for cp in own + sends:
            cp.start()

    def finish(self, srcs, outs, sems):
        send, recv, fsend, frecv, _ = sems
        x, y, c = _mesh_pos()
        sib = (x, y, 1 - c)
        forwards, fwd_arrivals = [], []
        for s, (si, oi, axis, base, stride) in enumerate(self.streams):
            out = outs[oi]
            width = srcs[si].shape[1 + axis]
            for k, (px, py) in enumerate(_other_chips(x, y)):
                start = base + (2 * px + py) * stride
                got = _slab(out, axis, start, width, half=c)
                _remote(got, got, send.at[s, k], recv.at[s, k], (px, py, c)).wait_recv()
                fwd = _remote(got, got, fsend.at[s, k], frecv.at[s, k], sib)
                fwd.start()
                forwards.append(fwd)
                theirs = _slab(out, axis, start, width, half=1 - c)
                fwd_arrivals.append(_remote(theirs, theirs, fsend.at[s, k], frecv.at[s, k], sib))
        for a in fwd_arrivals:
            a.wait_recv()
        for cp in forwards:
            cp.wait_send()
        own, sends = self._sent(srcs, outs, sems)
        for cp in sends:
            cp.wait_send()
        for cp in own:
            cp.wait()


class _Scatter:
    def __init__(self, grads, streams):
        self.ins, self.streams = list(grads), streams
        shard = lambda g, axis, width: (width, g.shape[1]) if axis == 0 else (g.shape[0], width)
        self.outs = [jax.ShapeDtypeStruct((3,) + shard(grads[gi], axis, width), BF16) for gi, axis, _, width, _ in streams]
        self.sems = [pltpu.SemaphoreType.DMA((len(streams), 3))] * 2

    def _copies(self, srcs, outs, sems):
        send, recv = sems
        x, y, c = _mesh_pos()
        copies = []
        for s, (gi, axis, base, width, stride) in enumerate(self.streams):
            for k, (px, py) in enumerate(_other_chips(x, y)):
                copies.append(_remote(_slab(srcs[gi], axis, base + (2 * px + py) * stride, width), outs[s].at[k],
                                      send.at[s, k], recv.at[s, k], (px, py, c)))
        return copies

    def start(self, srcs, outs, sems):
        for cp in self._copies(srcs, outs, sems):
            cp.start()

    def finish(self, srcs, outs, sems):
        for cp in self._copies(srcs, outs, sems):
            cp.wait()


class _Swap:
    def __init__(self, arrays):
        self.ins = list(arrays)
        self.outs = [jax.ShapeDtypeStruct(a.shape, a.dtype) for a in arrays]
        self.sems = [pltpu.SemaphoreType.DMA((len(arrays),))] * 2

    def _copies(self, srcs, outs, sems):
        send, recv = sems
        x, y, c = _mesh_pos()
        return [_remote(srcs[s], outs[s], send.at[s], recv.at[s], (x, y, 1 - c)) for s in range(len(srcs))]

    def start(self, srcs, outs, sems):
        for cp in self._copies(srcs, outs, sems):
            cp.start()

    def finish(self, srcs, outs, sems):
        for cp in self._copies(srcs, outs, sems):
            cp.wait()


class _AllGather:
    def __init__(self, part):
        self.ins = [part]
        self.outs = [jax.ShapeDtypeStruct((N_DEV,) + part.shape, part.dtype)]
        self.sems = [pltpu.SemaphoreType.DMA((7,)), pltpu.SemaphoreType.DMA((7,)), pltpu.SemaphoreType.DMA(())]

    def _first(self, srcs, outs, sems):
        (x_ref,), (out,), (send, recv, local) = srcs, outs, sems
        x, y, c = _mesh_pos()
        mine = out.at[4 * x + 2 * y + c]
        own = pltpu.make_async_copy(x_ref, mine, local)
        sends = [_remote(x_ref, mine, send.at[0], recv.at[0], (x, y, 1 - c))]
        sends += [_remote(x_ref, mine, send.at[1 + k], recv.at[1 + k], (px, py, c)) for k, (px, py) in enumerate(_other_chips(x, y))]
        return own, sends

    def start(self, srcs, outs, sems):
        own, sends = self._first(srcs, outs, sems)
        own.start()
        for cp in sends:
            cp.start()

    def finish(self, srcs, outs, sems):
        (out,), (send, recv, local) = outs, sems
        x, y, c = _mesh_pos()
        me, sib = (x, y, c), (x, y, 1 - c)
        slot = lambda px, py, pc: out.at[4 * px + 2 * py + pc]
        passed = []
        for k, (px, py) in enumerate(_other_chips(x, y)):
            blk = slot(px, py, c)
            _remote(blk, blk, send.at[1 + k], recv.at[1 + k], me).wait_recv()
            fwd = _remote(blk, blk, send.at[4 + k], recv.at[4 + k], sib)
            fwd.start()
            passed.append(fwd)
        blk = slot(x, y, 1 - c)
        _remote(blk, blk, send.at[0], recv.at[0], me).wait_recv()
        for k, (px, py) in enumerate(_other_chips(x, y)):
            blk = slot(px, py, 1 - c)
            _remote(blk, blk, send.at[4 + k], recv.at[4 + k], me).wait_recv()
        own, sends = self._first(srcs, outs, sems)
        for cp in sends + passed:
            cp.wait_send()
        own.wait()


def _call(body, *, grid, in_specs, out_specs, out_shape, args, name, scratch=(), sem=None, comm=(), prefetch=None,
          aliases=None):
    out_shape, out_specs = tuple(out_shape), tuple(out_specs)
    n_in, n_out, n_scr = len(in_specs), len(out_shape), len(scratch)
    n_pre = 0 if prefetch is None else 1
    c_ins = [a for u in comm for a in u.ins]
    c_outs = [o for u in comm for o in u.outs]
    c_sems = [s for u in comm for s in u.sems]

    def wrapped(*refs):
        pre, refs = refs[:n_pre], refs[n_pre:]
        ins, rest = refs[:n_in], refs[n_in:]
        cin, rest = rest[:len(c_ins)], rest[len(c_ins):]
        outs, rest = rest[:n_out], rest[n_out:]
        cout, rest = rest[:len(c_outs)], rest[len(c_outs):]
        scr, csem = rest[:n_scr], rest[n_scr:]

        def each(fn_name):
            i = o = s = 0
            for u in comm:
                getattr(u, fn_name)(cin[i:i + len(u.ins)], cout[o:o + len(u.outs)], csem[s:s + len(u.sems)])
                i, o, s = i + len(u.ins), o + len(u.outs), s + len(u.sems)

        if comm:
            pids = [pl.program_id(d) for d in range(len(grid))]
            first = functools.reduce(jnp.logical_and, [p == 0 for p in pids])
            last = functools.reduce(jnp.logical_and, [p == g - 1 for p, g in zip(pids, grid)])
            pl.when(first)(lambda: each("start"))
        body(*pre, *ins, *outs, *scr)
        if comm:
            pl.when(last)(lambda: each("finish"))

    if comm:
        sem = ("arbitrary",) * len(grid)
    all_in = list(in_specs) + [ANY] * len(c_ins)
    all_out = out_specs + tuple([ANY] * len(c_outs))
    all_scr = list(scratch) + c_sems
    kw = {}
    if aliases:
        kw["input_output_aliases"] = {n_pre + i: o for i, o in aliases.items()}
    if prefetch is None:
        kw.update(grid=grid, in_specs=all_in, out_specs=all_out, scratch_shapes=all_scr)
        pre_args = []
    else:
        kw["grid_spec"] = pltpu.PrefetchScalarGridSpec(num_scalar_prefetch=1, grid=grid, in_specs=all_in, out_specs=all_out,
                                                       scratch_shapes=all_scr)
        pre_args = [prefetch]
    return pl.pallas_call(wrapped, out_shape=out_shape + tuple(c_outs), compiler_params=_cparams(sem), name=name, **kw)(
        *pre_args, *args, *c_ins)


def _comm_only(comm, *, name):
    def body(tick_ref):
        tick_ref[...] = jnp.zeros(tick_ref.shape, tick_ref.dtype)

    outs = _call(body, grid=(1,), in_specs=[], out_specs=[pl.BlockSpec((8, LANES), lambda i: (0, 0))],
                 out_shape=[jax.ShapeDtypeStruct((8, LANES), F32)], args=[], name=name, comm=comm)
    return outs[1:]


def _split3(x):
    hi = x.astype(BF16)
    r1 = x - hi.astype(F32)
    mid = r1.astype(BF16)
    lo = (r1 - mid.astype(F32)).astype(BF16)
    return hi, mid, lo


def _dot3(x, g):
    hi, mid, lo = _split3(x)
    d = lambda a: jnp.dot(a, g, preferred_element_type=F32)
    return d(hi) + d(mid) + d(lo)


GROUP_TILE = 256


def _group_sum(x, pblk):
    hi = x.astype(BF16)
    lo = (x - hi.astype(F32)).astype(BF16)
    cols = []
    for b in range(x.shape[1] // GROUP_TILE):
        sl = slice(GROUP_TILE * b, GROUP_TILE * (b + 1))
        cols.append(jnp.dot(hi[:, sl], pblk, preferred_element_type=F32) + jnp.dot(lo[:, sl], pblk, preferred_element_type=F32))
    return cols[0] if len(cols) == 1 else jnp.concatenate(cols, axis=1)


def _matmul(a, b, *, mode, out_dtype, name, res=None, exact=False, tm=1024, tn=1024, tk=2048, b_col0=0, comm=()):
    M, K = a.shape
    N = b.shape[1] if mode == "nn" else b.shape[0]
    tm, tn, tk = _tile(M, tm, 8 if M < LANES else LANES), _tile(N, tn), _tile(K, tk)
    nk = K // tk
    dn = (((1,), (0,)), ((), ())) if mode == "nn" else (((1,), (1,)), ((), ()))

    def body(*refs):
        a_ref, b_ref = refs[0], refs[1]
        r_ref = refs[2] if res is not None else None
        o_ref = refs[3] if res is not None else refs[2]
        if exact:
            part = _dot3(a_ref[...], b_ref[...])
        else:
            part = lax.dot_general(a_ref[...].astype(BF16), b_ref[...].astype(BF16), dn, preferred_element_type=F32)

        def finish(total):
            if r_ref is not None:
                total = r_ref[...] + total
            o_ref[...] = total.astype(o_ref.dtype)

        if nk == 1:
            finish(part)
        else:
            acc = refs[-1]
            k = pl.program_id(2)

            @pl.when(k == 0)
            def _():
                acc[...] = part

            @pl.when(k > 0)
            def _():
                acc[...] += part

            @pl.when(k == nk - 1)
            def _():
                finish(acc[...])

    if mode == "nn":
        b_spec = pl.BlockSpec((tk, tn), lambda j, i, k: (k, j))
    else:
        assert b_col0 % tk == 0
        b_spec = pl.BlockSpec((tn, tk), lambda j, i, k: (j, k + b_col0 // tk))
    in_specs = [pl.BlockSpec((tm, tk), lambda j, i, k: (i, k)), b_spec]
    args = [a, b]
    if res is not None:
        in_specs.append(pl.BlockSpec((tm, tn), lambda j, i, k: (i, j)))
        args.append(res)
    return _call(
        body, grid=(N // tn, M // tm, nk), in_specs=in_specs,
        out_specs=[pl.BlockSpec((tm, tn), lambda j, i, k: (i, j))], out_shape=[jax.ShapeDtypeStruct((M, N), out_dtype)],
        scratch=[pltpu.VMEM((tm, tn), F32)] if nk > 1 else [], sem=("parallel", "parallel", "arbitrary"),
        args=args, name=name, comm=comm)


def _matmul_tn(a, b, *, out_dtype, name, tm=2048, tn=1024, tt=1024, comm=()):
    T, Mo = a.shape
    N = b.shape[1]
    tm, tn, tt = _tile(Mo, tm), _tile(N, tn), _tile(T, tt)
    nt = T // tt

    def body(a_ref, b_ref, o_ref, *scr):
        t = pl.program_id(2)
        part = lax.dot_general(a_ref[...].astype(BF16), b_ref[...].astype(BF16), (((0,), (0,)), ((), ())),
                               preferred_element_type=F32)
        if nt == 1:
            o_ref[...] = part.astype(o_ref.dtype)
            return
        (acc,) = scr

        @pl.when(t == 0)
        def _():
            acc[...] = part

        @pl.when(t > 0)
        def _():
            acc[...] += part

        @pl.when(t == nt - 1)
        def _():
            o_ref[...] = acc[...].astype(o_ref.dtype)

    return _call(
        body, grid=(Mo // tm, N // tn, nt),
        in_specs=[pl.BlockSpec((tt, tm), lambda i, j, t: (t, i)), pl.BlockSpec((tt, tn), lambda i, j, t: (t, j))],
        out_specs=[pl.BlockSpec((tm, tn), lambda i, j, t: (i, j))], out_shape=[jax.ShapeDtypeStruct((Mo, N), out_dtype)],
        scratch=[pltpu.VMEM((tm, tn), F32)] if nt > 1 else [], sem=("parallel", "parallel", "arbitrary"), args=[a, b],
        name=name, comm=comm)


def _rms_fwd(x, g, *, name, tr=512):
    R, D = x.shape
    tr = _tile(R, tr, 8)

    def body(x_ref, g_ref, o_ref):
        xv = x_ref[...]
        r = lax.rsqrt(jnp.mean(xv * xv, axis=-1, keepdims=True) + EPS)
        o_ref[...] = (xv * r * g_ref[...]).astype(o_ref.dtype)

    return pl.pallas_call(
        body, out_shape=jax.ShapeDtypeStruct((R, D), BF16), grid=(R // tr,),
        in_specs=[pl.BlockSpec((tr, D), lambda i: (i, 0)), pl.BlockSpec((1, D), lambda i: (0, 0))],
        out_specs=pl.BlockSpec((tr, D), lambda i: (i, 0)),
        compiler_params=_cparams(("parallel",)), name=name,
    )(x, g)


def _rms_bwd(x, g, dh, dres, *, name, tr=512):
    R, D = x.shape
    tr = _tile(R, tr, 8)

    def body(x_ref, g_ref, dh_ref, dres_ref, dx_ref, dxb_ref, dg_ref):
        i = pl.program_id(0)
        xv = x_ref[...]
        r = lax.rsqrt(jnp.mean(xv * xv, axis=-1, keepdims=True) + EPS)
        y = xv * r
        dhv = dh_ref[...]
        dy = dhv * g_ref[...]
        dx = dres_ref[...] + r * (dy - y * jnp.mean(dy * y, axis=-1, keepdims=True))
        dx_ref[...] = dx
        dxb_ref[...] = dx.astype(BF16)
        dg = jnp.sum(dhv * y, axis=0, keepdims=True)

        @pl.when(i == 0)
        def _():
            dg_ref[...] = dg

        @pl.when(i > 0)
        def _():
            dg_ref[...] += dg

    row = pl.BlockSpec((tr, D), lambda i: (i, 0))
    vec = pl.BlockSpec((1, D), lambda i: (0, 0))
    return pl.pallas_call(
        body, out_shape=(jax.ShapeDtypeStruct((R, D), F32), jax.ShapeDtypeStruct((R, D), BF16),
                         jax.ShapeDtypeStruct((1, D), F32)), grid=(R // tr,),
        in_specs=[row, vec, row, row], out_specs=(row, row, vec),
        compiler_params=_cparams(("arbitrary",)), name=name,
    )(x, g, dh, dres)


def _sigmoid(x):
    return 0.5 * (jnp.tanh(0.5 * x) + 1.0)


def _gate_up_swiglu(h, w_gu, n_ff, *, name, tm=512, comm=()):
    M, K = h.shape
    nsh = w_gu.shape[1] // (2 * n_ff)
    tm = _tile(M, tm)

    def body(h_ref, w_ref, ab_ref, f_ref):
        for c in range(tm // _tile(tm, ROW_CHUNK)):
            rows = pl.ds(c * _tile(tm, ROW_CHUNK), _tile(tm, ROW_CHUNK))
            ab = jnp.dot(h_ref[rows, :], w_ref[...], preferred_element_type=F32)
            ab_ref[rows, :] = ab.astype(ab_ref.dtype)
            a, b = ab[:, :n_ff], ab[:, n_ff:]
            f_ref[rows, :] = (a * _sigmoid(a) * b).astype(f_ref.dtype)

    return _call(
        body, grid=(nsh, M // tm),
        in_specs=[pl.BlockSpec((tm, K), lambda j, i: (i, 0)), pl.BlockSpec((K, 2 * n_ff), lambda j, i: (0, j))],
        out_specs=[pl.BlockSpec((tm, 2 * n_ff), lambda j, i: (i, j)), pl.BlockSpec((tm, n_ff), lambda j, i: (i, j))],
        out_shape=[jax.ShapeDtypeStruct((M, 2 * nsh * n_ff), BF16), jax.ShapeDtypeStruct((M, nsh * n_ff), BF16)],
        sem=("parallel", "parallel"), args=[h, w_gu], name=name, comm=comm)


def _down_dx_swiglu_bwd(dy, w_d, ab, n_ff, *, name, tm=512, shards=2, comm=()):
    M, K = dy.shape
    nsh = w_d.shape[0] // n_ff
    tm = _tile(M, tm)
    shards = shards if nsh % shards == 0 else 1

    def body(dy_ref, w_ref, ab_ref, dab_ref):
        for c in range(tm // _tile(tm, ROW_CHUNK)):
            rows = pl.ds(c * _tile(tm, ROW_CHUNK), _tile(tm, ROW_CHUNK))
            df_all = lax.dot_general(dy_ref[rows, :], w_ref[...], (((1,), (1,)), ((), ())), preferred_element_type=F32)
            for s in range(shards):
                df = df_all[:, s * n_ff:(s + 1) * n_ff]
                a_cols = slice(2 * s * n_ff, (2 * s + 1) * n_ff)
                b_cols = slice((2 * s + 1) * n_ff, (2 * s + 2) * n_ff)
                av = ab_ref[rows, a_cols].astype(F32)
                sg = _sigmoid(av)
                dab_ref[rows, a_cols] = (df * ab_ref[rows, b_cols].astype(F32) * (sg * (1.0 + av * (1.0 - sg)))).astype(dab_ref.dtype)
                dab_ref[rows, b_cols] = (df * (av * sg)).astype(dab_ref.dtype)

    cols = pl.BlockSpec((tm, 2 * shards * n_ff), lambda j, i: (i, j))
    return _call(
        body, grid=(nsh // shards, M // tm),
        in_specs=[pl.BlockSpec((tm, K), lambda j, i: (i, 0)), pl.BlockSpec((shards * n_ff, K), lambda j, i: (j, 0)), cols],
        out_specs=[cols], out_shape=[jax.ShapeDtypeStruct(ab.shape, BF16)],
        sem=("parallel", "parallel"), args=[dy, w_d, ab], name=name, comm=comm)


def _loss_head(y, target, *, name, tr=256):
    R, D = y.shape
    tr = _tile(R, tr, 8)

    def body(y_ref, t_ref, dy_ref, dyb_ref, l_ref):
        i = pl.program_id(0)
        e = y_ref[...] - t_ref[...]
        dy = e * (1.0 / D)
        dy_ref[...] = dy
        dyb_ref[...] = dy.astype(BF16)
        part = 0.5 * jnp.sum(jnp.mean(e * e, axis=-1, keepdims=True), axis=0, keepdims=True)
        part = jnp.broadcast_to(part, (8, LANES))

        @pl.when(i == 0)
        def _():
            l_ref[...] = part

        @pl.when(i > 0)
        def _():
            l_ref[...] += part

    row = pl.BlockSpec((tr, D), lambda i: (i, 0))
    return pl.pallas_call(
        body, out_shape=(jax.ShapeDtypeStruct((R, D), F32), jax.ShapeDtypeStruct((R, D), BF16),
                         jax.ShapeDtypeStruct((8, LANES), F32)), grid=(R // tr,),
        in_specs=[row, row], out_specs=(row, row, pl.BlockSpec((8, LANES), lambda i: (0, 0))),
        compiler_params=_cparams(("arbitrary",)), name=name,
    )(y, target)


def _gelu(x):
    return 0.5 * x * (1.0 + lax.erf(x * math.sqrt(0.5)))


def _gelu_grad(x):
    return 0.5 * (1.0 + lax.erf(x * math.sqrt(0.5))) + x * jnp.exp(-0.5 * x * x) * (1.0 / math.sqrt(2.0 * math.pi))


def _group_consts(width):
    lane = np.arange(width)
    col = np.arange(LANES)
    grp = (lane[:, None] // HEAD_DIM == col[None, :]).astype(np.float32)
    mod = ((lane[:, None] % HEAD_DIM == col[None, :]) & (col[None, :] < HEAD_DIM)).astype(np.float32)
    return jnp.asarray(grp, BF16), jnp.asarray(mod, BF16)


def _same_group():
    lane = np.arange(GROUP_TILE) // HEAD_DIM
    return jnp.asarray((lane[:, None] == lane[None, :]).astype(np.float32), BF16)


def _bucket_onehot():
    qi = np.arange(BLOCK)[:, None]
    kj = np.arange(2 * BLOCK)[None, :]
    n = np.maximum(qi + BLOCK - kj, 0)
    max_exact = NUM_BUCKETS // 2
    nf = np.maximum(n, 1).astype(np.float32)
    large = max_exact + (np.log(nf / np.float32(max_exact)) / np.float32(math.log(MAX_DISTANCE / max_exact))
                         * np.float32(NUM_BUCKETS - max_exact)).astype(np.int32)
    large = np.minimum(large, NUM_BUCKETS - 1)
    bucket = jnp.asarray(np.where(n < max_exact, n, large).reshape(-1).astype(np.int32))
    return (bucket[:, None] == jnp.arange(LANES, dtype=jnp.int32)[None, :]).astype(BF16)


class _MixerDims:
    def __init__(self, S, IN, SW, AW, KVW, H, NQ):
        self.S, self.IN, self.SW, self.AW, self.KVW, self.H, self.NQ = S, IN, SW, AW, KVW, H, NQ
        self.NKV = KVW // HEAD_DIM
        self.GROUP = NQ // self.NKV
        self.nb = S // BLOCK
        self.koff = 2 * SW + AW
        self.voff = self.koff + KVW
        assert SW % LANES == 0 and AW % LANES == 0 and KVW % LANES == 0 and self.GROUP % 2 == 0
        assert self.koff % (2 * KVW) == 0 and IN == self.voff + KVW and S % BLOCK == 0


def _mixer_block(dm, n, z, kvp, prm):
    SW, AW, KVW, H, NQ = dm.SW, dm.AW, dm.KVW, dm.H, dm.NQ
    lane = lax.broadcasted_iota(jnp.int32, (BLOCK, LANES), 1)
    lo = lane < HEAD_DIM
    row = lax.broadcasted_iota(jnp.int32, (BLOCK, BLOCK), 0)
    col = lax.broadcasted_iota(jnp.int32, (BLOCK, BLOCK), 1)
    tril = row >= col
    pblk = prm["pblk"][...]
    inv = 1.0 / HEAD_DIM

    def group_rsqrt(x):
        return lax.rsqrt(_group_sum(x * x, pblk) * inv + EPS)

    zu, zv = z[:, :SW], z[:, SW:2 * SW]
    u, v = _gelu(zu), _gelu(zv)
    rv = group_rsqrt(v)
    vn = v * rv * prm["sgu_g"][...]
    vnb = vn.astype(BF16)
    tmats, gate_blocks = [], []
    for p in range(H // 2):
        blk = slice(LANES * p, LANES * (p + 1))
        t0 = jnp.where(tril, prm["sgu_w"][2 * p], 0.0).astype(BF16)
        t1 = jnp.where(tril, prm["sgu_w"][2 * p + 1], 0.0).astype(BF16)
        tmats += [t0, t1]
        g0 = jnp.dot(t0, vnb[:, blk], preferred_element_type=F32)
        g1 = jnp.dot(t1, vnb[:, blk], preferred_element_type=F32)
        gate_blocks.append(jnp.where(lo, g0, g1) + prm["sgu_bias"][:, blk])
    gate = jnp.concatenate(gate_blocks, axis=1)
    outa = u * gate
    ra = lax.rsqrt(jnp.mean(outa * outa, axis=-1, keepdims=True) + EPS)

    q = z[:, 2 * SW:2 * SW + AW]
    kcat = jnp.concatenate([kvp[:, :KVW], z[:, dm.koff:dm.koff + KVW]], axis=0)
    vcat = jnp.concatenate([kvp[:, KVW:], z[:, dm.voff:dm.voff + KVW]], axis=0)
    rq = group_rsqrt(q)
    rk = group_rsqrt(kcat)
    qn = q * rq * prm["q_g"][...]
    kn = kcat * rk * prm["k_g"][...]
    knb, vcb = kn.astype(BF16), vcat.astype(BF16)
    qi = lax.broadcasted_iota(jnp.int32, (BLOCK, 2 * BLOCK), 0)
    kj = lax.broadcasted_iota(jnp.int32, (BLOCK, 2 * BLOCK), 1)
    valid = (kj > qi) & (kj <= qi + BLOCK) & ((n > 0) | (kj >= BLOCK))
    scale = 1.0 / math.sqrt(HEAD_DIM)
    heads = []
    out_blocks = []
    for hq in range(NQ):
        mb, e = hq // 2, hq % 2
        kv = hq // dm.GROUP
        kb, ek = kv // 2, kv % 2
        qblk = qn[:, LANES * mb:LANES * (mb + 1)]
        if e != ek:
            qblk = pltpu.roll(qblk, HEAD_DIM, 1)
        half = lo if ek == 0 else jnp.logical_not(lo)
        qm = jnp.where(half, qblk, 0.0).astype(BF16)
        kblk = knb[:, LANES * kb:LANES * (kb + 1)]
        vblk = vcb[:, LANES * kb:LANES * (kb + 1)]
        s = lax.dot_general(qm, kblk, (((1,), (1,)), ((), ())), preferred_element_type=F32) * scale + prm["bias"][hq]
        s = jnp.where(valid, s, NEG_INF)
        sink = prm["sinks"][hq]
        mx = jnp.maximum(jnp.max(s, axis=-1, keepdims=True), sink)
        ex = jnp.exp(s - mx)
        den = jnp.sum(ex, axis=-1, keepdims=True) + jnp.exp(sink - mx)
        inv_den = 1.0 / den
        pr = ex * inv_den
        psink = jnp.exp(sink - mx) * inv_den
        prb = pr.astype(BF16)
        r_h = jnp.dot(prb, vblk, preferred_element_type=F32)
        if e != ek:
            r_h = pltpu.roll(r_h, HEAD_DIM, 1)
        heads.append(dict(qm=qm, kblk=kblk, vblk=vblk, pr=pr, prb=prb, psink=psink, half=half, mb=mb, e=e, ek=ek, kb=kb))
        if e == 1:
            out_blocks.append(jnp.where(lo, prev_r, r_h))
        prev_r = r_h
    outb = jnp.concatenate(out_blocks, axis=1)
    rb = lax.rsqrt(jnp.mean(outb * outb, axis=-1, keepdims=True) + EPS)
    return dict(lo=lo, tril=tril, pblk=pblk, zu=zu, zv=zv, u=u, v=v, rv=rv, vnb=vnb, tmats=tmats, gate=gate,
                outa=outa, ra=ra, q=q, kcat=kcat, rq=rq, rk=rk, heads=heads, outb=outb, rb=rb, scale=scale)


_MIXER_PARAMS = ("sgu_g", "sgu_w", "sgu_bias", "norm_a", "q_g", "k_g", "norm_b", "sinks", "bias", "pblk")


def _mixer_param_specs(dm, idx):
    SW, AW, KVW = dm.SW, dm.AW, dm.KVW
    full = lambda shape: pl.BlockSpec(shape, lambda n: (0,) * len(shape))
    return [full((1, SW)), full((dm.H, BLOCK, BLOCK)), full((BLOCK, SW)), full((1, SW)), full((1, AW)), full((1, KVW)),
            full((1, AW)), pl.BlockSpec(memory_space=pltpu.SMEM), full((dm.NQ, BLOCK, 2 * BLOCK)),
            full((GROUP_TILE, GROUP_TILE))]


def _mixer_fwd(dm, z, params, *, name, comm=()):
    nb = dm.nb

    def body(z_ref, kvp_ref, *rest):
        prm = dict(zip(_MIXER_PARAMS, rest[:len(_MIXER_PARAMS)]))
        o_ref = rest[len(_MIXER_PARAMS)]
        n = pl.program_id(0)
        c = _mixer_block(dm, n, z_ref[...], kvp_ref[...], prm)
        o_ref[:, :dm.SW] = (c["outa"] * c["ra"] * prm["norm_a"][...]).astype(o_ref.dtype)
        o_ref[:, dm.SW:] = (c["outb"] * c["rb"] * prm["norm_b"][...]).astype(o_ref.dtype)

    kvblk = dm.koff // (2 * dm.KVW)
    in_specs = [pl.BlockSpec((BLOCK, dm.IN), lambda n: (n, 0)),
                pl.BlockSpec((BLOCK, 2 * dm.KVW), lambda n: (jnp.maximum(n - 1, 0), kvblk))] + _mixer_param_specs(dm, None)
    return _call(
        body, out_shape=[jax.ShapeDtypeStruct((dm.S, dm.SW + dm.AW), BF16)], grid=(nb,),
        in_specs=in_specs, out_specs=[pl.BlockSpec((BLOCK, dm.SW + dm.AW), lambda n: (n, 0))],
        sem=("arbitrary",), args=[z, z, *params], name=name, comm=comm)


def _mixer_bwd(dm, z, dmixed, dbias_in, params, gmats, *, name, comm=()):
    SW, AW, KVW, H, NQ, nb, IN = dm.SW, dm.AW, dm.KVW, dm.H, dm.NQ, dm.nb, dm.IN
    QW = 2 * SW + AW
    NP = len(_MIXER_PARAMS)

    def body(z_ref, kvp_ref, dm_ref, dbin_ref, *rest):
        prm = dict(zip(_MIXER_PARAMS, rest[:NP]))
        gs_ref, gmq_ref, gmk_ref = rest[NP:NP + 3]
        (dz_ref, dsg_ref, dt_ref, dsb_ref, dna_ref, dqg_ref, dkg_ref, dnb_ref, dsk_ref, dbias_ref) = rest[NP + 3:NP + 13]
        hold, tmpkv, newkv, carry, accb, accq, acck = rest[NP + 13:]
        n = pl.program_id(0)

        @pl.when(n == 0)
        def _():
            for r in (dsg_ref, dt_ref, dna_ref, dnb_ref, dsk_ref, accb, accq, acck):
                r[...] = jnp.zeros(r.shape, r.dtype)
            dbias_ref[...] = dbin_ref[...]

        @pl.when(n < nb)
        def _():
            c = _mixer_block(dm, n, z_ref[...], kvp_ref[...], prm)
            lo = c["lo"]
            dmx = dm_ref[...]
            inv = 1.0 / HEAD_DIM

            def rms_bwd_full(dy_scaled, y, r):
                return r * (dy_scaled - y * jnp.mean(dy_scaled * y, axis=-1, keepdims=True))

            def group_mean_b(x):
                return _group_sum(x, c["pblk"]) * inv

            dma = dmx[:, :SW]
            ya = c["outa"] * c["ra"]
            dna_ref[...] += jnp.sum(dma * ya, axis=0, keepdims=True)
            douta = rms_bwd_full(dma * prm["norm_a"][...], ya, c["ra"])
            du = douta * c["gate"]
            dgate = douta * c["u"]
            accb[...] += dgate
            dgb16 = dgate.astype(BF16)
            dvn_blocks = []
            for p in range(H // 2):
                blk = slice(LANES * p, LANES * (p + 1))
                dg = dgate[:, blk]
                d0 = jnp.where(lo, dg, 0.0).astype(BF16)
                d1 = jnp.where(lo, 0.0, dg).astype(BF16)
                vb = c["vnb"][:, blk]
                nt = lambda a, b: lax.dot_general(a, b, (((1,), (1,)), ((), ())), preferred_element_type=F32)
                tn = lambda a, b: lax.dot_general(a, b, (((0,), (0,)), ((), ())), preferred_element_type=F32)
                dt_ref[2 * p] += nt(d0, vb)
                dt_ref[2 * p + 1] += nt(d1, vb)
                dvn_blocks.append(jnp.where(lo, tn(c["tmats"][2 * p], dgb16[:, blk]), tn(c["tmats"][2 * p + 1], dgb16[:, blk])))
            dvn = jnp.concatenate(dvn_blocks, axis=1)
            yv = c["v"] * c["rv"]
            dsg_ref[...] += jnp.sum(dvn * yv, axis=0, keepdims=True)
            dyv = dvn * prm["sgu_g"][...]
            dv = c["rv"] * (dyv - yv * group_mean_b(dyv * yv))
            dzu = du * _gelu_grad(c["zu"])
            dzv = dv * _gelu_grad(c["zv"])

            dmb = dmx[:, SW:]
            yb = c["outb"] * c["rb"]
            dnb_ref[...] += jnp.sum(dmb * yb, axis=0, keepdims=True)
            doutb = rms_bwd_full(dmb * prm["norm_b"][...], yb, c["rb"])
            lane1 = lax.broadcasted_iota(jnp.int32, (1, LANES), 1)
            dqn_blocks = [None] * (AW // LANES)
            dkn_blocks = [None] * (KVW // LANES)
            dvc_blocks = [None] * (KVW // LANES)
            dsink_vec = jnp.zeros((1, LANES), F32)
            add = lambda old, new: new if old is None else old + new
            for hq, hd in enumerate(c["heads"]):
                mb, e, ek, kb, half = hd["mb"], hd["e"], hd["ek"], hd["kb"], hd["half"]
                dr = doutb[:, LANES * mb:LANES * (mb + 1)]
                if e != ek:
                    dr = pltpu.roll(dr, HEAD_DIM, 1)
                drm = jnp.where(half, dr, 0.0).astype(BF16)
                dp = lax.dot_general(drm, hd["vblk"], (((1,), (1,)), ((), ())), preferred_element_type=F32)
                dvc_blocks[kb] = add(dvc_blocks[kb], lax.dot_general(hd["prb"], drm, (((0,), (0,)), ((), ())),
                                                                     preferred_element_type=F32))
                rowdot = jnp.sum(hd["pr"] * dp, axis=-1, keepdims=True)
                ds = hd["pr"] * (dp - rowdot)
                dsink = jnp.sum(-hd["psink"] * rowdot, axis=0, keepdims=True)
                dsink_vec = dsink_vec + jnp.where(lane1 == hq, dsink, 0.0)
                dbias_ref[hq] += ds
                dsb = (ds * c["scale"]).astype(BF16)
                dqm = jnp.dot(dsb, hd["kblk"], preferred_element_type=F32)
                dqm = jnp.where(half, dqm, 0.0)
                if e != ek:
                    dqm = pltpu.roll(dqm, HEAD_DIM, 1)
                dqn_blocks[mb] = add(dqn_blocks[mb], dqm)
                dkn_blocks[kb] = add(dkn_blocks[kb], lax.dot_general(dsb, hd["qm"], (((0,), (0,)), ((), ())),
                                                                     preferred_element_type=F32))
            dsk_ref[...] += dsink_vec
            dqn = jnp.concatenate(dqn_blocks, axis=1)
            dkn = jnp.concatenate(dkn_blocks, axis=1)
            dvc = jnp.concatenate(dvc_blocks, axis=1)
            yq = c["q"] * c["rq"]
            accq[...] += jnp.sum(dqn * yq, axis=0, keepdims=True)
            dyq = dqn * prm["q_g"][...]
            dq = c["rq"] * (dyq - yq * group_mean_b(dyq * yq))
            yk = c["kcat"] * c["rk"]
            acck[...] += jnp.sum(dkn * yk, axis=0, keepdims=True)
            dyk = dkn * prm["k_g"][...]
            dk = c["rk"] * (dyk - yk * group_mean_b(dyk * yk))

            slot = n % 2
            hold[slot, :, :SW] = dzu
            hold[slot, :, SW:2 * SW] = dzv
            hold[slot, :, 2 * SW:] = dq
            tmpkv[:, :KVW] = dk[:BLOCK]
            tmpkv[:, KVW:] = dvc[:BLOCK]
            newkv[:, :KVW] = dk[BLOCK:]
            newkv[:, KVW:] = dvc[BLOCK:]

        @pl.when(n >= 1)
        def _():
            dz_ref[:, :QW] = hold[(n - 1) % 2].astype(dz_ref.dtype)

        @pl.when((n >= 1) & (n < nb))
        def _():
            dz_ref[:, QW:] = (carry[...] + tmpkv[...]).astype(dz_ref.dtype)

        @pl.when(n == nb)
        def _():
            dz_ref[:, QW:] = carry[...].astype(dz_ref.dtype)
            row = lax.broadcasted_iota(jnp.int32, (BLOCK, BLOCK), 0)
            col = lax.broadcasted_iota(jnp.int32, (BLOCK, BLOCK), 1)
            for h in range(H):
                dt_ref[h] = jnp.where(row >= col, dt_ref[h], 0.0)
            dsb_ref[...] = _dot3(accb[...], gs_ref[...])
            dqg_ref[...] = _dot3(accq[...], gmq_ref[...])
            dkg_ref[...] = _dot3(acck[...], gmk_ref[...])

        @pl.when(n < nb)
        def _():
            carry[...] = newkv[...]

    kvblk = dm.koff // (2 * KVW)
    clamp = lambda n: jnp.minimum(n, nb - 1)
    full = lambda shape: pl.BlockSpec(shape, lambda n: (0,) * len(shape))
    in_specs = [pl.BlockSpec((BLOCK, IN), lambda n: (clamp(n), 0)),
                pl.BlockSpec((BLOCK, 2 * KVW), lambda n: (jnp.maximum(clamp(n) - 1, 0), kvblk)),
                pl.BlockSpec((BLOCK, SW + AW), lambda n: (clamp(n), 0)),
                full((NQ, BLOCK, 2 * BLOCK))] + _mixer_param_specs(dm, None) + [full((SW, LANES)), full((AW, LANES)),
                                                                                full((KVW, LANES))]
    out_shape = (jax.ShapeDtypeStruct((dm.S, IN), BF16),
                 jax.ShapeDtypeStruct((1, SW), F32), jax.ShapeDtypeStruct((H, BLOCK, BLOCK), F32),
                 jax.ShapeDtypeStruct((BLOCK, LANES), F32), jax.ShapeDtypeStruct((1, SW), F32),
                 jax.ShapeDtypeStruct((1, LANES), F32), jax.ShapeDtypeStruct((1, LANES), F32),
                 jax.ShapeDtypeStruct((1, AW), F32), jax.ShapeDtypeStruct((1, LANES), F32),
                 jax.ShapeDtypeStruct((NQ, BLOCK, 2 * BLOCK), F32))
    out_specs = (pl.BlockSpec((BLOCK, IN), lambda n: (jnp.maximum(n - 1, 0), 0)),
                 full((1, SW)), full((H, BLOCK, BLOCK)), full((BLOCK, LANES)), full((1, SW)), full((1, LANES)),
                 full((1, LANES)), full((1, AW)), full((1, LANES)), full((NQ, BLOCK, 2 * BLOCK)))
    scratch = [pltpu.VMEM((2, BLOCK, QW), F32), pltpu.VMEM((BLOCK, 2 * KVW), F32), pltpu.VMEM((BLOCK, 2 * KVW), F32),
               pltpu.VMEM((BLOCK, 2 * KVW), F32), pltpu.VMEM((BLOCK, SW), F32), pltpu.VMEM((1, AW), F32),
               pltpu.VMEM((1, KVW), F32)]
    return _call(
        body, out_shape=out_shape, grid=(nb + 1,), in_specs=in_specs, out_specs=out_specs, scratch=scratch,
        sem=("arbitrary",), args=[z, z, dmixed, dbias_in, *params, *gmats], name=name, comm=comm)


def _adamw(w, gparts, m, v, *, name, layer=0, stacked=None, tr=256):
    R, C = gparts[0].shape
    tr = _tile(R, max(8, min(tr, (1 << 18) // C)), 8)
    ng = len(gparts)
    bc1 = 1.0 - ADAM_B1 ** ADAM_STEP
    bc2 = 1.0 - ADAM_B2 ** ADAM_STEP

    def body(w_ref, *rest):
        g_refs, (m_ref, v_ref), (go_ref, d_ref, mo_ref, vo_ref) = rest[:ng], rest[ng:ng + 2], rest[-4:]
        g = g_refs[0][...].astype(F32)
        for r in g_refs[1:]:
            g = g + r[...].astype(F32)
        mn = ADAM_B1 * m_ref[...] + (1.0 - ADAM_B1) * g
        vn = ADAM_B2 * v_ref[...] + (1.0 - ADAM_B2) * jnp.square(g)
        m_hat = mn / bc1
        v_hat = vn / bc2
        go_ref[...] = g
        d_ref[...] = -ADAM_LR * (m_hat / (jnp.sqrt(v_hat) + ADAM_EPS) + ADAM_WD * w_ref[...])
        mo_ref[...] = mn
        vo_ref[...] = vn

    here = pl.BlockSpec((tr, C), lambda i: (layer * (R // tr) + i, 0))
    blk = pl.BlockSpec((tr, C), lambda i: (i, 0))
    prev = [] if stacked is None else list(stacked)
    return _call(body, out_shape=[jax.ShapeDtypeStruct(w.shape, F32)] * 4, grid=(R // tr,),
                 in_specs=[here] + [blk] * ng + [here, here] + [ANY] * len(prev), out_specs=[here] * 4,
                 sem=("parallel",), args=[w, *gparts, m, v, *prev], name=name,
                 aliases={3 + ng + q: q for q in range(len(prev))})


def _sum_chip_partials(g, axis, base, width, stride, got, chip, *, name, tr=256, comm=()):
    _, R, C = got.shape
    tr = _tile(R, max(8, min(tr, (1 << 18) // C)), 8)
    assert base % width == 0 and stride % width == 0 and (C == width if axis == 1 else R == width)

    def body(chip_ref, own_ref, g0_ref, g1_ref, g2_ref, o_ref):
        o_ref[...] = ((own_ref[...].astype(F32) + g0_ref[...].astype(F32)) + g1_ref[...].astype(F32)) + g2_ref[...].astype(F32)

    band = lambda j: base // width + j[0] * (stride // width)
    if axis == 1:
        own_spec = pl.BlockSpec((tr, C), lambda i, j: (i, band(j)))
    else:
        own_spec = pl.BlockSpec((tr, C), lambda i, j: (band(j) * (R // tr) + i, 0))
    part = lambda k: pl.BlockSpec((None, tr, C), lambda i, j: (k, i, 0))
    return _call(body, out_shape=[jax.ShapeDtypeStruct((R, C), F32)], grid=(R // tr,),
                 in_specs=[own_spec, part(0), part(1), part(2)], out_specs=[pl.BlockSpec((tr, C), lambda i, j: (i, 0))],
                 sem=("parallel",), args=[g, got, got, got], name=name, prefetch=chip, comm=comm)


def _pack(arrays):
    parts, total = [], 0
    for a in arrays:
        flat = a.reshape(-1).astype(F32)
        pad = (-flat.shape[0]) % (8 * LANES)
        parts.append(jnp.pad(flat, (0, pad)))
        total += flat.shape[0] + pad
    parts.append(jnp.zeros(((-total) % (PACK_ROWS * LANES),), F32))
    return jnp.concatenate(parts).reshape(-1, LANES)


def _unpack(packed, like):
    flat = packed.reshape(-1)
    out, off = [], 0
    for a in like:
        n = int(np.prod(a.shape))
        out.append(flat[off:off + n].reshape(a.shape))
        off += n + ((-n) % (8 * LANES))
    return out


def kernel(x, rel_bias, norm1_g, w_in, sgu_norm_g, sgu_w, sgu_b, q_norm_g, k_norm_g, sinks, out_norm_a, out_norm_b, w_out, norm2_g, w_gate, w_up, w_down, loss_target, m_rel_bias, m_norm1_g, m_w_in, m_sgu_norm_g, m_sgu_w, m_sgu_b, m_q_norm_g, m_k_norm_g, m_sinks, m_out_norm_a, m_out_norm_b, m_w_out, m_norm2_g, m_w_gate, m_w_up, m_w_down, v_rel_bias, v_norm1_g, v_w_in, v_sgu_norm_g, v_sgu_w, v_sgu_b, v_q_norm_g, v_k_norm_g, v_sinks, v_out_norm_a, v_out_norm_b, v_w_out, v_norm2_g, v_w_gate, v_w_up, v_w_down):
    L, D, n_in = w_in.shape
    S = x.shape[1]
    IN = N_CHIPS * n_in
    n_ff = w_gate.shape[2]
    FF = N_CHIPS * n_ff
    H = sgu_w.shape[1]
    NQ = sinks.shape[1]
    SW, AW = H * HEAD_DIM, NQ * HEAD_DIM
    KVW = (IN - 2 * SW - AW) // 2
    dm = _MixerDims(S, IN, SW, AW, KVW, H, NQ)
    assert sgu_w.shape[2] == BLOCK and q_norm_g.shape[1] == HEAD_DIM and SW + AW == D

    wb = {k: w.astype(BF16) for k, w in (("in", w_in), ("out", w_out), ("gate", w_gate), ("up", w_up), ("down", w_down))}

    def gather(l, which):
        if which == "in":
            return _Gather([wb["in"]], [(0, 0, 1, 0, n_in)], [(D, IN)], l)
        if which == "out":
            return _Gather([wb["out"]], [(0, 0, 0, 0, D // N_CHIPS)], [(D, D)], l)
        if which == "gu":
            return _Gather([wb["gate"], wb["up"]], [(0, 0, 1, 0, 2 * n_ff), (1, 0, 1, n_ff, 2 * n_ff)], [(D, 2 * FF)], l)
        return _Gather([wb["down"]], [(0, 0, 0, 0, n_ff)], [(FF, D)], l)

    nxt = lambda l, *which: [gather(l + 1, w) for w in which] if l + 1 < L else []
    W = {}
    (W[0, "in"],) = _comm_only([gather(0, "in")], name="gather_first_weights")

    gs, gmod_q = _group_consts(SW)
    _, gmod_k = _group_consts(KVW)
    pblk = _same_group()
    onehot = _bucket_onehot()
    rbt = jnp.pad(rel_bias.T, ((0, 0), (0, LANES - NUM_BUCKETS)))
    (bias,) = _matmul(rbt, onehot.T, mode="nn", out_dtype=F32, exact=True, tn=4096, name="bias_table")
    bias = bias.reshape(NQ, BLOCK, 2 * BLOCK)

    def mixer_params(l):
        return [sgu_norm_g[l].reshape(1, SW), sgu_w[l], jnp.repeat(sgu_b[l].T, HEAD_DIM, axis=1),
                out_norm_a[l].reshape(1, SW), jnp.tile(q_norm_g[l], NQ).reshape(1, AW),
                jnp.tile(k_norm_g[l], dm.NKV).reshape(1, KVW), out_norm_b[l].reshape(1, AW), sinks[l], bias, pblk]

    xs = x.reshape(S, D)
    saved = []
    for l in range(L):
        h = _rms_fwd(xs, norm1_g[l].reshape(1, D), name="norm1_fwd")
        z, *got = _matmul(h, W[l, "in"], mode="nn", out_dtype=F32, tn=1792, name="in_proj",
                          comm=[gather(0, "out"), gather(0, "down")] if l == 0 else [])
        if l == 0:
            W[0, "out"], W[0, "down"] = got
        mixed, *got = _mixer_fwd(dm, z, mixer_params(l), name="mixer_fwd",
                                 comm=[gather(0, "gu")] if l == 0 else nxt(l, "in", "out"))
        if l == 0:
            (W[0, "gu"],) = got
        elif got:
            W[l + 1, "in"], W[l + 1, "out"] = got
        (x1,) = _matmul(mixed, W[l, "out"], mode="nn", out_dtype=F32, res=xs, name="out_proj")
        h2 = _rms_fwd(x1, norm2_g[l].reshape(1, D), name="norm2_fwd")
        ab, f, *got = _gate_up_swiglu(h2, W[l, "gu"], n_ff, name="gate_up_swiglu",
                                      comm=nxt(l, "gu") + (nxt(l, "in") if l == 0 else []))
        if got:
            W[l + 1, "gu"] = got.pop(0)
        if got:
            (W[l + 1, "in"],) = got
        x2, *got = _matmul(f, W[l, "down"], mode="nn", out_dtype=F32, res=x1, tm=512, tk=5632, name="down_proj",
                           comm=nxt(l, "down") + (nxt(l, "out") if l == 0 else []))
        if got:
            W[l + 1, "down"] = got.pop(0)
        if got:
            (W[l + 1, "out"],) = got
        saved.append((xs, h, z, mixed, x1, h2, ab, f))
        xs = x2

    dx, dxb, loss_part = _loss_head(xs, loss_target.reshape(S, D), name="loss_head")
    loss = lax.psum(loss_part[0, 0], ("x", "y", "c"))

    dbias = jnp.zeros((NQ, BLOCK, 2 * BLOCK), F32)
    n_out = D // N_CHIPS
    big_names = ("w_in", "w_out", "w_gate", "w_up", "w_down")
    own = {k: [None] * L for k in big_names}
    got = {k: [None] * L for k in big_names}
    names_layer = ("norm1_g", "sgu_norm_g", "sgu_w", "sgu_b", "q_norm_g", "k_norm_g", "sinks", "out_norm_a", "out_norm_b",
                   "norm2_g")
    small = {k: [None] * L for k in names_layer}
    small_unit = lambda layers: _AllGather(_pack([jnp.stack([small[k][i] for i in layers]) for k in names_layer]))
    for l in reversed(range(L)):
        xl, h, z, mixed, x1, h2, ab, f = saved[l]
        early = [small_unit(range(1, L))] if (l == 0 and L > 1) else []
        dab, *gathered_early = _down_dx_swiglu_bwd(dxb, W[l, "down"], ab, n_ff, name="down_dx_swiglu_bwd", comm=early)
        (g_d,) = _matmul_tn(f, dxb, out_dtype=BF16, tm=1408, tn=1024, tt=2048, name="down_proj_dw")
        g_gu, got["w_down"][l] = _matmul_tn(h2, dab, out_dtype=BF16, tm=512, tn=512, tt=S, name="gate_up_proj_dw",
                                            comm=[_Scatter([g_d], [(0, 0, 0, n_ff, n_ff)])])
        dh2, got["w_gate"][l] = _matmul(dab, W[l, "gu"], mode="nt", out_dtype=F32, tm=512, tn=512, tk=2 * FF,
                                        name="gate_up_proj_dx",
                                        comm=[_Scatter([g_gu], [(0, 1, 0, n_ff, 2 * n_ff)])])
        dx1, dx1b, dg2 = _rms_bwd(x1, norm2_g[l].reshape(1, D), dh2, dx, name="norm2_bwd")
        (dmixed,) = _matmul(dx1b, W[l, "out"], mode="nt", out_dtype=F32, name="out_proj_dx")
        (g_out,) = _matmul_tn(mixed, dx1b, out_dtype=BF16, tt=2048, name="out_proj_dw")
        (dz, d_sg, d_t, d_sb, d_na, d_qg, d_kg, d_nb, d_sk, dbias, got["w_up"][l], got["w_out"][l]) = _mixer_bwd(
            dm, z, dmixed, dbias, mixer_params(l), (gs, gmod_q, gmod_k), name="mixer_bwd",
            comm=[_Scatter([g_gu, g_out], [(0, 1, n_ff, n_ff, 2 * n_ff), (1, 0, 0, n_out, n_out)])])
        (g_in,) = _matmul_tn(h, dz, out_dtype=BF16, tn=896, tt=2048, name="in_proj_dw")
        dh, got["w_in"][l] = _matmul(dz, W[l, "in"], mode="nt", out_dtype=F32, tk=3584, name="in_proj_dx",
                                     comm=[_Scatter([g_in], [(0, 1, 0, n_in, n_in)])])
        dx, dxb, dg1 = _rms_bwd(xl, norm1_g[l].reshape(1, D), dh, dx1, name="norm1_bwd")
        own["w_in"][l], own["w_out"][l] = (g_in, 1, 0, n_in, n_in), (g_out, 0, 0, n_out, n_out)
        own["w_gate"][l], own["w_up"][l] = (g_gu, 1, 0, n_ff, 2 * n_ff), (g_gu, 1, n_ff, n_ff, 2 * n_ff)
        own["w_down"][l] = (g_d, 0, 0, n_ff, n_ff)
        small["norm1_g"][l] = dg1.reshape(D)
        small["norm2_g"][l] = dg2.reshape(D)
        small["sgu_norm_g"][l] = d_sg.reshape(H, HEAD_DIM)
        small["sgu_w"][l] = d_t
        small["sgu_b"][l] = d_sb[:, :H].T
        small["q_norm_g"][l] = d_qg[0, :HEAD_DIM]
        small["k_norm_g"][l] = d_kg[0, :HEAD_DIM]
        small["sinks"][l] = d_sk[0, :NQ]
        small["out_norm_a"][l] = d_na.reshape(SW)
        small["out_norm_b"][l] = d_nb.reshape(AW)
    grad_x = dx.reshape(1, S, D)
    (d_rb,) = _matmul(dbias.reshape(NQ, BLOCK * 2 * BLOCK), onehot, mode="nn", out_dtype=F32, exact=True, tk=4096,
                      name="rel_bias_grad")
    d_rel_bias = d_rb[:, :NUM_BUCKETS].T

    chip = (2 * lax.axis_index("x") + lax.axis_index("y")).astype(jnp.int32).reshape(1)
    jobs = [(nm, l) for nm in big_names for l in range(L)]
    late = _AllGather(_pack([small[k][0][None] for k in names_layer] + [d_rel_bias]))
    mine, sib = {}, {}
    for i, (nm, l) in enumerate(jobs):
        g, axis, base, width, stride = own[nm][l]
        units = ([late] if i == 0 else []) + ([_Swap([mine[jobs[i - 1]]])] if i > 0 else [])
        mine[nm, l], *rest = _sum_chip_partials(g, axis, base, width, stride, got[nm][l], chip, name="sum_chip_partials",
                                                comm=units)
        if i == 0:
            gathered_late = rest.pop(0)
        if i > 0:
            (sib[jobs[i - 1]],) = rest
    (sib[jobs[-1]],) = _comm_only([_Swap([mine[jobs[-1]]])], name="swap_last_sums")
    wmv = dict(w_in=(w_in, m_w_in, v_w_in), w_out=(w_out, m_w_out, v_w_out), w_gate=(w_gate, m_w_gate, v_w_gate),
               w_up=(w_up, m_w_up, v_w_up), w_down=(w_down, m_w_down, v_w_down))
    big = {}
    two = lambda t: t.reshape(-1, t.shape[-1])
    for nm in big_names:
        w, m, v = wmv[nm]
        res = None
        for l in range(L):
            res = _adamw(two(w), [mine[nm, l], sib[nm, l]], two(m), two(v), layer=l, stacked=res, name="adamw_" + nm)
        big[nm] = [r.reshape(w.shape) for r in res]

    w_small = dict(rel_bias=rel_bias, norm1_g=norm1_g, sgu_norm_g=sgu_norm_g, sgu_w=sgu_w, sgu_b=sgu_b, q_norm_g=q_norm_g,
                   k_norm_g=k_norm_g, sinks=sinks, out_norm_a=out_norm_a, out_norm_b=out_norm_b, norm2_g=norm2_g)
    m_small = dict(rel_bias=m_rel_bias, norm1_g=m_norm1_g, sgu_norm_g=m_sgu_norm_g, sgu_w=m_sgu_w, sgu_b=m_sgu_b,
                   q_norm_g=m_q_norm_g, k_norm_g=m_k_norm_g, sinks=m_sinks, out_norm_a=m_out_norm_a,
                   out_norm_b=m_out_norm_b, norm2_g=m_norm2_g)
    v_small = dict(rel_bias=v_rel_bias, norm1_g=v_norm1_g, sgu_norm_g=v_sgu_norm_g, sgu_w=v_sgu_w, sgu_b=v_sgu_b,
                   q_norm_g=v_q_norm_g, k_norm_g=v_k_norm_g, sinks=v_sinks, out_norm_a=v_out_norm_a,
                   out_norm_b=v_out_norm_b, norm2_g=v_norm2_g)

    def adamw_small(gathered, pick, name):
        like = pick(w_small)
        res = _adamw(_pack(like), [gathered[i] for i in range(N_DEV)], _pack(pick(m_small)), _pack(pick(v_small)), name=name)
        return [_unpack(r, like) for r in res]

    first_layer = adamw_small(gathered_late, lambda d: [d[k][:1] for k in names_layer] + [d["rel_bias"]], "adamw_small_late")
    if L > 1:
        others = adamw_small(gathered_early[0], lambda d: [d[k][1:] for k in names_layer], "adamw_small_early")
    sm = {"rel_bias": [r[-1] for r in first_layer]}
    for j, k in enumerate(names_layer):
        sm[k] = [jnp.concatenate([first_layer[q][j]] + ([others[q][j]] if L > 1 else [])) for q in range(4)]

    order = ("rel_bias", "norm1_g", "w_in", "sgu_norm_g", "sgu_w", "sgu_b", "q_norm_g", "k_norm_g", "sinks", "out_norm_a",
             "out_norm_b", "w_out", "norm2_g", "w_gate", "w_up", "w_down")
    pick = lambda k, i: big[k][i] if k in big else sm[k][i]
    outs = [loss, grad_x]
    for i in range(4):
        outs += [pick(k, i) for k in order]
    return tuple(outs)
```

```python
import functools
import math

import numpy as np

import jax
import jax.numpy as jnp
from jax import lax
from jax.experimental import pallas as pl
from jax.experimental.pallas import tpu as pltpu

F32 = jnp.float32
BF16 = jnp.bfloat16
MESH = pl.DeviceIdType.MESH
ANY = pl.BlockSpec(memory_space=pl.ANY)

HEAD_DIM = 64
BLOCK = 128
NUM_BUCKETS = 32
MAX_DISTANCE = 128
EPS = 1e-6
NEG_INF = -1e30
ADAM_LR, ADAM_B1, ADAM_B2, ADAM_EPS, ADAM_WD, ADAM_STEP = 0.001, 0.9, 0.999, 1e-08, 0.01, 10

LANES = 128
VMEM_LIMIT = 56 * 1024 * 1024
N_CHIPS = 4
N_DEV = 8
PACK_ROWS = 256
ROW_CHUNK = 256


def _cparams(sem=None):
    return pltpu.CompilerParams(dimension_semantics=sem, vmem_limit_bytes=VMEM_LIMIT)


def _tile(dim, target, align=LANES):
    best = None
    for t in range(align, min(dim, target) + 1, align):
        if dim % t == 0:
            best = t
    return best if best is not None else dim


def _mesh_pos():
    return lax.axis_index("x"), lax.axis_index("y"), lax.axis_index("c")


def _other_chips(x, y):
    return [(1 - x, y), (x, 1 - y), (1 - x, 1 - y)]


def _slab(ref, axis, start, size, half=None):
    if axis == 1:
        rows = ref.shape[0]
        r = pl.ds(0, rows) if half is None else pl.ds(pl.multiple_of(half * (rows // 2), 16), rows // 2)
        return ref.at[r, pl.ds(pl.multiple_of(start, LANES), size)]
    if half is None:
        return ref.at[pl.ds(pl.multiple_of(start, 16), size), :]
    return ref.at[pl.ds(pl.multiple_of(start + half * (size // 2), 16), size // 2), :]


def _remote(src, dst, send_sem, recv_sem, to):
    return pltpu.make_async_remote_copy(src_ref=src, dst_ref=dst, send_sem=send_sem, recv_sem=recv_sem,
                                        device_id=to, device_id_type=MESH)


class _Gather:
    def __init__(self, shards, streams, out_shapes, layer):
        self.ins, self.streams, self.layer = list(shards), streams, layer
        self.outs = [jax.ShapeDtypeStruct(s, BF16) for s in out_shapes]
        ns = len(streams)
        self.sems = [pltpu.SemaphoreType.DMA((ns, 3))] * 4 + [pltpu.SemaphoreType.DMA((ns,))]

    def _sent(self, srcs, outs, sems):
        send, recv, _, _, local = sems
        x, y, c = _mesh_pos()
        me_j = 2 * x + y
        own, sends = [], []
        for s, (si, oi, axis, base, stride) in enumerate(self.streams):
            src, out = srcs[si].at[self.layer], outs[oi]
            width = src.shape[axis]
            own.append(pltpu.make_async_copy(src, _slab(out, axis, base + me_j * stride, width), local.at[s]))
            half_rows = src.shape[0] // 2
            mine = src.at[pl.ds(pl.multiple_of(c * half_rows, 16), half_rows), :]
            for k, (px, py) in enumerate(_other_chips(x, y)):
                sends.append(_remote(mine, _slab(out, axis, base + me_j * stride, width, half=c),
                                     send.at[s, k], recv.at[s, k], (px, py, c)))
        return own, sends

    def start(self, srcs, outs, sems):
        own, sends = self._sent(srcs, outs, sems)
        for cp in own + sends:
            cp.start()

    def finish(self, srcs, outs, sems):
        send, recv, fsend, frecv, _ = sems
        x, y, c = _mesh_pos()
        sib = (x, y, 1 - c)
        forwards, fwd_arrivals = [], []
        for s, (si, oi, axis, base, stride) in enumerate(self.streams):
            out = outs[oi]
            width = srcs[si].shape[1 + axis]
            for k, (px, py) in enumerate(_other_chips(x, y)):
                start = base + (2 * px + py) * stride
                got = _slab(out, axis, start, width, half=c)
                _remote(got, got, send.at[s, k], recv.at[s, k], (px, py, c)).wait_recv()
                fwd = _remote(got, got, fsend.at[s, k], frecv.at[s, k], sib)
                fwd.start()
                forwards.append(fwd)
                theirs = _slab(out, axis, start, width, half=1 - c)
                fwd_arrivals.append(_remote(theirs, theirs, fsend.at[s, k], frecv.at[s, k], sib))
        for a in fwd_arrivals:
            a.wait_recv()
        for cp in forwards:
            cp.wait_send()
        own, sends = self._sent(srcs, outs, sems)
        for cp in sends:
            cp.wait_send()
        for cp in own:
            cp.wait()


class _Scatter:
    def __init__(self, grads, streams):
        self.ins, self.streams = list(grads), streams
        shard = lambda g, axis, width: (width, g.shape[1]) if axis == 0 else (g.shape[0], width)
        self.outs = [jax.ShapeDtypeStruct((3,) + shard(grads[gi], axis, width), BF16) for gi, axis, _, width, _ in streams]
        self.sems = [pltpu.SemaphoreType.DMA((len(streams), 3))] * 2

    def _copies(self, srcs, outs, sems):
        send, recv = sems
        x, y, c = _mesh_pos()
        copies = []
        for s, (gi, axis, base, width, stride) in enumerate(self.streams):
            for k, (px, py) in enumerate(_other_chips(x, y)):
                copies.append(_remote(_slab(srcs[gi], axis, base + (2 * px + py) * stride, width), outs[s].at[k],
                                      send.at[s, k], recv.at[s, k], (px, py, c)))
        return copies

    def start(self, srcs, outs, sems):
        for cp in self._copies(srcs, outs, sems):
            cp.start()

    def finish(self, srcs, outs, sems):
        for cp in self._copies(srcs, outs, sems):
            cp.wait()


class _Swap:
    def __init__(self, arrays):
        self.ins = list(arrays)
        self.outs = [jax.ShapeDtypeStruct(a.shape, a.dtype) for a in arrays]
        self.sems = [pltpu.SemaphoreType.DMA((len(arrays),))] * 2

    def _copies(self, srcs, outs, sems):
        send, recv = sems
        x, y, c = _mesh_pos()
        return [_remote(srcs[s], outs[s], send.at[s], recv.at[s], (x, y, 1 - c)) for s in range(len(srcs))]

    def start(self, srcs, outs, sems):
        for cp in self._copies(srcs, outs, sems):
            cp.start()

    def finish(self, srcs, outs, sems):
        for cp in self._copies(srcs, outs, sems):
            cp.wait()


class _AllGather:
    def __init__(self, part):
        self.ins = [part]
        self.outs = [jax.ShapeDtypeStruct((N_DEV,) + part.shape, part.dtype)]
        self.sems = [pltpu.SemaphoreType.DMA((7,)), pltpu.SemaphoreType.DMA((7,)), pltpu.SemaphoreType.DMA(())]

    def _first(self, srcs, outs, sems):
        (x_ref,), (out,), (send, recv, local) = srcs, outs, sems
        x, y, c = _mesh_pos()
        mine = out.at[4 * x + 2 * y + c]
        own = pltpu.make_async_copy(x_ref, mine, local)
        sends = [_remote(x_ref, mine, send.at[0], recv.at[0], (x, y, 1 - c))]
        sends += [_remote(x_ref, mine, send.at[1 + k], recv.at[1 + k], (px, py, c)) for k, (px, py) in enumerate(_other_chips(x, y))]
        return own, sends

    def start(self, srcs, outs, sems):
        own, sends = self._first(srcs, outs, sems)
        own.start()
        for cp in sends:
            cp.start()

    def finish(self, srcs, outs, sems):
        (out,), (send, recv, local) = outs, sems
        x, y, c = _mesh_pos()
        me, sib = (x, y, c), (x, y, 1 - c)
        slot = lambda px, py, pc: out.at[4 * px + 2 * py + pc]
        passed = []
        for k, (px, py) in enumerate(_other_chips(x, y)):
            blk = slot(px, py, c)
            _remote(blk, blk, send.at[1 + k], recv.at[1 + k], me).wait_recv()
            fwd = _remote(blk, blk, send.at[4 + k], recv.at[4 + k], sib)
            fwd.start()
            passed.append(fwd)
        blk = slot(x, y, 1 - c)
        _remote(blk, blk, send.at[0], recv.at[0], me).wait_recv()
        for k, (px, py) in enumerate(_other_chips(x, y)):
            blk = slot(px, py, 1 - c)
            _remote(blk, blk, send.at[4 + k], recv.at[4 + k], me).wait_recv()
        own, sends = self._first(srcs, outs, sems)
        for cp in sends + passed:
            cp.wait_send()
        own.wait()


def _call(body, *, grid, in_specs, out_specs, out_shape, args, name, scratch=(), sem=None, comm=(), prefetch=None,
          aliases=None):
    out_shape, out_specs = tuple(out_shape), tuple(out_specs)
    n_in, n_out, n_scr = len(in_specs), len(out_shape), len(scratch)
    n_pre = 0 if prefetch is None else 1
    c_ins = [a for u in comm for a in u.ins]
    c_outs = [o for u in comm for o in u.outs]
    c_sems = [s for u in comm for s in u.sems]

    def wrapped(*refs):
        pre, refs = refs[:n_pre], refs[n_pre:]
        ins, rest = refs[:n_in], refs[n_in:]
        cin, rest = rest[:len(c_ins)], rest[len(c_ins):]
        outs, rest = rest[:n_out], rest[n_out:]
        cout, rest = rest[:len(c_outs)], rest[len(c_outs):]
        scr, csem = rest[:n_scr], rest[n_scr:]

        def each(fn_name):
            i = o = s = 0
            for u in comm:
                getattr(u, fn_name)(cin[i:i + len(u.ins)], cout[o:o + len(u.outs)], csem[s:s + len(u.sems)])
                i, o, s = i + len(u.ins), o + len(u.outs), s + len(u.sems)

        if comm:
            pids = [pl.program_id(d) for d in range(len(grid))]
            first = functools.reduce(jnp.logical_and, [p == 0 for p in pids])
            last = functools.reduce(jnp.logical_and, [p == g - 1 for p, g in zip(pids, grid)])
            pl.when(first)(lambda: each("start"))
        body(*pre, *ins, *outs, *scr)
        if comm:
            pl.when(last)(lambda: each("finish"))

    if comm:
        sem = ("arbitrary",) * len(grid)
    all_in = list(in_specs) + [ANY] * len(c_ins)
    all_out = out_specs + tuple([ANY] * len(c_outs))
    all_scr = list(scratch) + c_sems
    kw = {}
    if aliases:
        kw["input_output_aliases"] = {n_pre + i: o for i, o in aliases.items()}
    if prefetch is None:
        kw.update(grid=grid, in_specs=all_in, out_specs=all_out, scratch_shapes=all_scr)
        pre_args = []
    else:
        kw["grid_spec"] = pltpu.PrefetchScalarGridSpec(num_scalar_prefetch=1, grid=grid, in_specs=all_in, out_specs=all_out,
                                                       scratch_shapes=all_scr)
        pre_args = [prefetch]
    return pl.pallas_call(wrapped, out_shape=out_shape + tuple(c_outs), compiler_params=_cparams(sem), name=name, **kw)(
        *pre_args, *args, *c_ins)


def _comm_only(comm, *, name):
    def body(tick_ref):
        tick_ref[...] = jnp.zeros(tick_ref.shape, tick_ref.dtype)

    outs = _call(body, grid=(1,), in_specs=[], out_specs=[pl.BlockSpec((8, LANES), lambda i: (0, 0))],
                 out_shape=[jax.ShapeDtypeStruct((8, LANES), F32)], args=[], name=name, comm=comm)
    return outs[1:]


def _split3(x):
    hi = x.astype(BF16)
    r1 = x - hi.astype(F32)
    mid = r1.astype(BF16)
    lo = (r1 - mid.astype(F32)).astype(BF16)
    return hi, mid, lo


def _dot3(x, g):
    hi, mid, lo = _split3(x)
    d = lambda a: jnp.dot(a, g, preferred_element_type=F32)
    return d(hi) + d(mid) + d(lo)


GROUP_TILE = 256


def _group_sum(x, pblk):
    hi = x.astype(BF16)
    lo = (x - hi.astype(F32)).astype(BF16)
    cols = []
    for b in range(x.shape[1] // GROUP_TILE):
        sl = slice(GROUP_TILE * b, GROUP_TILE * (b + 1))
        cols.append(jnp.dot(hi[:, sl], pblk, preferred_element_type=F32) + jnp.dot(lo[:, sl], pblk, preferred_element_type=F32))
    return cols[0] if len(cols) == 1 else jnp.concatenate(cols, axis=1)


def _matmul(a, b, *, mode, out_dtype, name, res=None, exact=False, tm=1024, tn=1024, tk=2048, b_col0=0, comm=()):
    M, K = a.shape
    N = b.shape[1] if mode == "nn" else b.shape[0]
    tm, tn, tk = _tile(M, tm, 8 if M < LANES else LANES), _tile(N, tn), _tile(K, tk)
    nk = K // tk
    dn = (((1,), (0,)), ((), ())) if mode == "nn" else (((1,), (1,)), ((), ()))

    def body(*refs):
        a_ref, b_ref = refs[0], refs[1]
        r_ref = refs[2] if res is not None else None
        o_ref = refs[3] if res is not None else refs[2]
        if exact:
            part = _dot3(a_ref[...], b_ref[...])
        else:
            part = lax.dot_general(a_ref[...].astype(BF16), b_ref[...].astype(BF16), dn, preferred_element_type=F32)

        def finish(total):
            if r_ref is not None:
                total = r_ref[...] + total
            o_ref[...] = total.astype(o_ref.dtype)

        if nk == 1:
            finish(part)
        else:
            acc = refs[-1]
            k = pl.program_id(2)

            @pl.when(k == 0)
            def _():
                acc[...] = part

            @pl.when(k > 0)
            def _():
                acc[...] += part

            @pl.when(k == nk - 1)
            def _():
                finish(acc[...])

    if mode == "nn":
        b_spec = pl.BlockSpec((tk, tn), lambda j, i, k: (k, j))
    else:
        assert b_col0 % tk == 0
        b_spec = pl.BlockSpec((tn, tk), lambda j, i, k: (j, k + b_col0 // tk))
    in_specs = [pl.BlockSpec((tm, tk), lambda j, i, k: (i, k)), b_spec]
    args = [a, b]
    if res is not None:
        in_specs.append(pl.BlockSpec((tm, tn), lambda j, i, k: (i, j)))
        args.append(res)
    return _call(
        body, grid=(N // tn, M // tm, nk), in_specs=in_specs,
        out_specs=[pl.BlockSpec((tm, tn), lambda j, i, k: (i, j))], out_shape=[jax.ShapeDtypeStruct((M, N), out_dtype)],
        scratch=[pltpu.VMEM((tm, tn), F32)] if nk > 1 else [], sem=("parallel", "parallel", "arbitrary"),
        args=args, name=name, comm=comm)


def _matmul_tn(a, b, *, out_dtype, name, tm=2048, tn=1024, tt=1024, comm=()):
    T, Mo = a.shape
    N = b.shape[1]
    tm, tn, tt = _tile(Mo, tm), _tile(N, tn), _tile(T, tt)
    nt = T // tt

    def body(a_ref, b_ref, o_ref, *scr):
        t = pl.program_id(2)
        part = lax.dot_general(a_ref[...].astype(BF16), b_ref[...].astype(BF16), (((0,), (0,)), ((), ())),
                               preferred_element_type=F32)
        if nt == 1:
            o_ref[...] = part.astype(o_ref.dtype)
            return
        (acc,) = scr

        @pl.when(t == 0)
        def _():
            acc[...] = part

        @pl.when(t > 0)
        def _():
            acc[...] += part

        @pl.when(t == nt - 1)
        def _():
            o_ref[...] = acc[...].astype(o_ref.dtype)

    return _call(
        body, grid=(Mo // tm, N // tn, nt),
        in_specs=[pl.BlockSpec((tt, tm), lambda i, j, t: (t, i)), pl.BlockSpec((tt, tn), lambda i, j, t: (t, j))],
        out_specs=[pl.BlockSpec((tm, tn), lambda i, j, t: (i, j))], out_shape=[jax.ShapeDtypeStruct((Mo, N), out_dtype)],
        scratch=[pltpu.VMEM((tm, tn), F32)] if nt > 1 else [], sem=("parallel", "parallel", "arbitrary"), args=[a, b],
        name=name, comm=comm)


def _rms_fwd(x, g, *, name, tr=512):
    R, D = x.shape
    tr = _tile(R, tr, 8)

    def body(x_ref, g_ref, o_ref):
        xv = x_ref[...]
        r = lax.rsqrt(jnp.mean(xv * xv, axis=-1, keepdims=True) + EPS)
        o_ref[...] = (xv * r * g_ref[...]).astype(o_ref.dtype)

    return pl.pallas_call(
        body, out_shape=jax.ShapeDtypeStruct((R, D), BF16), grid=(R // tr,),
        in_specs=[pl.BlockSpec((tr, D), lambda i: (i, 0)), pl.BlockSpec((1, D), lambda i: (0, 0))],
        out_specs=pl.BlockSpec((tr, D), lambda i: (i, 0)),
        compiler_params=_cparams(("parallel",)), name=name,
    )(x, g)


def _rms_bwd(x, g, dh, dres, *, name, tr=512):
    R, D = x.shape
    tr = _tile(R, tr, 8)

    def body(x_ref, g_ref, dh_ref, dres_ref, dx_ref, dxb_ref, dg_ref):
        i = pl.program_id(0)
        xv = x_ref[...]
        r = lax.rsqrt(jnp.mean(xv * xv, axis=-1, keepdims=True) + EPS)
        y = xv * r
        dhv = dh_ref[...]
        dy = dhv * g_ref[...]
        dx = dres_ref[...] + r * (dy - y * jnp.mean(dy * y, axis=-1, keepdims=True))
        dx_ref[...] = dx
        dxb_ref[...] = dx.astype(BF16)
        dg = jnp.sum(dhv * y, axis=0, keepdims=True)

        @pl.when(i == 0)
        def _():
            dg_ref[...] = dg

        @pl.when(i > 0)
        def _():
            dg_ref[...] += dg

    row = pl.BlockSpec((tr, D), lambda i: (i, 0))
    vec = pl.BlockSpec((1, D), lambda i: (0, 0))
    return pl.pallas_call(
        body, out_shape=(jax.ShapeDtypeStruct((R, D), F32), jax.ShapeDtypeStruct((R, D), BF16),
                         jax.ShapeDtypeStruct((1, D), F32)), grid=(R // tr,),
        in_specs=[row, vec, row, row], out_specs=(row, row, vec),
        compiler_params=_cparams(("arbitrary",)), name=name,
    )(x, g, dh, dres)


def _sigmoid(x):
    return 0.5 * (jnp.tanh(0.5 * x) + 1.0)


def _gate_up_swiglu(h, w_gu, n_ff, *, name, tm=512, comm=()):
    M, K = h.shape
    nsh = w_gu.shape[1] // (2 * n_ff)
    tm = _tile(M, tm)

    def body(h_ref, w_ref, ab_ref, f_ref):
        for c in range(tm // _tile(tm, ROW_CHUNK)):
            rows = pl.ds(c * _tile(tm, ROW_CHUNK), _tile(tm, ROW_CHUNK))
            ab = jnp.dot(h_ref[rows, :], w_ref[...], preferred_element_type=F32)
            ab_ref[rows, :] = ab.astype(ab_ref.dtype)
            a, b = ab[:, :n_ff], ab[:, n_ff:]
            f_ref[rows, :] = (a * _sigmoid(a) * b).astype(f_ref.dtype)

    return _call(
        body, grid=(nsh, M // tm),
        in_specs=[pl.BlockSpec((tm, K), lambda j, i: (i, 0)), pl.BlockSpec((K, 2 * n_ff), lambda j, i: (0, j))],
        out_specs=[pl.BlockSpec((tm, 2 * n_ff), lambda j, i: (i, j)), pl.BlockSpec((tm, n_ff), lambda j, i: (i, j))],
        out_shape=[jax.ShapeDtypeStruct((M, 2 * nsh * n_ff), BF16), jax.ShapeDtypeStruct((M, nsh * n_ff), BF16)],
        sem=("parallel", "parallel"), args=[h, w_gu], name=name, comm=comm)


def _down_dx_swiglu_bwd(dy, w_d, ab, n_ff, *, name, tm=512, shards=2, comm=()):
    M, K = dy.shape
    nsh = w_d.shape[0] // n_ff
    tm = _tile(M, tm)
    shards = shards if nsh % shards == 0 else 1

    def body(dy_ref, w_ref, ab_ref, dab_ref):
        for c in range(tm // _tile(tm, ROW_CHUNK)):
            rows = pl.ds(c * _tile(tm, ROW_CHUNK), _tile(tm, ROW_CHUNK))
            df_all = lax.dot_general(dy_ref[rows, :], w_ref[...], (((1,), (1,)), ((), ())), preferred_element_type=F32)
            for s in range(shards):
                df = df_all[:, s * n_ff:(s + 1) * n_ff]
                a_cols = slice(2 * s * n_ff, (2 * s + 1) * n_ff)
                b_cols = slice((2 * s + 1) * n_ff, (2 * s + 2) * n_ff)
                av = ab_ref[rows, a_cols].astype(F32)
                sg = _sigmoid(av)
                dab_ref[rows, a_cols] = (df * ab_ref[rows, b_cols].astype(F32) * (sg * (1.0 + av * (1.0 - sg)))).astype(dab_ref.dtype)
                dab_ref[rows, b_cols] = (df * (av * sg)).astype(dab_ref.dtype)

    cols = pl.BlockSpec((tm, 2 * shards * n_ff), lambda j, i: (i, j))
    return _call(
        body, grid=(nsh // shards, M // tm),
        in_specs=[pl.BlockSpec((tm, K), lambda j, i: (i, 0)), pl.BlockSpec((shards * n_ff, K), lambda j, i: (j, 0)), cols],
        out_specs=[cols], out_shape=[jax.ShapeDtypeStruct(ab.shape, BF16)],
        sem=("parallel", "parallel"), args=[dy, w_d, ab], name=name, comm=comm)


def _loss_head(y, target, *, name, tr=256):
    R, D = y.shape
    tr = _tile(R, tr, 8)

    def body(y_ref, t_ref, dy_ref, dyb_ref, l_ref):
        i = pl.program_id(0)
        e = y_ref[...] - t_ref[...]
        dy = e * (1.0 / D)
        dy_ref[...] = dy
        dyb_ref[...] = dy.astype(BF16)
        part = 0.5 * jnp.sum(jnp.mean(e * e, axis=-1, keepdims=True), axis=0, keepdims=True)
        part = jnp.broadcast_to(part, (8, LANES))

        @pl.when(i == 0)
        def _():
            l_ref[...] = part

        @pl.when(i > 0)
        def _():
            l_ref[...] += part

    row = pl.BlockSpec((tr, D), lambda i: (i, 0))
    return pl.pallas_call(
        body, out_shape=(jax.ShapeDtypeStruct((R, D), F32), jax.ShapeDtypeStruct((R, D), BF16),
                         jax.ShapeDtypeStruct((8, LANES), F32)), grid=(R // tr,),
        in_specs=[row, row], out_specs=(row, row, pl.BlockSpec((8, LANES), lambda i: (0, 0))),
        compiler_params=_cparams(("arbitrary",)), name=name,
    )(y, target)


def _gelu(x):
    return 0.5 * x * (1.0 + lax.erf(x * math.sqrt(0.5)))


def _gelu_grad(x):
    return 0.5 * (1.0 + lax.erf(x * math.sqrt(0.5))) + x * jnp.exp(-0.5 * x * x) * (1.0 / math.sqrt(2.0 * math.pi))


def _group_consts(width):
    lane = np.arange(width)
    col = np.arange(LANES)
    grp = (lane[:, None] // HEAD_DIM == col[None, :]).astype(np.float32)
    mod = ((lane[:, None] % HEAD_DIM == col[None, :]) & (col[None, :] < HEAD_DIM)).astype(np.float32)
    return jnp.asarray(grp, BF16), jnp.asarray(mod, BF16)


def _same_group():
    lane = np.arange(GROUP_TILE) // HEAD_DIM
    return jnp.asarray((lane[:, None] == lane[None, :]).astype(np.float32), BF16)


def _bucket_onehot():
    qi = np.arange(BLOCK)[:, None]
    kj = np.arange(2 * BLOCK)[None, :]
    n = np.maximum(qi + BLOCK - kj, 0)
    max_exact = NUM_BUCKETS // 2
    nf = np.maximum(n, 1).astype(np.float32)
    large = max_exact + (np.log(nf / np.float32(max_exact)) / np.float32(math.log(MAX_DISTANCE / max_exact))
                         * np.float32(NUM_BUCKETS - max_exact)).astype(np.int32)
    large = np.minimum(large, NUM_BUCKETS - 1)
    bucket = jnp.asarray(np.where(n < max_exact, n, large).reshape(-1).astype(np.int32))
    return (bucket[:, None] == jnp.arange(LANES, dtype=jnp.int32)[None, :]).astype(BF16)


class _MixerDims:
    def __init__(self, S, IN, SW, AW, KVW, H, NQ):
        self.S, self.IN, self.SW, self.AW, self.KVW, self.H, self.NQ = S, IN, SW, AW, KVW, H, NQ
        self.NKV = KVW // HEAD_DIM
        self.GROUP = NQ // self.NKV
        self.nb = S // BLOCK
        self.koff = 2 * SW + AW
        self.voff = self.koff + KVW
        assert SW % LANES == 0 and AW % LANES == 0 and KVW % LANES == 0 and self.GROUP % 2 == 0
        assert self.koff % (2 * KVW) == 0 and IN == self.voff + KVW and S % BLOCK == 0


def _mixer_block(dm, n, z, kvp, prm):
    SW, AW, KVW, H, NQ = dm.SW, dm.AW, dm.KVW, dm.H, dm.NQ
    lane = lax.broadcasted_iota(jnp.int32, (BLOCK, LANES), 1)
    lo = lane < HEAD_DIM
    row = lax.broadcasted_iota(jnp.int32, (BLOCK, BLOCK), 0)
    col = lax.broadcasted_iota(jnp.int32, (BLOCK, BLOCK), 1)
    tril = row >= col
    pblk = prm["pblk"][...]
    inv = 1.0 / HEAD_DIM

    def group_rsqrt(x):
        return lax.rsqrt(_group_sum(x * x, pblk) * inv + EPS)

    zu, zv = z[:, :SW], z[:, SW:2 * SW]
    u, v = _gelu(zu), _gelu(zv)
    rv = group_rsqrt(v)
    vn = v * rv * prm["sgu_g"][...]
    vnb = vn.astype(BF16)
    tmats, gate_blocks = [], []
    for p in range(H // 2):
        blk = slice(LANES * p, LANES * (p + 1))
        t0 = jnp.where(tril, prm["sgu_w"][2 * p], 0.0).astype(BF16)
        t1 = jnp.where(tril, prm["sgu_w"][2 * p + 1], 0.0).astype(BF16)
        tmats += [t0, t1]
        g0 = jnp.dot(t0, vnb[:, blk], preferred_element_type=F32)
        g1 = jnp.dot(t1, vnb[:, blk], preferred_element_type=F32)
        gate_blocks.append(jnp.where(lo, g0, g1) + prm["sgu_bias"][:, blk])
    gate = jnp.concatenate(gate_blocks, axis=1)
    outa = u * gate
    ra = lax.rsqrt(jnp.mean(outa * outa, axis=-1, keepdims=True) + EPS)

    q = z[:, 2 * SW:2 * SW + AW]
    kcat = jnp.concatenate([kvp[:, :KVW], z[:, dm.koff:dm.koff + KVW]], axis=0)
    vcat = jnp.concatenate([kvp[:, KVW:], z[:, dm.voff:dm.voff + KVW]], axis=0)
    rq = group_rsqrt(q)
    rk = group_rsqrt(kcat)
    qn = q * rq * prm["q_g"][...]
    kn = kcat * rk * prm["k_g"][...]
    knb, vcb = kn.astype(BF16), vcat.astype(BF16)
    qi = lax.broadcasted_iota(jnp.int32, (BLOCK, 2 * BLOCK), 0)
    kj = lax.broadcasted_iota(jnp.int32, (BLOCK, 2 * BLOCK), 1)
    valid = (kj > qi) & (kj <= qi + BLOCK) & ((n > 0) | (kj >= BLOCK))
    scale = 1.0 / math.sqrt(HEAD_DIM)
    heads = []
    out_blocks = []
    for hq in range(NQ):
        mb, e = hq // 2, hq % 2
        kv = hq // dm.GROUP
        kb, ek = kv // 2, kv % 2
        qblk = qn[:, LANES * mb:LANES * (mb + 1)]
        if e != ek:
            qblk = pltpu.roll(qblk, HEAD_DIM, 1)
        half = lo if ek == 0 else jnp.logical_not(lo)
        qm = jnp.where(half, qblk, 0.0).astype(BF16)
        kblk = knb[:, LANES * kb:LANES * (kb + 1)]
        vblk = vcb[:, LANES * kb:LANES * (kb + 1)]
        s = lax.dot_general(qm, kblk, (((1,), (1,)), ((), ())), preferred_element_type=F32) * scale + prm["bias"][hq]
        s = jnp.where(valid, s, NEG_INF)
        sink = prm["sinks"][hq]
        mx = jnp.maximum(jnp.max(s, axis=-1, keepdims=True), sink)
        ex = jnp.exp(s - mx)
        den = jnp.sum(ex, axis=-1, keepdims=True) + jnp.exp(sink - mx)
        inv_den = 1.0 / den
        pr = ex * inv_den
        psink = jnp.exp(sink - mx) * inv_den
        prb = pr.astype(BF16)
        r_h = jnp.dot(prb, vblk, preferred_element_type=F32)
        if e != ek:
            r_h = pltpu.roll(r_h, HEAD_DIM, 1)
        heads.append(dict(qm=qm, kblk=kblk, vblk=vblk, pr=pr, prb=prb, psink=psink, half=half, mb=mb, e=e, ek=ek, kb=kb))
        if e == 1:
            out_blocks.append(jnp.where(lo, prev_r, r_h))
        prev_r = r_h
    outb = jnp.concatenate(out_blocks, axis=1)
    rb = lax.rsqrt(jnp.mean(outb * outb, axis=-1, keepdims=True) + EPS)
    return dict(lo=lo, tril=tril, pblk=pblk, zu=zu, zv=zv, u=u, v=v, rv=rv, vnb=vnb, tmats=tmats, gate=gate,
                outa=outa, ra=ra, q=q, kcat=kcat, rq=rq, rk=rk, heads=heads, outb=outb, rb=rb, scale=scale)


_MIXER_PARAMS = ("sgu_g", "sgu_w", "sgu_bias", "norm_a", "q_g", "k_g", "norm_b", "sinks", "bias", "pblk")


def _mixer_param_specs(dm, idx):
    SW, AW, KVW = dm.SW, dm.AW, dm.KVW
    full = lambda shape: pl.BlockSpec(shape, lambda n: (0,) * len(shape))
    return [full((1, SW)), full((dm.H, BLOCK, BLOCK)), full((BLOCK, SW)), full((1, SW)), full((1, AW)), full((1, KVW)),
            full((1, AW)), pl.BlockSpec(memory_space=pltpu.SMEM), full((dm.NQ, BLOCK, 2 * BLOCK)),
            full((GROUP_TILE, GROUP_TILE))]


def _mixer_fwd(dm, z, params, *, name, comm=()):
    nb = dm.nb

    def body(z_ref, kvp_ref, *rest):
        prm = dict(zip(_MIXER_PARAMS, rest[:len(_MIXER_PARAMS)]))
        o_ref = rest[len(_MIXER_PARAMS)]
        n = pl.program_id(0)
        c = _mixer_block(dm, n, z_ref[...], kvp_ref[...], prm)
        o_ref[:, :dm.SW] = (c["outa"] * c["ra"] * prm["norm_a"][...]).astype(o_ref.dtype)
        o_ref[:, dm.SW:] = (c["outb"] * c["rb"] * prm["norm_b"][...]).astype(o_ref.dtype)

    kvblk = dm.koff // (2 * dm.KVW)
    in_specs = [pl.BlockSpec((BLOCK, dm.IN), lambda n: (n, 0)),
                pl.BlockSpec((BLOCK, 2 * dm.KVW), lambda n: (jnp.maximum(n - 1, 0), kvblk))] + _mixer_param_specs(dm, None)
    return _call(
        body, out_shape=[jax.ShapeDtypeStruct((dm.S, dm.SW + dm.AW), BF16)], grid=(nb,),
        in_specs=in_specs, out_specs=[pl.BlockSpec((BLOCK, dm.SW + dm.AW), lambda n: (n, 0))],
        sem=("arbitrary",), args=[z, z, *params], name=name, comm=comm)


def _mixer_bwd(dm, z, dmixed, dbias_in, params, gmats, *, name, comm=()):
    SW, AW, KVW, H, NQ, nb, IN = dm.SW, dm.AW, dm.KVW, dm.H, dm.NQ, dm.nb, dm.IN
    QW = 2 * SW + AW
    NP = len(_MIXER_PARAMS)

    def body(z_ref, kvp_ref, dm_ref, dbin_ref, *rest):
        prm = dict(zip(_MIXER_PARAMS, rest[:NP]))
        gs_ref, gmq_ref, gmk_ref = rest[NP:NP + 3]
        (dz_ref, dsg_ref, dt_ref, dsb_ref, dna_ref, dqg_ref, dkg_ref, dnb_ref, dsk_ref, dbias_ref) = rest[NP + 3:NP + 13]
        hold, tmpkv, newkv, carry, accb, accq, acck = rest[NP + 13:]
        n = pl.program_id(0)

        @pl.when(n == 0)
        def _():
            for r in (dsg_ref, dt_ref, dna_ref, dnb_ref, dsk_ref, accb, accq, acck):
                r[...] = jnp.zeros(r.shape, r.dtype)
            dbias_ref[...] = dbin_ref[...]

        @pl.when(n < nb)
        def _():
            c = _mixer_block(dm, n, z_ref[...], kvp_ref[...], prm)
            lo = c["lo"]
            dmx = dm_ref[...]
            inv = 1.0 / HEAD_DIM

            def rms_bwd_full(dy_scaled, y, r):
                return r * (dy_scaled - y * jnp.mean(dy_scaled * y, axis=-1, keepdims=True))

            def group_mean_b(x):
                return _group_sum(x, c["pblk"]) * inv

            dma = dmx[:, :SW]
            ya = c["outa"] * c["ra"]
            dna_ref[...] += jnp.sum(dma * ya, axis=0, keepdims=True)
            douta = rms_bwd_full(dma * prm["norm_a"][...], ya, c["ra"])
            du = douta * c["gate"]
            dgate = douta * c["u"]
            accb[...] += dgate
            dgb16 = dgate.astype(BF16)
            dvn_blocks = []
            for p in range(H // 2):
                blk = slice(LANES * p, LANES * (p + 1))
                dg = dgate[:, blk]
                d0 = jnp.where(lo, dg, 0.0).astype(BF16)
                d1 = jnp.where(lo, 0.0, dg).astype(BF16)
                vb = c["vnb"][:, blk]
                nt = lambda a, b: lax.dot_general(a, b, (((1,), (1,)), ((), ())), preferred_element_type=F32)
                tn = lambda a, b: lax.dot_general(a, b, (((0,), (0,)), ((), ())), preferred_element_type=F32)
                dt_ref[2 * p] += nt(d0, vb)
                dt_ref[2 * p + 1] += nt(d1, vb)
                dvn_blocks.append(jnp.where(lo, tn(c["tmats"][2 * p], dgb16[:, blk]), tn(c["tmats"][2 * p + 1], dgb16[:, blk])))
            dvn = jnp.concatenate(dvn_blocks, axis=1)
            yv = c["v"] * c["rv"]
            dsg_ref[...] += jnp.sum(dvn * yv, axis=0, keepdims=True)
            dyv = dvn * prm["sgu_g"][...]
            dv = c["rv"] * (dyv - yv * group_mean_b(dyv * yv))
            dzu = du * _gelu_grad(c["zu"])
            dzv = dv * _gelu_grad(c["zv"])

            dmb = dmx[:, SW:]
            yb = c["outb"] * c["rb"]
            dnb_ref[...] += jnp.sum(dmb * yb, axis=0, keepdims=True)
            doutb = rms_bwd_full(dmb * prm["norm_b"][...], yb, c["rb"])
            lane1 = lax.broadcasted_iota(jnp.int32, (1, LANES), 1)
            dqn_blocks = [None] * (AW // LANES)
            dkn_blocks = [None] * (KVW // LANES)
            dvc_blocks = [None] * (KVW // LANES)
            dsink_vec = jnp.zeros((1, LANES), F32)
            add = lambda old, new: new if old is None else old + new
            for hq, hd in enumerate(c["heads"]):
                mb, e, ek, kb, half = hd["mb"], hd["e"], hd["ek"], hd["kb"], hd["half"]
                dr = doutb[:, LANES * mb:LANES * (mb + 1)]
                if e != ek:
                    dr = pltpu.roll(dr, HEAD_DIM, 1)
                drm = jnp.where(half, dr, 0.0).astype(BF16)
                dp = lax.dot_general(drm, hd["vblk"], (((1,), (1,)), ((), ())), preferred_element_type=F32)
                dvc_blocks[kb] = add(dvc_blocks[kb], lax.dot_general(hd["prb"], drm, (((0,), (0,)), ((), ())),
                                                                     preferred_element_type=F32))
                rowdot = jnp.sum(hd["pr"] * dp, axis=-1, keepdims=True)
                ds = hd["pr"] * (dp - rowdot)
                dsink = jnp.sum(-hd["psink"] * rowdot, axis=0, keepdims=True)
                dsink_vec = dsink_vec + jnp.where(lane1 == hq, dsink, 0.0)
                dbias_ref[hq] += ds
                dsb = (ds * c["scale"]).astype(BF16)
                dqm = jnp.dot(dsb, hd["kblk"], preferred_element_type=F32)
                dqm = jnp.where(half, dqm, 0.0)
                if e != ek:
                    dqm = pltpu.roll(dqm, HEAD_DIM, 1)
                dqn_blocks[mb] = add(dqn_blocks[mb], dqm)
                dkn_blocks[kb] = add(dkn_blocks[kb], lax.dot_general(dsb, hd["qm"], (((0,), (0,)), ((), ())),
                                                                     preferred_element_type=F32))
            dsk_ref[...] += dsink_vec
            dqn = jnp.concatenate(dqn_blocks, axis=1)
            dkn = jnp.concatenate(dkn_blocks, axis=1)
            dvc = jnp.concatenate(dvc_blocks, axis=1)
            yq = c["q"] * c["rq"]
            accq[...] += jnp.sum(dqn * yq, axis=0, keepdims=True)
            dyq = dqn * prm["q_g"][...]
            dq = c["rq"] * (dyq - yq * group_mean_b(dyq * yq))
            yk = c["kcat"] * c["rk"]
            acck[...] += jnp.sum(dkn * yk, axis=0, keepdims=True)
            dyk = dkn * prm["k_g"][...]
            dk = c["rk"] * (dyk - yk * group_mean_b(dyk * yk))

            slot = n % 2
            hold[slot, :, :SW] = dzu
            hold[slot, :, SW:2 * SW] = dzv
            hold[slot, :, 2 * SW:] = dq
            tmpkv[:, :KVW] = dk[:BLOCK]
            tmpkv[:, KVW:] = dvc[:BLOCK]
            newkv[:, :KVW] = dk[BLOCK:]
            newkv[:, KVW:] = dvc[BLOCK:]

        @pl.when(n >= 1)
        def _():
            dz_ref[:, :QW] = hold[(n - 1) % 2].astype(dz_ref.dtype)

        @pl.when((n >= 1) & (n < nb))
        def _():
            dz_ref[:, QW:] = (carry[...] + tmpkv[...]).astype(dz_ref.dtype)

        @pl.when(n == nb)
        def _():
            dz_ref[:, QW:] = carry[...].astype(dz_ref.dtype)
            row = lax.broadcasted_iota(jnp.int32, (BLOCK, BLOCK), 0)
            col = lax.broadcasted_iota(jnp.int32, (BLOCK, BLOCK), 1)
            for h in range(H):
                dt_ref[h] = jnp.where(row >= col, dt_ref[h], 0.0)
            dsb_ref[...] = _dot3(accb[...], gs_ref[...])
            dqg_ref[...] = _dot3(accq[...], gmq_ref[...])
            dkg_ref[...] = _dot3(acck[...], gmk_ref[...])

        @pl.when(n < nb)
        def _():
            carry[...] = newkv[...]

    kvblk = dm.koff // (2 * KVW)
    clamp = lambda n: jnp.minimum(n, nb - 1)
    full = lambda shape: pl.BlockSpec(shape, lambda n: (0,) * len(shape))
    in_specs = [pl.BlockSpec((BLOCK, IN), lambda n: (clamp(n), 0)),
                pl.BlockSpec((BLOCK, 2 * KVW), lambda n: (jnp.maximum(clamp(n) - 1, 0), kvblk)),
                pl.BlockSpec((BLOCK, SW + AW), lambda n: (clamp(n), 0)),
                full((NQ, BLOCK, 2 * BLOCK))] + _mixer_param_specs(dm, None) + [full((SW, LANES)), full((AW, LANES)),
                                                                                full((KVW, LANES))]
    out_shape = (jax.ShapeDtypeStruct((dm.S, IN), BF16),
                 jax.ShapeDtypeStruct((1, SW), F32), jax.ShapeDtypeStruct((H, BLOCK, BLOCK), F32),
                 jax.ShapeDtypeStruct((BLOCK, LANES), F32), jax.ShapeDtypeStruct((1, SW), F32),
                 jax.ShapeDtypeStruct((1, LANES), F32), jax.ShapeDtypeStruct((1, LANES), F32),
                 jax.ShapeDtypeStruct((1, AW), F32), jax.ShapeDtypeStruct((1, LANES), F32),
                 jax.ShapeDtypeStruct((NQ, BLOCK, 2 * BLOCK), F32))
    out_specs = (pl.BlockSpec((BLOCK, IN), lambda n: (jnp.maximum(n - 1, 0), 0)),
                 full((1, SW)), full((H, BLOCK, BLOCK)), full((BLOCK, LANES)), full((1, SW)), full((1, LANES)),
                 full((1, LANES)), full((1, AW)), full((1, LANES)), full((NQ, BLOCK, 2 * BLOCK)))
    scratch = [pltpu.VMEM((2, BLOCK, QW), F32), pltpu.VMEM((BLOCK, 2 * KVW), F32), pltpu.VMEM((BLOCK, 2 * KVW), F32),
               pltpu.VMEM((BLOCK, 2 * KVW), F32), pltpu.VMEM((BLOCK, SW), F32), pltpu.VMEM((1, AW), F32),
               pltpu.VMEM((1, KVW), F32)]
    return _call(
        body, out_shape=out_shape, grid=(nb + 1,), in_specs=in_specs, out_specs=out_specs, scratch=scratch,
        sem=("arbitrary",), args=[z, z, dmixed, dbias_in, *params, *gmats], name=name, comm=comm)


def _adamw(w, gparts, m, v, *, name, layer=0, stacked=None, tr=256):
    R, C = gparts[0].shape
    tr = _tile(R, max(8, min(tr, (1 << 18) // C)), 8)
    ng = len(gparts)
    bc1 = 1.0 - ADAM_B1 ** ADAM_STEP
    bc2 = 1.0 - ADAM_B2 ** ADAM_STEP

    def body(w_ref, *rest):
        g_refs, (m_ref, v_ref), (go_ref, d_ref, mo_ref, vo_ref) = rest[:ng], rest[ng:ng + 2], rest[-4:]
        g = g_refs[0][...].astype(F32)
        for r in g_refs[1:]:
            g = g + r[...].astype(F32)
        mn = ADAM_B1 * m_ref[...] + (1.0 - ADAM_B1) * g
        vn = ADAM_B2 * v_ref[...] + (1.0 - ADAM_B2) * jnp.square(g)
        m_hat = mn / bc1
        v_hat = vn / bc2
        go_ref[...] = g
        d_ref[...] = -ADAM_LR * (m_hat / (jnp.sqrt(v_hat) + ADAM_EPS) + ADAM_WD * w_ref[...])
        mo_ref[...] = mn
        vo_ref[...] = vn

    here = pl.BlockSpec((tr, C), lambda i: (layer * (R // tr) + i, 0))
    blk = pl.BlockSpec((tr, C), lambda i: (i, 0))
    prev = [] if stacked is None else list(stacked)
    return _call(body, out_shape=[jax.ShapeDtypeStruct(w.shape, F32)] * 4, grid=(R // tr,),
                 in_specs=[here] + [blk] * ng + [here, here] + [ANY] * len(prev), out_specs=[here] * 4,
                 sem=("parallel",), args=[w, *gparts, m, v, *prev], name=name,
                 aliases={3 + ng + q: q for q in range(len(prev))})


def _sum_chip_partials(g, axis, base, width, stride, got, chip, *, name, tr=256, comm=()):
    _, R, C = got.shape
    tr = _tile(R, max(8, min(tr, (1 << 18) // C)), 8)
    assert base % width == 0 and stride % width == 0 and (C == width if axis == 1 else R == width)

    def body(chip_ref, own_ref, g0_ref, g1_ref, g2_ref, o_ref):
        o_ref[...] = ((own_ref[...].astype(F32) + g0_ref[...].astype(F32)) + g1_ref[...].astype(F32)) + g2_ref[...].astype(F32)

    band = lambda j: base // width + j[0] * (stride // width)
    if axis == 1:
        own_spec = pl.BlockSpec((tr, C), lambda i, j: (i, band(j)))
    else:
        own_spec = pl.BlockSpec((tr, C), lambda i, j: (band(j) * (R // tr) + i, 0))
    part = lambda k: pl.BlockSpec((None, tr, C), lambda i, j: (k, i, 0))
    return _call(body, out_shape=[jax.ShapeDtypeStruct((R, C), F32)], grid=(R // tr,),
                 in_specs=[own_spec, part(0), part(1), part(2)], out_specs=[pl.BlockSpec((tr, C), lambda i, j: (i, 0))],
                 sem=("parallel",), args=[g, got, got, got], name=name, prefetch=chip, comm=comm)


def _pack(arrays):
    parts, total = [], 0
    for a in arrays:
        flat = a.reshape(-1).astype(F32)
        pad = (-flat.shape[0]) % (8 * LANES)
        parts.append(jnp.pad(flat, (0, pad)))
        total += flat.shape[0] + pad
    parts.append(jnp.zeros(((-total) % (PACK_ROWS * LANES),), F32))
    return jnp.concatenate(parts).reshape(-1, LANES)


def _unpack(packed, like):
    flat = packed.reshape(-1)
    out, off = [], 0
    for a in like:
        n = int(np.prod(a.shape))
        out.append(flat[off:off + n].reshape(a.shape))
        off += n + ((-n) % (8 * LANES))
    return out


def kernel(x, rel_bias, norm1_g, w_in, sgu_norm_g, sgu_w, sgu_b, q_norm_g, k_norm_g, sinks, out_norm_a, out_norm_b, w_out, norm2_g, w_gate, w_up, w_down, loss_target, m_rel_bias, m_norm1_g, m_w_in, m_sgu_norm_g, m_sgu_w, m_sgu_b, m_q_norm_g, m_k_norm_g, m_sinks, m_out_norm_a, m_out_norm_b, m_w_out, m_norm2_g, m_w_gate, m_w_up, m_w_down, v_rel_bias, v_norm1_g, v_w_in, v_sgu_norm_g, v_sgu_w, v_sgu_b, v_q_norm_g, v_k_norm_g, v_sinks, v_out_norm_a, v_out_norm_b, v_w_out, v_norm2_g, v_w_gate, v_w_up, v_w_down):
    L, D, n_in = w_in.shape
    S = x.shape[1]
    IN = N_CHIPS * n_in
    n_ff = w_gate.shape[2]
    FF = N_CHIPS * n_ff
    H = sgu_w.shape[1]
    NQ = sinks.shape[1]
    SW, AW = H * HEAD_DIM, NQ * HEAD_DIM
    KVW = (IN - 2 * SW - AW) // 2
    dm = _MixerDims(S, IN, SW, AW, KVW, H, NQ)
    assert sgu_w.shape[2] == BLOCK and q_norm_g.shape[1] == HEAD_DIM and SW + AW == D

    wb = {k: w.astype(BF16) for k, w in (("in", w_in), ("out", w_out), ("gate", w_gate), ("up", w_up), ("down", w_down))}

    def gather(l, which):
        if which == "in":
            return _Gather([wb["in"]], [(0, 0, 1, 0, n_in)], [(D, IN)], l)
        if which == "out":
            return _Gather([wb["out"]], [(0, 0, 0, 0, D // N_CHIPS)], [(D, D)], l)
        if which == "gu":
            return _Gather([wb["gate"], wb["up"]], [(0, 0, 1, 0, 2 * n_ff), (1, 0, 1, n_ff, 2 * n_ff)], [(D, 2 * FF)], l)
        return _Gather([wb["down"]], [(0, 0, 0, 0, n_ff)], [(FF, D)], l)

    nxt = lambda l, *which: [gather(l + 1, w) for w in which] if l + 1 < L else []
    W = {}
    (W[0, "in"],) = _comm_only([gather(0, "in")], name="gather_first_weights")

    gs, gmod_q = _group_consts(SW)
    _, gmod_k = _group_consts(KVW)
    pblk = _same_group()
    onehot = _bucket_onehot()
    rbt = jnp.pad(rel_bias.T, ((0, 0), (0, LANES - NUM_BUCKETS)))
    (bias,) = _matmul(rbt, onehot.T, mode="nn", out_dtype=F32, exact=True, tn=4096, name="bias_table")
    bias = bias.reshape(NQ, BLOCK, 2 * BLOCK)

    def mixer_params(l):
        return [sgu_norm_g[l].reshape(1, SW), sgu_w[l], jnp.repeat(sgu_b[l].T, HEAD_DIM, axis=1),
                out_norm_a[l].reshape(1, SW), jnp.tile(q_norm_g[l], NQ).reshape(1, AW),
                jnp.tile(k_norm_g[l], dm.NKV).reshape(1, KVW), out_norm_b[l].reshape(1, AW), sinks[l], bias, pblk]

    xs = x.reshape(S, D)
    saved = []
    for l in range(L):
        h = _rms_fwd(xs, norm1_g[l].reshape(1, D), name="norm1_fwd")
        z, *got = _matmul(h, W[l, "in"], mode="nn", out_dtype=F32, tn=1792, name="in_proj",
                          comm=[gather(0, "out"), gather(0, "down")] if l == 0 else [])
        if l == 0:
            W[0, "out"], W[0, "down"] = got
        mixed, *got = _mixer_fwd(dm, z, mixer_params(l), name="mixer_fwd",
                                 comm=[gather(0, "gu")] if l == 0 else nxt(l, "in", "out"))
        if l == 0:
            (W[0, "gu"],) = got
        elif got:
            W[l + 1, "in"], W[l + 1, "out"] = got
        (x1,) = _matmul(mixed, W[l, "out"], mode="nn", out_dtype=F32, res=xs, name="out_proj")
        h2 = _rms_fwd(x1, norm2_g[l].reshape(1, D), name="norm2_fwd")
        ab, f, *got = _gate_up_swiglu(h2, W[l, "gu"], n_ff, name="gate_up_swiglu",
                                      comm=nxt(l, "gu") + (nxt(l, "in") if l == 0 else []))
        if got:
            W[l + 1, "gu"] = got.pop(0)
        if got:
            (W[l + 1, "in"],) = got
        x2, *got = _matmul(f, W[l, "down"], mode="nn", out_dtype=F32, res=x1, tm=512, tk=5632, name="down_proj",
                           comm=nxt(l, "down") + (nxt(l, "out") if l == 0 else []))
        if got:
            W[l + 1, "down"] = got.pop(0)
        if got:
            (W[l + 1, "out"],) = got
        saved.append((xs, h, z, mixed, x1, h2, ab, f))
        xs = x2

    dx, dxb, loss_part = _loss_head(xs, loss_target.reshape(S, D), name="loss_head")
    loss = lax.psum(loss_part[0, 0], ("x", "y", "c"))

    dbias = jnp.zeros((NQ, BLOCK, 2 * BLOCK), F32)
    n_out = D // N_CHIPS
    big_names = ("w_in", "w_out", "w_gate", "w_up", "w_down")
    own = {k: [None] * L for k in big_names}
    got = {k: [None] * L for k in big_names}
    names_layer = ("norm1_g", "sgu_norm_g", "sgu_w", "sgu_b", "q_norm_g", "k_norm_g", "sinks", "out_norm_a", "out_norm_b",
                   "norm2_g")
    small = {k: [None] * L for k in names_layer}
    small_unit = lambda layers: _AllGather(_pack([jnp.stack([small[k][i] for i in layers]) for k in names_layer]))
    for l in reversed(range(L)):
        xl, h, z, mixed, x1, h2, ab, f = saved[l]
        early = [small_unit(range(1, L))] if (l == 0 and L > 1) else []
        dab, *gathered_early = _down_dx_swiglu_bwd(dxb, W[l, "down"], ab, n_ff, name="down_dx_swiglu_bwd", comm=early)
        (g_d,) = _matmul_tn(f, dxb, out_dtype=BF16, tm=1408, tn=1024, tt=2048, name="down_proj_dw")
        g_gu, got["w_down"][l] = _matmul_tn(h2, dab, out_dtype=BF16, tm=512, tn=512, tt=S, name="gate_up_proj_dw",
                                            comm=[_Scatter([g_d], [(0, 0, 0, n_ff, n_ff)])])
        dh2, got["w_gate"][l] = _matmul(dab, W[l, "gu"], mode="nt", out_dtype=F32, tm=512, tn=512, tk=2 * FF,
                                        name="gate_up_proj_dx",
                                        comm=[_Scatter([g_gu], [(0, 1, 0, n_ff, 2 * n_ff)])])
        dx1, dx1b, dg2 = _rms_bwd(x1, norm2_g[l].reshape(1, D), dh2, dx, name="norm2_bwd")
        (dmixed,) = _matmul(dx1b, W[l, "out"], mode="nt", out_dtype=F32, name="out_proj_dx")
        (g_out,) = _matmul_tn(mixed, dx1b, out_dtype=BF16, tt=2048, name="out_proj_dw")
        (dz, d_sg, d_t, d_sb, d_na, d_qg, d_kg, d_nb, d_sk, dbias, got["w_up"][l], got["w_out"][l]) = _mixer_bwd(
            dm, z, dmixed, dbias, mixer_params(l), (gs, gmod_q, gmod_k), name="mixer_bwd",
            comm=[_Scatter([g_gu, g_out], [(0, 1, n_ff, n_ff, 2 * n_ff), (1, 0, 0, n_out, n_out)])])
        (g_in,) = _matmul_tn(h, dz, out_dtype=BF16, tn=896, tt=2048, name="in_proj_dw")
        dh, got["w_in"][l] = _matmul(dz, W[l, "in"], mode="nt", out_dtype=F32, tk=3584, name="in_proj_dx",
                                     comm=[_Scatter([g_in], [(0, 1, 0, n_in, n_in)])])
        dx, dxb, dg1 = _rms_bwd(xl, norm1_g[l].reshape(1, D), dh, dx1, name="norm1_bwd")
        own["w_in"][l], own["w_out"][l] = (g_in, 1, 0, n_in, n_in), (g_out, 0, 0, n_out, n_out)
        own["w_gate"][l], own["w_up"][l] = (g_gu, 1, 0, n_ff, 2 * n_ff), (g_gu, 1, n_ff, n_ff, 2 * n_ff)
        own["w_down"][l] = (g_d, 0, 0, n_ff, n_ff)
        small["norm1_g"][l] = dg1.reshape(D)
        small["norm2_g"][l] = dg2.reshape(D)
        small["sgu_norm_g"][l] = d_sg.reshape(H, HEAD_DIM)
        small["sgu_w"][l] = d_t
        small["sgu_b"][l] = d_sb[:, :H].T
        small["q_norm_g"][l] = d_qg[0, :HEAD_DIM]
        small["k_norm_g"][l] = d_kg[0, :HEAD_DIM]
        small["sinks"][l] = d_sk[0, :NQ]
        small["out_norm_a"][l] = d_na.reshape(SW)
        small["out_norm_b"][l] = d_nb.reshape(AW)
    grad_x = dx.reshape(1, S, D)
    (d_rb,) = _matmul(dbias.reshape(NQ, BLOCK * 2 * BLOCK), onehot, mode="nn", out_dtype=F32, exact=True, tk=4096,
                      name="rel_bias_grad")
    d_rel_bias = d_rb[:, :NUM_BUCKETS].T

    chip = (2 * lax.axis_index("x") + lax.axis_index("y")).astype(jnp.int32).reshape(1)
    jobs = [(nm, l) for nm in big_names for l in range(L)]
    late = _AllGather(_pack([small[k][0][None] for k in names_layer] + [d_rel_bias]))
    mine, sib, pending = {}, {}, []
    for i, (nm, l) in enumerate(jobs):
        g, axis, base, width, stride = own[nm][l]
        swapped = pending if len(pending) == 2 else []
        units = ([late] if i == 0 else []) + ([_Swap([mine[j] for j in swapped])] if swapped else [])
        mine[nm, l], *rest = _sum_chip_partials(g, axis, base, width, stride, got[nm][l], chip, name="sum_chip_partials",
                                                comm=units)
        if i == 0:
            gathered_late = rest.pop(0)
        if swapped:
            for j, arrived in zip(swapped, rest):
                sib[j] = arrived
            pending = []
        pending.append((nm, l))
    for j, arrived in zip(pending, _comm_only([_Swap([mine[j] for j in pending])], name="swap_last_sums")):
        sib[j] = arrived
    wmv = dict(w_in=(w_in, m_w_in, v_w_in), w_out=(w_out, m_w_out, v_w_out), w_gate=(w_gate, m_w_gate, v_w_gate),
               w_up=(w_up, m_w_up, v_w_up), w_down=(w_down, m_w_down, v_w_down))
    big = {}
    two = lambda t: t.reshape(-1, t.shape[-1])
    for nm in big_names:
        w, m, v = wmv[nm]
        res = None
        for l in range(L):
            res = _adamw(two(w), [mine[nm, l], sib[nm, l]], two(m), two(v), layer=l, stacked=res, name="adamw_" + nm)
        big[nm] = [r.reshape(w.shape) for r in res]

    w_small = dict(rel_bias=rel_bias, norm1_g=norm1_g, sgu_norm_g=sgu_norm_g, sgu_w=sgu_w, sgu_b=sgu_b, q_norm_g=q_norm_g,
                   k_norm_g=k_norm_g, sinks=sinks, out_norm_a=out_norm_a, out_norm_b=out_norm_b, norm2_g=norm2_g)
    m_small = dict(rel_bias=m_rel_bias, norm1_g=m_norm1_g, sgu_norm_g=m_sgu_norm_g, sgu_w=m_sgu_w, sgu_b=m_sgu_b,
                   q_norm_g=m_q_norm_g, k_norm_g=m_k_norm_g, sinks=m_sinks, out_norm_a=m_out_norm_a,
                   out_norm_b=m_out_norm_b, norm2_g=m_norm2_g)
    v_small = dict(rel_bias=v_rel_bias, norm1_g=v_norm1_g, sgu_norm_g=v_sgu_norm_g, sgu_w=v_sgu_w, sgu_b=v_sgu_b,
                   q_norm_g=v_q_norm_g, k_norm_g=v_k_norm_g, sinks=v_sinks, out_norm_a=v_out_norm_a,
                   out_norm_b=v_out_norm_b, norm2_g=v_norm2_g)

    def adamw_small(gathered, pick, name):
        like = pick(w_small)
        res = _adamw(_pack(like), [gathered[i] for i in range(N_DEV)], _pack(pick(m_small)), _pack(pick(v_small)), name=name)
        return [_unpack(r, like) for r in res]

    first_layer = adamw_small(gathered_late, lambda d: [d[k][:1] for k in names_layer] + [d["rel_bias"]], "adamw_small_late")
    if L > 1:
        others = adamw_small(gathered_early[0], lambda d: [d[k][1:] for k in names_layer], "adamw_small_early")
    sm = {"rel_bias": [r[-1] for r in first_layer]}
    for j, k in enumerate(names_layer):
        sm[k] = [jnp.concatenate([first_layer[q][j]] + ([others[q][j]] if L > 1 else [])) for q in range(4)]

    order = ("rel_bias", "norm1_g", "w_in", "sgu_norm_g", "sgu_w", "sgu_b", "q_norm_g", "k_norm_g", "sinks", "out_norm_a",
             "out_norm_b", "w_out", "norm2_g", "w_gate", "w_up", "w_down")
    pick = lambda k, i: big[k][i] if k in big else sm[k][i]
    outs = [loss, grad_x]
    for i in range(4):
        outs += [pick(k, i) for k in order]
    return tuple(outs)
```
